```python
import jax, jax.numpy as jnp
from jax import lax
import numpy as np

D_MODEL = 1024
BATCH = 4
SEQ = 4096
DEPTH = 2

N_MEM = 256
EPS = 1e-6
SC_WIDTH = D_MODEL // 2
SC_GROUPS = 8
CONV_K = 3
POOL_WIDTH = D_MODEL // 2
POOL_WINDOWS = (2, 4, 8, 16)
POOL_GROUP = POOL_WIDTH // len(POOL_WINDOWS)
EVEN_IN = 3 * SC_WIDTH + POOL_WIDTH
EVEN_MIX = SC_WIDTH + POOL_WIDTH
ML_HEADS = 4
ML_HEAD_DIM = 128
ML_WIDTH = ML_HEADS * ML_HEAD_DIM
ML_CHUNK = 64
MLA_HEADS = 8
QK_NOPE = 64
QK_ROPE = 32
V_DIM = 64
Q_LORA = 384
KV_LORA = 256
MLA_WIDTH = MLA_HEADS * V_DIM
Q_BLOCK = 128
ROPE_THETA = 10000.0
ODD_SIZES = (ML_WIDTH, ML_WIDTH, ML_WIDTH, ML_WIDTH, 2 * ML_HEADS, Q_LORA, KV_LORA, QK_ROPE)
ODD_SPLIT = tuple(int(v) for v in np.cumsum(ODD_SIZES)[:-1])
ODD_IN = sum(ODD_SIZES)
ODD_MIX = ML_WIDTH + MLA_WIDTH
XA_HEADS = 4
XA_HEAD_DIM = 128
XA_WIDTH = XA_HEADS * XA_HEAD_DIM
FFN_HIDDEN = ((8 * D_MODEL + 3 * 256 - 1) // (3 * 256)) * 256
N_EVEN = (DEPTH + 1) // 2
N_ODD = DEPTH // 2

kernel_name = 'hybrid_conv_pool_mlstm_mla_trunk'


def rmsnorm(x, g):
    xf = x.astype(jnp.float32)
    y = xf * lax.rsqrt(jnp.mean(xf * xf, axis=-1, keepdims=True) + EPS)
    return (y * g.astype(jnp.float32)).astype(x.dtype)


def causal_dwconv(u, w):
    c = u.shape[-1]
    return lax.conv_general_dilated(u, w[:, None, :].astype(u.dtype), window_strides=(1,),
                                    padding=[(CONV_K - 1, 0)],
                                    dimension_numbers=('NWC', 'WIO', 'NWC'),
                                    feature_group_count=c)


def causal_pool_mixer(u, pool_w, pool_scale):
    B, S, _ = u.shape
    ug = u.reshape(B, S, len(POOL_WINDOWS), POOL_GROUP)
    csum = jnp.cumsum(ug.astype(jnp.float32), axis=1)
    t1 = jnp.arange(1, S + 1)
    means = []
    for gi, w in enumerate(POOL_WINDOWS):
        cg = csum[:, :, gi]
        shifted = jnp.pad(cg, ((0, 0), (w, 0), (0, 0)))[:, :S]
        cnt = jnp.minimum(t1, w).astype(jnp.float32)
        means.append((cg - shifted) / cnt[None, :, None])
    pooled = jnp.stack(means, axis=2).astype(u.dtype)
    y = jnp.einsum('bsgc,gcd->bsgd', pooled - ug, pool_w)
    return y.reshape(B, S, POOL_WIDTH) * pool_scale


def even_mixer(h, w_in, conv_w, pool_w, pool_scale, w_out):
    z = h @ w_in
    g_b, g_c, xa, xb = jnp.split(z, [SC_WIDTH, 2 * SC_WIDTH, 3 * SC_WIDTH], axis=-1)
    ya = g_b * causal_dwconv(g_c * xa, conv_w)
    yb = causal_pool_mixer(xb, pool_w, pool_scale)
    return jnp.concatenate([ya, yb], axis=-1) @ w_out


def mlstm_chunkwise(q, k, v, i_pre, f_pre):
    B, S, H, D = q.shape
    L = ML_CHUNK
    NC = S // L

    def chunk(t):
        return t.reshape(B, NC, L, H, D).transpose(0, 3, 1, 2, 4)

    def chunk_g(t):
        return t.reshape(B, NC, L, H).transpose(0, 3, 1, 2)

    qc, kc, vc = chunk(q), chunk(k) * (D ** -0.5), chunk(v)
    ig = chunk_g(i_pre)
    bcum = jnp.cumsum(chunk_g(jax.nn.log_sigmoid(f_pre)), axis=-1)
    gtot = bcum[..., -1]

    a = gtot[..., None] - bcum + ig
    m_loc = jnp.max(a, axis=-1)
    w_loc = jnp.exp(a - m_loc[..., None])
    C_loc = jnp.einsum('bhcl,bhclv,bhclk->bhcvk', w_loc, vc, kc)
    n_loc = jnp.einsum('bhcl,bhclk->bhck', w_loc, kc)

    def step(carry, xs):
        C, n, m = carry
        g_c, m_l, C_l, n_l = xs
        m_new = jnp.maximum(g_c + m, m_l)
        s_old = jnp.exp(g_c + m - m_new)
        s_new = jnp.exp(m_l - m_new)
        C_new = s_old[..., None, None] * C + s_new[..., None, None] * C_l
        n_new = s_old[..., None] * n + s_new[..., None] * n_l
        return (C_new, n_new, m_new), (C, n, m)

    init = (jnp.zeros((B, H, D, D), jnp.float32), jnp.zeros((B, H, D), jnp.float32),
            jnp.zeros((B, H), jnp.float32))
    xs = (jnp.moveaxis(gtot, 2, 0), jnp.moveaxis(m_loc, 2, 0),
          jnp.moveaxis(C_loc, 2, 0), jnp.moveaxis(n_loc, 2, 0))
    _, (C_prev, n_prev, m_prev) = lax.scan(step, init, xs)
    C_prev = jnp.moveaxis(C_prev, 0, 2)
    n_prev = jnp.moveaxis(n_prev, 0, 2)
    m_prev = jnp.moveaxis(m_prev, 0, 2)

    causal = jnp.tril(jnp.ones((L, L), dtype=bool))
    dmat = jnp.where(causal, bcum[..., :, None] - bcum[..., None, :] + ig[..., None, :], -jnp.inf)
    inter = bcum + m_prev[..., None]
    m_t = jnp.maximum(inter, jnp.max(dmat, axis=-1))
    w_intra = jnp.exp(dmat - m_t[..., None])
    w_inter = jnp.exp(inter - m_t)
    sw = w_intra * jnp.einsum('bhcld,bhcsd->bhcls', qc, kc)
    num = (w_inter[..., None] * jnp.einsum('bhcvk,bhclk->bhclv', C_prev, qc)
           + jnp.einsum('bhcls,bhcsv->bhclv', sw, vc))
    den_raw = w_inter * jnp.einsum('bhck,bhclk->bhcl', n_prev, qc) + jnp.sum(sw, axis=-1)
    den = jnp.maximum(jnp.abs(den_raw), jnp.exp(-m_t))
    hc = num / den[..., None]
    return hc.transpose(0, 2, 3, 1, 4).reshape(B, S, H, D)


def rope_tables(positions):
    half = QK_ROPE // 2
    inv_freq = ROPE_THETA ** (-jnp.arange(half, dtype=jnp.float32) / half)
    ang = positions.astype(jnp.float32)[..., None] * inv_freq
    return jnp.cos(ang)[:, :, None, :], jnp.sin(ang)[:, :, None, :]


def apply_rope(t, cos, sin):
    half = t.shape[-1] // 2
    tf = t.astype(jnp.float32)
    t1, t2 = tf[..., :half], tf[..., half:]
    return jnp.concatenate([t1 * cos - t2 * sin, t1 * sin + t2 * cos], axis=-1).astype(t.dtype)


def mla_attention(q_nope, q_rope, k_nope, k_rope, v):
    B, S, H, _ = q_nope.shape
    nb = S // Q_BLOCK
    scale = (QK_NOPE + QK_ROPE) ** -0.5
    k_idx = jnp.arange(S)

    def to_blocks(t):
        return jnp.moveaxis(t.reshape(B, nb, Q_BLOCK, H, t.shape[-1]), 1, 0)

    def attend_block(args):
        blk, qn, qr = args
        s = (jnp.einsum('bqhd,bkhd->bhqk', qn, k_nope)
             + jnp.einsum('bqhr,bkr->bhqk', qr, k_rope)).astype(jnp.float32) * scale
        q_idx = blk * Q_BLOCK + jnp.arange(Q_BLOCK)
        s = jnp.where(k_idx[None, :] <= q_idx[:, None], s, -jnp.inf)
        p = jax.nn.softmax(s, axis=-1).astype(v.dtype)
        return jnp.einsum('bhqk,bkhd->bqhd', p, v)

    out = lax.map(attend_block, (jnp.arange(nb), to_blocks(q_nope), to_blocks(q_rope)))
    return jnp.moveaxis(out, 0, 1).reshape(B, S, H * V_DIM)


def odd_mixer(h, positions, w_in, gate_bias, ml_norm_g, q_norm_g, kv_norm_g, w_uq, w_ukv, w_out):
    B, S, _ = h.shape
    z = h @ w_in
    q_m, k_m, v_m, o_m, gates, c_q, c_kv, k_r = jnp.split(z, ODD_SPLIT, axis=-1)
    gates = (gates + gate_bias).astype(jnp.float32)
    i_pre, f_pre = gates[..., :ML_HEADS], gates[..., ML_HEADS:]
    hs = (B, S, ML_HEADS, ML_HEAD_DIM)
    hm = mlstm_chunkwise(q_m.reshape(hs).astype(jnp.float32), k_m.reshape(hs).astype(jnp.float32),
                         v_m.reshape(hs).astype(jnp.float32), i_pre, f_pre).astype(h.dtype)
    hm = jax.nn.sigmoid(o_m).reshape(hs) * hm
    hm = rmsnorm(hm, ml_norm_g.reshape(ML_HEADS, ML_HEAD_DIM)).reshape(B, S, ML_WIDTH)

    q = (rmsnorm(c_q, q_norm_g) @ w_uq).reshape(B, S, MLA_HEADS, QK_NOPE + QK_ROPE)
    kv = (rmsnorm(c_kv, kv_norm_g) @ w_ukv).reshape(B, S, MLA_HEADS, QK_NOPE + V_DIM)
    cos, sin = rope_tables(positions)
    q_nope, q_rope = q[..., :QK_NOPE], apply_rope(q[..., QK_NOPE:], cos, sin)
    k_nope, v = kv[..., :QK_NOPE], kv[..., QK_NOPE:]
    k_rope = apply_rope(k_r[:, :, None, :], cos, sin)[:, :, 0]
    ha = mla_attention(q_nope, q_rope, k_nope, k_rope, v)
    return jnp.concatenate([hm, ha], axis=-1) @ w_out


def memory_cross_attention(h, mem_n, wq, wkv, wo):
    B, S, _ = h.shape
    q = (h @ wq).reshape(B, S, XA_HEADS, XA_HEAD_DIM)
    k, v = jnp.split(mem_n @ wkv, 2, axis=-1)
    k = k.reshape(B, N_MEM, XA_HEADS, XA_HEAD_DIM)
    v = v.reshape(B, N_MEM, XA_HEADS, XA_HEAD_DIM)
    s = jnp.einsum('bshd,bmhd->bhsm', q, k).astype(jnp.float32) * (XA_HEAD_DIM ** -0.5)
    p = jax.nn.softmax(s, axis=-1).astype(v.dtype)
    return jnp.einsum('bhsm,bmhd->bshd', p, v).reshape(B, S, XA_WIDTH) @ wo


def swiglu(h, w_gate_up, w_down):
    g, u = jnp.split(h @ w_gate_up, 2, axis=-1)
    return (jax.nn.silu(g) * u) @ w_down


def setup_inputs(seed: int = 0) -> dict:
    key = jax.random.key(seed)
    ks = iter(jax.random.split(key, 40))
    f32 = jnp.float32

    def dense(shape, fan_in):
        return jax.random.normal(next(ks), shape, f32) * (fan_in ** -0.5)

    def gain(shape):
        return 1.0 + 0.05 * jax.random.normal(next(ks), shape, f32)

    x = jax.random.normal(next(ks), (BATCH, SEQ, D_MODEL), f32)
    mem = jax.random.normal(next(ks), (BATCH, N_MEM, D_MODEL), f32)
    offsets = jax.random.randint(next(ks), (BATCH, 1), 0, 1024, dtype=jnp.int32)
    positions = offsets + jnp.arange(SEQ, dtype=jnp.int32)[None, :]

    f_bias = jnp.linspace(3.0, 6.0, ML_HEADS, dtype=f32)[None, :] + 0.1 * jax.random.normal(next(ks), (N_ODD, ML_HEADS), f32)
    i_bias = 0.1 * jax.random.normal(next(ks), (N_ODD, ML_HEADS), f32)

    return {
        'x': x,
        'mem': mem,
        'positions': positions,
        'norm_mix_g': gain((DEPTH, D_MODEL)),
        'norm_xattn_g': gain((DEPTH, D_MODEL)),
        'mem_norm_g': gain((DEPTH, D_MODEL)),
        'xattn_wq': dense((DEPTH, D_MODEL, XA_WIDTH), D_MODEL),
        'xattn_wkv': dense((DEPTH, D_MODEL, 2 * XA_WIDTH), D_MODEL),
        'xattn_wo': dense((DEPTH, XA_WIDTH, D_MODEL), XA_WIDTH),
        'norm_ffn_g': gain((DEPTH, D_MODEL)),
        'ffn_w_gate_up': dense((DEPTH, D_MODEL, 2 * FFN_HIDDEN), D_MODEL),
        'ffn_w_down': dense((DEPTH, FFN_HIDDEN, D_MODEL), FFN_HIDDEN),
        'ev_w_in': dense((N_EVEN, D_MODEL, EVEN_IN), D_MODEL),
        'ev_conv_w': dense((N_EVEN, CONV_K, SC_WIDTH), CONV_K),
        'ev_pool_w': dense((N_EVEN, len(POOL_WINDOWS), POOL_GROUP, POOL_GROUP), POOL_GROUP),
        'ev_pool_scale': 1.0 + 0.1 * jax.random.normal(next(ks), (N_EVEN, POOL_WIDTH), f32),
        'ev_w_out': dense((N_EVEN, EVEN_MIX, D_MODEL), EVEN_MIX),
        'od_w_in': dense((N_ODD, D_MODEL, ODD_IN), D_MODEL),
        'od_gate_bias': jnp.concatenate([i_bias, f_bias], axis=-1),
        'od_ml_norm_g': gain((N_ODD, ML_WIDTH)),
        'od_q_norm_g': gain((N_ODD, Q_LORA)),
        'od_kv_norm_g': gain((N_ODD, KV_LORA)),
        'od_w_uq': dense((N_ODD, Q_LORA, MLA_HEADS * (QK_NOPE + QK_ROPE)), Q_LORA),
        'od_w_ukv': dense((N_ODD, KV_LORA, MLA_HEADS * (QK_NOPE + V_DIM)), KV_LORA),
        'od_w_out': dense((N_ODD, ODD_MIX, D_MODEL), ODD_MIX),
        'final_norm_g': gain((D_MODEL,)),
    }


def reference(x, mem, positions, norm_mix_g, norm_xattn_g, mem_norm_g, xattn_wq, xattn_wkv, xattn_wo,
              norm_ffn_g, ffn_w_gate_up, ffn_w_down, ev_w_in, ev_conv_w, ev_pool_w, ev_pool_scale,
              ev_w_out, od_w_in, od_gate_bias, od_ml_norm_g, od_q_norm_g, od_kv_norm_g, od_w_uq,
              od_w_ukv, od_w_out, final_norm_g):
    for layer in range(DEPTH):
        h = rmsnorm(x, norm_mix_g[layer])
        if layer % 2 == 0:
            e = layer // 2
            mix = even_mixer(h, ev_w_in[e], ev_conv_w[e], ev_pool_w[e], ev_pool_scale[e], ev_w_out[e])
        else:
            o = layer // 2
            mix = odd_mixer(h, positions, od_w_in[o], od_gate_bias[o], od_ml_norm_g[o], od_q_norm_g[o],
                            od_kv_norm_g[o], od_w_uq[o], od_w_ukv[o], od_w_out[o])
        x = x + mix
        mem_n = rmsnorm(mem, mem_norm_g[layer])
        x = x + memory_cross_attention(rmsnorm(x, norm_xattn_g[layer]), mem_n,
                                       xattn_wq[layer], xattn_wkv[layer], xattn_wo[layer])
        x = x + swiglu(rmsnorm(x, norm_ffn_g[layer]), ffn_w_gate_up[layer], ffn_w_down[layer])
    return rmsnorm(x, final_norm_g)
```

```python
import functools

import jax
import jax.numpy as jnp
from jax import lax
from jax.experimental import pallas as pl
from jax.experimental.pallas import tpu as pltpu

F32 = jnp.float32
BF16 = jnp.bfloat16

EPS = 1e-6
ROPE_THETA = 10000.0
LANES = 128
SUBLANES = 8
VMEM_LIMIT_BYTES = 56 * 1024 * 1024
NEG_BIG = -1e30

SC_WIDTH = 512
CONV_K = 3
POOL_WINDOWS = (2, 4, 8, 16)
POOL_GROUP = 128
HALO = 16
ML_HEADS = 4
ML_HEAD_DIM = 128
ML_WIDTH = ML_HEADS * ML_HEAD_DIM
MLA_HEADS = 8
QK_NOPE = 64
QK_ROPE = 32
V_DIM = 64
Q_LORA = 384
KV_LORA = 256
MLA_WIDTH = MLA_HEADS * V_DIM
XA_HEADS = 4
XA_HEAD_DIM = 128
XA_WIDTH = XA_HEADS * XA_HEAD_DIM


def _params(*sem):
    return pltpu.CompilerParams(dimension_semantics=sem, vmem_limit_bytes=VMEM_LIMIT_BYTES)


def _const_spec(shape):
    nd = len(shape)
    return pl.BlockSpec(shape, lambda *_: (0,) * nd, pipeline_mode=pl.Buffered(1))


def _rms(x, g):
    ms = jnp.mean(x * x, axis=-1, keepdims=True)
    return x * lax.rsqrt(ms + EPS) * g


def _dot(a, b):
    return jnp.dot(a, b, preferred_element_type=F32)


def _dot_nt(a, b):
    return lax.dot_general(a, b, (((1,), (1,)), ((), ())), preferred_element_type=F32)


def _dot_tn(a, b):
    return lax.dot_general(a, b, (((0,), (0,)), ((), ())), preferred_element_type=F32)


def _even_kernel(x_ref, g_ref, win_ref, convw_ref, poolw_ref, pscale_ref, wout_ref, o_ref,
                 ubuf, xbbuf, *, ts):
    si = pl.program_id(1)

    @pl.when(si == 0)
    def _():
        ubuf[0:HALO, :] = jnp.zeros((HALO, SC_WIDTH), F32)
        xbbuf[0:HALO, :] = jnp.zeros((HALO, SC_WIDTH), F32)

    x = x_ref[0]
    h = _rms(x, g_ref[...]).astype(BF16)
    z = _dot(h, win_ref[...])
    g_b = z[:, 0:SC_WIDTH]
    u = z[:, SC_WIDTH:2 * SC_WIDTH] * z[:, 2 * SC_WIDTH:3 * SC_WIDTH]
    xb = z[:, 3 * SC_WIDTH:4 * SC_WIDTH]
    ubuf[HALO:HALO + ts, :] = u
    xbbuf[HALO:HALO + ts, :] = xb

    cw = convw_ref[...]
    conv = cw[2:3, :] * u
    for k in range(CONV_K - 1):
        back = CONV_K - 1 - k
        conv = conv + cw[k:k + 1, :] * ubuf[HALO - back:HALO - back + ts, :]
    ya = g_b * conv

    t_idx = si * ts + lax.broadcasted_iota(jnp.int32, (ts, POOL_GROUP), 0)
    ys = [ya.astype(BF16)]
    for gi, w in enumerate(POOL_WINDOWS):
        lo = gi * POOL_GROUP
        xg = xb[:, lo:lo + POOL_GROUP]
        acc = xg
        for j in range(1, w):
            acc = acc + xbbuf[HALO - j:HALO - j + ts, lo:lo + POOL_GROUP]
        cnt = jnp.minimum(t_idx + 1, w).astype(F32)
        d = (acc / cnt - xg).astype(BF16)
        yg = _dot(d, poolw_ref[gi]) * pscale_ref[:, lo:lo + POOL_GROUP]
        ys.append(yg.astype(BF16))

    ubuf[0:HALO, :] = ubuf[ts:ts + HALO, :]
    xbbuf[0:HALO, :] = xbbuf[ts:ts + HALO, :]

    mix = jnp.concatenate(ys, axis=-1)
    o_ref[0] = x + _dot(mix, wout_ref[...])


def _even_mixer(x, g, w_in, conv_w, pool_w, pool_scale, w_out, *, ts):
    B, S, D = x.shape
    ts = min(ts, S)
    kern = functools.partial(_even_kernel, ts=ts)
    return pl.pallas_call(
        kern,
        grid=(B, S // ts),
        in_specs=[
            pl.BlockSpec((1, ts, D), lambda b, s: (b, s, 0)),
            _const_spec((1, D)),
            _const_spec(w_in.shape),
            _const_spec(conv_w.shape),
            _const_spec(pool_w.shape),
            _const_spec((1, SC_WIDTH)),
            _const_spec(w_out.shape),
        ],
        out_specs=pl.BlockSpec((1, ts, D), lambda b, s: (b, s, 0)),
        out_shape=jax.ShapeDtypeStruct(x.shape, F32),
        scratch_shapes=[pltpu.VMEM((HALO + ts, SC_WIDTH), F32),
                        pltpu.VMEM((HALO + ts, SC_WIDTH), F32)],
        compiler_params=_params("arbitrary", "arbitrary"),
        name="even_mixer",
    )(x, g.reshape(1, D), w_in, conv_w, pool_w, pool_scale.reshape(1, SC_WIDTH), w_out)


def _memkv_kernel(mem_ref, g_ref, wkv_ref, kv_ref):
    h = _rms(mem_ref[0], g_ref[...]).astype(BF16)
    kv_ref[0] = _dot(h, wkv_ref[...]).astype(BF16)


def _mem_kv(mem, g, wkv):
    B, M, D = mem.shape
    N = wkv.shape[1]
    return pl.pallas_call(
        _memkv_kernel,
        grid=(B,),
        in_specs=[pl.BlockSpec((1, M, D), lambda b: (b, 0, 0)),
                  _const_spec((1, D)), _const_spec(wkv.shape)],
        out_specs=pl.BlockSpec((1, M, N), lambda b: (b, 0, 0)),
        out_shape=jax.ShapeDtypeStruct((B, M, N), BF16),
        compiler_params=_params("arbitrary"),
        name="mem_kv",
    )(mem, g.reshape(1, D), wkv)


def _xattn_kernel(x_ref, g_ref, wq_ref, kv_ref, wo_ref, o_ref):
    x = x_ref[0]
    h = _rms(x, g_ref[...]).astype(BF16)
    q = (_dot(h, wq_ref[...]) * (XA_HEAD_DIM ** -0.5)).astype(BF16)
    kv = kv_ref[0]
    outs = []
    for hd in range(XA_HEADS):
        lo = hd * XA_HEAD_DIM
        s = _dot_nt(q[:, lo:lo + XA_HEAD_DIM], kv[:, lo:lo + XA_HEAD_DIM])
        m = jnp.max(s, axis=-1, keepdims=True)
        e = jnp.exp(s - m)
        l = jnp.sum(e, axis=-1, keepdims=True)
        p = (e / l).astype(BF16)
        outs.append(_dot(p, kv[:, XA_WIDTH + lo:XA_WIDTH + lo + XA_HEAD_DIM]).astype(BF16))
    a = jnp.concatenate(outs, axis=-1)
    o_ref[0] = x + _dot(a, wo_ref[...])


def _xattn(x, g, wq, kv, wo, *, ts):
    B, S, D = x.shape
    ts = min(ts, S)
    M = kv.shape[1]
    return pl.pallas_call(
        _xattn_kernel,
        grid=(B, S // ts),
        in_specs=[
            pl.BlockSpec((1, ts, D), lambda b, s: (b, s, 0)),
            _const_spec((1, D)),
            _const_spec(wq.shape),
            pl.BlockSpec((1, M, kv.shape[2]), lambda b, s: (b, 0, 0)),
            _const_spec(wo.shape),
        ],
        out_specs=pl.BlockSpec((1, ts, D), lambda b, s: (b, s, 0)),
        out_shape=jax.ShapeDtypeStruct(x.shape, F32),
        compiler_params=_params("arbitrary", "arbitrary"),
        name="xattn",
    )(x, g.reshape(1, D), wq, kv, wo)


def _swiglu_kernel(x_ref, g_ref, wgu_ref, wd_ref, fg_ref, o_ref, *, hidden, th, final_norm):
    x = x_ref[...]
    h = _rms(x, g_ref[...]).astype(BF16)
    acc = x
    for j in range(hidden // th):
        gate = _dot(h, wgu_ref[:, j * th:(j + 1) * th])
        up = _dot(h, wgu_ref[:, hidden + j * th:hidden + (j + 1) * th])
        act = (gate * jax.nn.sigmoid(gate) * up).astype(BF16)
        acc = acc + _dot(act, wd_ref[j * th:(j + 1) * th, :])
    if final_norm:
        acc = _rms(acc, fg_ref[...])
    o_ref[...] = acc


def _swiglu(x, g, w_gu, w_d, final_g, *, tm, th, final_norm):
    B, S, D = x.shape
    T = B * S
    tm = min(tm, T)
    hidden = w_d.shape[0]
    kern = functools.partial(_swiglu_kernel, hidden=hidden, th=th, final_norm=final_norm)
    out = pl.pallas_call(
        kern,
        grid=(T // tm,),
        in_specs=[
            pl.BlockSpec((tm, D), lambda i: (i, 0)),
            _const_spec((1, D)),
            _const_spec(w_gu.shape),
            _const_spec(w_d.shape),
            _const_spec((1, D)),
        ],
        out_specs=pl.BlockSpec((tm, D), lambda i: (i, 0)),
        out_shape=jax.ShapeDtypeStruct((T, D), F32),
        compiler_params=_params("arbitrary"),
        name="swiglu",
    )(x.reshape(T, D), g.reshape(1, D), w_gu, w_d, final_g.reshape(1, D))
    return out.reshape(B, S, D)


def _odd_proj_kernel(x_ref, pos_ref, g_ref, wmain_ref, wgt_ref, gbias_ref, qg_ref, kvg_ref,
                     wuq_ref, wuqs_ref, wuk_ref, wuv_ref, invf_ref,
                     qm_ref, km_ref, vm_ref, om_ref, gates_ref, qp_ref, kp_ref, va_ref):
    x = x_ref[0]
    h = _rms(x, g_ref[...]).astype(BF16)
    z = _dot(h, wmain_ref[...])
    qm_ref[0] = z[:, 0:ML_WIDTH].astype(BF16)
    km_ref[0] = (z[:, ML_WIDTH:2 * ML_WIDTH] * (ML_HEAD_DIM ** -0.5)).astype(BF16)
    vm_ref[0] = z[:, 2 * ML_WIDTH:3 * ML_WIDTH].astype(BF16)
    om_ref[0] = z[:, 3 * ML_WIDTH:4 * ML_WIDTH].astype(BF16)
    gates_ref[0] = _dot_nt(wgt_ref[...], h) + gbias_ref[...]

    c0 = 4 * ML_WIDTH
    c_q = z[:, c0:c0 + Q_LORA]
    c_kv = z[:, c0 + Q_LORA:c0 + Q_LORA + KV_LORA]
    kr = z[:, c0 + Q_LORA + KV_LORA:c0 + Q_LORA + KV_LORA + LANES]
    kr_sw = z[:, c0 + Q_LORA + KV_LORA + LANES:c0 + Q_LORA + KV_LORA + 2 * LANES]

    ang = pos_ref[0].astype(F32) * invf_ref[...]
    lane = lax.broadcasted_iota(jnp.int32, ang.shape, 1)
    in_rope = (lane >= QK_NOPE) & (lane < QK_NOPE + QK_ROPE)
    first_half = lane < QK_NOPE + QK_ROPE // 2
    cos_t = jnp.where(in_rope, jnp.cos(ang), 1.0)
    sin_v = jnp.sin(ang)
    sin_t = jnp.where(in_rope, jnp.where(first_half, -sin_v, sin_v), 0.0)

    cqn = _rms(c_q, qg_ref[...]).astype(BF16)
    qa = _dot(cqn, wuq_ref[...])
    qb = _dot(cqn, wuqs_ref[...])
    ckn = _rms(c_kv, kvg_ref[...]).astype(BF16)
    kn = _dot(ckn, wuk_ref[...])
    va_ref[0] = _dot(ckn, wuv_ref[...]).astype(BF16)
    k_rope = kr * cos_t + kr_sw * sin_t
    scale = (QK_NOPE + QK_ROPE) ** -0.5
    for hd in range(MLA_HEADS):
        lo = hd * LANES
        qh = qa[:, lo:lo + LANES] * cos_t + qb[:, lo:lo + LANES] * sin_t
        qp_ref[0, :, lo:lo + LANES] = (qh * scale).astype(BF16)
        kp_ref[0, :, lo:lo + LANES] = (kn[:, lo:lo + LANES] + k_rope).astype(BF16)


def _odd_proj(x, pos3, g, w_main, w_gt, gbias, q_g, kv_g, w_uq_p, w_uq_s, w_uk_p, w_uv, invf, *, ts):
    B, S, D = x.shape
    ts = min(ts, S)
    tok = lambda n: pl.BlockSpec((1, ts, n), lambda b, s: (b, s, 0))
    out_shapes = (
        jax.ShapeDtypeStruct((B, S, ML_WIDTH), BF16),
        jax.ShapeDtypeStruct((B, S, ML_WIDTH), BF16),
        jax.ShapeDtypeStruct((B, S, ML_WIDTH), BF16),
        jax.ShapeDtypeStruct((B, S, ML_WIDTH), BF16),
        jax.ShapeDtypeStruct((B, 2 * ML_HEADS, S), F32),
        jax.ShapeDtypeStruct((B, S, MLA_HEADS * LANES), BF16),
        jax.ShapeDtypeStruct((B, S, MLA_HEADS * LANES), BF16),
        jax.ShapeDtypeStruct((B, S, MLA_WIDTH), BF16),
    )
    return pl.pallas_call(
        _odd_proj_kernel,
        grid=(B, S // ts),
        in_specs=[
            tok(D),
            pl.BlockSpec((1, ts, 1), lambda b, s: (b, s, 0)),
            _const_spec((1, D)),
            _const_spec(w_main.shape),
            _const_spec(w_gt.shape),
            _const_spec(gbias.shape),
            _const_spec(q_g.shape),
            _const_spec(kv_g.shape),
            _const_spec(w_uq_p.shape),
            _const_spec(w_uq_s.shape),
            _const_spec(w_uk_p.shape),
            _const_spec(w_uv.shape),
            _const_spec(invf.shape),
        ],
        out_specs=(tok(ML_WIDTH), tok(ML_WIDTH), tok(ML_WIDTH), tok(ML_WIDTH),
                   pl.BlockSpec((1, 2 * ML_HEADS, ts), lambda b, s: (b, 0, s)),
                   tok(MLA_HEADS * LANES), tok(MLA_HEADS * LANES), tok(MLA_WIDTH)),
        out_shape=out_shapes,
        compiler_params=_params("arbitrary", "arbitrary"),
        name="odd_proj",
    )(x, pos3, g.reshape(1, D), w_main, w_gt, gbias, q_g, kv_g, w_uq_p, w_uq_s, w_uk_p, w_uv, invf)


def _mlstm_kernel(q_ref, k_ref, v_ref, o_ref, gates_ref, ng_ref, h_ref, c_scr, n_scr, m_scr, *, L):
    ci = pl.program_id(1)

    @pl.when(ci == 0)
    def _():
        c_scr[...] = jnp.zeros(c_scr.shape, F32)
        n_scr[...] = jnp.zeros(n_scr.shape, F32)
        m_scr[...] = jnp.zeros(m_scr.shape, F32)

    row = lax.broadcasted_iota(jnp.int32, (L, L), 0)
    col = lax.broadcasted_iota(jnp.int32, (L, L), 1)
    causal = col <= row
    eye = col == row
    gates = gates_ref[0]

    for hd in range(ML_HEADS):
        lo = hd * ML_HEAD_DIM
        q = q_ref[0, :, lo:lo + ML_HEAD_DIM]
        k = k_ref[0, :, lo:lo + ML_HEAD_DIM]
        v = v_ref[0, :, lo:lo + ML_HEAD_DIM]
        ig_row = gates[hd:hd + 1, :]
        f_row = gates[ML_HEADS + hd:ML_HEADS + hd + 1, :]
        lf_row = jnp.minimum(f_row, 0.0) - jnp.log(1.0 + jnp.exp(-jnp.abs(f_row)))

        b_col = jnp.sum(jnp.where(causal, lf_row, 0.0), axis=1, keepdims=True)
        b_row = jnp.sum(jnp.where(eye, b_col, 0.0), axis=0, keepdims=True)
        ig_col = jnp.sum(jnp.where(eye, ig_row, 0.0), axis=1, keepdims=True)
        gtot = jnp.sum(lf_row, axis=1, keepdims=True)

        m_prev = m_scr[hd][:, 0:1]
        n_prev = n_scr[hd]
        c_prev = c_scr[hd]

        dmat = jnp.where(causal, b_col - b_row + ig_row, NEG_BIG)
        inter = b_col + m_prev
        m_t = jnp.maximum(inter, jnp.max(dmat, axis=1, keepdims=True))
        w_intra = jnp.exp(dmat - m_t)
        w_inter = jnp.exp(inter - m_t)
        sw = w_intra * _dot_nt(q, k)
        num = w_inter * _dot(q, c_prev.astype(BF16)) + _dot(sw.astype(BF16), v)
        qf = q.astype(F32)
        den_raw = (w_inter * jnp.sum(qf * n_prev, axis=1, keepdims=True)
                   + jnp.sum(sw, axis=1, keepdims=True))
        den = jnp.maximum(jnp.abs(den_raw), jnp.exp(-m_t))
        hc = num / den

        og = jax.nn.sigmoid(o_ref[0, :, lo:lo + ML_HEAD_DIM].astype(F32))
        hg = og * hc
        h_ref[0, :, lo:lo + ML_HEAD_DIM] = _rms(hg, ng_ref[:, lo:lo + ML_HEAD_DIM]).astype(BF16)

        a_row = gtot - b_row + ig_row
        m_loc = jnp.max(a_row, axis=1, keepdims=True)
        w_loc = jnp.exp(gtot - b_col + ig_col - m_loc)
        m_new = jnp.maximum(gtot + m_prev, m_loc)
        s_old = jnp.exp(gtot + m_prev - m_new)
        s_new = jnp.exp(m_loc - m_new)
        kw = k.astype(F32) * w_loc
        c_scr[hd] = s_old * c_prev + s_new * _dot_tn(kw.astype(BF16), v)
        n_scr[hd] = s_old * n_prev + s_new * jnp.sum(kw, axis=0, keepdims=True)
        m_scr[hd] = jnp.broadcast_to(m_new, (1, LANES))


def _mlstm(q, k, v, o, gates_t, norm_g, *, L):
    B, S, W = q.shape
    L = min(L, S)
    kern = functools.partial(_mlstm_kernel, L=L)
    tok = pl.BlockSpec((1, L, W), lambda b, c: (b, c, 0))
    return pl.pallas_call(
        kern,
        grid=(B, S // L),
        in_specs=[tok, tok, tok, tok,
                  pl.BlockSpec((1, 2 * ML_HEADS, L), lambda b, c: (b, 0, c)),
                  _const_spec((1, W))],
        out_specs=tok,
        out_shape=jax.ShapeDtypeStruct((B, S, W), BF16),
        scratch_shapes=[pltpu.VMEM((ML_HEADS, ML_HEAD_DIM, ML_HEAD_DIM), F32),
                        pltpu.VMEM((ML_HEADS, 1, ML_HEAD_DIM), F32),
                        pltpu.VMEM((ML_HEADS, 1, LANES), F32)],
        compiler_params=_params("arbitrary", "arbitrary"),
        name="mlstm",
    )(q, k, v, o, gates_t, norm_g.reshape(1, W))


def _mla_kernel(q_ref, k_ref, v_ref, o_ref, *, tq):
    qi = pl.program_id(2)
    heads = 2
    qs = [q_ref[0, :, hh * LANES:(hh + 1) * LANES] for hh in range(heads)]

    def block(j, carry, masked):
        start = pl.multiple_of(j * tq, tq)
        vb = v_ref[0, pl.ds(start, tq), :]
        new = []
        for hh in range(heads):
            m, l, acc = carry[hh]
            kb = k_ref[0, pl.ds(start, tq), hh * LANES:(hh + 1) * LANES]
            s = _dot_nt(qs[hh], kb)
            if masked:
                r = lax.broadcasted_iota(jnp.int32, s.shape, 0)
                c = lax.broadcasted_iota(jnp.int32, s.shape, 1)
                s = jnp.where(c <= r, s, NEG_BIG)
            m_new = jnp.maximum(m, jnp.max(s, axis=-1, keepdims=True))
            alpha = jnp.exp(m - m_new)
            p = jnp.exp(s - m_new)
            l_new = alpha * l + jnp.sum(p, axis=-1, keepdims=True)
            acc_new = alpha * acc + _dot(p.astype(BF16), vb)
            new.append((m_new, l_new, acc_new))
        return tuple(new)

    init = tuple((jnp.full((tq, 1), NEG_BIG, F32), jnp.zeros((tq, 1), F32),
                  jnp.zeros((tq, LANES), F32)) for _ in range(heads))
    carry = lax.fori_loop(0, qi, lambda j, c: block(j, c, False), init)
    carry = block(qi, carry, True)
    lane = lax.broadcasted_iota(jnp.int32, (tq, LANES), 1)
    o0 = carry[0][2] / carry[0][1]
    o1 = carry[1][2] / carry[1][1]
    o_ref[0] = jnp.where(lane < V_DIM, o0, o1).astype(BF16)


def _mla_attention(qp, kp, va, *, tq):
    B, S, _ = qp.shape
    tq = min(tq, S)
    pairs = MLA_HEADS // 2
    kern = functools.partial(_mla_kernel, tq=tq)
    return pl.pallas_call(
        kern,
        grid=(B, pairs, S // tq),
        in_specs=[
            pl.BlockSpec((1, tq, 2 * LANES), lambda b, p, i: (b, i, p)),
            pl.BlockSpec((1, S, 2 * LANES), lambda b, p, i: (b, 0, p)),
            pl.BlockSpec((1, S, LANES), lambda b, p, i: (b, 0, p)),
        ],
        out_specs=pl.BlockSpec((1, tq, LANES), lambda b, p, i: (b, i, p)),
        out_shape=jax.ShapeDtypeStruct((B, S, MLA_WIDTH), BF16),
        compiler_params=_params("arbitrary", "arbitrary", "arbitrary"),
        name="mla_attention",
    )(qp, kp, va)


def _odd_out_kernel(x_ref, hm_ref, ha_ref, wout_ref, o_ref):
    mix = jnp.concatenate([hm_ref[...], ha_ref[...]], axis=-1)
    o_ref[...] = x_ref[...] + _dot(mix, wout_ref[...])


def _odd_out(x, hm, ha, w_out, *, tm):
    B, S, D = x.shape
    T = B * S
    tm = min(tm, T)
    out = pl.pallas_call(
        _odd_out_kernel,
        grid=(T // tm,),
        in_specs=[pl.BlockSpec((tm, D), lambda i: (i, 0)),
                  pl.BlockSpec((tm, ML_WIDTH), lambda i: (i, 0)),
                  pl.BlockSpec((tm, MLA_WIDTH), lambda i: (i, 0)),
                  _const_spec(w_out.shape)],
        out_specs=pl.BlockSpec((tm, D), lambda i: (i, 0)),
        out_shape=jax.ShapeDtypeStruct((T, D), F32),
        compiler_params=_params("arbitrary"),
        name="odd_out",
    )(x.reshape(T, D), hm.reshape(T, ML_WIDTH), ha.reshape(T, MLA_WIDTH), w_out)
    return out.reshape(B, S, D)


def _pack_odd_weights(w_in, w_uq, w_ukv):
    D = w_in.shape[0]
    c = 4 * ML_WIDTH
    w_g = w_in[:, c:c + 2 * ML_HEADS]
    c += 2 * ML_HEADS
    w_cq = w_in[:, c:c + Q_LORA]
    c += Q_LORA
    w_ckv = w_in[:, c:c + KV_LORA]
    c += KV_LORA
    w_kr = w_in[:, c:c + QK_ROPE]
    half = QK_ROPE // 2
    w_kr_sw = jnp.concatenate([w_kr[:, half:], w_kr[:, :half]], axis=1)
    zl = jnp.zeros((D, QK_NOPE), w_in.dtype)
    zr = jnp.zeros((D, LANES - QK_NOPE - QK_ROPE), w_in.dtype)
    w_main = jnp.concatenate([w_in[:, :4 * ML_WIDTH], w_cq, w_ckv,
                              zl, w_kr, zr, zl, w_kr_sw, zr], axis=1).astype(BF16)
    w_gt = w_g.T.astype(BF16)

    uq = w_uq.reshape(Q_LORA, MLA_HEADS, QK_NOPE + QK_ROPE)
    uq_n, uq_r = uq[..., :QK_NOPE], uq[..., QK_NOPE:]
    uq_r_sw = jnp.concatenate([uq_r[..., half:], uq_r[..., :half]], axis=-1)
    zpad = jnp.zeros((Q_LORA, MLA_HEADS, LANES - QK_NOPE - QK_ROPE), w_uq.dtype)
    w_uq_p = jnp.concatenate([uq_n, uq_r, zpad], axis=-1).reshape(Q_LORA, MLA_HEADS * LANES)
    w_uq_s = jnp.concatenate([jnp.zeros_like(uq_n), uq_r_sw, zpad], axis=-1)
    w_uq_s = w_uq_s.reshape(Q_LORA, MLA_HEADS * LANES)

    ukv = w_ukv.reshape(KV_LORA, MLA_HEADS, QK_NOPE + V_DIM)
    uk, uv = ukv[..., :QK_NOPE], ukv[..., QK_NOPE:]
    w_uk_p = jnp.concatenate([uk, jnp.zeros((KV_LORA, MLA_HEADS, LANES - QK_NOPE), w_ukv.dtype)],
                             axis=-1).reshape(KV_LORA, MLA_HEADS * LANES)
    w_uv = uv.reshape(KV_LORA, MLA_WIDTH)
    return (w_main, w_gt, w_uq_p.astype(BF16), w_uq_s.astype(BF16), w_uk_p.astype(BF16),
            w_uv.astype(BF16))


def _rope_inv_freq():
    half = QK_ROPE // 2
    inv = ROPE_THETA ** (-jnp.arange(half, dtype=F32) / half)
    blk = jnp.concatenate([jnp.zeros((QK_NOPE,), F32), inv, inv,
                           jnp.zeros((LANES - QK_NOPE - QK_ROPE,), F32)])
    return blk.reshape(1, LANES)


TS_EVEN = 512
TS_XATTN = 512
TM_FFN = 512
TH_FFN = 256
TS_ODD = 512
L_MLSTM = 256
TQ_MLA = 512
TM_OUT = 1024


def kernel(x, mem, positions, norm_mix_g, norm_xattn_g, mem_norm_g, xattn_wq, xattn_wkv, xattn_wo,
           norm_ffn_g, ffn_w_gate_up, ffn_w_down, ev_w_in, ev_conv_w, ev_pool_w, ev_pool_scale,
           ev_w_out, od_w_in, od_gate_bias, od_ml_norm_g, od_q_norm_g, od_kv_norm_g, od_w_uq,
           od_w_ukv, od_w_out, final_norm_g):
    depth = norm_mix_g.shape[0]
    B, S, D = x.shape
    pos3 = positions.reshape(B, S, 1)
    invf = _rope_inv_freq()
    for layer in range(depth):
        if layer % 2 == 0:
            e = layer // 2
            x = _even_mixer(x, norm_mix_g[layer], ev_w_in[e].astype(BF16), ev_conv_w[e],
                            ev_pool_w[e].astype(BF16), ev_pool_scale[e], ev_w_out[e].astype(BF16),
                            ts=TS_EVEN)
        else:
            o = layer // 2
            w_main, w_gt, w_uq_p, w_uq_s, w_uk_p, w_uv = _pack_odd_weights(
                od_w_in[o], od_w_uq[o], od_w_ukv[o])
            qm, km, vm, om, gates_t, qp, kp, va = _odd_proj(
                x, pos3, norm_mix_g[layer], w_main, w_gt, od_gate_bias[o].reshape(2 * ML_HEADS, 1),
                od_q_norm_g[o].reshape(1, Q_LORA), od_kv_norm_g[o].reshape(1, KV_LORA),
                w_uq_p, w_uq_s, w_uk_p, w_uv, invf, ts=TS_ODD)
            hm = _mlstm(qm, km, vm, om, gates_t, od_ml_norm_g[o], L=L_MLSTM)
            ha = _mla_attention(qp, kp, va, tq=TQ_MLA)
            x = _odd_out(x, hm, ha, od_w_out[o].astype(BF16), tm=TM_OUT)
        kv = _mem_kv(mem, mem_norm_g[layer], xattn_wkv[layer].astype(BF16))
        x = _xattn(x, norm_xattn_g[layer], xattn_wq[layer].astype(BF16), kv,
                   xattn_wo[layer].astype(BF16), ts=TS_XATTN)
        x = _swiglu(x, norm_ffn_g[layer], ffn_w_gate_up[layer].astype(BF16),
                    ffn_w_down[layer].astype(BF16), final_norm_g,
                    tm=TM_FFN, th=TH_FFN, final_norm=(layer == depth - 1))
    return x
```

```python
import functools

import jax
import jax.numpy as jnp
from jax import lax
from jax.experimental import pallas as pl
from jax.experimental.pallas import tpu as pltpu

F32 = jnp.float32
BF16 = jnp.bfloat16

EPS = 1e-6
ROPE_THETA = 10000.0
LANES = 128
SUBLANES = 8
BF16_ROWS = 16
VMEM_LIMIT_BYTES = 56 * 1024 * 1024
NEG_BIG = -1e30
LOG2_E = 1.4426950408889634

SC_WIDTH = 512
CONV_K = 3
POOL_WINDOWS = (2, 4, 8, 16)
POOL_GROUP = 128
HALO = 16
ML_HEADS = 4
ML_HEAD_DIM = 128
ML_WIDTH = ML_HEADS * ML_HEAD_DIM
MLA_HEADS = 8
QK_NOPE = 64
QK_ROPE = 32
V_DIM = 64
Q_LORA = 384
KV_LORA = 256
MLA_WIDTH = MLA_HEADS * V_DIM
XA_HEADS = 4
XA_HEAD_DIM = 128
XA_WIDTH = XA_HEADS * XA_HEAD_DIM


def _params(*sem):
    return pltpu.CompilerParams(dimension_semantics=sem, vmem_limit_bytes=VMEM_LIMIT_BYTES)


def _const_spec(shape):
    nd = len(shape)
    return pl.BlockSpec(shape, lambda *_: (0,) * nd, pipeline_mode=pl.Buffered(1))


def _rms(x, g):
    ms = jnp.mean(x * x, axis=-1, keepdims=True)
    return x * lax.rsqrt(ms + EPS) * g


def _dot(a, b):
    return jnp.dot(a, b, preferred_element_type=F32)


def _dot_nt(a, b):
    return lax.dot_general(a, b, (((1,), (1,)), ((), ())), preferred_element_type=F32)


def _dot_tn(a, b):
    return lax.dot_general(a, b, (((0,), (0,)), ((), ())), preferred_element_type=F32)


def _even_kernel(x_ref, g_ref, win_ref, convw_ref, poolw_ref, pscale_ref, wout_ref, o_ref,
                 ubuf, xbbuf, *, ts):
    si = pl.program_id(1)

    @pl.when(si == 0)
    def _():
        ubuf[0:HALO, :] = jnp.zeros((HALO, SC_WIDTH), F32)
        xbbuf[0:HALO, :] = jnp.zeros((HALO, SC_WIDTH), F32)

    x = x_ref[0]
    h = _rms(x, g_ref[...]).astype(BF16)
    z = _dot(h, win_ref[...])
    g_b = z[:, 0:SC_WIDTH]
    u = z[:, SC_WIDTH:2 * SC_WIDTH] * z[:, 2 * SC_WIDTH:3 * SC_WIDTH]
    xb = z[:, 3 * SC_WIDTH:4 * SC_WIDTH]
    ubuf[HALO:HALO + ts, :] = u
    xbbuf[HALO:HALO + ts, :] = xb

    cw = convw_ref[...]
    conv = cw[2:3, :] * u
    for k in range(CONV_K - 1):
        back = CONV_K - 1 - k
        conv = conv + cw[k:k + 1, :] * ubuf[HALO - back:HALO - back + ts, :]
    ya = g_b * conv

    t_idx = si * ts + lax.broadcasted_iota(jnp.int32, (ts, POOL_GROUP), 0)
    ys = [ya.astype(BF16)]
    for gi, w in enumerate(POOL_WINDOWS):
        lo = gi * POOL_GROUP
        xg = xb[:, lo:lo + POOL_GROUP]
        acc = xg
        for j in range(1, w):
            acc = acc + xbbuf[HALO - j:HALO - j + ts, lo:lo + POOL_GROUP]
        cnt = jnp.minimum(t_idx + 1, w).astype(F32)
        d = (acc / cnt - xg).astype(BF16)
        yg = _dot(d, poolw_ref[gi]) * pscale_ref[:, lo:lo + POOL_GROUP]
        ys.append(yg.astype(BF16))

    ubuf[0:HALO, :] = ubuf[ts:ts + HALO, :]
    xbbuf[0:HALO, :] = xbbuf[ts:ts + HALO, :]

    mix = jnp.concatenate(ys, axis=-1)
    o_ref[0] = x + _dot(mix, wout_ref[...])


def _even_mixer(x, g, w_in, conv_w, pool_w, pool_scale, w_out, *, ts):
    B, S, D = x.shape
    ts = min(ts, S)
    kern = functools.partial(_even_kernel, ts=ts)
    return pl.pallas_call(
        kern,
        grid=(B, S // ts),
        in_specs=[
            pl.BlockSpec((1, ts, D), lambda b, s: (b, s, 0)),
            _const_spec((1, D)),
            _const_spec(w_in.shape),
            _const_spec(conv_w.shape),
            _const_spec(pool_w.shape),
            _const_spec((1, SC_WIDTH)),
            _const_spec(w_out.shape),
        ],
        out_specs=pl.BlockSpec((1, ts, D), lambda b, s: (b, s, 0)),
        out_shape=jax.ShapeDtypeStruct(x.shape, F32),
        scratch_shapes=[pltpu.VMEM((HALO + ts, SC_WIDTH), F32),
                        pltpu.VMEM((HALO + ts, SC_WIDTH), F32)],
        compiler_params=_params("arbitrary", "arbitrary"),
        name="even_mixer",
    )(x, g.reshape(1, D), w_in, conv_w, pool_w, pool_scale.reshape(1, SC_WIDTH), w_out)


def _memkv_kernel(mem_ref, g_ref, wkv_ref, kv_ref):
    h = _rms(mem_ref[0], g_ref[...]).astype(BF16)
    kv_ref[0] = _dot(h, wkv_ref[...]).astype(BF16)


def _mem_kv(mem, g, wkv):
    B, M, D = mem.shape
    N = wkv.shape[1]
    return pl.pallas_call(
        _memkv_kernel,
        grid=(B,),
        in_specs=[pl.BlockSpec((1, M, D), lambda b: (b, 0, 0)),
                  _const_spec((1, D)), _const_spec(wkv.shape)],
        out_specs=pl.BlockSpec((1, M, N), lambda b: (b, 0, 0)),
        out_shape=jax.ShapeDtypeStruct((B, M, N), BF16),
        compiler_params=_params("arbitrary"),
        name="mem_kv",
    )(mem, g.reshape(1, D), wkv)


def _xattn_kernel(x_ref, g_ref, wq_ref, kv_ref, wo_ref, o_ref):
    x = x_ref[0]
    h = _rms(x, g_ref[...]).astype(BF16)
    q = (_dot(h, wq_ref[...]) * (XA_HEAD_DIM ** -0.5)).astype(BF16)
    kv = kv_ref[0]
    outs = []
    for hd in range(XA_HEADS):
        lo = hd * XA_HEAD_DIM
        s = _dot_nt(q[:, lo:lo + XA_HEAD_DIM], kv[:, lo:lo + XA_HEAD_DIM])
        m = jnp.max(s, axis=-1, keepdims=True)
        e = jnp.exp(s - m)
        l = jnp.sum(e, axis=-1, keepdims=True)
        p = (e / l).astype(BF16)
        outs.append(_dot(p, kv[:, XA_WIDTH + lo:XA_WIDTH + lo + XA_HEAD_DIM]).astype(BF16))
    a = jnp.concatenate(outs, axis=-1)
    o_ref[0] = x + _dot(a, wo_ref[...])


def _xattn(x, g, wq, kv, wo, *, ts):
    B, S, D = x.shape
    ts = min(ts, S)
    M = kv.shape[1]
    return pl.pallas_call(
        _xattn_kernel,
        grid=(B, S // ts),
        in_specs=[
            pl.BlockSpec((1, ts, D), lambda b, s: (b, s, 0)),
            _const_spec((1, D)),
            _const_spec(wq.shape),
            pl.BlockSpec((1, M, kv.shape[2]), lambda b, s: (b, 0, 0)),
            _const_spec(wo.shape),
        ],
        out_specs=pl.BlockSpec((1, ts, D), lambda b, s: (b, s, 0)),
        out_shape=jax.ShapeDtypeStruct(x.shape, F32),
        compiler_params=_params("arbitrary", "arbitrary"),
        name="xattn",
    )(x, g.reshape(1, D), wq, kv, wo)


def _swiglu_kernel(x_ref, g_ref, wgu_ref, wd_ref, fg_ref, o_ref, *, hidden, th, final_norm):
    x = x_ref[...]
    h = _rms(x, g_ref[...]).astype(BF16)
    acc = x
    for j in range(hidden // th):
        gate = _dot(h, wgu_ref[:, j * th:(j + 1) * th])
        up = _dot(h, wgu_ref[:, hidden + j * th:hidden + (j + 1) * th])
        act = (gate * jax.nn.sigmoid(gate) * up).astype(BF16)
        acc = acc + _dot(act, wd_ref[j * th:(j + 1) * th, :])
    if final_norm:
        acc = _rms(acc, fg_ref[...])
    o_ref[...] = acc


def _swiglu(x, g, w_gu, w_d, final_g, *, tm, th, final_norm):
    B, S, D = x.shape
    T = B * S
    tm = min(tm, T)
    hidden = w_d.shape[0]
    kern = functools.partial(_swiglu_kernel, hidden=hidden, th=th, final_norm=final_norm)
    out = pl.pallas_call(
        kern,
        grid=(T // tm,),
        in_specs=[
            pl.BlockSpec((tm, D), lambda i: (i, 0)),
            _const_spec((1, D)),
            _const_spec(w_gu.shape),
            _const_spec(w_d.shape),
            _const_spec((1, D)),
        ],
        out_specs=pl.BlockSpec((tm, D), lambda i: (i, 0)),
        out_shape=jax.ShapeDtypeStruct((T, D), F32),
        compiler_params=_params("arbitrary"),
        name="swiglu",
    )(x.reshape(T, D), g.reshape(1, D), w_gu, w_d, final_g.reshape(1, D))
    return out.reshape(B, S, D)


def _odd_proj_kernel(x_ref, pos_ref, g_ref, wmain_ref, wgt_ref, gbias_ref, qg_ref, kvg_ref,
                     wuq_ref, wuqs_ref, wuk_ref, wuvt_ref, invf_ref,
                     qm_ref, km_ref, vm_ref, om_ref, gates_ref, qp_ref, kp_ref, vt_ref):
    x = x_ref[0]
    h = _rms(x, g_ref[...]).astype(BF16)
    z = _dot(h, wmain_ref[...])
    qm_ref[0] = z[:, 0:ML_WIDTH].astype(BF16)
    km_ref[0] = (z[:, ML_WIDTH:2 * ML_WIDTH] * (ML_HEAD_DIM ** -0.5)).astype(BF16)
    vm_ref[0] = z[:, 2 * ML_WIDTH:3 * ML_WIDTH].astype(BF16)
    om_ref[0] = z[:, 3 * ML_WIDTH:4 * ML_WIDTH].astype(BF16)
    gates_ref[0] = _dot_nt(wgt_ref[...], h) + gbias_ref[...]

    c0 = 4 * ML_WIDTH
    c_q = z[:, c0:c0 + Q_LORA]
    c_kv = z[:, c0 + Q_LORA:c0 + Q_LORA + KV_LORA]
    kr = z[:, c0 + Q_LORA + KV_LORA:c0 + Q_LORA + KV_LORA + LANES]
    kr_sw = z[:, c0 + Q_LORA + KV_LORA + LANES:c0 + Q_LORA + KV_LORA + 2 * LANES]

    ang = pos_ref[0].astype(F32) * invf_ref[...]
    lane = lax.broadcasted_iota(jnp.int32, ang.shape, 1)
    in_rope = (lane >= QK_NOPE) & (lane < QK_NOPE + QK_ROPE)
    first_half = lane < QK_NOPE + QK_ROPE // 2
    cos_t = jnp.where(in_rope, jnp.cos(ang), 1.0)
    sin_v = jnp.sin(ang)
    sin_t = jnp.where(in_rope, jnp.where(first_half, -sin_v, sin_v), 0.0)

    cqn = _rms(c_q, qg_ref[...]).astype(BF16)
    qa = _dot(cqn, wuq_ref[...])
    qb = _dot(cqn, wuqs_ref[...])
    ckn = _rms(c_kv, kvg_ref[...]).astype(BF16)
    kn = _dot(ckn, wuk_ref[...])
    vt_ref[0] = _dot_nt(wuvt_ref[...], ckn).astype(BF16)
    k_rope = kr * cos_t + kr_sw * sin_t
    scale = (QK_NOPE + QK_ROPE) ** -0.5 * LOG2_E
    for hd in range(MLA_HEADS):
        lo = hd * LANES
        qh = qa[:, lo:lo + LANES] * cos_t + qb[:, lo:lo + LANES] * sin_t
        qp_ref[0, :, lo:lo + LANES] = (qh * scale).astype(BF16)
        kp_ref[0, :, lo:lo + LANES] = (kn[:, lo:lo + LANES] + k_rope).astype(BF16)


def _odd_proj(x, pos3, g, w_main, w_gt, gbias, q_g, kv_g, w_uq_p, w_uq_s, w_uk_p, w_uvt, invf, *, ts):
    B, S, D = x.shape
    ts = min(ts, S)
    tok = lambda n: pl.BlockSpec((1, ts, n), lambda b, s: (b, s, 0))
    out_shapes = (
        jax.ShapeDtypeStruct((B, S, ML_WIDTH), BF16),
        jax.ShapeDtypeStruct((B, S, ML_WIDTH), BF16),
        jax.ShapeDtypeStruct((B, S, ML_WIDTH), BF16),
        jax.ShapeDtypeStruct((B, S, ML_WIDTH), BF16),
        jax.ShapeDtypeStruct((B, 2 * ML_HEADS, S), F32),
        jax.ShapeDtypeStruct((B, S, MLA_HEADS * LANES), BF16),
        jax.ShapeDtypeStruct((B, S, MLA_HEADS * LANES), BF16),
        jax.ShapeDtypeStruct((B, MLA_WIDTH, S), BF16),
    )
    return pl.pallas_call(
        _odd_proj_kernel,
        grid=(B, S // ts),
        in_specs=[
            tok(D),
            pl.BlockSpec((1, ts, 1), lambda b, s: (b, s, 0)),
            _const_spec((1, D)),
            _const_spec(w_main.shape),
            _const_spec(w_gt.shape),
            _const_spec(gbias.shape),
            _const_spec(q_g.shape),
            _const_spec(kv_g.shape),
            _const_spec(w_uq_p.shape),
            _const_spec(w_uq_s.shape),
            _const_spec(w_uk_p.shape),
            _const_spec(w_uvt.shape),
            _const_spec(invf.shape),
        ],
        out_specs=(tok(ML_WIDTH), tok(ML_WIDTH), tok(ML_WIDTH), tok(ML_WIDTH),
                   pl.BlockSpec((1, 2 * ML_HEADS, ts), lambda b, s: (b, 0, s)),
                   tok(MLA_HEADS * LANES), tok(MLA_HEADS * LANES),
                   pl.BlockSpec((1, MLA_WIDTH, ts), lambda b, s: (b, 0, s))),
        out_shape=out_shapes,
        compiler_params=_params("arbitrary", "arbitrary"),
        name="odd_proj",
    )(x, pos3, g.reshape(1, D), w_main, w_gt, gbias, q_g, kv_g, w_uq_p, w_uq_s, w_uk_p, w_uvt, invf)


def _mlstm_kernel(q_ref, k_ref, v_ref, o_ref, gates_ref, ng_ref, h_ref, c_scr, n_scr, m_scr, *, L):
    ci = pl.program_id(1)

    @pl.when(ci == 0)
    def _():
        c_scr[...] = jnp.zeros(c_scr.shape, F32)
        n_scr[...] = jnp.zeros(n_scr.shape, F32)
        m_scr[...] = jnp.zeros(m_scr.shape, F32)

    row = lax.broadcasted_iota(jnp.int32, (L, L), 0)
    col = lax.broadcasted_iota(jnp.int32, (L, L), 1)
    causal = col <= row
    eye = col == row
    gates = gates_ref[0]

    for hd in range(ML_HEADS):
        lo = hd * ML_HEAD_DIM
        q = q_ref[0, :, lo:lo + ML_HEAD_DIM]
        k = k_ref[0, :, lo:lo + ML_HEAD_DIM]
        v = v_ref[0, :, lo:lo + ML_HEAD_DIM]
        ig_row = gates[hd:hd + 1, :]
        f_row = gates[ML_HEADS + hd:ML_HEADS + hd + 1, :]
        lf_row = jnp.minimum(f_row, 0.0) - jnp.log(1.0 + jnp.exp(-jnp.abs(f_row)))

        b_col = jnp.sum(jnp.where(causal, lf_row, 0.0), axis=1, keepdims=True)
        b_row = jnp.sum(jnp.where(eye, b_col, 0.0), axis=0, keepdims=True)
        ig_col = jnp.sum(jnp.where(eye, ig_row, 0.0), axis=1, keepdims=True)
        gtot = jnp.sum(lf_row, axis=1, keepdims=True)

        m_prev = m_scr[hd][:, 0:1]
        n_prev = n_scr[hd]
        c_prev = c_scr[hd]

        dmat = jnp.where(causal, b_col - b_row + ig_row, NEG_BIG)
        inter = b_col + m_prev
        m_t = jnp.maximum(inter, jnp.max(dmat, axis=1, keepdims=True))
        w_intra = jnp.exp(dmat - m_t)
        w_inter = jnp.exp(inter - m_t)
        sw = w_intra * _dot_nt(q, k)
        num = w_inter * _dot(q, c_prev.astype(BF16)) + _dot(sw.astype(BF16), v)
        qf = q.astype(F32)
        den_raw = (w_inter * jnp.sum(qf * n_prev, axis=1, keepdims=True)
                   + jnp.sum(sw, axis=1, keepdims=True))
        den = jnp.maximum(jnp.abs(den_raw), jnp.exp(-m_t))
        hc = num / den

        og = jax.nn.sigmoid(o_ref[0, :, lo:lo + ML_HEAD_DIM].astype(F32))
        hg = og * hc
        h_ref[0, :, lo:lo + ML_HEAD_DIM] = _rms(hg, ng_ref[:, lo:lo + ML_HEAD_DIM]).astype(BF16)

        a_row = gtot - b_row + ig_row
        m_loc = jnp.max(a_row, axis=1, keepdims=True)
        w_loc = jnp.exp(gtot - b_col + ig_col - m_loc)
        m_new = jnp.maximum(gtot + m_prev, m_loc)
        s_old = jnp.exp(gtot + m_prev - m_new)
        s_new = jnp.exp(m_loc - m_new)
        kw = k.astype(F32) * w_loc
        c_scr[hd] = s_old * c_prev + s_new * _dot_tn(kw.astype(BF16), v)
        n_scr[hd] = s_old * n_prev + s_new * jnp.sum(kw, axis=0, keepdims=True)
        m_scr[hd] = jnp.broadcast_to(m_new, (1, LANES))


def _mlstm(q, k, v, o, gates_t, norm_g, *, L):
    B, S, W = q.shape
    L = min(L, S)
    kern = functools.partial(_mlstm_kernel, L=L)
    tok = pl.BlockSpec((1, L, W), lambda b, c: (b, c, 0))
    return pl.pallas_call(
        kern,
        grid=(B, S // L),
        in_specs=[tok, tok, tok, tok,
                  pl.BlockSpec((1, 2 * ML_HEADS, L), lambda b, c: (b, 0, c)),
                  _const_spec((1, W))],
        out_specs=tok,
        out_shape=jax.ShapeDtypeStruct((B, S, W), BF16),
        scratch_shapes=[pltpu.VMEM((ML_HEADS, ML_HEAD_DIM, ML_HEAD_DIM), F32),
                        pltpu.VMEM((ML_HEADS, 1, ML_HEAD_DIM), F32),
                        pltpu.VMEM((ML_HEADS, 1, LANES), F32)],
        compiler_params=_params("arbitrary", "arbitrary"),
        name="mlstm",
    )(q, k, v, o, gates_t, norm_g.reshape(1, W))


def _mla_kernel(q_ref, k_ref, vt_ref, o_ref, s_scr, *, tq):
    qi = pl.program_id(2)
    heads = 2
    qs = [q_ref[0, :, hh * LANES:(hh + 1) * LANES] for hh in range(heads)]

    def scores(j, slot):
        start = pl.multiple_of(j * tq, tq)
        for hh in range(heads):
            kb = k_ref[0, pl.ds(start, tq), hh * LANES:(hh + 1) * LANES]
            s_scr[slot, hh] = _dot_nt(kb, qs[hh])

    def block(j, slot, carry, masked):
        start = pl.multiple_of(j * tq, tq)
        new = []
        for hh in range(heads):
            m, acc = carry[hh]
            vtb = jnp.concatenate([vt_ref[0, hh * V_DIM:(hh + 1) * V_DIM, pl.ds(start, tq)],
                                   ones_rows], axis=0)
            s = s_scr[slot, hh]
            if masked:
                kk = lax.broadcasted_iota(jnp.int32, s.shape, 0)
                qq = lax.broadcasted_iota(jnp.int32, s.shape, 1)
                s = jnp.where(kk <= qq, s, NEG_BIG)
            m_new = jnp.maximum(m, jnp.max(s, axis=0, keepdims=True))
            alpha = jnp.exp2(m - m_new)
            p = jnp.exp2(s - m_new).astype(BF16)
            new.append((m_new, alpha * acc + _dot(vtb, p)))
        return tuple(new)

    ones_rows = jnp.ones((BF16_ROWS, tq), BF16)
    init = tuple((jnp.full((1, tq), NEG_BIG, F32), jnp.zeros((V_DIM + BF16_ROWS, tq), F32))
                 for _ in range(heads))
    scores(0, 0)

    def body(i, c):
        scores(2 * i + 1, 1)
        c = block(2 * i, 0, c, False)
        scores(2 * i + 2, 0)
        return block(2 * i + 1, 1, c, False)

    carry = lax.fori_loop(0, qi // 2, body, init)

    def finish(c):
        ot = jnp.concatenate([acc[:V_DIM] / acc[V_DIM:V_DIM + 1] for _, acc in c], axis=0)
        o_ref[0] = ot.T.astype(BF16)

    @pl.when(qi % 2 == 0)
    def _():
        finish(block(qi, 0, carry, True))

    @pl.when(qi % 2 == 1)
    def _():
        scores(qi, 1)
        finish(block(qi, 1, block(qi - 1, 0, carry, False), True))


def _mla_attention(qp, kp, vt, *, tq):
    B, S, _ = qp.shape
    tq = min(tq, S)
    pairs = MLA_HEADS // 2
    kern = functools.partial(_mla_kernel, tq=tq)
    return pl.pallas_call(
        kern,
        grid=(B, pairs, S // tq),
        in_specs=[
            pl.BlockSpec((1, tq, 2 * LANES), lambda b, p, i: (b, i, p)),
            pl.BlockSpec((1, S, 2 * LANES), lambda b, p, i: (b, 0, p)),
            pl.BlockSpec((1, 2 * V_DIM, S), lambda b, p, i: (b, p, 0)),
        ],
        out_specs=pl.BlockSpec((1, tq, LANES), lambda b, p, i: (b, i, p)),
        out_shape=jax.ShapeDtypeStruct((B, S, MLA_WIDTH), BF16),
        scratch_shapes=[pltpu.VMEM((2, 2, tq, tq), F32)],
        compiler_params=_params("arbitrary", "arbitrary", "arbitrary"),
        name="mla_attention",
    )(qp, kp, vt)


def _odd_out_kernel(x_ref, hm_ref, ha_ref, wout_ref, o_ref):
    mix = jnp.concatenate([hm_ref[...], ha_ref[...]], axis=-1)
    o_ref[...] = x_ref[...] + _dot(mix, wout_ref[...])


def _odd_out(x, hm, ha, w_out, *, tm):
    B, S, D = x.shape
    T = B * S
    tm = min(tm, T)
    out = pl.pallas_call(
        _odd_out_kernel,
        grid=(T // tm,),
        in_specs=[pl.BlockSpec((tm, D), lambda i: (i, 0)),
                  pl.BlockSpec((tm, ML_WIDTH), lambda i: (i, 0)),
                  pl.BlockSpec((tm, MLA_WIDTH), lambda i: (i, 0)),
                  _const_spec(w_out.shape)],
        out_specs=pl.BlockSpec((tm, D), lambda i: (i, 0)),
        out_shape=jax.ShapeDtypeStruct((T, D), F32),
        compiler_params=_params("arbitrary"),
        name="odd_out",
    )(x.reshape(T, D), hm.reshape(T, ML_WIDTH), ha.reshape(T, MLA_WIDTH), w_out)
    return out.reshape(B, S, D)


def _pack_odd_weights(w_in, w_uq, w_ukv):
    D = w_in.shape[0]
    c = 4 * ML_WIDTH
    w_g = w_in[:, c:c + 2 * ML_HEADS]
    c += 2 * ML_HEADS
    w_cq = w_in[:, c:c + Q_LORA]
    c += Q_LORA
    w_ckv = w_in[:, c:c + KV_LORA]
    c += KV_LORA
    w_kr = w_in[:, c:c + QK_ROPE]
    half = QK_ROPE // 2
    w_kr_sw = jnp.concatenate([w_kr[:, half:], w_kr[:, :half]], axis=1)
    zl = jnp.zeros((D, QK_NOPE), w_in.dtype)
    zr = jnp.zeros((D, LANES - QK_NOPE - QK_ROPE), w_in.dtype)
    w_main = jnp.concatenate([w_in[:, :4 * ML_WIDTH], w_cq, w_ckv,
                              zl, w_kr, zr, zl, w_kr_sw, zr], axis=1).astype(BF16)
    w_gt = w_g.T.astype(BF16)

    uq = w_uq.reshape(Q_LORA, MLA_HEADS, QK_NOPE + QK_ROPE)
    uq_n, uq_r = uq[..., :QK_NOPE], uq[..., QK_NOPE:]
    uq_r_sw = jnp.concatenate([uq_r[..., half:], uq_r[..., :half]], axis=-1)
    zpad = jnp.zeros((Q_LORA, MLA_HEADS, LANES - QK_NOPE - QK_ROPE), w_uq.dtype)
    w_uq_p = jnp.concatenate([uq_n, uq_r, zpad], axis=-1).reshape(Q_LORA, MLA_HEADS * LANES)
    w_uq_s = jnp.concatenate([jnp.zeros_like(uq_n), uq_r_sw, zpad], axis=-1)
    w_uq_s = w_uq_s.reshape(Q_LORA, MLA_HEADS * LANES)

    ukv = w_ukv.reshape(KV_LORA, MLA_HEADS, QK_NOPE + V_DIM)
    uk, uv = ukv[..., :QK_NOPE], ukv[..., QK_NOPE:]
    w_uk_p = jnp.concatenate([uk, jnp.zeros((KV_LORA, MLA_HEADS, LANES - QK_NOPE), w_ukv.dtype)],
                             axis=-1).reshape(KV_LORA, MLA_HEADS * LANES)
    w_uvt = uv.reshape(KV_LORA, MLA_WIDTH).T
    return (w_main, w_gt, w_uq_p.astype(BF16), w_uq_s.astype(BF16), w_uk_p.astype(BF16),
            w_uvt.astype(BF16))


def _rope_inv_freq():
    half = QK_ROPE // 2
    inv = ROPE_THETA ** (-jnp.arange(half, dtype=F32) / half)
    blk = jnp.concatenate([jnp.zeros((QK_NOPE,), F32), inv, inv,
                           jnp.zeros((LANES - QK_NOPE - QK_ROPE,), F32)])
    return blk.reshape(1, LANES)


TS_EVEN = 512
TS_XATTN = 512
TM_FFN = 512
TH_FFN = 256
TS_ODD = 512
L_MLSTM = 256
TQ_MLA = 512
TM_OUT = 1024


def kernel(x, mem, positions, norm_mix_g, norm_xattn_g, mem_norm_g, xattn_wq, xattn_wkv, xattn_wo,
           norm_ffn_g, ffn_w_gate_up, ffn_w_down, ev_w_in, ev_conv_w, ev_pool_w, ev_pool_scale,
           ev_w_out, od_w_in, od_gate_bias, od_ml_norm_g, od_q_norm_g, od_kv_norm_g, od_w_uq,
           od_w_ukv, od_w_out, final_norm_g):
    depth = norm_mix_g.shape[0]
    B, S, D = x.shape
    pos3 = positions.reshape(B, S, 1)
    invf = _rope_inv_freq()
    for layer in range(depth):
        if layer % 2 == 0:
            e = layer // 2
            x = _even_mixer(x, norm_mix_g[layer], ev_w_in[e].astype(BF16), ev_conv_w[e],
                            ev_pool_w[e].astype(BF16), ev_pool_scale[e], ev_w_out[e].astype(BF16),
                            ts=TS_EVEN)
        else:
            o = layer // 2
            w_main, w_gt, w_uq_p, w_uq_s, w_uk_p, w_uvt = _pack_odd_weights(
                od_w_in[o], od_w_uq[o], od_w_ukv[o])
            qm, km, vm, om, gates_t, qp, kp, vt = _odd_proj(
                x, pos3, norm_mix_g[layer], w_main, w_gt, od_gate_bias[o].reshape(2 * ML_HEADS, 1),
                od_q_norm_g[o].reshape(1, Q_LORA), od_kv_norm_g[o].reshape(1, KV_LORA),
                w_uq_p, w_uq_s, w_uk_p, w_uvt, invf, ts=TS_ODD)
            hm = _mlstm(qm, km, vm, om, gates_t, od_ml_norm_g[o], L=L_MLSTM)
            ha = _mla_attention(qp, kp, vt, tq=TQ_MLA)
            x = _odd_out(x, hm, ha, od_w_out[o].astype(BF16), tm=TM_OUT)
        kv = _mem_kv(mem, mem_norm_g[layer], xattn_wkv[layer].astype(BF16))
        x = _xattn(x, norm_xattn_g[layer], xattn_wq[layer].astype(BF16), kv,
                   xattn_wo[layer].astype(BF16), ts=TS_XATTN)
        x = _swiglu(x, norm_ffn_g[layer], ffn_w_gate_up[layer].astype(BF16),
                    ffn_w_down[layer].astype(BF16), final_norm_g,
                    tm=TM_FFN, th=TH_FFN, final_norm=(layer == depth - 1))
    return x
```

```python
import functools

import jax
import jax.numpy as jnp
from jax import lax
from jax.experimental import pallas as pl
from jax.experimental.pallas import tpu as pltpu

F32 = jnp.float32
BF16 = jnp.bfloat16

EPS = 1e-6
ROPE_THETA = 10000.0
LANES = 128
SUBLANES = 8
BF16_ROWS = 16
VMEM_LIMIT_BYTES = 56 * 1024 * 1024
NEG_BIG = -1e30
LOG2_E = 1.4426950408889634

SC_WIDTH = 512
CONV_K = 3
POOL_WINDOWS = (2, 4, 8, 16)
POOL_GROUP = 128
HALO = 16
ML_HEADS = 4
ML_HEAD_DIM = 128
ML_WIDTH = ML_HEADS * ML_HEAD_DIM
MLA_HEADS = 8
QK_NOPE = 64
QK_ROPE = 32
V_DIM = 64
Q_LORA = 384
KV_LORA = 256
MLA_WIDTH = MLA_HEADS * V_DIM
XA_HEADS = 4
XA_HEAD_DIM = 128
XA_WIDTH = XA_HEADS * XA_HEAD_DIM


def _params(*sem):
    return pltpu.CompilerParams(dimension_semantics=sem, vmem_limit_bytes=VMEM_LIMIT_BYTES)


def _const_spec(shape):
    nd = len(shape)
    return pl.BlockSpec(shape, lambda *_: (0,) * nd, pipeline_mode=pl.Buffered(1))


def _rms(x, g):
    ms = jnp.mean(x * x, axis=-1, keepdims=True)
    return x * lax.rsqrt(ms + EPS) * g


def _dot(a, b):
    return jnp.dot(a, b, preferred_element_type=F32)


def _dot_nt(a, b):
    return lax.dot_general(a, b, (((1,), (1,)), ((), ())), preferred_element_type=F32)


def _even_kernel(x_ref, g_ref, win_ref, convw_ref, poolw_ref, pscale_ref, wout_ref,
                 xg_ref, wq_ref, kv_ref, wo_ref, o_ref, ubuf, xbbuf, *, ts):
    si = pl.program_id(1)

    @pl.when(si == 0)
    def _():
        ubuf[0:HALO, :] = jnp.zeros((HALO, SC_WIDTH), F32)
        xbbuf[0:HALO, :] = jnp.zeros((HALO, SC_WIDTH), F32)

    x = x_ref[0]
    h = _rms(x, g_ref[...]).astype(BF16)
    z = _dot(h, win_ref[...])
    g_b = z[:, 0:SC_WIDTH]
    u = z[:, SC_WIDTH:2 * SC_WIDTH] * z[:, 2 * SC_WIDTH:3 * SC_WIDTH]
    xb = z[:, 3 * SC_WIDTH:4 * SC_WIDTH]
    ubuf[HALO:HALO + ts, :] = u
    xbbuf[HALO:HALO + ts, :] = xb

    cw = convw_ref[...]
    conv = cw[2:3, :] * u
    for k in range(CONV_K - 1):
        back = CONV_K - 1 - k
        conv = conv + cw[k:k + 1, :] * ubuf[HALO - back:HALO - back + ts, :]
    ya = g_b * conv

    t_idx = si * ts + lax.broadcasted_iota(jnp.int32, (ts, POOL_GROUP), 0)
    ys = [ya.astype(BF16)]
    for gi, w in enumerate(POOL_WINDOWS):
        lo = gi * POOL_GROUP
        xg = xb[:, lo:lo + POOL_GROUP]
        acc = xg
        for j in range(1, w):
            acc = acc + xbbuf[HALO - j:HALO - j + ts, lo:lo + POOL_GROUP]
        cnt = jnp.minimum(t_idx + 1, w).astype(F32)
        d = (acc / cnt - xg).astype(BF16)
        yg = _dot(d, poolw_ref[gi]) * pscale_ref[:, lo:lo + POOL_GROUP]
        ys.append(yg.astype(BF16))

    ubuf[0:HALO, :] = ubuf[ts:ts + HALO, :]
    xbbuf[0:HALO, :] = xbbuf[ts:ts + HALO, :]

    mix = jnp.concatenate(ys, axis=-1)
    x1 = x + _dot(mix, wout_ref[...])
    o_ref[0] = _xattn_body(x1, xg_ref, wq_ref, kv_ref, wo_ref)


def _even_mixer(x, g, w_in, conv_w, pool_w, pool_scale, w_out, xg, wq, kv, wo, *, ts):
    B, S, D = x.shape
    ts = min(ts, S)
    kern = functools.partial(_even_kernel, ts=ts)
    return pl.pallas_call(
        kern,
        grid=(B, S // ts),
        in_specs=[
            pl.BlockSpec((1, ts, D), lambda b, s: (b, s, 0)),
            _const_spec((1, D)),
            _const_spec(w_in.shape),
            _const_spec(conv_w.shape),
            _const_spec(pool_w.shape),
            _const_spec((1, SC_WIDTH)),
            _const_spec(w_out.shape),
        ] + _xattn_specs(D, wq, kv, wo),
        out_specs=pl.BlockSpec((1, ts, D), lambda b, s: (b, s, 0)),
        out_shape=jax.ShapeDtypeStruct(x.shape, F32),
        scratch_shapes=[pltpu.VMEM((HALO + ts, SC_WIDTH), F32),
                        pltpu.VMEM((HALO + ts, SC_WIDTH), F32)],
        compiler_params=_params("arbitrary", "arbitrary"),
        name="even_mixer",
    )(x, g.reshape(1, D), w_in, conv_w, pool_w, pool_scale.reshape(1, SC_WIDTH), w_out,
      xg.reshape(1, D), wq, kv, wo)


def _memkv_kernel(mem_ref, g_ref, wkv_ref, kv_ref):
    h = _rms(mem_ref[0], g_ref[...]).astype(BF16)
    kv_ref[0] = _dot(h, wkv_ref[...]).astype(BF16)


def _mem_kv(mem, g, wkv):
    B, M, D = mem.shape
    N = wkv.shape[1]
    return pl.pallas_call(
        _memkv_kernel,
        grid=(B,),
        in_specs=[pl.BlockSpec((1, M, D), lambda b: (b, 0, 0)),
                  _const_spec((1, D)), _const_spec(wkv.shape)],
        out_specs=pl.BlockSpec((1, M, N), lambda b: (b, 0, 0)),
        out_shape=jax.ShapeDtypeStruct((B, M, N), BF16),
        compiler_params=_params("arbitrary"),
        name="mem_kv",
    )(mem, g.reshape(1, D), wkv)


def _xattn_body(x, g_ref, wq_ref, kv_ref, wo_ref):
    h = _rms(x, g_ref[...]).astype(BF16)
    q = (_dot(h, wq_ref[...]) * (XA_HEAD_DIM ** -0.5)).astype(BF16)
    kv = kv_ref[0]
    outs = []
    for hd in range(XA_HEADS):
        lo = hd * XA_HEAD_DIM
        s = _dot_nt(q[:, lo:lo + XA_HEAD_DIM], kv[:, lo:lo + XA_HEAD_DIM])
        m = jnp.max(s, axis=-1, keepdims=True)
        e = jnp.exp(s - m)
        l = jnp.sum(e, axis=-1, keepdims=True)
        p = (e / l).astype(BF16)
        outs.append(_dot(p, kv[:, XA_WIDTH + lo:XA_WIDTH + lo + XA_HEAD_DIM]).astype(BF16))
    a = jnp.concatenate(outs, axis=-1)
    return x + _dot(a, wo_ref[...])


def _xattn_specs(D, wq, kv, wo):
    M = kv.shape[1]
    return [_const_spec((1, D)), _const_spec(wq.shape),
            pl.BlockSpec((1, M, kv.shape[2]), lambda b, s: (b, 0, 0)), _const_spec(wo.shape)]


def _swiglu_kernel(x_ref, g_ref, wgu_ref, wd_ref, fg_ref, o_ref, *, hidden, th, final_norm):
    x = x_ref[...]
    h = _rms(x, g_ref[...]).astype(BF16)
    acc = x
    for j in range(hidden // th):
        gate = _dot(h, wgu_ref[:, j * th:(j + 1) * th])
        up = _dot(h, wgu_ref[:, hidden + j * th:hidden + (j + 1) * th])
        act = (gate * jax.nn.sigmoid(gate) * up).astype(BF16)
        acc = acc + _dot(act, wd_ref[j * th:(j + 1) * th, :])
    if final_norm:
        acc = _rms(acc, fg_ref[...])
    o_ref[...] = acc


def _swiglu(x, g, w_gu, w_d, final_g, *, tm, th, final_norm):
    B, S, D = x.shape
    T = B * S
    tm = min(tm, T)
    hidden = w_d.shape[0]
    kern = functools.partial(_swiglu_kernel, hidden=hidden, th=th, final_norm=final_norm)
    out = pl.pallas_call(
        kern,
        grid=(T // tm,),
        in_specs=[
            pl.BlockSpec((tm, D), lambda i: (i, 0)),
            _const_spec((1, D)),
            _const_spec(w_gu.shape),
            _const_spec(w_d.shape),
            _const_spec((1, D)),
        ],
        out_specs=pl.BlockSpec((tm, D), lambda i: (i, 0)),
        out_shape=jax.ShapeDtypeStruct((T, D), F32),
        compiler_params=_params("arbitrary"),
        name="swiglu",
    )(x.reshape(T, D), g.reshape(1, D), w_gu, w_d, final_g.reshape(1, D))
    return out.reshape(B, S, D)


def _odd_proj_kernel(x_ref, pos_ref, g_ref, wmain_ref, wgt_ref, gbias_ref, gbias_row_ref,
                     qg_ref, kvg_ref, wuq_ref, wuqs_ref, wuk_ref, wuvt_ref, invf_ref,
                     qm_ref, km_ref, vm_ref, om_ref, gates_ref, gates_c_ref, qp_ref, kp_ref, vt_ref):
    x = x_ref[0]
    h = _rms(x, g_ref[...]).astype(BF16)
    z = _dot(h, wmain_ref[...])
    c_g = 4 * ML_WIDTH + Q_LORA + KV_LORA + 2 * LANES
    gates_c_ref[0] = z[:, c_g:c_g + LANES] + gbias_row_ref[...]
    qm_ref[0] = z[:, 0:ML_WIDTH].astype(BF16)
    km_ref[0] = (z[:, ML_WIDTH:2 * ML_WIDTH] * (ML_HEAD_DIM ** -0.5)).astype(BF16)
    vm_ref[0] = z[:, 2 * ML_WIDTH:3 * ML_WIDTH].astype(BF16)
    om_ref[0] = z[:, 3 * ML_WIDTH:4 * ML_WIDTH].astype(BF16)
    gates_ref[0] = _dot_nt(wgt_ref[...], h) + gbias_ref[...]

    c0 = 4 * ML_WIDTH
    c_q = z[:, c0:c0 + Q_LORA]
    c_kv = z[:, c0 + Q_LORA:c0 + Q_LORA + KV_LORA]
    kr = z[:, c0 + Q_LORA + KV_LORA:c0 + Q_LORA + KV_LORA + LANES]
    kr_sw = z[:, c0 + Q_LORA + KV_LORA + LANES:c0 + Q_LORA + KV_LORA + 2 * LANES]

    ang = pos_ref[0].astype(F32) * invf_ref[...]
    lane = lax.broadcasted_iota(jnp.int32, ang.shape, 1)
    in_rope = (lane >= QK_NOPE) & (lane < QK_NOPE + QK_ROPE)
    first_half = lane < QK_NOPE + QK_ROPE // 2
    cos_t = jnp.where(in_rope, jnp.cos(ang), 1.0)
    sin_v = jnp.sin(ang)
    sin_t = jnp.where(in_rope, jnp.where(first_half, -sin_v, sin_v), 0.0)

    cqn = _rms(c_q, qg_ref[...]).astype(BF16)
    qa = _dot(cqn, wuq_ref[...])
    qb = _dot(cqn, wuqs_ref[...])
    ckn = _rms(c_kv, kvg_ref[...]).astype(BF16)
    kn = _dot(ckn, wuk_ref[...])
    vt_ref[0] = _dot_nt(wuvt_ref[...], ckn).astype(BF16)
    k_rope = kr * cos_t + kr_sw * sin_t
    scale = (QK_NOPE + QK_ROPE) ** -0.5 * LOG2_E
    for hd in range(MLA_HEADS):
        lo = hd * LANES
        qh = qa[:, lo:lo + LANES] * cos_t + qb[:, lo:lo + LANES] * sin_t
        qp_ref[0, :, lo:lo + LANES] = (qh * scale).astype(BF16)
        kp_ref[0, :, lo:lo + LANES] = (kn[:, lo:lo + LANES] + k_rope).astype(BF16)


def _odd_proj(x, pos3, g, w_main, w_gt, gbias, gbias_row, q_g, kv_g, w_uq_p, w_uq_s, w_uk_p, w_uvt,
              invf, *, ts):
    B, S, D = x.shape
    ts = min(ts, S)
    tok = lambda n: pl.BlockSpec((1, ts, n), lambda b, s: (b, s, 0))
    out_shapes = (
        jax.ShapeDtypeStruct((B, S, ML_WIDTH), BF16),
        jax.ShapeDtypeStruct((B, S, ML_WIDTH), BF16),
        jax.ShapeDtypeStruct((B, S, ML_WIDTH), BF16),
        jax.ShapeDtypeStruct((B, S, ML_WIDTH), BF16),
        jax.ShapeDtypeStruct((B, 2 * ML_HEADS, S), F32),
        jax.ShapeDtypeStruct((B, S, LANES), F32),
        jax.ShapeDtypeStruct((B, S, MLA_HEADS * LANES), BF16),
        jax.ShapeDtypeStruct((B, S, MLA_HEADS * LANES), BF16),
        jax.ShapeDtypeStruct((B, MLA_WIDTH, S), BF16),
    )
    return pl.pallas_call(
        _odd_proj_kernel,
        grid=(B, S // ts),
        in_specs=[
            tok(D),
            pl.BlockSpec((1, ts, 1), lambda b, s: (b, s, 0)),
            _const_spec((1, D)),
            _const_spec(w_main.shape),
            _const_spec(w_gt.shape),
            _const_spec(gbias.shape),
            _const_spec(gbias_row.shape),
            _const_spec(q_g.shape),
            _const_spec(kv_g.shape),
            _const_spec(w_uq_p.shape),
            _const_spec(w_uq_s.shape),
            _const_spec(w_uk_p.shape),
            _const_spec(w_uvt.shape),
            _const_spec(invf.shape),
        ],
        out_specs=(tok(ML_WIDTH), tok(ML_WIDTH), tok(ML_WIDTH), tok(ML_WIDTH),
                   pl.BlockSpec((1, 2 * ML_HEADS, ts), lambda b, s: (b, 0, s)), tok(LANES),
                   tok(MLA_HEADS * LANES), tok(MLA_HEADS * LANES),
                   pl.BlockSpec((1, MLA_WIDTH, ts), lambda b, s: (b, 0, s))),
        out_shape=out_shapes,
        compiler_params=_params("arbitrary", "arbitrary"),
        name="odd_proj",
    )(x, pos3, g.reshape(1, D), w_main, w_gt, gbias, gbias_row, q_g, kv_g, w_uq_p, w_uq_s, w_uk_p,
      w_uvt, invf)


def _split3(a):
    hi = a.astype(BF16)
    r1 = a - hi.astype(F32)
    mid = r1.astype(BF16)
    lo = (r1 - mid.astype(F32)).astype(BF16)
    return hi, mid, lo


def _lane_sum_mxu(a):
    ones = jnp.ones((a.shape[1], LANES), BF16)
    return sum(_dot(t, ones) for t in _split3(a))


def _log2_sigmoid(x):
    return (jnp.minimum(x, 0.0) - jnp.log(1.0 + jnp.exp(-jnp.abs(x)))) * LOG2_E


def _mlstm_kernel(q_ref, k_ref, v_ref, o_ref, gt_ref, gc_ref, ng_ref, h_ref, c_scr, m_scr, *, L):
    ci = pl.program_id(1)

    @pl.when(ci == 0)
    def _():
        c_scr[...] = jnp.zeros(c_scr.shape, F32)
        m_scr[...] = jnp.zeros(m_scr.shape, F32)

    row = lax.broadcasted_iota(jnp.int32, (L, L), 0)
    col = lax.broadcasted_iota(jnp.int32, (L, L), 1)
    causal = col <= row
    gt = gt_ref[0]
    ones_v = jnp.ones((L, LANES), BF16)
    reps = L // LANES

    def widen(a):
        return jnp.concatenate([a] * reps, axis=1)

    lf_t = _log2_sigmoid(gt)
    b_t = sum(_dot(t, (row <= col).astype(BF16)) for t in _split3(lf_t))
    tril = causal.astype(BF16)
    b_c = sum(_dot(tril, t) for t in _split3(_log2_sigmoid(gc_ref[0])))

    for hd in range(ML_HEADS):
        lo = hd * ML_HEAD_DIM
        fg = ML_HEADS + hd
        q = q_ref[0, :, lo:lo + ML_HEAD_DIM]
        k = k_ref[0, :, lo:lo + ML_HEAD_DIM]
        v_aug = jnp.concatenate([v_ref[0, :, lo:lo + ML_HEAD_DIM], ones_v], axis=1)
        g_row = gt[hd:hd + 1, :] * LOG2_E - b_t[fg:fg + 1, :]
        gtot = jnp.sum(lf_t[fg:fg + 1, :], axis=1, keepdims=True)
        b_col = jnp.broadcast_to(b_c[:, fg:fg + 1], (L, LANES))

        m_prev = m_scr[hd]
        c_prev = c_scr[hd]

        g_mask = jnp.where(causal, g_row, NEG_BIG)
        gmax = jnp.max(g_mask, axis=1, keepdims=True)
        h_col = jnp.maximum(m_prev, jnp.broadcast_to(gmax, (L, LANES)))
        w_inter = jnp.exp2(m_prev - h_col)
        sw = jnp.exp2(g_mask - widen(h_col)) * _dot_nt(q, k)
        out = (jnp.concatenate([w_inter, w_inter], axis=1) * _dot(q, c_prev.astype(BF16))
               + _dot(sw.astype(BF16), v_aug))
        den = jnp.maximum(jnp.abs(out[:, ML_HEAD_DIM:]), jnp.exp2(-(b_col + h_col)))
        hc = out[:, :ML_HEAD_DIM] / den

        og = jax.nn.sigmoid(o_ref[0, :, lo:lo + ML_HEAD_DIM].astype(F32))
        hg = og * hc
        ms = _lane_sum_mxu(hg * hg) * (1.0 / ML_HEAD_DIM)
        h_ref[0, :, lo:lo + ML_HEAD_DIM] = (
            hg * lax.rsqrt(ms + EPS) * ng_ref[:, lo:lo + ML_HEAD_DIM]).astype(BF16)

        m_prev1 = m_prev[:, 0:1]
        a_row = gtot + g_row
        m_loc = jnp.max(a_row, axis=1, keepdims=True)
        w_loc = jnp.exp2(a_row - m_loc)
        m_new = jnp.maximum(gtot + m_prev1, m_loc)
        s_old = jnp.exp2(gtot + m_prev1 - m_new)
        s_new = jnp.exp2(m_loc - m_new)
        kw_t = (k.T.astype(F32) * w_loc).astype(BF16)
        c_scr[hd] = s_old * c_prev + s_new * _dot(kw_t, v_aug)
        m_scr[hd] = jnp.broadcast_to(m_new, (1, LANES))


def _mlstm(q, k, v, o, gates_t, gates_c, norm_g, *, L):
    B, S, W = q.shape
    L = min(L, S)
    kern = functools.partial(_mlstm_kernel, L=L)
    tok = pl.BlockSpec((1, L, W), lambda b, c: (b, c, 0))
    return pl.pallas_call(
        kern,
        grid=(B, S // L),
        in_specs=[tok, tok, tok, tok,
                  pl.BlockSpec((1, 2 * ML_HEADS, L), lambda b, c: (b, 0, c)),
                  pl.BlockSpec((1, L, LANES), lambda b, c: (b, c, 0)),
                  _const_spec((1, W))],
        out_specs=tok,
        out_shape=jax.ShapeDtypeStruct((B, S, W), BF16),
        scratch_shapes=[pltpu.VMEM((ML_HEADS, ML_HEAD_DIM, 2 * ML_HEAD_DIM), F32),
                        pltpu.VMEM((ML_HEADS, 1, LANES), F32)],
        compiler_params=_params("arbitrary", "arbitrary"),
        name="mlstm",
    )(q, k, v, o, gates_t, gates_c, norm_g.reshape(1, W))


def _mla_kernel(q_ref, k_ref, vt_ref, o_ref, s_scr, *, tq):
    qi = pl.program_id(2)
    heads = 2
    qs = [q_ref[0, :, hh * LANES:(hh + 1) * LANES] for hh in range(heads)]

    def scores(j, slot):
        start = pl.multiple_of(j * tq, tq)
        for hh in range(heads):
            kb = k_ref[0, pl.ds(start, tq), hh * LANES:(hh + 1) * LANES]
            s_scr[slot, hh] = _dot_nt(kb, qs[hh])

    def block(j, slot, carry, masked):
        start = pl.multiple_of(j * tq, tq)
        new = []
        for hh in range(heads):
            m, acc = carry[hh]
            vtb = jnp.concatenate([vt_ref[0, hh * V_DIM:(hh + 1) * V_DIM, pl.ds(start, tq)],
                                   ones_rows], axis=0)
            s = s_scr[slot, hh]
            if masked:
                kk = lax.broadcasted_iota(jnp.int32, s.shape, 0)
                qq = lax.broadcasted_iota(jnp.int32, s.shape, 1)
                s = jnp.where(kk <= qq, s, NEG_BIG)
            m_new = jnp.maximum(m, jnp.max(s, axis=0, keepdims=True))
            alpha = jnp.exp2(m - m_new)
            p = jnp.exp2(s - m_new).astype(BF16)
            new.append((m_new, alpha * acc + _dot(vtb, p)))
        return tuple(new)

    ones_rows = jnp.ones((BF16_ROWS, tq), BF16)
    init = tuple((jnp.full((1, tq), NEG_BIG, F32), jnp.zeros((V_DIM + BF16_ROWS, tq), F32))
                 for _ in range(heads))
    scores(0, 0)

    def body(i, c):
        scores(2 * i + 1, 1)
        c = block(2 * i, 0, c, False)
        scores(2 * i + 2, 0)
        return block(2 * i + 1, 1, c, False)

    carry = lax.fori_loop(0, qi // 2, body, init)

    def finish(c):
        ot = jnp.concatenate([acc[:V_DIM] / acc[V_DIM:V_DIM + 1] for _, acc in c], axis=0)
        o_ref[0] = ot.T.astype(BF16)

    @pl.when(qi % 2 == 0)
    def _():
        finish(block(qi, 0, carry, True))

    @pl.when(qi % 2 == 1)
    def _():
        scores(qi, 1)
        finish(block(qi, 1, block(qi - 1, 0, carry, False), True))


def _mla_attention(qp, kp, vt, *, tq):
    B, S, _ = qp.shape
    tq = min(tq, S)
    pairs = MLA_HEADS // 2
    kern = functools.partial(_mla_kernel, tq=tq)
    return pl.pallas_call(
        kern,
        grid=(B, pairs, S // tq),
        in_specs=[
            pl.BlockSpec((1, tq, 2 * LANES), lambda b, p, i: (b, i, p)),
            pl.BlockSpec((1, S, 2 * LANES), lambda b, p, i: (b, 0, p)),
            pl.BlockSpec((1, 2 * V_DIM, S), lambda b, p, i: (b, p, 0)),
        ],
        out_specs=pl.BlockSpec((1, tq, LANES), lambda b, p, i: (b, i, p)),
        out_shape=jax.ShapeDtypeStruct((B, S, MLA_WIDTH), BF16),
        scratch_shapes=[pltpu.VMEM((2, 2, tq, tq), F32)],
        compiler_params=_params("arbitrary", "arbitrary", "arbitrary"),
        name="mla_attention",
    )(qp, kp, vt)


def _odd_out_kernel(x_ref, hm_ref, ha_ref, wout_ref, xg_ref, wq_ref, kv_ref, wo_ref, o_ref):
    mix = jnp.concatenate([hm_ref[0], ha_ref[0]], axis=-1)
    x1 = x_ref[0] + _dot(mix, wout_ref[...])
    o_ref[0] = _xattn_body(x1, xg_ref, wq_ref, kv_ref, wo_ref)


def _odd_out(x, hm, ha, w_out, xg, wq, kv, wo, *, ts):
    B, S, D = x.shape
    ts = min(ts, S)
    tok = lambda n: pl.BlockSpec((1, ts, n), lambda b, s: (b, s, 0))
    return pl.pallas_call(
        _odd_out_kernel,
        grid=(B, S // ts),
        in_specs=[tok(D), tok(ML_WIDTH), tok(MLA_WIDTH), _const_spec(w_out.shape)]
        + _xattn_specs(D, wq, kv, wo),
        out_specs=tok(D),
        out_shape=jax.ShapeDtypeStruct(x.shape, F32),
        compiler_params=_params("arbitrary", "arbitrary"),
        name="odd_out",
    )(x, hm, ha, w_out, xg.reshape(1, D), wq, kv, wo)


def _pack_odd_weights(w_in, w_uq, w_ukv):
    D = w_in.shape[0]
    c = 4 * ML_WIDTH
    w_g = w_in[:, c:c + 2 * ML_HEADS]
    c += 2 * ML_HEADS
    w_cq = w_in[:, c:c + Q_LORA]
    c += Q_LORA
    w_ckv = w_in[:, c:c + KV_LORA]
    c += KV_LORA
    w_kr = w_in[:, c:c + QK_ROPE]
    half = QK_ROPE // 2
    w_kr_sw = jnp.concatenate([w_kr[:, half:], w_kr[:, :half]], axis=1)
    zl = jnp.zeros((D, QK_NOPE), w_in.dtype)
    zr = jnp.zeros((D, LANES - QK_NOPE - QK_ROPE), w_in.dtype)
    zg = jnp.zeros((D, LANES - 2 * ML_HEADS), w_in.dtype)
    w_main = jnp.concatenate([w_in[:, :4 * ML_WIDTH], w_cq, w_ckv,
                              zl, w_kr, zr, zl, w_kr_sw, zr, w_g, zg], axis=1).astype(BF16)
    w_gt = w_g.T.astype(BF16)

    uq = w_uq.reshape(Q_LORA, MLA_HEADS, QK_NOPE + QK_ROPE)
    uq_n, uq_r = uq[..., :QK_NOPE], uq[..., QK_NOPE:]
    uq_r_sw = jnp.concatenate([uq_r[..., half:], uq_r[..., :half]], axis=-1)
    zpad = jnp.zeros((Q_LORA, MLA_HEADS, LANES - QK_NOPE - QK_ROPE), w_uq.dtype)
    w_uq_p = jnp.concatenate([uq_n, uq_r, zpad], axis=-1).reshape(Q_LORA, MLA_HEADS * LANES)
    w_uq_s = jnp.concatenate([jnp.zeros_like(uq_n), uq_r_sw, zpad], axis=-1)
    w_uq_s = w_uq_s.reshape(Q_LORA, MLA_HEADS * LANES)

    ukv = w_ukv.reshape(KV_LORA, MLA_HEADS, QK_NOPE + V_DIM)
    uk, uv = ukv[..., :QK_NOPE], ukv[..., QK_NOPE:]
    w_uk_p = jnp.concatenate([uk, jnp.zeros((KV_LORA, MLA_HEADS, LANES - QK_NOPE), w_ukv.dtype)],
                             axis=-1).reshape(KV_LORA, MLA_HEADS * LANES)
    w_uvt = uv.reshape(KV_LORA, MLA_WIDTH).T
    return (w_main, w_gt, w_uq_p.astype(BF16), w_uq_s.astype(BF16), w_uk_p.astype(BF16),
            w_uvt.astype(BF16))


def _rope_inv_freq():
    half = QK_ROPE // 2
    inv = ROPE_THETA ** (-jnp.arange(half, dtype=F32) / half)
    blk = jnp.concatenate([jnp.zeros((QK_NOPE,), F32), inv, inv,
                           jnp.zeros((LANES - QK_NOPE - QK_ROPE,), F32)])
    return blk.reshape(1, LANES)


TS_EVEN = 1024
TM_FFN = 1024
TH_FFN = 256
TS_ODD = 512
L_MLSTM = 256
TQ_MLA = 512
TS_OUT = 1024


def kernel(x, mem, positions, norm_mix_g, norm_xattn_g, mem_norm_g, xattn_wq, xattn_wkv, xattn_wo,
           norm_ffn_g, ffn_w_gate_up, ffn_w_down, ev_w_in, ev_conv_w, ev_pool_w, ev_pool_scale,
           ev_w_out, od_w_in, od_gate_bias, od_ml_norm_g, od_q_norm_g, od_kv_norm_g, od_w_uq,
           od_w_ukv, od_w_out, final_norm_g):
    depth = norm_mix_g.shape[0]
    B, S, D = x.shape
    pos3 = positions.reshape(B, S, 1)
    invf = _rope_inv_freq()
    for layer in range(depth):
        kv = _mem_kv(mem, mem_norm_g[layer], xattn_wkv[layer].astype(BF16))
        xa = (norm_xattn_g[layer], xattn_wq[layer].astype(BF16), kv, xattn_wo[layer].astype(BF16))
        if layer % 2 == 0:
            e = layer // 2
            x = _even_mixer(x, norm_mix_g[layer], ev_w_in[e].astype(BF16), ev_conv_w[e],
                            ev_pool_w[e].astype(BF16), ev_pool_scale[e], ev_w_out[e].astype(BF16),
                            *xa, ts=TS_EVEN)
        else:
            o = layer // 2
            w_main, w_gt, w_uq_p, w_uq_s, w_uk_p, w_uvt = _pack_odd_weights(
                od_w_in[o], od_w_uq[o], od_w_ukv[o])
            gbias_row = jnp.pad(od_gate_bias[o], (0, LANES - 2 * ML_HEADS)).reshape(1, LANES)
            qm, km, vm, om, gates_t, gates_c, qp, kp, vt = _odd_proj(
                x, pos3, norm_mix_g[layer], w_main, w_gt, od_gate_bias[o].reshape(2 * ML_HEADS, 1),
                gbias_row, od_q_norm_g[o].reshape(1, Q_LORA), od_kv_norm_g[o].reshape(1, KV_LORA),
                w_uq_p, w_uq_s, w_uk_p, w_uvt, invf, ts=TS_ODD)
            hm = _mlstm(qm, km, vm, om, gates_t, gates_c, od_ml_norm_g[o], L=L_MLSTM)
            ha = _mla_attention(qp, kp, vt, tq=TQ_MLA)
            x = _odd_out(x, hm, ha, od_w_out[o].astype(BF16), *xa, ts=TS_OUT)
        x = _swiglu(x, norm_ffn_g[layer], ffn_w_gate_up[layer].astype(BF16),
                    ffn_w_down[layer].astype(BF16), final_norm_g,
                    tm=TM_FFN, th=TH_FFN, final_norm=(layer == depth - 1))
    return x
```

```python
import functools

import jax
import jax.numpy as jnp
from jax import lax
from jax.experimental import pallas as pl
from jax.experimental.pallas import tpu as pltpu

F32 = jnp.float32
BF16 = jnp.bfloat16

EPS = 1e-6
ROPE_THETA = 10000.0
LANES = 128
SUBLANES = 8
BF16_ROWS = 16
VMEM_LIMIT_BYTES = 56 * 1024 * 1024
NEG_BIG = -1e30
LOG2_E = 1.4426950408889634

SC_WIDTH = 512
CONV_K = 3
POOL_WINDOWS = (2, 4, 8, 16)
POOL_GROUP = 128
HALO = 16
ML_HEADS = 4
ML_HEAD_DIM = 128
ML_WIDTH = ML_HEADS * ML_HEAD_DIM
MLA_HEADS = 8
QK_NOPE = 64
QK_ROPE = 32
V_DIM = 64
Q_LORA = 384
KV_LORA = 256
MLA_WIDTH = MLA_HEADS * V_DIM
XA_HEADS = 4
XA_HEAD_DIM = 128
XA_WIDTH = XA_HEADS * XA_HEAD_DIM


def _params(*sem):
    return pltpu.CompilerParams(dimension_semantics=sem, vmem_limit_bytes=VMEM_LIMIT_BYTES)


def _const_spec(shape):
    nd = len(shape)
    return pl.BlockSpec(shape, lambda *_: (0,) * nd, pipeline_mode=pl.Buffered(1))


def _layer_spec(stacked, layer):
    return pl.BlockSpec((None,) + stacked.shape[1:], lambda *_: (layer, 0, 0),
                        pipeline_mode=pl.Buffered(1))


def _rms(x, g):
    ms = jnp.mean(x * x, axis=-1, keepdims=True)
    return x * lax.rsqrt(ms + EPS) * g


def _dot(a, b):
    return jnp.dot(a, b, preferred_element_type=F32)


def _dot_nt(a, b):
    return lax.dot_general(a, b, (((1,), (1,)), ((), ())), preferred_element_type=F32)


def _even_kernel(x_ref, g_ref, win_ref, convw_ref, poolw_ref, pscale_ref, wout_ref,
                 xg_ref, wq_ref, kv_ref, wo_ref, o_ref, ubuf, xbbuf, *, ts):
    si = pl.program_id(1)

    @pl.when(si == 0)
    def _():
        ubuf[0:HALO, :] = jnp.zeros((HALO, SC_WIDTH), F32)
        xbbuf[0:HALO, :] = jnp.zeros((HALO, SC_WIDTH), F32)

    x = x_ref[0]
    h = _rms(x, g_ref[...]).astype(BF16)
    z = _dot(h, win_ref[...])
    g_b = z[:, 0:SC_WIDTH]
    u = z[:, SC_WIDTH:2 * SC_WIDTH] * z[:, 2 * SC_WIDTH:3 * SC_WIDTH]
    xb = z[:, 3 * SC_WIDTH:4 * SC_WIDTH]
    ubuf[HALO:HALO + ts, :] = u
    xbbuf[HALO:HALO + ts, :] = xb

    cw = convw_ref[...]
    conv = cw[2:3, :] * u
    for k in range(CONV_K - 1):
        back = CONV_K - 1 - k
        conv = conv + cw[k:k + 1, :] * ubuf[HALO - back:HALO - back + ts, :]
    ya = g_b * conv

    t_idx = si * ts + lax.broadcasted_iota(jnp.int32, (ts, POOL_GROUP), 0)
    ys = [ya.astype(BF16)]
    for gi, w in enumerate(POOL_WINDOWS):
        lo = gi * POOL_GROUP
        xg = xb[:, lo:lo + POOL_GROUP]
        acc = xg
        for j in range(1, w):
            acc = acc + xbbuf[HALO - j:HALO - j + ts, lo:lo + POOL_GROUP]
        cnt = jnp.minimum(t_idx + 1, w).astype(F32)
        d = (acc / cnt - xg).astype(BF16)
        yg = _dot(d, poolw_ref[gi]) * pscale_ref[:, lo:lo + POOL_GROUP]
        ys.append(yg.astype(BF16))

    ubuf[0:HALO, :] = ubuf[ts:ts + HALO, :]
    xbbuf[0:HALO, :] = xbbuf[ts:ts + HALO, :]

    mix = jnp.concatenate(ys, axis=-1)
    x1 = x + _dot(mix, wout_ref[...])
    o_ref[0] = _xattn_body(x1, xg_ref, wq_ref, kv_ref, wo_ref)


def _even_mixer(x, g, w_in, conv_w, pool_w, pool_scale, w_out, xg, wq_all, kv, wo_all, layer, *, ts):
    B, S, D = x.shape
    ts = min(ts, S)
    kern = functools.partial(_even_kernel, ts=ts)
    return pl.pallas_call(
        kern,
        grid=(B, S // ts),
        in_specs=[
            pl.BlockSpec((1, ts, D), lambda b, s: (b, s, 0)),
            _const_spec((1, D)),
            _const_spec(w_in.shape),
            _const_spec(conv_w.shape),
            _const_spec(pool_w.shape),
            _const_spec((1, SC_WIDTH)),
            _const_spec(w_out.shape),
        ] + _xattn_specs(D, wq_all, kv, wo_all, layer),
        out_specs=pl.BlockSpec((1, ts, D), lambda b, s: (b, s, 0)),
        out_shape=jax.ShapeDtypeStruct(x.shape, F32),
        scratch_shapes=[pltpu.VMEM((HALO + ts, SC_WIDTH), F32),
                        pltpu.VMEM((HALO + ts, SC_WIDTH), F32)],
        compiler_params=_params("arbitrary", "arbitrary"),
        name="even_mixer",
    )(x, g.reshape(1, D), w_in, conv_w, pool_w, pool_scale.reshape(1, SC_WIDTH), w_out,
      xg.reshape(1, D), wq_all, kv, wo_all)


def _memkv_kernel(mem_ref, g_ref, wkv_ref, kv_ref):
    h = _rms(mem_ref[0], g_ref[...]).astype(BF16)
    kv_ref[0] = _dot(h, wkv_ref[...]).astype(BF16)


def _mem_kv(mem, g, wkv_all, layer):
    B, M, D = mem.shape
    N = wkv_all.shape[2]
    return pl.pallas_call(
        _memkv_kernel,
        grid=(B,),
        in_specs=[pl.BlockSpec((1, M, D), lambda b: (b, 0, 0)),
                  _const_spec((1, D)), _layer_spec(wkv_all, layer)],
        out_specs=pl.BlockSpec((1, M, N), lambda b: (b, 0, 0)),
        out_shape=jax.ShapeDtypeStruct((B, M, N), BF16),
        compiler_params=_params("arbitrary"),
        name="mem_kv",
    )(mem, g.reshape(1, D), wkv_all)


def _xattn_body(x, g_ref, wq_ref, kv_ref, wo_ref):
    h = _rms(x, g_ref[...]).astype(BF16)
    q = (_dot(h, wq_ref[...]) * (XA_HEAD_DIM ** -0.5)).astype(BF16)
    kv = kv_ref[0]
    ones_v = jnp.ones((kv.shape[0], XA_HEAD_DIM), BF16)
    outs = []
    for hd in range(XA_HEADS):
        lo = hd * XA_HEAD_DIM
        s = _dot_nt(q[:, lo:lo + XA_HEAD_DIM], kv[:, lo:lo + XA_HEAD_DIM])
        p = jnp.exp(s - jnp.max(s, axis=-1, keepdims=True)).astype(BF16)
        v_aug = jnp.concatenate([kv[:, XA_WIDTH + lo:XA_WIDTH + lo + XA_HEAD_DIM], ones_v], axis=1)
        o = _dot(p, v_aug)
        outs.append((o[:, :XA_HEAD_DIM] / o[:, XA_HEAD_DIM:]).astype(BF16))
    a = jnp.concatenate(outs, axis=-1)
    return x + _dot(a, wo_ref[...])


def _xattn_specs(D, wq_all, kv, wo_all, layer):
    M = kv.shape[1]
    return [_const_spec((1, D)), _layer_spec(wq_all, layer),
            pl.BlockSpec((1, M, kv.shape[2]), lambda b, s: (b, 0, 0)), _layer_spec(wo_all, layer)]


def _swiglu_kernel(x_ref, g_ref, wgu_ref, wd_ref, fg_ref, o_ref, *, hidden, th, final_norm):
    x = x_ref[...]
    h = _rms(x, g_ref[...]).astype(BF16)
    acc = x
    for j in range(hidden // th):
        gate = _dot(h, wgu_ref[:, j * th:(j + 1) * th])
        up = _dot(h, wgu_ref[:, hidden + j * th:hidden + (j + 1) * th])
        act = (gate * jax.nn.sigmoid(gate) * up).astype(BF16)
        acc = acc + _dot(act, wd_ref[j * th:(j + 1) * th, :])
    if final_norm:
        acc = _rms(acc, fg_ref[...])
    o_ref[...] = acc


def _swiglu(x, g, w_gu_all, w_d_all, final_g, layer, *, tm, th, final_norm):
    B, S, D = x.shape
    T = B * S
    tm = min(tm, T)
    hidden = w_d_all.shape[1]
    kern = functools.partial(_swiglu_kernel, hidden=hidden, th=th, final_norm=final_norm)
    out = pl.pallas_call(
        kern,
        grid=(T // tm,),
        in_specs=[
            pl.BlockSpec((tm, D), lambda i: (i, 0)),
            _const_spec((1, D)),
            _layer_spec(w_gu_all, layer),
            _layer_spec(w_d_all, layer),
            _const_spec((1, D)),
        ],
        out_specs=pl.BlockSpec((tm, D), lambda i: (i, 0)),
        out_shape=jax.ShapeDtypeStruct((T, D), F32),
        compiler_params=_params("arbitrary"),
        name="swiglu",
    )(x.reshape(T, D), g.reshape(1, D), w_gu_all, w_d_all, final_g.reshape(1, D))
    return out.reshape(B, S, D)


def _odd_proj_kernel(x_ref, pos_ref, g_ref, wmain_ref, wgt_ref, gbias_ref, gbias_row_ref,
                     qg_ref, kvg_ref, wuq_ref, wuqs_ref, wuk_ref, wuvt_ref, invf_ref,
                     qm_ref, km_ref, vm_ref, om_ref, gates_ref, gates_c_ref, qp_ref, kp_ref, vt_ref):
    x = x_ref[0]
    h = _rms(x, g_ref[...]).astype(BF16)
    z = _dot(h, wmain_ref[...])
    c_g = 4 * ML_WIDTH + Q_LORA + KV_LORA + 2 * LANES
    gates_c_ref[0] = z[:, c_g:c_g + LANES] + gbias_row_ref[...]
    qm_ref[0] = z[:, 0:ML_WIDTH].astype(BF16)
    km_ref[0] = (z[:, ML_WIDTH:2 * ML_WIDTH] * (ML_HEAD_DIM ** -0.5)).astype(BF16)
    vm_ref[0] = z[:, 2 * ML_WIDTH:3 * ML_WIDTH].astype(BF16)
    om_ref[0] = z[:, 3 * ML_WIDTH:4 * ML_WIDTH].astype(BF16)
    gates_ref[0] = _dot_nt(wgt_ref[...], h) + gbias_ref[...]

    c0 = 4 * ML_WIDTH
    c_q = z[:, c0:c0 + Q_LORA]
    c_kv = z[:, c0 + Q_LORA:c0 + Q_LORA + KV_LORA]
    kr = z[:, c0 + Q_LORA + KV_LORA:c0 + Q_LORA + KV_LORA + LANES]
    kr_sw = z[:, c0 + Q_LORA + KV_LORA + LANES:c0 + Q_LORA + KV_LORA + 2 * LANES]

    ang = pos_ref[0].astype(F32) * invf_ref[...]
    lane = lax.broadcasted_iota(jnp.int32, ang.shape, 1)
    in_rope = (lane >= QK_NOPE) & (lane < QK_NOPE + QK_ROPE)
    first_half = lane < QK_NOPE + QK_ROPE // 2
    cos_t = jnp.where(in_rope, jnp.cos(ang), 1.0)
    sin_v = jnp.sin(ang)
    sin_t = jnp.where(in_rope, jnp.where(first_half, -sin_v, sin_v), 0.0)

    cqn = _rms(c_q, qg_ref[...]).astype(BF16)
    qa = _dot(cqn, wuq_ref[...])
    qb = _dot(cqn, wuqs_ref[...])
    ckn = _rms(c_kv, kvg_ref[...]).astype(BF16)
    kn = _dot(ckn, wuk_ref[...])
    vt_ref[0] = _dot_nt(wuvt_ref[...], ckn).astype(BF16)
    k_rope = kr * cos_t + kr_sw * sin_t
    scale = (QK_NOPE + QK_ROPE) ** -0.5 * LOG2_E
    for hd in range(MLA_HEADS):
        lo = hd * LANES
        qh = qa[:, lo:lo + LANES] * cos_t + qb[:, lo:lo + LANES] * sin_t
        qp_ref[0, :, lo:lo + LANES] = (qh * scale).astype(BF16)
        kp_ref[0, :, lo:lo + LANES] = (kn[:, lo:lo + LANES] + k_rope).astype(BF16)


def _odd_proj(x, pos3, g, w_main, w_gt, gbias, gbias_row, q_g, kv_g, w_uq_p, w_uq_s, w_uk_p, w_uvt,
              invf, *, ts):
    B, S, D = x.shape
    ts = min(ts, S)
    tok = lambda n: pl.BlockSpec((1, ts, n), lambda b, s: (b, s, 0))
    out_shapes = (
        jax.ShapeDtypeStruct((B, S, ML_WIDTH), BF16),
        jax.ShapeDtypeStruct((B, S, ML_WIDTH), BF16),
        jax.ShapeDtypeStruct((B, S, ML_WIDTH), BF16),
        jax.ShapeDtypeStruct((B, S, ML_WIDTH), BF16),
        jax.ShapeDtypeStruct((B, 2 * ML_HEADS, S), F32),
        jax.ShapeDtypeStruct((B, S, LANES), F32),
        jax.ShapeDtypeStruct((B, S, MLA_HEADS * LANES), BF16),
        jax.ShapeDtypeStruct((B, S, MLA_HEADS * LANES), BF16),
        jax.ShapeDtypeStruct((B, MLA_WIDTH, S), BF16),
    )
    return pl.pallas_call(
        _odd_proj_kernel,
        grid=(B, S // ts),
        in_specs=[
            tok(D),
            pl.BlockSpec((1, ts, 1), lambda b, s: (b, s, 0)),
            _const_spec((1, D)),
            _const_spec(w_main.shape),
            _const_spec(w_gt.shape),
            _const_spec(gbias.shape),
            _const_spec(gbias_row.shape),
            _const_spec(q_g.shape),
            _const_spec(kv_g.shape),
            _const_spec(w_uq_p.shape),
            _const_spec(w_uq_s.shape),
            _const_spec(w_uk_p.shape),
            _const_spec(w_uvt.shape),
            _const_spec(invf.shape),
        ],
        out_specs=(tok(ML_WIDTH), tok(ML_WIDTH), tok(ML_WIDTH), tok(ML_WIDTH),
                   pl.BlockSpec((1, 2 * ML_HEADS, ts), lambda b, s: (b, 0, s)), tok(LANES),
                   tok(MLA_HEADS * LANES), tok(MLA_HEADS * LANES),
                   pl.BlockSpec((1, MLA_WIDTH, ts), lambda b, s: (b, 0, s))),
        out_shape=out_shapes,
        compiler_params=_params("arbitrary", "arbitrary"),
        name="odd_proj",
    )(x, pos3, g.reshape(1, D), w_main, w_gt, gbias, gbias_row, q_g, kv_g, w_uq_p, w_uq_s, w_uk_p,
      w_uvt, invf)


def _split3(a):
    hi = a.astype(BF16)
    r1 = a - hi.astype(F32)
    mid = r1.astype(BF16)
    lo = (r1 - mid.astype(F32)).astype(BF16)
    return hi, mid, lo


def _lane_sum_mxu(a):
    ones = jnp.ones((a.shape[1], LANES), BF16)
    return sum(_dot(t, ones) for t in _split3(a))


def _log2_sigmoid(x):
    return (jnp.minimum(x, 0.0) - jnp.log(1.0 + jnp.exp(-jnp.abs(x)))) * LOG2_E


def _mlstm_kernel(q_ref, k_ref, v_ref, o_ref, gt_ref, gc_ref, ng_ref, h_ref, c_scr, m_scr, *, L):
    ci = pl.program_id(1)

    @pl.when(ci == 0)
    def _():
        c_scr[...] = jnp.zeros(c_scr.shape, F32)
        m_scr[...] = jnp.zeros(m_scr.shape, F32)

    row = lax.broadcasted_iota(jnp.int32, (L, L), 0)
    col = lax.broadcasted_iota(jnp.int32, (L, L), 1)
    causal = col <= row
    gt = gt_ref[0]
    ones_v = jnp.ones((L, LANES), BF16)
    reps = L // LANES

    def widen(a):
        return jnp.concatenate([a] * reps, axis=1)

    lf_t = _log2_sigmoid(gt)
    b_t = sum(_dot(t, (row <= col).astype(BF16)) for t in _split3(lf_t))
    tril = causal.astype(BF16)
    b_c = sum(_dot(tril, t) for t in _split3(_log2_sigmoid(gc_ref[0])))

    for hd in range(ML_HEADS):
        lo = hd * ML_HEAD_DIM
        fg = ML_HEADS + hd
        q = q_ref[0, :, lo:lo + ML_HEAD_DIM]
        k = k_ref[0, :, lo:lo + ML_HEAD_DIM]
        v_aug = jnp.concatenate([v_ref[0, :, lo:lo + ML_HEAD_DIM], ones_v], axis=1)
        g_row = gt[hd:hd + 1, :] * LOG2_E - b_t[fg:fg + 1, :]
        gtot = jnp.sum(lf_t[fg:fg + 1, :], axis=1, keepdims=True)
        b_col = jnp.broadcast_to(b_c[:, fg:fg + 1], (L, LANES))

        m_prev = m_scr[hd]
        c_prev = c_scr[hd]

        g_mask = jnp.where(causal, g_row, NEG_BIG)
        gmax = jnp.max(g_mask, axis=1, keepdims=True)
        h_col = jnp.maximum(m_prev, jnp.broadcast_to(gmax, (L, LANES)))
        w_inter = jnp.exp2(m_prev - h_col)
        sw = jnp.exp2(g_mask - widen(h_col)) * _dot_nt(q, k)
        out = (jnp.concatenate([w_inter, w_inter], axis=1) * _dot(q, c_prev.astype(BF16))
               + _dot(sw.astype(BF16), v_aug))
        den = jnp.maximum(jnp.abs(out[:, ML_HEAD_DIM:]), jnp.exp2(-(b_col + h_col)))
        hc = out[:, :ML_HEAD_DIM] / den

        og = jax.nn.sigmoid(o_ref[0, :, lo:lo + ML_HEAD_DIM].astype(F32))
        hg = og * hc
        ms = _lane_sum_mxu(hg * hg) * (1.0 / ML_HEAD_DIM)
        h_ref[0, :, lo:lo + ML_HEAD_DIM] = (
            hg * lax.rsqrt(ms + EPS) * ng_ref[:, lo:lo + ML_HEAD_DIM]).astype(BF16)

        m_prev1 = m_prev[:, 0:1]
        a_row = gtot + g_row
        m_loc = jnp.max(a_row, axis=1, keepdims=True)
        w_loc = jnp.exp2(a_row - m_loc)
        m_new = jnp.maximum(gtot + m_prev1, m_loc)
        s_old = jnp.exp2(gtot + m_prev1 - m_new)
        s_new = jnp.exp2(m_loc - m_new)
        kw_t = (k.T.astype(F32) * w_loc).astype(BF16)
        c_scr[hd] = s_old * c_prev + s_new * _dot(kw_t, v_aug)
        m_scr[hd] = jnp.broadcast_to(m_new, (1, LANES))


def _mlstm(q, k, v, o, gates_t, gates_c, norm_g, *, L):
    B, S, W = q.shape
    L = min(L, S)
    kern = functools.partial(_mlstm_kernel, L=L)
    tok = pl.BlockSpec((1, L, W), lambda b, c: (b, c, 0))
    return pl.pallas_call(
        kern,
        grid=(B, S // L),
        in_specs=[tok, tok, tok, tok,
                  pl.BlockSpec((1, 2 * ML_HEADS, L), lambda b, c: (b, 0, c)),
                  pl.BlockSpec((1, L, LANES), lambda b, c: (b, c, 0)),
                  _const_spec((1, W))],
        out_specs=tok,
        out_shape=jax.ShapeDtypeStruct((B, S, W), BF16),
        scratch_shapes=[pltpu.VMEM((ML_HEADS, ML_HEAD_DIM, 2 * ML_HEAD_DIM), F32),
                        pltpu.VMEM((ML_HEADS, 1, LANES), F32)],
        compiler_params=_params("arbitrary", "arbitrary"),
        name="mlstm",
    )(q, k, v, o, gates_t, gates_c, norm_g.reshape(1, W))


def _mla_kernel(q_ref, k_ref, vt_ref, o_ref, s00, s01, s10, s11, *, tq):
    qi = pl.program_id(2)
    heads = 2
    pieces = 2
    rows = tq // pieces
    s_scr = ((s00, s01), (s10, s11))
    qs = [q_ref[0, :, hh * LANES:(hh + 1) * LANES] for hh in range(heads)]
    ones_rows = jnp.ones((BF16_ROWS, tq), BF16)

    def score_piece(j, slot, hh, pc):
        start = pl.multiple_of(j * tq + pc * rows, rows)
        kb = k_ref[0, pl.ds(start, rows), hh * LANES:(hh + 1) * LANES]
        s_scr[slot][hh][pc * rows:(pc + 1) * rows, :] = _dot_nt(kb, qs[hh])

    def scores(j, slot):
        for hh in range(heads):
            for pc in range(pieces):
                score_piece(j, slot, hh, pc)

    def stage_max(slot, hh, m, masked):
        s = s_scr[slot][hh][...]
        if masked:
            kk = lax.broadcasted_iota(jnp.int32, s.shape, 0)
            qq = lax.broadcasted_iota(jnp.int32, s.shape, 1)
            s = jnp.where(kk <= qq, s, NEG_BIG)
        m_new = jnp.maximum(m, jnp.max(s, axis=0, keepdims=True))
        return s, m_new

    def stage_exp(j, hh, s, m_new, acc, half):
        hk = tq // 2
        start = pl.multiple_of(j * tq + half * hk, hk)
        vtb = jnp.concatenate([vt_ref[0, hh * V_DIM:(hh + 1) * V_DIM, pl.ds(start, hk)],
                               ones_rows[:, :hk]], axis=0)
        p = jnp.exp2(s[half * hk:(half + 1) * hk, :] - m_new).astype(BF16)
        return acc + _dot(vtb, p)

    def block(j, slot, carry, masked, nxt=None):
        def piece(hh, pc):
            if nxt is not None:
                score_piece(nxt[0], nxt[1], hh, pc)

        new = []
        for hh in range(heads):
            m, acc = carry[hh]
            piece(hh, 0)
            s, m_new = stage_max(slot, hh, m, masked)
            piece(hh, 1)
            acc = stage_exp(j, hh, s, m_new, jnp.exp2(m - m_new) * acc, 0)
            acc = stage_exp(j, hh, s, m_new, acc, 1)
            new.append((m_new, acc))
        return tuple(new)

    init = tuple((jnp.full((1, tq), NEG_BIG, F32), jnp.zeros((V_DIM + BF16_ROWS, tq), F32))
                 for _ in range(heads))
    scores(0, 0)

    def body(i, c):
        c = block(2 * i, 0, c, False, nxt=(2 * i + 1, 1))
        return block(2 * i + 1, 1, c, False, nxt=(2 * i + 2, 0))

    carry = lax.fori_loop(0, qi // 2, body, init)

    def finish(c):
        ot = jnp.concatenate([acc[:V_DIM] / acc[V_DIM:V_DIM + 1] for _, acc in c], axis=0)
        o_ref[0] = ot.T.astype(BF16)

    @pl.when(qi % 2 == 0)
    def _():
        finish(block(qi, 0, carry, True))

    @pl.when(qi % 2 == 1)
    def _():
        finish(block(qi, 1, block(qi - 1, 0, carry, False, nxt=(qi, 1)), True))


def _mla_attention(qp, kp, vt, *, tq):
    B, S, _ = qp.shape
    tq = min(tq, S)
    pairs = MLA_HEADS // 2
    kern = functools.partial(_mla_kernel, tq=tq)
    return pl.pallas_call(
        kern,
        grid=(B, pairs, S // tq),
        in_specs=[
            pl.BlockSpec((1, tq, 2 * LANES), lambda b, p, i: (b, i, p)),
            pl.BlockSpec((1, S, 2 * LANES), lambda b, p, i: (b, 0, p)),
            pl.BlockSpec((1, 2 * V_DIM, S), lambda b, p, i: (b, p, 0)),
        ],
        out_specs=pl.BlockSpec((1, tq, LANES), lambda b, p, i: (b, i, p)),
        out_shape=jax.ShapeDtypeStruct((B, S, MLA_WIDTH), BF16),
        scratch_shapes=[pltpu.VMEM((tq, tq), F32) for _ in range(4)],
        compiler_params=_params("arbitrary", "arbitrary", "arbitrary"),
        name="mla_attention",
    )(qp, kp, vt)


def _odd_out_kernel(x_ref, hm_ref, ha_ref, wout_ref, xg_ref, wq_ref, kv_ref, wo_ref, o_ref):
    mix = jnp.concatenate([hm_ref[0], ha_ref[0]], axis=-1)
    x1 = x_ref[0] + _dot(mix, wout_ref[...])
    o_ref[0] = _xattn_body(x1, xg_ref, wq_ref, kv_ref, wo_ref)


def _odd_out(x, hm, ha, w_out, xg, wq_all, kv, wo_all, layer, *, ts):
    B, S, D = x.shape
    ts = min(ts, S)
    tok = lambda n: pl.BlockSpec((1, ts, n), lambda b, s: (b, s, 0))
    return pl.pallas_call(
        _odd_out_kernel,
        grid=(B, S // ts),
        in_specs=[tok(D), tok(ML_WIDTH), tok(MLA_WIDTH), _const_spec(w_out.shape)]
        + _xattn_specs(D, wq_all, kv, wo_all, layer),
        out_specs=tok(D),
        out_shape=jax.ShapeDtypeStruct(x.shape, F32),
        compiler_params=_params("arbitrary", "arbitrary"),
        name="odd_out",
    )(x, hm, ha, w_out, xg.reshape(1, D), wq_all, kv, wo_all)


def _pack_odd_weights(w_in, w_uq, w_ukv):
    D = w_in.shape[0]
    c = 4 * ML_WIDTH
    w_g = w_in[:, c:c + 2 * ML_HEADS]
    c += 2 * ML_HEADS
    w_cq = w_in[:, c:c + Q_LORA]
    c += Q_LORA
    w_ckv = w_in[:, c:c + KV_LORA]
    c += KV_LORA
    w_kr = w_in[:, c:c + QK_ROPE]
    half = QK_ROPE // 2
    w_kr_sw = jnp.concatenate([w_kr[:, half:], w_kr[:, :half]], axis=1)
    zl = jnp.zeros((D, QK_NOPE), w_in.dtype)
    zr = jnp.zeros((D, LANES - QK_NOPE - QK_ROPE), w_in.dtype)
    zg = jnp.zeros((D, LANES - 2 * ML_HEADS), w_in.dtype)
    w_main = jnp.concatenate([w_in[:, :4 * ML_WIDTH], w_cq, w_ckv,
                              zl, w_kr, zr, zl, w_kr_sw, zr, w_g, zg], axis=1).astype(BF16)
    w_gt = w_g.T.astype(BF16)

    uq = w_uq.reshape(Q_LORA, MLA_HEADS, QK_NOPE + QK_ROPE)
    uq_n, uq_r = uq[..., :QK_NOPE], uq[..., QK_NOPE:]
    uq_r_sw = jnp.concatenate([uq_r[..., half:], uq_r[..., :half]], axis=-1)
    zpad = jnp.zeros((Q_LORA, MLA_HEADS, LANES - QK_NOPE - QK_ROPE), w_uq.dtype)
    w_uq_p = jnp.concatenate([uq_n, uq_r, zpad], axis=-1).reshape(Q_LORA, MLA_HEADS * LANES)
    w_uq_s = jnp.concatenate([jnp.zeros_like(uq_n), uq_r_sw, zpad], axis=-1)
    w_uq_s = w_uq_s.reshape(Q_LORA, MLA_HEADS * LANES)

    ukv = w_ukv.reshape(KV_LORA, MLA_HEADS, QK_NOPE + V_DIM)
    uk, uv = ukv[..., :QK_NOPE], ukv[..., QK_NOPE:]
    w_uk_p = jnp.concatenate([uk, jnp.zeros((KV_LORA, MLA_HEADS, LANES - QK_NOPE), w_ukv.dtype)],
                             axis=-1).reshape(KV_LORA, MLA_HEADS * LANES)
    w_uvt = uv.reshape(KV_LORA, MLA_WIDTH).T
    return (w_main, w_gt, w_uq_p.astype(BF16), w_uq_s.astype(BF16), w_uk_p.astype(BF16),
            w_uvt.astype(BF16))


def _rope_inv_freq():
    half = QK_ROPE // 2
    inv = ROPE_THETA ** (-jnp.arange(half, dtype=F32) / half)
    blk = jnp.concatenate([jnp.zeros((QK_NOPE,), F32), inv, inv,
                           jnp.zeros((LANES - QK_NOPE - QK_ROPE,), F32)])
    return blk.reshape(1, LANES)


TS_EVEN = 1024
TM_FFN = 1024
TH_FFN = 256
TS_ODD = 512
L_MLSTM = 256
TQ_MLA = 512
TS_OUT = 1024


def kernel(x, mem, positions, norm_mix_g, norm_xattn_g, mem_norm_g, xattn_wq, xattn_wkv, xattn_wo,
           norm_ffn_g, ffn_w_gate_up, ffn_w_down, ev_w_in, ev_conv_w, ev_pool_w, ev_pool_scale,
           ev_w_out, od_w_in, od_gate_bias, od_ml_norm_g, od_q_norm_g, od_kv_norm_g, od_w_uq,
           od_w_ukv, od_w_out, final_norm_g):
    depth = norm_mix_g.shape[0]
    B, S, D = x.shape
    pos3 = positions.reshape(B, S, 1)
    invf = _rope_inv_freq()
    wq_all, wkv_all, wo_all = (w.astype(BF16) for w in (xattn_wq, xattn_wkv, xattn_wo))
    w_gu_all, w_d_all = ffn_w_gate_up.astype(BF16), ffn_w_down.astype(BF16)
    for layer in range(depth):
        kv = _mem_kv(mem, mem_norm_g[layer], wkv_all, layer)
        xa = (norm_xattn_g[layer], wq_all, kv, wo_all, layer)
        if layer % 2 == 0:
            e = layer // 2
            x = _even_mixer(x, norm_mix_g[layer], ev_w_in[e].astype(BF16), ev_conv_w[e],
                            ev_pool_w[e].astype(BF16), ev_pool_scale[e], ev_w_out[e].astype(BF16),
                            *xa, ts=TS_EVEN)
        else:
            o = layer // 2
            w_main, w_gt, w_uq_p, w_uq_s, w_uk_p, w_uvt = _pack_odd_weights(
                od_w_in[o], od_w_uq[o], od_w_ukv[o])
            gbias_row = jnp.pad(od_gate_bias[o], (0, LANES - 2 * ML_HEADS)).reshape(1, LANES)
            qm, km, vm, om, gates_t, gates_c, qp, kp, vt = _odd_proj(
                x, pos3, norm_mix_g[layer], w_main, w_gt, od_gate_bias[o].reshape(2 * ML_HEADS, 1),
                gbias_row, od_q_norm_g[o].reshape(1, Q_LORA), od_kv_norm_g[o].reshape(1, KV_LORA),
                w_uq_p, w_uq_s, w_uk_p, w_uvt, invf, ts=TS_ODD)
            hm = _mlstm(qm, km, vm, om, gates_t, gates_c, od_ml_norm_g[o], L=L_MLSTM)
            ha = _mla_attention(qp, kp, vt, tq=TQ_MLA)
            x = _odd_out(x, hm, ha, od_w_out[o].astype(BF16), *xa, ts=TS_OUT)
        x = _swiglu(x, norm_ffn_g[layer], w_gu_all, w_d_all, final_norm_g, layer,
                    tm=TM_FFN, th=TH_FFN, final_norm=(layer == depth - 1))
    return x
```

```python
import functools

import jax
import jax.numpy as jnp
from jax import lax
from jax.experimental import pallas as pl
from jax.experimental.pallas import tpu as pltpu

F32 = jnp.float32
BF16 = jnp.bfloat16

EPS = 1e-6
ROPE_THETA = 10000.0
LANES = 128
SUBLANES = 8
BF16_ROWS = 16
VMEM_LIMIT_BYTES = 56 * 1024 * 1024
NEG_BIG = -1e30
LOG2_E = 1.4426950408889634

SC_WIDTH = 512
CONV_K = 3
POOL_WINDOWS = (2, 4, 8, 16)
POOL_GROUP = 128
HALO = 16
ML_HEADS = 4
ML_HEAD_DIM = 128
ML_WIDTH = ML_HEADS * ML_HEAD_DIM
MLA_HEADS = 8
QK_NOPE = 64
QK_ROPE = 32
V_DIM = 64
Q_LORA = 384
KV_LORA = 256
MLA_WIDTH = MLA_HEADS * V_DIM
XA_HEADS = 4
XA_HEAD_DIM = 128
XA_WIDTH = XA_HEADS * XA_HEAD_DIM


def _params(*sem):
    return pltpu.CompilerParams(dimension_semantics=sem, vmem_limit_bytes=VMEM_LIMIT_BYTES)


def _const_spec(shape):
    nd = len(shape)
    return pl.BlockSpec(shape, lambda *_: (0,) * nd, pipeline_mode=pl.Buffered(1))


def _layer_spec(stacked, layer):
    return pl.BlockSpec((None,) + stacked.shape[1:], lambda *_: (layer, 0, 0),
                        pipeline_mode=pl.Buffered(1))


def _rms(x, g):
    ms = jnp.mean(x * x, axis=-1, keepdims=True)
    return x * lax.rsqrt(ms + EPS) * g


def _dot(a, b):
    return jnp.dot(a, b, preferred_element_type=F32)


def _dot_nt(a, b):
    return lax.dot_general(a, b, (((1,), (1,)), ((), ())), preferred_element_type=F32)


def _even_kernel(x_ref, g_ref, win_ref, convw_ref, poolw_ref, pscale_ref, wout_ref,
                 xg_ref, wq_ref, kv_ref, wo_ref, o_ref, ubuf, xbbuf, *, ts):
    si = pl.program_id(1)

    @pl.when(si == 0)
    def _():
        ubuf[0:HALO, :] = jnp.zeros((HALO, SC_WIDTH), F32)
        xbbuf[0:HALO, :] = jnp.zeros((HALO, SC_WIDTH), F32)

    x = x_ref[0]
    h = _rms(x, g_ref[...]).astype(BF16)
    z = _dot(h, win_ref[...])
    g_b = z[:, 0:SC_WIDTH]
    u = z[:, SC_WIDTH:2 * SC_WIDTH] * z[:, 2 * SC_WIDTH:3 * SC_WIDTH]
    xb = z[:, 3 * SC_WIDTH:4 * SC_WIDTH]
    ubuf[HALO:HALO + ts, :] = u
    xbbuf[HALO:HALO + ts, :] = xb

    cw = convw_ref[...]
    conv = cw[2:3, :] * u
    for k in range(CONV_K - 1):
        back = CONV_K - 1 - k
        conv = conv + cw[k:k + 1, :] * ubuf[HALO - back:HALO - back + ts, :]
    ya = g_b * conv

    t_idx = si * ts + lax.broadcasted_iota(jnp.int32, (ts, POOL_GROUP), 0)
    ys = [ya.astype(BF16)]
    for gi, w in enumerate(POOL_WINDOWS):
        lo = gi * POOL_GROUP
        xg = xb[:, lo:lo + POOL_GROUP]
        acc = xg
        for j in range(1, w):
            acc = acc + xbbuf[HALO - j:HALO - j + ts, lo:lo + POOL_GROUP]
        cnt = jnp.minimum(t_idx + 1, w).astype(F32)
        d = (acc / cnt - xg).astype(BF16)
        yg = _dot(d, poolw_ref[gi]) * pscale_ref[:, lo:lo + POOL_GROUP]
        ys.append(yg.astype(BF16))

    ubuf[0:HALO, :] = ubuf[ts:ts + HALO, :]
    xbbuf[0:HALO, :] = xbbuf[ts:ts + HALO, :]

    mix = jnp.concatenate(ys, axis=-1)
    x1 = x + _dot(mix, wout_ref[...])
    o_ref[0] = _xattn_body(x1, xg_ref, wq_ref, kv_ref, wo_ref)


def _even_mixer(x, g, w_in, conv_w, pool_w, pool_scale, w_out, xg, wq_all, kv, wo_all, layer, *, ts):
    B, S, D = x.shape
    ts = min(ts, S)
    kern = functools.partial(_even_kernel, ts=ts)
    return pl.pallas_call(
        kern,
        grid=(B, S // ts),
        in_specs=[
            pl.BlockSpec((1, ts, D), lambda b, s: (b, s, 0)),
            _const_spec((1, D)),
            _const_spec(w_in.shape),
            _const_spec(conv_w.shape),
            _const_spec(pool_w.shape),
            _const_spec((1, SC_WIDTH)),
            _const_spec(w_out.shape),
        ] + _xattn_specs(D, wq_all, kv, wo_all, layer),
        out_specs=pl.BlockSpec((1, ts, D), lambda b, s: (b, s, 0)),
        out_shape=jax.ShapeDtypeStruct(x.shape, F32),
        scratch_shapes=[pltpu.VMEM((HALO + ts, SC_WIDTH), F32),
                        pltpu.VMEM((HALO + ts, SC_WIDTH), F32)],
        compiler_params=_params("arbitrary", "arbitrary"),
        name="even_mixer",
    )(x, g.reshape(1, D), w_in, conv_w, pool_w, pool_scale.reshape(1, SC_WIDTH), w_out,
      xg.reshape(1, D), wq_all, kv, wo_all)


def _memkv_kernel(mem_ref, g_ref, wkv_ref, kv_ref):
    h = _rms(mem_ref[0], g_ref[...]).astype(BF16)
    kv_ref[0] = _dot(h, wkv_ref[...]).astype(BF16)


def _mem_kv(mem, g, wkv_all, layer):
    B, M, D = mem.shape
    N = wkv_all.shape[2]
    return pl.pallas_call(
        _memkv_kernel,
        grid=(B,),
        in_specs=[pl.BlockSpec((1, M, D), lambda b: (b, 0, 0)),
                  _const_spec((1, D)), _layer_spec(wkv_all, layer)],
        out_specs=pl.BlockSpec((1, M, N), lambda b: (b, 0, 0)),
        out_shape=jax.ShapeDtypeStruct((B, M, N), BF16),
        compiler_params=_params("arbitrary"),
        name="mem_kv",
    )(mem, g.reshape(1, D), wkv_all)


def _xattn_body(x, g_ref, wq_ref, kv_ref, wo_ref):
    h = _rms(x, g_ref[...]).astype(BF16)
    q = (_dot(h, wq_ref[...]) * (XA_HEAD_DIM ** -0.5)).astype(BF16)
    kv = kv_ref[0]
    ones_v = jnp.ones((kv.shape[0], XA_HEAD_DIM), BF16)
    outs = []
    for hd in range(XA_HEADS):
        lo = hd * XA_HEAD_DIM
        s = _dot_nt(q[:, lo:lo + XA_HEAD_DIM], kv[:, lo:lo + XA_HEAD_DIM])
        p = jnp.exp(s - jnp.max(s, axis=-1, keepdims=True)).astype(BF16)
        v_aug = jnp.concatenate([kv[:, XA_WIDTH + lo:XA_WIDTH + lo + XA_HEAD_DIM], ones_v], axis=1)
        o = _dot(p, v_aug)
        outs.append((o[:, :XA_HEAD_DIM] / o[:, XA_HEAD_DIM:]).astype(BF16))
    a = jnp.concatenate(outs, axis=-1)
    return x + _dot(a, wo_ref[...])


def _xattn_specs(D, wq_all, kv, wo_all, layer):
    M = kv.shape[1]
    return [_const_spec((1, D)), _layer_spec(wq_all, layer),
            pl.BlockSpec((1, M, kv.shape[2]), lambda b, s: (b, 0, 0)), _layer_spec(wo_all, layer)]


def _swiglu_kernel(x_ref, g_ref, wgu_ref, wd_ref, fg_ref, o_ref, *, hidden, th, final_norm):
    x = x_ref[...]
    h = _rms(x, g_ref[...]).astype(BF16)
    acc = x
    for j in range(hidden // th):
        gate = _dot(h, wgu_ref[:, j * th:(j + 1) * th])
        up = _dot(h, wgu_ref[:, hidden + j * th:hidden + (j + 1) * th])
        act = (gate * jax.nn.sigmoid(gate) * up).astype(BF16)
        acc = acc + _dot(act, wd_ref[j * th:(j + 1) * th, :])
    if final_norm:
        acc = _rms(acc, fg_ref[...])
    o_ref[...] = acc


def _swiglu(x, g, w_gu_all, w_d_all, final_g, layer, *, tm, th, final_norm):
    B, S, D = x.shape
    T = B * S
    tm = min(tm, T)
    hidden = w_d_all.shape[1]
    kern = functools.partial(_swiglu_kernel, hidden=hidden, th=th, final_norm=final_norm)
    out = pl.pallas_call(
        kern,
        grid=(T // tm,),
        in_specs=[
            pl.BlockSpec((tm, D), lambda i: (i, 0)),
            _const_spec((1, D)),
            _layer_spec(w_gu_all, layer),
            _layer_spec(w_d_all, layer),
            _const_spec((1, D)),
        ],
        out_specs=pl.BlockSpec((tm, D), lambda i: (i, 0)),
        out_shape=jax.ShapeDtypeStruct((T, D), F32),
        compiler_params=_params("arbitrary"),
        name="swiglu",
    )(x.reshape(T, D), g.reshape(1, D), w_gu_all, w_d_all, final_g.reshape(1, D))
    return out.reshape(B, S, D)


def _odd_proj_kernel(x_ref, pos_ref, g_ref, wmain_ref, wgt_ref, gbias_ref, gbias_row_ref,
                     qg_ref, kvg_ref, wuq_ref, wuqs_ref, wuk_ref, wuvt_ref, invf_ref,
                     qm_ref, km_ref, vm_ref, om_ref, gates_ref, gates_c_ref, qp_ref, kp_ref, vt_ref):
    x = x_ref[0]
    h = _rms(x, g_ref[...]).astype(BF16)
    c0 = 4 * ML_WIDTH

    def ml_cols(i):
        return _dot(h, wmain_ref[:, i * ML_WIDTH:(i + 1) * ML_WIDTH])

    zl = _dot(h, wmain_ref[:, c0:])
    c_q = zl[:, 0:Q_LORA]
    c_kv = zl[:, Q_LORA:Q_LORA + KV_LORA]
    kr = zl[:, Q_LORA + KV_LORA:Q_LORA + KV_LORA + LANES]
    kr_sw = zl[:, Q_LORA + KV_LORA + LANES:Q_LORA + KV_LORA + 2 * LANES]
    c_g = Q_LORA + KV_LORA + 2 * LANES
    gates_c_ref[0] = zl[:, c_g:c_g + LANES] + gbias_row_ref[...]
    gates_ref[0] = _dot_nt(wgt_ref[...], h) + gbias_ref[...]

    half = QK_ROPE // 2
    ang_t = invf_ref[...] * pos_ref[0].astype(F32)
    cos_c, sin_c = jnp.cos(ang_t), jnp.sin(ang_t)
    ts = ang_t.shape[1]
    pad_lo = jnp.zeros((QK_NOPE, ts), F32)
    pad_hi = jnp.zeros((LANES - QK_NOPE - QK_ROPE, ts), F32)
    cos_t = jnp.concatenate([pad_lo + 1.0, cos_c, cos_c, pad_hi], axis=0).T
    sin_t = jnp.concatenate([pad_lo, -sin_c, sin_c, pad_hi], axis=0).T
    scale = (QK_NOPE + QK_ROPE) ** -0.5 * LOG2_E
    cos_q, sin_q = cos_t * scale, sin_t * scale
    qm_ref[0] = ml_cols(0).astype(BF16)
    km_ref[0] = (ml_cols(1) * (ML_HEAD_DIM ** -0.5)).astype(BF16)

    cqn = _rms(c_q, qg_ref[...]).astype(BF16)
    qa = _dot(cqn, wuq_ref[...])
    qb = _dot(cqn, wuqs_ref[...])
    ckn = _rms(c_kv, kvg_ref[...]).astype(BF16)
    kn = _dot(ckn, wuk_ref[...])
    vt_ref[0] = _dot_nt(wuvt_ref[...], ckn).astype(BF16)
    k_rope = kr * cos_t + kr_sw * sin_t

    def rope_heads(first, last):
        for hd in range(first, last):
            lo = hd * LANES
            qp_ref[0, :, lo:lo + LANES] = (qa[:, lo:lo + LANES] * cos_q
                                           + qb[:, lo:lo + LANES] * sin_q).astype(BF16)
            kp_ref[0, :, lo:lo + LANES] = (kn[:, lo:lo + LANES] + k_rope).astype(BF16)

    vm_ref[0] = ml_cols(2).astype(BF16)
    rope_heads(0, MLA_HEADS // 2)
    om_ref[0] = ml_cols(3).astype(BF16)
    rope_heads(MLA_HEADS // 2, MLA_HEADS)


def _odd_proj(x, pos3, g, w_main, w_gt, gbias, gbias_row, q_g, kv_g, w_uq_p, w_uq_s, w_uk_p, w_uvt,
              invf, *, ts):
    B, S, D = x.shape
    ts = min(ts, S)
    tok = lambda n: pl.BlockSpec((1, ts, n), lambda b, s: (b, s, 0))
    out_shapes = (
        jax.ShapeDtypeStruct((B, S, ML_WIDTH), BF16),
        jax.ShapeDtypeStruct((B, S, ML_WIDTH), BF16),
        jax.ShapeDtypeStruct((B, S, ML_WIDTH), BF16),
        jax.ShapeDtypeStruct((B, S, ML_WIDTH), BF16),
        jax.ShapeDtypeStruct((B, 2 * ML_HEADS, S), F32),
        jax.ShapeDtypeStruct((B, S, LANES), F32),
        jax.ShapeDtypeStruct((B, S, MLA_HEADS * LANES), BF16),
        jax.ShapeDtypeStruct((B, S, MLA_HEADS * LANES), BF16),
        jax.ShapeDtypeStruct((B, MLA_WIDTH, S), BF16),
    )
    return pl.pallas_call(
        _odd_proj_kernel,
        grid=(B, S // ts),
        in_specs=[
            tok(D),
            pl.BlockSpec((1, 1, ts), lambda b, s: (b, 0, s)),
            _const_spec((1, D)),
            _const_spec(w_main.shape),
            _const_spec(w_gt.shape),
            _const_spec(gbias.shape),
            _const_spec(gbias_row.shape),
            _const_spec(q_g.shape),
            _const_spec(kv_g.shape),
            _const_spec(w_uq_p.shape),
            _const_spec(w_uq_s.shape),
            _const_spec(w_uk_p.shape),
            _const_spec(w_uvt.shape),
            _const_spec(invf.shape),
        ],
        out_specs=(tok(ML_WIDTH), tok(ML_WIDTH), tok(ML_WIDTH), tok(ML_WIDTH),
                   pl.BlockSpec((1, 2 * ML_HEADS, ts), lambda b, s: (b, 0, s)), tok(LANES),
                   tok(MLA_HEADS * LANES), tok(MLA_HEADS * LANES),
                   pl.BlockSpec((1, MLA_WIDTH, ts), lambda b, s: (b, 0, s))),
        out_shape=out_shapes,
        compiler_params=_params("arbitrary", "arbitrary"),
        name="odd_proj",
    )(x, pos3, g.reshape(1, D), w_main, w_gt, gbias, gbias_row, q_g, kv_g, w_uq_p, w_uq_s, w_uk_p,
      w_uvt, invf)


def _split3(a):
    hi = a.astype(BF16)
    r1 = a - hi.astype(F32)
    mid = r1.astype(BF16)
    lo = (r1 - mid.astype(F32)).astype(BF16)
    return hi, mid, lo


def _lane_sum_mxu(a):
    ones = jnp.ones((a.shape[1], LANES), BF16)
    return sum(_dot(t, ones) for t in _split3(a))


def _log2_sigmoid(x):
    return (jnp.minimum(x, 0.0) - jnp.log(1.0 + jnp.exp(-jnp.abs(x)))) * LOG2_E


def _mlstm_kernel(q_ref, k_ref, v_ref, o_ref, gt_ref, gc_ref, ng_ref, h_ref, c_scr, m_scr, *, L):
    ci = pl.program_id(1)

    @pl.when(ci == 0)
    def _():
        c_scr[...] = jnp.zeros(c_scr.shape, F32)
        m_scr[...] = jnp.zeros(m_scr.shape, F32)

    row = lax.broadcasted_iota(jnp.int32, (L, L), 0)
    col = lax.broadcasted_iota(jnp.int32, (L, L), 1)
    causal = col <= row
    gt = gt_ref[0]
    ones_v = jnp.ones((L, LANES), BF16)
    reps = L // LANES

    def widen(a):
        return jnp.concatenate([a] * reps, axis=1)

    lf_t = _log2_sigmoid(gt)
    b_t = sum(_dot(t, (row <= col).astype(BF16)) for t in _split3(lf_t))
    tril = causal.astype(BF16)
    b_c = sum(_dot(tril, t) for t in _split3(_log2_sigmoid(gc_ref[0])))

    for hd in range(ML_HEADS):
        lo = hd * ML_HEAD_DIM
        fg = ML_HEADS + hd
        q = q_ref[0, :, lo:lo + ML_HEAD_DIM]
        k = k_ref[0, :, lo:lo + ML_HEAD_DIM]
        v_aug = jnp.concatenate([v_ref[0, :, lo:lo + ML_HEAD_DIM], ones_v], axis=1)
        g_row = gt[hd:hd + 1, :] * LOG2_E - b_t[fg:fg + 1, :]
        gtot = jnp.sum(lf_t[fg:fg + 1, :], axis=1, keepdims=True)
        b_col = jnp.broadcast_to(b_c[:, fg:fg + 1], (L, LANES))

        m_prev = m_scr[hd]
        c_prev = c_scr[hd]

        g_mask = jnp.where(causal, g_row, NEG_BIG)
        gmax = jnp.max(g_mask, axis=1, keepdims=True)
        h_col = jnp.maximum(m_prev, jnp.broadcast_to(gmax, (L, LANES)))
        w_inter = jnp.exp2(m_prev - h_col)
        sw = jnp.exp2(g_mask - widen(h_col)) * _dot_nt(q, k)
        out = (jnp.concatenate([w_inter, w_inter], axis=1) * _dot(q, c_prev.astype(BF16))
               + _dot(sw.astype(BF16), v_aug))
        den = jnp.maximum(jnp.abs(out[:, ML_HEAD_DIM:]), jnp.exp2(-(b_col + h_col)))
        hc = out[:, :ML_HEAD_DIM] / den

        og = jax.nn.sigmoid(o_ref[0, :, lo:lo + ML_HEAD_DIM].astype(F32))
        hg = og * hc
        ms = _lane_sum_mxu(hg * hg) * (1.0 / ML_HEAD_DIM)
        h_ref[0, :, lo:lo + ML_HEAD_DIM] = (
            hg * lax.rsqrt(ms + EPS) * ng_ref[:, lo:lo + ML_HEAD_DIM]).astype(BF16)

        m_prev1 = m_prev[:, 0:1]
        a_row = gtot + g_row
        m_loc = jnp.max(a_row, axis=1, keepdims=True)
        w_loc = jnp.exp2(a_row - m_loc)
        m_new = jnp.maximum(gtot + m_prev1, m_loc)
        s_old = jnp.exp2(gtot + m_prev1 - m_new)
        s_new = jnp.exp2(m_loc - m_new)
        kw_t = (k.T.astype(F32) * w_loc).astype(BF16)
        c_scr[hd] = s_old * c_prev + s_new * _dot(kw_t, v_aug)
        m_scr[hd] = jnp.broadcast_to(m_new, (1, LANES))


def _mlstm(q, k, v, o, gates_t, gates_c, norm_g, *, L):
    B, S, W = q.shape
    L = min(L, S)
    kern = functools.partial(_mlstm_kernel, L=L)
    tok = pl.BlockSpec((1, L, W), lambda b, c: (b, c, 0))
    return pl.pallas_call(
        kern,
        grid=(B, S // L),
        in_specs=[tok, tok, tok, tok,
                  pl.BlockSpec((1, 2 * ML_HEADS, L), lambda b, c: (b, 0, c)),
                  pl.BlockSpec((1, L, LANES), lambda b, c: (b, c, 0)),
                  _const_spec((1, W))],
        out_specs=tok,
        out_shape=jax.ShapeDtypeStruct((B, S, W), BF16),
        scratch_shapes=[pltpu.VMEM((ML_HEADS, ML_HEAD_DIM, 2 * ML_HEAD_DIM), F32),
                        pltpu.VMEM((ML_HEADS, 1, LANES), F32)],
        compiler_params=_params("arbitrary", "arbitrary"),
        name="mlstm",
    )(q, k, v, o, gates_t, gates_c, norm_g.reshape(1, W))


def _mla_kernel(q_ref, k_ref, vt_ref, o_ref, s00, s01, s10, s11, *, tq):
    qi = pl.program_id(2)
    heads = 2
    pieces = 2
    rows = tq // pieces
    s_scr = ((s00, s01), (s10, s11))
    qs = [q_ref[0, :, hh * LANES:(hh + 1) * LANES] for hh in range(heads)]
    ones_rows = jnp.ones((BF16_ROWS, tq), BF16)

    def score_piece(j, slot, hh, pc):
        start = pl.multiple_of(j * tq + pc * rows, rows)
        kb = k_ref[0, pl.ds(start, rows), hh * LANES:(hh + 1) * LANES]
        s_scr[slot][hh][pc * rows:(pc + 1) * rows, :] = _dot_nt(kb, qs[hh])

    def scores(j, slot):
        for hh in range(heads):
            for pc in range(pieces):
                score_piece(j, slot, hh, pc)

    def stage_max(slot, hh, m, masked):
        s = s_scr[slot][hh][...]
        if masked:
            kk = lax.broadcasted_iota(jnp.int32, s.shape, 0)
            qq = lax.broadcasted_iota(jnp.int32, s.shape, 1)
            s = jnp.where(kk <= qq, s, NEG_BIG)
        m_new = jnp.maximum(m, jnp.max(s, axis=0, keepdims=True))
        return s, m_new

    def stage_exp(j, hh, s, m_new, acc, half):
        hk = tq // 2
        start = pl.multiple_of(j * tq + half * hk, hk)
        vtb = jnp.concatenate([vt_ref[0, hh * V_DIM:(hh + 1) * V_DIM, pl.ds(start, hk)],
                               ones_rows[:, :hk]], axis=0)
        p = jnp.exp2(s[half * hk:(half + 1) * hk, :] - m_new).astype(BF16)
        return acc + _dot(vtb, p)

    def block(j, slot, carry, masked, nxt=None):
        def piece(hh, pc):
            if nxt is not None:
                score_piece(nxt[0], nxt[1], hh, pc)

        new = []
        for hh in range(heads):
            m, acc = carry[hh]
            piece(hh, 0)
            s, m_new = stage_max(slot, hh, m, masked)
            piece(hh, 1)
            acc = stage_exp(j, hh, s, m_new, jnp.exp2(m - m_new) * acc, 0)
            acc = stage_exp(j, hh, s, m_new, acc, 1)
            new.append((m_new, acc))
        return tuple(new)

    init = tuple((jnp.full((1, tq), NEG_BIG, F32), jnp.zeros((V_DIM + BF16_ROWS, tq), F32))
                 for _ in range(heads))
    scores(0, 0)

    def body(i, c):
        c = block(2 * i, 0, c, False, nxt=(2 * i + 1, 1))
        return block(2 * i + 1, 1, c, False, nxt=(2 * i + 2, 0))

    carry = lax.fori_loop(0, qi // 2, body, init)

    def finish(c):
        ot = jnp.concatenate([acc[:V_DIM] / acc[V_DIM:V_DIM + 1] for _, acc in c], axis=0)
        o_ref[0] = ot.T.astype(BF16)

    @pl.when(qi % 2 == 0)
    def _():
        finish(block(qi, 0, carry, True))

    @pl.when(qi % 2 == 1)
    def _():
        finish(block(qi, 1, block(qi - 1, 0, carry, False, nxt=(qi, 1)), True))


def _mla_attention(qp, kp, vt, *, tq):
    B, S, _ = qp.shape
    tq = min(tq, S)
    pairs = MLA_HEADS // 2
    kern = functools.partial(_mla_kernel, tq=tq)
    return pl.pallas_call(
        kern,
        grid=(B, pairs, S // tq),
        in_specs=[
            pl.BlockSpec((1, tq, 2 * LANES), lambda b, p, i: (b, i, p)),
            pl.BlockSpec((1, S, 2 * LANES), lambda b, p, i: (b, 0, p)),
            pl.BlockSpec((1, 2 * V_DIM, S), lambda b, p, i: (b, p, 0)),
        ],
        out_specs=pl.BlockSpec((1, tq, LANES), lambda b, p, i: (b, i, p)),
        out_shape=jax.ShapeDtypeStruct((B, S, MLA_WIDTH), BF16),
        scratch_shapes=[pltpu.VMEM((tq, tq), F32) for _ in range(4)],
        compiler_params=_params("arbitrary", "arbitrary", "arbitrary"),
        name="mla_attention",
    )(qp, kp, vt)


def _odd_out_kernel(x_ref, hm_ref, ha_ref, wout_ref, xg_ref, wq_ref, kv_ref, wo_ref, o_ref):
    mix = jnp.concatenate([hm_ref[0], ha_ref[0]], axis=-1)
    x1 = x_ref[0] + _dot(mix, wout_ref[...])
    o_ref[0] = _xattn_body(x1, xg_ref, wq_ref, kv_ref, wo_ref)


def _odd_out(x, hm, ha, w_out, xg, wq_all, kv, wo_all, layer, *, ts):
    B, S, D = x.shape
    ts = min(ts, S)
    tok = lambda n: pl.BlockSpec((1, ts, n), lambda b, s: (b, s, 0))
    return pl.pallas_call(
        _odd_out_kernel,
        grid=(B, S // ts),
        in_specs=[tok(D), tok(ML_WIDTH), tok(MLA_WIDTH), _const_spec(w_out.shape)]
        + _xattn_specs(D, wq_all, kv, wo_all, layer),
        out_specs=tok(D),
        out_shape=jax.ShapeDtypeStruct(x.shape, F32),
        compiler_params=_params("arbitrary", "arbitrary"),
        name="odd_out",
    )(x, hm, ha, w_out, xg.reshape(1, D), wq_all, kv, wo_all)


def _pack_odd_weights(w_in, w_uq, w_ukv):
    D = w_in.shape[0]
    c = 4 * ML_WIDTH
    w_g = w_in[:, c:c + 2 * ML_HEADS]
    c += 2 * ML_HEADS
    w_cq = w_in[:, c:c + Q_LORA]
    c += Q_LORA
    w_ckv = w_in[:, c:c + KV_LORA]
    c += KV_LORA
    w_kr = w_in[:, c:c + QK_ROPE]
    half = QK_ROPE // 2
    w_kr_sw = jnp.concatenate([w_kr[:, half:], w_kr[:, :half]], axis=1)
    zl = jnp.zeros((D, QK_NOPE), w_in.dtype)
    zr = jnp.zeros((D, LANES - QK_NOPE - QK_ROPE), w_in.dtype)
    zg = jnp.zeros((D, LANES - 2 * ML_HEADS), w_in.dtype)
    w_main = jnp.concatenate([w_in[:, :4 * ML_WIDTH], w_cq, w_ckv,
                              zl, w_kr, zr, zl, w_kr_sw, zr, w_g, zg], axis=1).astype(BF16)
    w_gt = w_g.T.astype(BF16)

    uq = w_uq.reshape(Q_LORA, MLA_HEADS, QK_NOPE + QK_ROPE)
    uq_n, uq_r = uq[..., :QK_NOPE], uq[..., QK_NOPE:]
    uq_r_sw = jnp.concatenate([uq_r[..., half:], uq_r[..., :half]], axis=-1)
    zpad = jnp.zeros((Q_LORA, MLA_HEADS, LANES - QK_NOPE - QK_ROPE), w_uq.dtype)
    w_uq_p = jnp.concatenate([uq_n, uq_r, zpad], axis=-1).reshape(Q_LORA, MLA_HEADS * LANES)
    w_uq_s = jnp.concatenate([jnp.zeros_like(uq_n), uq_r_sw, zpad], axis=-1)
    w_uq_s = w_uq_s.reshape(Q_LORA, MLA_HEADS * LANES)

    ukv = w_ukv.reshape(KV_LORA, MLA_HEADS, QK_NOPE + V_DIM)
    uk, uv = ukv[..., :QK_NOPE], ukv[..., QK_NOPE:]
    w_uk_p = jnp.concatenate([uk, jnp.zeros((KV_LORA, MLA_HEADS, LANES - QK_NOPE), w_ukv.dtype)],
                             axis=-1).reshape(KV_LORA, MLA_HEADS * LANES)
    w_uvt = uv.reshape(KV_LORA, MLA_WIDTH).T
    return (w_main, w_gt, w_uq_p.astype(BF16), w_uq_s.astype(BF16), w_uk_p.astype(BF16),
            w_uvt.astype(BF16))


def _rope_inv_freq():
    half = QK_ROPE // 2
    inv = ROPE_THETA ** (-jnp.arange(half, dtype=F32) / half)
    return inv.reshape(half, 1)


TS_EVEN = 1024
TM_FFN = 1024
TH_FFN = 256
TS_ODD = 512
L_MLSTM = 256
TQ_MLA = 512
TS_OUT = 1024


def kernel(x, mem, positions, norm_mix_g, norm_xattn_g, mem_norm_g, xattn_wq, xattn_wkv, xattn_wo,
           norm_ffn_g, ffn_w_gate_up, ffn_w_down, ev_w_in, ev_conv_w, ev_pool_w, ev_pool_scale,
           ev_w_out, od_w_in, od_gate_bias, od_ml_norm_g, od_q_norm_g, od_kv_norm_g, od_w_uq,
           od_w_ukv, od_w_out, final_norm_g):
    depth = norm_mix_g.shape[0]
    B, S, D = x.shape
    pos3 = positions.reshape(B, 1, S)
    invf = _rope_inv_freq()
    wq_all, wkv_all, wo_all = (w.astype(BF16) for w in (xattn_wq, xattn_wkv, xattn_wo))
    w_gu_all, w_d_all = ffn_w_gate_up.astype(BF16), ffn_w_down.astype(BF16)
    for layer in range(depth):
        kv = _mem_kv(mem, mem_norm_g[layer], wkv_all, layer)
        xa = (norm_xattn_g[layer], wq_all, kv, wo_all, layer)
        if layer % 2 == 0:
            e = layer // 2
            x = _even_mixer(x, norm_mix_g[layer], ev_w_in[e].astype(BF16), ev_conv_w[e],
                            ev_pool_w[e].astype(BF16), ev_pool_scale[e], ev_w_out[e].astype(BF16),
                            *xa, ts=TS_EVEN)
        else:
            o = layer // 2
            w_main, w_gt, w_uq_p, w_uq_s, w_uk_p, w_uvt = _pack_odd_weights(
                od_w_in[o], od_w_uq[o], od_w_ukv[o])
            gbias_row = jnp.pad(od_gate_bias[o], (0, LANES - 2 * ML_HEADS)).reshape(1, LANES)
            qm, km, vm, om, gates_t, gates_c, qp, kp, vt = _odd_proj(
                x, pos3, norm_mix_g[layer], w_main, w_gt, od_gate_bias[o].reshape(2 * ML_HEADS, 1),
                gbias_row, od_q_norm_g[o].reshape(1, Q_LORA), od_kv_norm_g[o].reshape(1, KV_LORA),
                w_uq_p, w_uq_s, w_uk_p, w_uvt, invf, ts=TS_ODD)
            hm = _mlstm(qm, km, vm, om, gates_t, gates_c, od_ml_norm_g[o], L=L_MLSTM)
            ha = _mla_attention(qp, kp, vt, tq=TQ_MLA)
            x = _odd_out(x, hm, ha, od_w_out[o].astype(BF16), *xa, ts=TS_OUT)
        x = _swiglu(x, norm_ffn_g[layer], w_gu_all, w_d_all, final_norm_g, layer,
                    tm=TM_FFN, th=TH_FFN, final_norm=(layer == depth - 1))
    return x
```

```python
import functools

import jax
import jax.numpy as jnp
from jax import lax
from jax.experimental import pallas as pl
from jax.experimental.pallas import tpu as pltpu

F32 = jnp.float32
BF16 = jnp.bfloat16

EPS = 1e-6
ROPE_THETA = 10000.0
LANES = 128
SUBLANES = 8
BF16_ROWS = 16
VMEM_LIMIT_BYTES = 56 * 1024 * 1024
NEG_BIG = -1e30
LOG2_E = 1.4426950408889634

SC_WIDTH = 512
CONV_K = 3
POOL_WINDOWS = (2, 4, 8, 16)
POOL_GROUP = 128
HALO = 16
ML_HEADS = 4
ML_HEAD_DIM = 128
ML_WIDTH = ML_HEADS * ML_HEAD_DIM
MLA_HEADS = 8
QK_NOPE = 64
QK_ROPE = 32
V_DIM = 64
Q_LORA = 384
KV_LORA = 256
MLA_WIDTH = MLA_HEADS * V_DIM
XA_HEADS = 4
XA_HEAD_DIM = 128
XA_WIDTH = XA_HEADS * XA_HEAD_DIM


def _params(*sem):
    return pltpu.CompilerParams(dimension_semantics=sem, vmem_limit_bytes=VMEM_LIMIT_BYTES)


def _const_spec(shape):
    nd = len(shape)
    return pl.BlockSpec(shape, lambda *_: (0,) * nd, pipeline_mode=pl.Buffered(1))


def _layer_spec(stacked, layer):
    return pl.BlockSpec((None,) + stacked.shape[1:], lambda *_: (layer, 0, 0),
                        pipeline_mode=pl.Buffered(1))


def _rms(x, g):
    ms = jnp.mean(x * x, axis=-1, keepdims=True)
    return x * lax.rsqrt(ms + EPS) * g


def _dot(a, b):
    return jnp.dot(a, b, preferred_element_type=F32)


def _dot_nt(a, b):
    return lax.dot_general(a, b, (((1,), (1,)), ((), ())), preferred_element_type=F32)


def _even_kernel(x_ref, g_ref, win_ref, convw_ref, poolw_ref, pscale_ref, wout_ref,
                 xg_ref, wq_ref, kv_ref, wo_ref, o_ref, ubuf, xbbuf, *, ts):
    si = pl.program_id(1)

    @pl.when(si == 0)
    def _():
        ubuf[0:HALO, :] = jnp.zeros((HALO, SC_WIDTH), F32)
        xbbuf[0:HALO, :] = jnp.zeros((HALO, SC_WIDTH), F32)

    x = x_ref[0]
    h = _rms(x, g_ref[...]).astype(BF16)
    z = _dot(h, win_ref[...])
    g_b = z[:, 0:SC_WIDTH]
    u = z[:, SC_WIDTH:2 * SC_WIDTH] * z[:, 2 * SC_WIDTH:3 * SC_WIDTH]
    xb = z[:, 3 * SC_WIDTH:4 * SC_WIDTH]
    ubuf[HALO:HALO + ts, :] = u
    xbbuf[HALO:HALO + ts, :] = xb

    cw = convw_ref[...]
    conv = cw[2:3, :] * u
    for k in range(CONV_K - 1):
        back = CONV_K - 1 - k
        conv = conv + cw[k:k + 1, :] * ubuf[HALO - back:HALO - back + ts, :]
    ya = g_b * conv

    t_idx = si * ts + lax.broadcasted_iota(jnp.int32, (ts, POOL_GROUP), 0)
    ys = [ya.astype(BF16)]
    for gi, w in enumerate(POOL_WINDOWS):
        lo = gi * POOL_GROUP
        xg = xb[:, lo:lo + POOL_GROUP]
        acc = xg
        for j in range(1, w):
            acc = acc + xbbuf[HALO - j:HALO - j + ts, lo:lo + POOL_GROUP]
        cnt = jnp.minimum(t_idx + 1, w).astype(F32)
        d = (acc / cnt - xg).astype(BF16)
        yg = _dot(d, poolw_ref[gi]) * pscale_ref[:, lo:lo + POOL_GROUP]
        ys.append(yg.astype(BF16))

    ubuf[0:HALO, :] = ubuf[ts:ts + HALO, :]
    xbbuf[0:HALO, :] = xbbuf[ts:ts + HALO, :]

    mix = jnp.concatenate(ys, axis=-1)
    x1 = x + _dot(mix, wout_ref[...])
    o_ref[0] = _xattn_body(x1, xg_ref, wq_ref, kv_ref, wo_ref)


def _even_mixer(x, g, w_in, conv_w, pool_w, pool_scale, w_out, xg, wq_all, kv, wo_all, layer, *, ts):
    B, S, D = x.shape
    ts = min(ts, S)
    kern = functools.partial(_even_kernel, ts=ts)
    return pl.pallas_call(
        kern,
        grid=(B, S // ts),
        in_specs=[
            pl.BlockSpec((1, ts, D), lambda b, s: (b, s, 0)),
            _const_spec((1, D)),
            _const_spec(w_in.shape),
            _const_spec(conv_w.shape),
            _const_spec(pool_w.shape),
            _const_spec((1, SC_WIDTH)),
            _const_spec(w_out.shape),
        ] + _xattn_specs(D, wq_all, kv, wo_all, layer),
        out_specs=pl.BlockSpec((1, ts, D), lambda b, s: (b, s, 0)),
        out_shape=jax.ShapeDtypeStruct(x.shape, F32),
        scratch_shapes=[pltpu.VMEM((HALO + ts, SC_WIDTH), F32),
                        pltpu.VMEM((HALO + ts, SC_WIDTH), F32)],
        compiler_params=_params("arbitrary", "arbitrary"),
        name="even_mixer",
    )(x, g.reshape(1, D), w_in, conv_w, pool_w, pool_scale.reshape(1, SC_WIDTH), w_out,
      xg.reshape(1, D), wq_all, kv, wo_all)


def _memkv_kernel(mem_ref, g_ref, wkv_ref, kv_ref):
    h = _rms(mem_ref[0], g_ref[...]).astype(BF16)
    kv_ref[0] = _dot(h, wkv_ref[...]).astype(BF16)


def _mem_kv(mem, g, wkv_all, layer):
    B, M, D = mem.shape
    N = wkv_all.shape[2]
    return pl.pallas_call(
        _memkv_kernel,
        grid=(B,),
        in_specs=[pl.BlockSpec((1, M, D), lambda b: (b, 0, 0)),
                  _const_spec((1, D)), _layer_spec(wkv_all, layer)],
        out_specs=pl.BlockSpec((1, M, N), lambda b: (b, 0, 0)),
        out_shape=jax.ShapeDtypeStruct((B, M, N), BF16),
        compiler_params=_params("arbitrary"),
        name="mem_kv",
    )(mem, g.reshape(1, D), wkv_all)


def _xattn_body(x, g_ref, wq_ref, kv_ref, wo_ref):
    h = _rms(x, g_ref[...]).astype(BF16)
    q = (_dot(h, wq_ref[...]) * (XA_HEAD_DIM ** -0.5)).astype(BF16)
    kv = kv_ref[0]
    ones_v = jnp.ones((kv.shape[0], XA_HEAD_DIM), BF16)
    outs = []
    for hd in range(XA_HEADS):
        lo = hd * XA_HEAD_DIM
        s = _dot_nt(q[:, lo:lo + XA_HEAD_DIM], kv[:, lo:lo + XA_HEAD_DIM])
        p = jnp.exp(s - jnp.max(s, axis=-1, keepdims=True)).astype(BF16)
        v_aug = jnp.concatenate([kv[:, XA_WIDTH + lo:XA_WIDTH + lo + XA_HEAD_DIM], ones_v], axis=1)
        o = _dot(p, v_aug)
        outs.append((o[:, :XA_HEAD_DIM] / o[:, XA_HEAD_DIM:]).astype(BF16))
    a = jnp.concatenate(outs, axis=-1)
    return x + _dot(a, wo_ref[...])


def _xattn_specs(D, wq_all, kv, wo_all, layer):
    M = kv.shape[1]
    return [_const_spec((1, D)), _layer_spec(wq_all, layer),
            pl.BlockSpec((1, M, kv.shape[2]), lambda b, s: (b, 0, 0)), _layer_spec(wo_all, layer)]


def _swiglu_kernel(x_ref, g_ref, wgu_ref, wd_ref, fg_ref, o_ref, *, hidden, th, final_norm):
    x = x_ref[...]
    h = _rms(x, g_ref[...]).astype(BF16)
    acc = x
    for j in range(hidden // th):
        gate = _dot(h, wgu_ref[:, j * th:(j + 1) * th])
        up = _dot(h, wgu_ref[:, hidden + j * th:hidden + (j + 1) * th])
        act = (gate * jax.nn.sigmoid(gate) * up).astype(BF16)
        acc = acc + _dot(act, wd_ref[j * th:(j + 1) * th, :])
    if final_norm:
        acc = _rms(acc, fg_ref[...])
    o_ref[...] = acc


def _swiglu(x, g, w_gu_all, w_d_all, final_g, layer, *, tm, th, final_norm):
    B, S, D = x.shape
    T = B * S
    tm = min(tm, T)
    hidden = w_d_all.shape[1]
    kern = functools.partial(_swiglu_kernel, hidden=hidden, th=th, final_norm=final_norm)
    out = pl.pallas_call(
        kern,
        grid=(T // tm,),
        in_specs=[
            pl.BlockSpec((tm, D), lambda i: (i, 0)),
            _const_spec((1, D)),
            _layer_spec(w_gu_all, layer),
            _layer_spec(w_d_all, layer),
            _const_spec((1, D)),
        ],
        out_specs=pl.BlockSpec((tm, D), lambda i: (i, 0)),
        out_shape=jax.ShapeDtypeStruct((T, D), F32),
        compiler_params=_params("arbitrary"),
        name="swiglu",
    )(x.reshape(T, D), g.reshape(1, D), w_gu_all, w_d_all, final_g.reshape(1, D))
    return out.reshape(B, S, D)


def _odd_proj_kernel(x_ref, pos_ref, g_ref, wmain_ref, wgt_ref, gbias_ref, gbias_row_ref,
                     qg_ref, kvg_ref, wuq_ref, wuqs_ref, wuk_ref, wuvt_ref, invf_ref,
                     qm_ref, km_ref, vm_ref, om_ref, gates_ref, gates_c_ref, qp_ref, kp_ref, vt_ref):
    x = x_ref[0]
    h = _rms(x, g_ref[...]).astype(BF16)
    c0 = 4 * ML_WIDTH

    def ml_cols(i):
        return _dot(h, wmain_ref[:, i * ML_WIDTH:(i + 1) * ML_WIDTH])

    zl = _dot(h, wmain_ref[:, c0:])
    c_q = zl[:, 0:Q_LORA]
    c_kv = zl[:, Q_LORA:Q_LORA + KV_LORA]
    kr = zl[:, Q_LORA + KV_LORA:Q_LORA + KV_LORA + LANES]
    kr_sw = zl[:, Q_LORA + KV_LORA + LANES:Q_LORA + KV_LORA + 2 * LANES]
    c_g = Q_LORA + KV_LORA + 2 * LANES
    gates_c_ref[0] = zl[:, c_g:c_g + LANES] + gbias_row_ref[...]
    gates_ref[0] = _dot_nt(wgt_ref[...], h) + gbias_ref[...]

    half = QK_ROPE // 2
    ang_t = invf_ref[...] * pos_ref[0].astype(F32)
    cos_c, sin_c = jnp.cos(ang_t), jnp.sin(ang_t)
    ts = ang_t.shape[1]
    pad_lo = jnp.zeros((QK_NOPE, ts), F32)
    pad_hi = jnp.zeros((LANES - QK_NOPE - QK_ROPE, ts), F32)
    cos_t = jnp.concatenate([pad_lo + 1.0, cos_c, cos_c, pad_hi], axis=0).T
    sin_t = jnp.concatenate([pad_lo, -sin_c, sin_c, pad_hi], axis=0).T
    scale = (QK_NOPE + QK_ROPE) ** -0.5 * LOG2_E
    cos_q, sin_q = cos_t * scale, sin_t * scale
    qm_ref[0] = ml_cols(0).astype(BF16)
    km_ref[0] = (ml_cols(1) * (ML_HEAD_DIM ** -0.5)).astype(BF16)

    cqn = _rms(c_q, qg_ref[...]).astype(BF16)
    qa = _dot(cqn, wuq_ref[...])
    qb = _dot(cqn, wuqs_ref[...])
    ckn = _rms(c_kv, kvg_ref[...]).astype(BF16)
    kn = _dot(ckn, wuk_ref[...])
    vt_ref[0] = _dot_nt(wuvt_ref[...], ckn).astype(BF16)
    k_rope = kr * cos_t + kr_sw * sin_t

    def rope_heads(first, last):
        for hd in range(first, last):
            lo = hd * LANES
            qp_ref[0, :, lo:lo + LANES] = (qa[:, lo:lo + LANES] * cos_q
                                           + qb[:, lo:lo + LANES] * sin_q).astype(BF16)
            kp_ref[0, :, lo:lo + LANES] = (kn[:, lo:lo + LANES] + k_rope).astype(BF16)

    vm_ref[0] = ml_cols(2).astype(BF16)
    rope_heads(0, MLA_HEADS // 2)
    om_ref[0] = ml_cols(3).astype(BF16)
    rope_heads(MLA_HEADS // 2, MLA_HEADS)


def _odd_proj(x, pos3, g, w_main, w_gt, gbias, gbias_row, q_g, kv_g, w_uq_p, w_uq_s, w_uk_p, w_uvt,
              invf, *, ts):
    B, S, D = x.shape
    ts = min(ts, S)
    tok = lambda n: pl.BlockSpec((1, ts, n), lambda b, s: (b, s, 0))
    out_shapes = (
        jax.ShapeDtypeStruct((B, S, ML_WIDTH), BF16),
        jax.ShapeDtypeStruct((B, S, ML_WIDTH), BF16),
        jax.ShapeDtypeStruct((B, S, ML_WIDTH), BF16),
        jax.ShapeDtypeStruct((B, S, ML_WIDTH), BF16),
        jax.ShapeDtypeStruct((B, 2 * ML_HEADS, S), F32),
        jax.ShapeDtypeStruct((B, S, LANES), F32),
        jax.ShapeDtypeStruct((B, S, MLA_HEADS * LANES), BF16),
        jax.ShapeDtypeStruct((B, S, MLA_HEADS * LANES), BF16),
        jax.ShapeDtypeStruct((B, MLA_WIDTH, S), BF16),
    )
    return pl.pallas_call(
        _odd_proj_kernel,
        grid=(B, S // ts),
        in_specs=[
            tok(D),
            pl.BlockSpec((1, 1, ts), lambda b, s: (b, 0, s)),
            _const_spec((1, D)),
            _const_spec(w_main.shape),
            _const_spec(w_gt.shape),
            _const_spec(gbias.shape),
            _const_spec(gbias_row.shape),
            _const_spec(q_g.shape),
            _const_spec(kv_g.shape),
            _const_spec(w_uq_p.shape),
            _const_spec(w_uq_s.shape),
            _const_spec(w_uk_p.shape),
            _const_spec(w_uvt.shape),
            _const_spec(invf.shape),
        ],
        out_specs=(tok(ML_WIDTH), tok(ML_WIDTH), tok(ML_WIDTH), tok(ML_WIDTH),
                   pl.BlockSpec((1, 2 * ML_HEADS, ts), lambda b, s: (b, 0, s)), tok(LANES),
                   tok(MLA_HEADS * LANES), tok(MLA_HEADS * LANES),
                   pl.BlockSpec((1, MLA_WIDTH, ts), lambda b, s: (b, 0, s))),
        out_shape=out_shapes,
        compiler_params=_params("arbitrary", "arbitrary"),
        name="odd_proj",
    )(x, pos3, g.reshape(1, D), w_main, w_gt, gbias, gbias_row, q_g, kv_g, w_uq_p, w_uq_s, w_uk_p,
      w_uvt, invf)


def _split3(a):
    hi = a.astype(BF16)
    r1 = a - hi.astype(F32)
    mid = r1.astype(BF16)
    lo = (r1 - mid.astype(F32)).astype(BF16)
    return hi, mid, lo


def _lane_sum_mxu(a):
    ones = jnp.ones((a.shape[1], LANES), BF16)
    return sum(_dot(t, ones) for t in _split3(a))


def _log2_sigmoid(x):
    return (jnp.minimum(x, 0.0) - jnp.log(1.0 + jnp.exp(-jnp.abs(x)))) * LOG2_E


def _mlstm_kernel(q_ref, k_ref, v_ref, o_ref, gt_ref, gc_ref, ng_ref, h_ref, c_scr, m_scr, *, L):
    ci = pl.program_id(1)

    @pl.when(ci == 0)
    def _():
        c_scr[...] = jnp.zeros(c_scr.shape, F32)
        m_scr[...] = jnp.zeros(m_scr.shape, F32)

    row = lax.broadcasted_iota(jnp.int32, (L, L), 0)
    col = lax.broadcasted_iota(jnp.int32, (L, L), 1)
    causal = col <= row
    gt = gt_ref[0]
    ones_v = jnp.ones((L, LANES), BF16)
    reps = L // LANES

    def widen(a):
        return jnp.concatenate([a] * reps, axis=1)

    lf_t = _log2_sigmoid(gt)
    b_t = sum(_dot(t, (row <= col).astype(BF16)) for t in _split3(lf_t))
    tril = causal.astype(BF16)
    b_c = sum(_dot(tril, t) for t in _split3(_log2_sigmoid(gc_ref[0])))

    for hd in range(ML_HEADS):
        lo = hd * ML_HEAD_DIM
        fg = ML_HEADS + hd
        q = q_ref[0, :, lo:lo + ML_HEAD_DIM]
        k = k_ref[0, :, lo:lo + ML_HEAD_DIM]
        v_aug = jnp.concatenate([v_ref[0, :, lo:lo + ML_HEAD_DIM], ones_v], axis=1)
        g_row = gt[hd:hd + 1, :] * LOG2_E - b_t[fg:fg + 1, :]
        gtot = jnp.sum(lf_t[fg:fg + 1, :], axis=1, keepdims=True)
        b_col = jnp.broadcast_to(b_c[:, fg:fg + 1], (L, LANES))

        m_prev = m_scr[hd]
        c_prev = c_scr[hd]

        g_mask = jnp.where(causal, g_row, NEG_BIG)
        gmax = jnp.max(g_mask, axis=1, keepdims=True)
        h_col = jnp.maximum(m_prev, jnp.broadcast_to(gmax, (L, LANES)))
        w_inter = jnp.exp2(m_prev - h_col)
        sw = jnp.exp2(g_mask - widen(h_col)) * _dot_nt(q, k)
        out = (jnp.concatenate([w_inter, w_inter], axis=1) * _dot(q, c_prev.astype(BF16))
               + _dot(sw.astype(BF16), v_aug))
        den = jnp.maximum(jnp.abs(out[:, ML_HEAD_DIM:]), jnp.exp2(-(b_col + h_col)))
        hc = out[:, :ML_HEAD_DIM] / den

        og = jax.nn.sigmoid(o_ref[0, :, lo:lo + ML_HEAD_DIM].astype(F32))
        hg = og * hc
        ms = _lane_sum_mxu(hg * hg) * (1.0 / ML_HEAD_DIM)
        h_ref[0, :, lo:lo + ML_HEAD_DIM] = (
            hg * lax.rsqrt(ms + EPS) * ng_ref[:, lo:lo + ML_HEAD_DIM]).astype(BF16)

        m_prev1 = m_prev[:, 0:1]
        a_row = gtot + g_row
        m_loc = jnp.max(a_row, axis=1, keepdims=True)
        w_loc = jnp.exp2(a_row - m_loc)
        m_new = jnp.maximum(gtot + m_prev1, m_loc)
        s_old = jnp.exp2(gtot + m_prev1 - m_new)
        s_new = jnp.exp2(m_loc - m_new)
        kw_t = (k.T.astype(F32) * w_loc).astype(BF16)
        c_scr[hd] = s_old * c_prev + s_new * _dot(kw_t, v_aug)
        m_scr[hd] = jnp.broadcast_to(m_new, (1, LANES))


def _mlstm(q, k, v, o, gates_t, gates_c, norm_g, *, L):
    B, S, W = q.shape
    L = min(L, S)
    kern = functools.partial(_mlstm_kernel, L=L)
    tok = pl.BlockSpec((1, L, W), lambda b, c: (b, c, 0))
    return pl.pallas_call(
        kern,
        grid=(B, S // L),
        in_specs=[tok, tok, tok, tok,
                  pl.BlockSpec((1, 2 * ML_HEADS, L), lambda b, c: (b, 0, c)),
                  pl.BlockSpec((1, L, LANES), lambda b, c: (b, c, 0)),
                  _const_spec((1, W))],
        out_specs=tok,
        out_shape=jax.ShapeDtypeStruct((B, S, W), BF16),
        scratch_shapes=[pltpu.VMEM((ML_HEADS, ML_HEAD_DIM, 2 * ML_HEAD_DIM), F32),
                        pltpu.VMEM((ML_HEADS, 1, LANES), F32)],
        compiler_params=_params("arbitrary", "arbitrary"),
        name="mlstm",
    )(q, k, v, o, gates_t, gates_c, norm_g.reshape(1, W))


def _mla_kernel(q_ref, k_ref, vt_ref, o_ref, s00, s01, s10, s11, *, tq):
    qi = pl.program_id(2)
    heads = 2
    pieces = 2
    rows = tq // pieces
    s_scr = ((s00, s01), (s10, s11))
    qs = [q_ref[0, :, hh * LANES:(hh + 1) * LANES] for hh in range(heads)]
    ones_rows = jnp.ones((BF16_ROWS, tq), BF16)

    def score_piece(j, slot, hh, pc):
        start = pl.multiple_of(j * tq + pc * rows, rows)
        kb = k_ref[0, pl.ds(start, rows), hh * LANES:(hh + 1) * LANES]
        s_scr[slot][hh][pc * rows:(pc + 1) * rows, :] = _dot_nt(kb, qs[hh])

    def scores(j, slot):
        for hh in range(heads):
            for pc in range(pieces):
                score_piece(j, slot, hh, pc)

    def stage_max(slot, hh, m, masked):
        s = s_scr[slot][hh][...]
        if masked:
            kk = lax.broadcasted_iota(jnp.int32, s.shape, 0)
            qq = lax.broadcasted_iota(jnp.int32, s.shape, 1)
            s = jnp.where(kk <= qq, s, NEG_BIG)
        m_new = jnp.maximum(m, jnp.max(s, axis=0, keepdims=True))
        return s, m_new

    def stage_exp(j, hh, s, m_new, acc, half):
        hk = tq // 2
        start = pl.multiple_of(j * tq + half * hk, hk)
        vtb = jnp.concatenate([vt_ref[0, hh * V_DIM:(hh + 1) * V_DIM, pl.ds(start, hk)],
                               ones_rows[:, :hk]], axis=0)
        p = jnp.exp2(s[half * hk:(half + 1) * hk, :] - m_new).astype(BF16)
        return acc + _dot(vtb, p)

    def block(j, slot, carry, masked, nxt=None):
        def piece(hh, pc):
            if nxt is not None:
                score_piece(nxt[0], nxt[1], hh, pc)

        new = []
        for hh in range(heads):
            m, acc = carry[hh]
            piece(hh, 0)
            s, m_new = stage_max(slot, hh, m, masked)
            piece(hh, 1)
            acc = stage_exp(j, hh, s, m_new, jnp.exp2(m - m_new) * acc, 0)
            acc = stage_exp(j, hh, s, m_new, acc, 1)
            new.append((m_new, acc))
        return tuple(new)

    init = tuple((jnp.full((1, tq), NEG_BIG, F32), jnp.zeros((V_DIM + BF16_ROWS, tq), F32))
                 for _ in range(heads))
    def finish(c):
        ot = jnp.concatenate([acc[:V_DIM] / acc[V_DIM:V_DIM + 1] for _, acc in c], axis=0)
        o_ref[0] = ot.T.astype(BF16)

    def pair(j0, c, prefetch):
        c = block(j0, 1, c, False, nxt=(j0 + 1, 0))
        return block(j0 + 1, 0, c, False, nxt=(j0 + 2, 1) if prefetch else None)

    scores(qi, 0)

    @pl.when(qi == 0)
    def _():
        finish(block(qi, 0, init, True))

    @pl.when(qi > 0)
    def _():
        c = block(qi, 0, init, True, nxt=(0, 1))
        trips = (qi - 1) // 2
        c = lax.fori_loop(0, trips, lambda i, cc: pair(2 * i, cc, True), c)
        rest = 2 * trips

        @pl.when(qi % 2 == 0)
        def _():
            finish(pair(rest, c, False))

        @pl.when(qi % 2 == 1)
        def _():
            finish(block(rest, 1, c, False))


def _mla_attention(qp, kp, vt, *, tq):
    B, S, _ = qp.shape
    tq = min(tq, S)
    pairs = MLA_HEADS // 2
    kern = functools.partial(_mla_kernel, tq=tq)
    return pl.pallas_call(
        kern,
        grid=(B, pairs, S // tq),
        in_specs=[
            pl.BlockSpec((1, tq, 2 * LANES), lambda b, p, i: (b, i, p)),
            pl.BlockSpec((1, S, 2 * LANES), lambda b, p, i: (b, 0, p)),
            pl.BlockSpec((1, 2 * V_DIM, S), lambda b, p, i: (b, p, 0)),
        ],
        out_specs=pl.BlockSpec((1, tq, LANES), lambda b, p, i: (b, i, p)),
        out_shape=jax.ShapeDtypeStruct((B, S, MLA_WIDTH), BF16),
        scratch_shapes=[pltpu.VMEM((tq, tq), F32) for _ in range(4)],
        compiler_params=_params("arbitrary", "arbitrary", "arbitrary"),
        name="mla_attention",
    )(qp, kp, vt)


def _odd_out_kernel(x_ref, hm_ref, ha_ref, wout_ref, xg_ref, wq_ref, kv_ref, wo_ref, o_ref):
    mix = jnp.concatenate([hm_ref[0], ha_ref[0]], axis=-1)
    x1 = x_ref[0] + _dot(mix, wout_ref[...])
    o_ref[0] = _xattn_body(x1, xg_ref, wq_ref, kv_ref, wo_ref)


def _odd_out(x, hm, ha, w_out, xg, wq_all, kv, wo_all, layer, *, ts):
    B, S, D = x.shape
    ts = min(ts, S)
    tok = lambda n: pl.BlockSpec((1, ts, n), lambda b, s: (b, s, 0))
    return pl.pallas_call(
        _odd_out_kernel,
        grid=(B, S // ts),
        in_specs=[tok(D), tok(ML_WIDTH), tok(MLA_WIDTH), _const_spec(w_out.shape)]
        + _xattn_specs(D, wq_all, kv, wo_all, layer),
        out_specs=tok(D),
        out_shape=jax.ShapeDtypeStruct(x.shape, F32),
        compiler_params=_params("arbitrary", "arbitrary"),
        name="odd_out",
    )(x, hm, ha, w_out, xg.reshape(1, D), wq_all, kv, wo_all)


def _pack_odd_weights(w_in, w_uq, w_ukv):
    D = w_in.shape[0]
    c = 4 * ML_WIDTH
    w_g = w_in[:, c:c + 2 * ML_HEADS]
    c += 2 * ML_HEADS
    w_cq = w_in[:, c:c + Q_LORA]
    c += Q_LORA
    w_ckv = w_in[:, c:c + KV_LORA]
    c += KV_LORA
    w_kr = w_in[:, c:c + QK_ROPE]
    half = QK_ROPE // 2
    w_kr_sw = jnp.concatenate([w_kr[:, half:], w_kr[:, :half]], axis=1)
    zl = jnp.zeros((D, QK_NOPE), w_in.dtype)
    zr = jnp.zeros((D, LANES - QK_NOPE - QK_ROPE), w_in.dtype)
    zg = jnp.zeros((D, LANES - 2 * ML_HEADS), w_in.dtype)
    w_main = jnp.concatenate([w_in[:, :4 * ML_WIDTH], w_cq, w_ckv,
                              zl, w_kr, zr, zl, w_kr_sw, zr, w_g, zg], axis=1).astype(BF16)
    w_gt = w_g.T.astype(BF16)

    uq = w_uq.reshape(Q_LORA, MLA_HEADS, QK_NOPE + QK_ROPE)
    uq_n, uq_r = uq[..., :QK_NOPE], uq[..., QK_NOPE:]
    uq_r_sw = jnp.concatenate([uq_r[..., half:], uq_r[..., :half]], axis=-1)
    zpad = jnp.zeros((Q_LORA, MLA_HEADS, LANES - QK_NOPE - QK_ROPE), w_uq.dtype)
    w_uq_p = jnp.concatenate([uq_n, uq_r, zpad], axis=-1).reshape(Q_LORA, MLA_HEADS * LANES)
    w_uq_s = jnp.concatenate([jnp.zeros_like(uq_n), uq_r_sw, zpad], axis=-1)
    w_uq_s = w_uq_s.reshape(Q_LORA, MLA_HEADS * LANES)

    ukv = w_ukv.reshape(KV_LORA, MLA_HEADS, QK_NOPE + V_DIM)
    uk, uv = ukv[..., :QK_NOPE], ukv[..., QK_NOPE:]
    w_uk_p = jnp.concatenate([uk, jnp.zeros((KV_LORA, MLA_HEADS, LANES - QK_NOPE), w_ukv.dtype)],
                             axis=-1).reshape(KV_LORA, MLA_HEADS * LANES)
    w_uvt = uv.reshape(KV_LORA, MLA_WIDTH).T
    return (w_main, w_gt, w_uq_p.astype(BF16), w_uq_s.astype(BF16), w_uk_p.astype(BF16),
            w_uvt.astype(BF16))


def _rope_inv_freq():
    half = QK_ROPE // 2
    inv = ROPE_THETA ** (-jnp.arange(half, dtype=F32) / half)
    return inv.reshape(half, 1)


TS_EVEN = 1024
TM_FFN = 1024
TH_FFN = 256
TS_ODD = 512
L_MLSTM = 256
TQ_MLA = 512
TS_OUT = 1024


def kernel(x, mem, positions, norm_mix_g, norm_xattn_g, mem_norm_g, xattn_wq, xattn_wkv, xattn_wo,
           norm_ffn_g, ffn_w_gate_up, ffn_w_down, ev_w_in, ev_conv_w, ev_pool_w, ev_pool_scale,
           ev_w_out, od_w_in, od_gate_bias, od_ml_norm_g, od_q_norm_g, od_kv_norm_g, od_w_uq,
           od_w_ukv, od_w_out, final_norm_g):
    depth = norm_mix_g.shape[0]
    B, S, D = x.shape
    pos3 = positions.reshape(B, 1, S)
    invf = _rope_inv_freq()
    wq_all, wkv_all, wo_all = (w.astype(BF16) for w in (xattn_wq, xattn_wkv, xattn_wo))
    w_gu_all, w_d_all = ffn_w_gate_up.astype(BF16), ffn_w_down.astype(BF16)
    for layer in range(depth):
        kv = _mem_kv(mem, mem_norm_g[layer], wkv_all, layer)
        xa = (norm_xattn_g[layer], wq_all, kv, wo_all, layer)
        if layer % 2 == 0:
            e = layer // 2
            x = _even_mixer(x, norm_mix_g[layer], ev_w_in[e].astype(BF16), ev_conv_w[e],
                            ev_pool_w[e].astype(BF16), ev_pool_scale[e], ev_w_out[e].astype(BF16),
                            *xa, ts=TS_EVEN)
        else:
            o = layer // 2
            w_main, w_gt, w_uq_p, w_uq_s, w_uk_p, w_uvt = _pack_odd_weights(
                od_w_in[o], od_w_uq[o], od_w_ukv[o])
            gbias_row = jnp.pad(od_gate_bias[o], (0, LANES - 2 * ML_HEADS)).reshape(1, LANES)
            qm, km, vm, om, gates_t, gates_c, qp, kp, vt = _odd_proj(
                x, pos3, norm_mix_g[layer], w_main, w_gt, od_gate_bias[o].reshape(2 * ML_HEADS, 1),
                gbias_row, od_q_norm_g[o].reshape(1, Q_LORA), od_kv_norm_g[o].reshape(1, KV_LORA),
                w_uq_p, w_uq_s, w_uk_p, w_uvt, invf, ts=TS_ODD)
            hm = _mlstm(qm, km, vm, om, gates_t, gates_c, od_ml_norm_g[o], L=L_MLSTM)
            ha = _mla_attention(qp, kp, vt, tq=TQ_MLA)
            x = _odd_out(x, hm, ha, od_w_out[o].astype(BF16), *xa, ts=TS_OUT)
        x = _swiglu(x, norm_ffn_g[layer], w_gu_all, w_d_all, final_norm_g, layer,
                    tm=TM_FFN, th=TH_FFN, final_norm=(layer == depth - 1))
    return x
```

```python
import functools

import jax
import jax.numpy as jnp
from jax import lax
from jax.experimental import pallas as pl
from jax.experimental.pallas import tpu as pltpu

F32 = jnp.float32
BF16 = jnp.bfloat16

EPS = 1e-6
ROPE_THETA = 10000.0
LANES = 128
SUBLANES = 8
BF16_ROWS = 16
VMEM_LIMIT_BYTES = 56 * 1024 * 1024
NEG_BIG = -1e30
LOG2_E = 1.4426950408889634

SC_WIDTH = 512
CONV_K = 3
POOL_WINDOWS = (2, 4, 8, 16)
POOL_GROUP = 128
HALO = 32
ML_HEADS = 4
ML_HEAD_DIM = 128
ML_WIDTH = ML_HEADS * ML_HEAD_DIM
MLA_HEADS = 8
QK_NOPE = 64
QK_ROPE = 32
V_DIM = 64
Q_LORA = 384
KV_LORA = 256
MLA_WIDTH = MLA_HEADS * V_DIM
XA_HEADS = 4
XA_HEAD_DIM = 128
XA_WIDTH = XA_HEADS * XA_HEAD_DIM


def _params(*sem):
    return pltpu.CompilerParams(dimension_semantics=sem, vmem_limit_bytes=VMEM_LIMIT_BYTES)


def _const_spec(shape):
    nd = len(shape)
    return pl.BlockSpec(shape, lambda *_: (0,) * nd, pipeline_mode=pl.Buffered(1))


def _layer_spec(stacked, layer):
    return pl.BlockSpec((None,) + stacked.shape[1:], lambda *_: (layer, 0, 0),
                        pipeline_mode=pl.Buffered(1))


def _rms(x, g):
    ms = jnp.mean(x * x, axis=-1, keepdims=True)
    return x * lax.rsqrt(ms + EPS) * g


def _dot(a, b):
    return jnp.dot(a, b, preferred_element_type=F32)


def _dot_nt(a, b):
    return lax.dot_general(a, b, (((1,), (1,)), ((), ())), preferred_element_type=F32)


def _even_kernel(x_ref, g_ref, win_ref, convw_ref, poolw_ref, pscale_ref, wout_ref,
                 xg_ref, wq_ref, kv_ref, wo_ref, o_ref, ubuf, xbbuf, tmp_a, tmp_b, *, ts):
    si = pl.program_id(1)

    @pl.when(si == 0)
    def _():
        ubuf[0:HALO, :] = jnp.zeros((HALO, SC_WIDTH), F32)
        xbbuf[0:HALO, :] = jnp.zeros((HALO, SC_WIDTH), F32)

    x = x_ref[0]
    h = _rms(x, g_ref[...]).astype(BF16)
    z = _dot(h, win_ref[...])
    g_b = z[:, 0:SC_WIDTH]
    u = z[:, SC_WIDTH:2 * SC_WIDTH] * z[:, 2 * SC_WIDTH:3 * SC_WIDTH]
    xb = z[:, 3 * SC_WIDTH:4 * SC_WIDTH]
    ubuf[HALO:HALO + ts, :] = u
    xbbuf[HALO:HALO + ts, :] = xb

    cw = convw_ref[...]
    conv = cw[2:3, :] * u
    for k in range(CONV_K - 1):
        back = CONV_K - 1 - k
        conv = conv + cw[k:k + 1, :] * ubuf[HALO - back:HALO - back + ts, :]
    ya = g_b * conv

    def window_level(src, c0, dst, r0, shift):
        w2 = (src[r0:HALO + ts, c0:c0 + POOL_GROUP]
              + src[r0 - shift:HALO + ts - shift, c0:c0 + POOL_GROUP])
        if dst is not None:
            dst[r0:HALO + ts, :] = w2
        return w2

    t_idx = si * ts + lax.broadcasted_iota(jnp.int32, (ts, POOL_GROUP), 0)
    ys = [ya.astype(BF16)]
    for gi, w in enumerate(POOL_WINDOWS):
        lo = gi * POOL_GROUP
        xg = xb[:, lo:lo + POOL_GROUP]
        src, c0, shift, level = xbbuf, lo, 1, 1
        while 2 * shift < w:
            dst = tmp_a if src is not tmp_a else tmp_b
            window_level(src, c0, dst, SUBLANES * level, shift)
            src, c0, shift, level = dst, 0, 2 * shift, level + 1
        acc = window_level(src, c0, None, HALO, shift)
        cnt = jnp.minimum(t_idx + 1, w).astype(F32)
        d = (acc / cnt - xg).astype(BF16)
        yg = _dot(d, poolw_ref[gi]) * pscale_ref[:, lo:lo + POOL_GROUP]
        ys.append(yg.astype(BF16))

    ubuf[0:HALO, :] = ubuf[ts:ts + HALO, :]
    xbbuf[0:HALO, :] = xbbuf[ts:ts + HALO, :]

    mix = jnp.concatenate(ys, axis=-1)
    x1 = x + _dot(mix, wout_ref[...])
    o_ref[0] = _xattn_body(x1, xg_ref, wq_ref, kv_ref, wo_ref)


def _even_mixer(x, g, w_in, conv_w, pool_w, pool_scale, w_out, xg, wq_all, kv, wo_all, layer, *, ts):
    B, S, D = x.shape
    ts = min(ts, S)
    kern = functools.partial(_even_kernel, ts=ts)
    return pl.pallas_call(
        kern,
        grid=(B, S // ts),
        in_specs=[
            pl.BlockSpec((1, ts, D), lambda b, s: (b, s, 0)),
            _const_spec((1, D)),
            _const_spec(w_in.shape),
            _const_spec(conv_w.shape),
            _const_spec(pool_w.shape),
            _const_spec((1, SC_WIDTH)),
            _const_spec(w_out.shape),
        ] + _xattn_specs(D, wq_all, kv, wo_all, layer),
        out_specs=pl.BlockSpec((1, ts, D), lambda b, s: (b, s, 0)),
        out_shape=jax.ShapeDtypeStruct(x.shape, F32),
        scratch_shapes=[pltpu.VMEM((HALO + ts, SC_WIDTH), F32),
                        pltpu.VMEM((HALO + ts, SC_WIDTH), F32),
                        pltpu.VMEM((HALO + ts, POOL_GROUP), F32),
                        pltpu.VMEM((HALO + ts, POOL_GROUP), F32)],
        compiler_params=_params("arbitrary", "arbitrary"),
        name="even_mixer",
    )(x, g.reshape(1, D), w_in, conv_w, pool_w, pool_scale.reshape(1, SC_WIDTH), w_out,
      xg.reshape(1, D), wq_all, kv, wo_all)


def _memkv_kernel(mem_ref, g_ref, wkv_ref, kv_ref):
    h = _rms(mem_ref[0], g_ref[...]).astype(BF16)
    kv_ref[0] = _dot(h, wkv_ref[...]).astype(BF16)


def _mem_kv(mem, g, wkv_all, layer):
    B, M, D = mem.shape
    N = wkv_all.shape[2]
    return pl.pallas_call(
        _memkv_kernel,
        grid=(B,),
        in_specs=[pl.BlockSpec((1, M, D), lambda b: (b, 0, 0)),
                  _const_spec((1, D)), _layer_spec(wkv_all, layer)],
        out_specs=pl.BlockSpec((1, M, N), lambda b: (b, 0, 0)),
        out_shape=jax.ShapeDtypeStruct((B, M, N), BF16),
        compiler_params=_params("arbitrary"),
        name="mem_kv",
    )(mem, g.reshape(1, D), wkv_all)


def _xattn_body(x, g_ref, wq_ref, kv_ref, wo_ref):
    h = _rms(x, g_ref[...]).astype(BF16)
    q = (_dot(h, wq_ref[...]) * (XA_HEAD_DIM ** -0.5)).astype(BF16)
    kv = kv_ref[0]
    ones_v = jnp.ones((kv.shape[0], XA_HEAD_DIM), BF16)
    outs = []
    for hd in range(XA_HEADS):
        lo = hd * XA_HEAD_DIM
        s = _dot_nt(q[:, lo:lo + XA_HEAD_DIM], kv[:, lo:lo + XA_HEAD_DIM])
        p = jnp.exp(s - jnp.max(s, axis=-1, keepdims=True)).astype(BF16)
        v_aug = jnp.concatenate([kv[:, XA_WIDTH + lo:XA_WIDTH + lo + XA_HEAD_DIM], ones_v], axis=1)
        o = _dot(p, v_aug)
        outs.append((o[:, :XA_HEAD_DIM] / o[:, XA_HEAD_DIM:]).astype(BF16))
    a = jnp.concatenate(outs, axis=-1)
    return x + _dot(a, wo_ref[...])


def _xattn_specs(D, wq_all, kv, wo_all, layer):
    M = kv.shape[1]
    return [_const_spec((1, D)), _layer_spec(wq_all, layer),
            pl.BlockSpec((1, M, kv.shape[2]), lambda b, s: (b, 0, 0)), _layer_spec(wo_all, layer)]


def _swiglu_kernel(x_ref, g_ref, wgu_ref, wd_ref, fg_ref, o_ref, *, hidden, th, final_norm):
    x = x_ref[...]
    h = _rms(x, g_ref[...]).astype(BF16)
    acc = x
    for j in range(hidden // th):
        gate = _dot(h, wgu_ref[:, j * th:(j + 1) * th])
        up = _dot(h, wgu_ref[:, hidden + j * th:hidden + (j + 1) * th])
        act = (gate * jax.nn.sigmoid(gate) * up).astype(BF16)
        acc = acc + _dot(act, wd_ref[j * th:(j + 1) * th, :])
    if final_norm:
        acc = _rms(acc, fg_ref[...])
    o_ref[...] = acc


def _swiglu(x, g, w_gu_all, w_d_all, final_g, layer, *, tm, th, final_norm):
    B, S, D = x.shape
    T = B * S
    tm = min(tm, T)
    hidden = w_d_all.shape[1]
    kern = functools.partial(_swiglu_kernel, hidden=hidden, th=th, final_norm=final_norm)
    out = pl.pallas_call(
        kern,
        grid=(T // tm,),
        in_specs=[
            pl.BlockSpec((tm, D), lambda i: (i, 0)),
            _const_spec((1, D)),
            _layer_spec(w_gu_all, layer),
            _layer_spec(w_d_all, layer),
            _const_spec((1, D)),
        ],
        out_specs=pl.BlockSpec((tm, D), lambda i: (i, 0)),
        out_shape=jax.ShapeDtypeStruct((T, D), F32),
        compiler_params=_params("arbitrary"),
        name="swiglu",
    )(x.reshape(T, D), g.reshape(1, D), w_gu_all, w_d_all, final_g.reshape(1, D))
    return out.reshape(B, S, D)


def _odd_proj_kernel(x_ref, pos_ref, g_ref, wmain_ref, wgt_ref, gbias_ref, gbias_row_ref,
                     qg_ref, kvg_ref, wuq_ref, wuqs_ref, wuk_ref, wuvt_ref, invf_ref,
                     qm_ref, km_ref, vm_ref, om_ref, gates_ref, gates_c_ref, qp_ref, kp_ref, vt_ref):
    x = x_ref[0]
    h = _rms(x, g_ref[...]).astype(BF16)
    c0 = 4 * ML_WIDTH

    def ml_cols(i):
        return _dot(h, wmain_ref[:, i * ML_WIDTH:(i + 1) * ML_WIDTH])

    zl = _dot(h, wmain_ref[:, c0:])
    c_q = zl[:, 0:Q_LORA]
    c_kv = zl[:, Q_LORA:Q_LORA + KV_LORA]
    kr = zl[:, Q_LORA + KV_LORA:Q_LORA + KV_LORA + LANES]
    kr_sw = zl[:, Q_LORA + KV_LORA + LANES:Q_LORA + KV_LORA + 2 * LANES]
    c_g = Q_LORA + KV_LORA + 2 * LANES
    gates_c_ref[0] = zl[:, c_g:c_g + LANES] + gbias_row_ref[...]
    gates_ref[0] = _dot_nt(wgt_ref[...], h) + gbias_ref[...]

    half = QK_ROPE // 2
    ang_t = invf_ref[...] * pos_ref[0].astype(F32)
    cos_c, sin_c = jnp.cos(ang_t), jnp.sin(ang_t)
    ts = ang_t.shape[1]
    pad_lo = jnp.zeros((QK_NOPE, ts), F32)
    pad_hi = jnp.zeros((LANES - QK_NOPE - QK_ROPE, ts), F32)
    cos_t = jnp.concatenate([pad_lo + 1.0, cos_c, cos_c, pad_hi], axis=0).T
    sin_t = jnp.concatenate([pad_lo, -sin_c, sin_c, pad_hi], axis=0).T
    scale = (QK_NOPE + QK_ROPE) ** -0.5 * LOG2_E
    cos_q, sin_q = cos_t * scale, sin_t * scale
    qm_ref[0] = ml_cols(0).astype(BF16)
    km_ref[0] = (ml_cols(1) * (ML_HEAD_DIM ** -0.5)).astype(BF16)

    cqn = _rms(c_q, qg_ref[...]).astype(BF16)
    qa = _dot(cqn, wuq_ref[...])
    qb = _dot(cqn, wuqs_ref[...])
    ckn = _rms(c_kv, kvg_ref[...]).astype(BF16)
    kn = _dot(ckn, wuk_ref[...])
    vt_ref[0] = _dot_nt(wuvt_ref[...], ckn).astype(BF16)
    k_rope = kr * cos_t + kr_sw * sin_t

    def rope_heads(first, last):
        for hd in range(first, last):
            lo = hd * LANES
            qp_ref[0, :, lo:lo + LANES] = (qa[:, lo:lo + LANES] * cos_q
                                           + qb[:, lo:lo + LANES] * sin_q).astype(BF16)
            kp_ref[0, :, lo:lo + LANES] = (kn[:, lo:lo + LANES] + k_rope).astype(BF16)

    vm_ref[0] = ml_cols(2).astype(BF16)
    rope_heads(0, MLA_HEADS // 2)
    om_ref[0] = ml_cols(3).astype(BF16)
    rope_heads(MLA_HEADS // 2, MLA_HEADS)


def _odd_proj(x, pos3, g, w_main, w_gt, gbias, gbias_row, q_g, kv_g, w_uq_p, w_uq_s, w_uk_p, w_uvt,
              invf, *, ts):
    B, S, D = x.shape
    ts = min(ts, S)
    tok = lambda n: pl.BlockSpec((1, ts, n), lambda b, s: (b, s, 0))
    out_shapes = (
        jax.ShapeDtypeStruct((B, S, ML_WIDTH), BF16),
        jax.ShapeDtypeStruct((B, S, ML_WIDTH), BF16),
        jax.ShapeDtypeStruct((B, S, ML_WIDTH), BF16),
        jax.ShapeDtypeStruct((B, S, ML_WIDTH), BF16),
        jax.ShapeDtypeStruct((B, 2 * ML_HEADS, S), F32),
        jax.ShapeDtypeStruct((B, S, LANES), F32),
        jax.ShapeDtypeStruct((B, S, MLA_HEADS * LANES), BF16),
        jax.ShapeDtypeStruct((B, S, MLA_HEADS * LANES), BF16),
        jax.ShapeDtypeStruct((B, MLA_WIDTH, S), BF16),
    )
    return pl.pallas_call(
        _odd_proj_kernel,
        grid=(B, S // ts),
        in_specs=[
            tok(D),
            pl.BlockSpec((1, 1, ts), lambda b, s: (b, 0, s)),
            _const_spec((1, D)),
            _const_spec(w_main.shape),
            _const_spec(w_gt.shape),
            _const_spec(gbias.shape),
            _const_spec(gbias_row.shape),
            _const_spec(q_g.shape),
            _const_spec(kv_g.shape),
            _const_spec(w_uq_p.shape),
            _const_spec(w_uq_s.shape),
            _const_spec(w_uk_p.shape),
            _const_spec(w_uvt.shape),
            _const_spec(invf.shape),
        ],
        out_specs=(tok(ML_WIDTH), tok(ML_WIDTH), tok(ML_WIDTH), tok(ML_WIDTH),
                   pl.BlockSpec((1, 2 * ML_HEADS, ts), lambda b, s: (b, 0, s)), tok(LANES),
                   tok(MLA_HEADS * LANES), tok(MLA_HEADS * LANES),
                   pl.BlockSpec((1, MLA_WIDTH, ts), lambda b, s: (b, 0, s))),
        out_shape=out_shapes,
        compiler_params=_params("arbitrary", "arbitrary"),
        name="odd_proj",
    )(x, pos3, g.reshape(1, D), w_main, w_gt, gbias, gbias_row, q_g, kv_g, w_uq_p, w_uq_s, w_uk_p,
      w_uvt, invf)


def _split3(a):
    hi = a.astype(BF16)
    r1 = a - hi.astype(F32)
    mid = r1.astype(BF16)
    lo = (r1 - mid.astype(F32)).astype(BF16)
    return hi, mid, lo


def _lane_sum_mxu(a):
    ones = jnp.ones((a.shape[1], LANES), BF16)
    return sum(_dot(t, ones) for t in _split3(a))


def _log2_sigmoid(x):
    return (jnp.minimum(x, 0.0) - jnp.log(1.0 + jnp.exp(-jnp.abs(x)))) * LOG2_E


def _mlstm_kernel(q_ref, k_ref, v_ref, o_ref, gt_ref, gc_ref, ng_ref, h_ref, c_scr, m_scr, *, L):
    ci = pl.program_id(1)

    @pl.when(ci == 0)
    def _():
        c_scr[...] = jnp.zeros(c_scr.shape, F32)
        m_scr[...] = jnp.zeros(m_scr.shape, F32)

    row = lax.broadcasted_iota(jnp.int32, (L, L), 0)
    col = lax.broadcasted_iota(jnp.int32, (L, L), 1)
    causal = col <= row
    gt = gt_ref[0]
    ones_v = jnp.ones((L, LANES), BF16)
    reps = L // LANES

    def widen(a):
        return jnp.concatenate([a] * reps, axis=1)

    lf_t = _log2_sigmoid(gt)
    b_t = sum(_dot(t, (row <= col).astype(BF16)) for t in _split3(lf_t))
    tril = causal.astype(BF16)
    b_c = sum(_dot(tril, t) for t in _split3(_log2_sigmoid(gc_ref[0])))

    for hd in range(ML_HEADS):
        lo = hd * ML_HEAD_DIM
        fg = ML_HEADS + hd
        q = q_ref[0, :, lo:lo + ML_HEAD_DIM]
        k = k_ref[0, :, lo:lo + ML_HEAD_DIM]
        v_aug = jnp.concatenate([v_ref[0, :, lo:lo + ML_HEAD_DIM], ones_v], axis=1)
        g_row = gt[hd:hd + 1, :] * LOG2_E - b_t[fg:fg + 1, :]
        gtot = jnp.sum(lf_t[fg:fg + 1, :], axis=1, keepdims=True)
        b_col = jnp.broadcast_to(b_c[:, fg:fg + 1], (L, LANES))

        m_prev = m_scr[hd]
        c_prev = c_scr[hd]

        g_mask = jnp.where(causal, g_row, NEG_BIG)
        gmax = jnp.max(g_mask, axis=1, keepdims=True)
        h_col = jnp.maximum(m_prev, jnp.broadcast_to(gmax, (L, LANES)))
        w_inter = jnp.exp2(m_prev - h_col)
        sw = jnp.exp2(g_mask - widen(h_col)) * _dot_nt(q, k)
        out = (jnp.concatenate([w_inter, w_inter], axis=1) * _dot(q, c_prev.astype(BF16))
               + _dot(sw.astype(BF16), v_aug))
        den = jnp.maximum(jnp.abs(out[:, ML_HEAD_DIM:]), jnp.exp2(-(b_col + h_col)))
        hc = out[:, :ML_HEAD_DIM] / den

        og = jax.nn.sigmoid(o_ref[0, :, lo:lo + ML_HEAD_DIM].astype(F32))
        hg = og * hc
        ms = _lane_sum_mxu(hg * hg) * (1.0 / ML_HEAD_DIM)
        h_ref[0, :, lo:lo + ML_HEAD_DIM] = (
            hg * lax.rsqrt(ms + EPS) * ng_ref[:, lo:lo + ML_HEAD_DIM]).astype(BF16)

        m_prev1 = m_prev[:, 0:1]
        a_row = gtot + g_row
        m_loc = jnp.max(a_row, axis=1, keepdims=True)
        w_loc = jnp.exp2(a_row - m_loc)
        m_new = jnp.maximum(gtot + m_prev1, m_loc)
        s_old = jnp.exp2(gtot + m_prev1 - m_new)
        s_new = jnp.exp2(m_loc - m_new)
        kw_t = (k.T.astype(F32) * w_loc).astype(BF16)
        c_scr[hd] = s_old * c_prev + s_new * _dot(kw_t, v_aug)
        m_scr[hd] = jnp.broadcast_to(m_new, (1, LANES))


def _mlstm(q, k, v, o, gates_t, gates_c, norm_g, *, L):
    B, S, W = q.shape
    L = min(L, S)
    kern = functools.partial(_mlstm_kernel, L=L)
    tok = pl.BlockSpec((1, L, W), lambda b, c: (b, c, 0))
    return pl.pallas_call(
        kern,
        grid=(B, S // L),
        in_specs=[tok, tok, tok, tok,
                  pl.BlockSpec((1, 2 * ML_HEADS, L), lambda b, c: (b, 0, c)),
                  pl.BlockSpec((1, L, LANES), lambda b, c: (b, c, 0)),
                  _const_spec((1, W))],
        out_specs=tok,
        out_shape=jax.ShapeDtypeStruct((B, S, W), BF16),
        scratch_shapes=[pltpu.VMEM((ML_HEADS, ML_HEAD_DIM, 2 * ML_HEAD_DIM), F32),
                        pltpu.VMEM((ML_HEADS, 1, LANES), F32)],
        compiler_params=_params("arbitrary", "arbitrary"),
        name="mlstm",
    )(q, k, v, o, gates_t, gates_c, norm_g.reshape(1, W))


def _mla_kernel(q_ref, k_ref, vt_ref, o_ref, s00, s01, s10, s11, *, tq):
    qi = pl.program_id(2)
    heads = 2
    pieces = 2
    rows = tq // pieces
    s_scr = ((s00, s01), (s10, s11))
    qs = [q_ref[0, :, hh * LANES:(hh + 1) * LANES] for hh in range(heads)]
    ones_rows = jnp.ones((BF16_ROWS, tq), BF16)

    def score_piece(j, slot, hh, pc):
        start = pl.multiple_of(j * tq + pc * rows, rows)
        kb = k_ref[0, pl.ds(start, rows), hh * LANES:(hh + 1) * LANES]
        s_scr[slot][hh][pc * rows:(pc + 1) * rows, :] = _dot_nt(kb, qs[hh])

    def scores(j, slot):
        for hh in range(heads):
            for pc in range(pieces):
                score_piece(j, slot, hh, pc)

    def stage_max(slot, hh, m, masked):
        s = s_scr[slot][hh][...]
        if masked:
            kk = lax.broadcasted_iota(jnp.int32, s.shape, 0)
            qq = lax.broadcasted_iota(jnp.int32, s.shape, 1)
            s = jnp.where(kk <= qq, s, NEG_BIG)
            s_scr[slot][hh][...] = s
        return jnp.maximum(m, jnp.max(s, axis=0, keepdims=True))

    def stage_exp(j, slot, hh, m_new, acc, half):
        hk = tq // 2
        start = pl.multiple_of(j * tq + half * hk, hk)
        vtb = jnp.concatenate([vt_ref[0, hh * V_DIM:(hh + 1) * V_DIM, pl.ds(start, hk)],
                               ones_rows[:, :hk]], axis=0)
        p = jnp.exp2(s_scr[slot][hh][half * hk:(half + 1) * hk, :] - m_new).astype(BF16)
        return acc + _dot(vtb, p)

    def block(j, slot, carry, masked, nxt=None):
        def piece(hh, pc):
            if nxt is not None:
                score_piece(nxt[0], nxt[1], hh, pc)

        new = []
        for hh in range(heads):
            m, acc = carry[hh]
            piece(hh, 0)
            m_new = stage_max(slot, hh, m, masked)
            piece(hh, 1)
            acc = stage_exp(j, slot, hh, m_new, jnp.exp2(m - m_new) * acc, 0)
            acc = stage_exp(j, slot, hh, m_new, acc, 1)
            new.append((m_new, acc))
        return tuple(new)

    init = tuple((jnp.full((1, tq), NEG_BIG, F32), jnp.zeros((V_DIM + BF16_ROWS, tq), F32))
                 for _ in range(heads))
    scores(0, 0)

    def body(i, c):
        c = block(2 * i, 0, c, False, nxt=(2 * i + 1, 1))
        return block(2 * i + 1, 1, c, False, nxt=(2 * i + 2, 0))

    carry = lax.fori_loop(0, qi // 2, body, init)

    def finish(c):
        ot = jnp.concatenate([acc[:V_DIM] / acc[V_DIM:V_DIM + 1] for _, acc in c], axis=0)
        o_ref[0] = ot.T.astype(BF16)

    @pl.when(qi % 2 == 0)
    def _():
        finish(block(qi, 0, carry, True))

    @pl.when(qi % 2 == 1)
    def _():
        finish(block(qi, 1, block(qi - 1, 0, carry, False, nxt=(qi, 1)), True))


def _mla_attention(qp, kp, vt, *, tq):
    B, S, _ = qp.shape
    tq = min(tq, S)
    pairs = MLA_HEADS // 2
    kern = functools.partial(_mla_kernel, tq=tq)
    return pl.pallas_call(
        kern,
        grid=(B, pairs, S // tq),
        in_specs=[
            pl.BlockSpec((1, tq, 2 * LANES), lambda b, p, i: (b, i, p)),
            pl.BlockSpec((1, S, 2 * LANES), lambda b, p, i: (b, 0, p)),
            pl.BlockSpec((1, 2 * V_DIM, S), lambda b, p, i: (b, p, 0)),
        ],
        out_specs=pl.BlockSpec((1, tq, LANES), lambda b, p, i: (b, i, p)),
        out_shape=jax.ShapeDtypeStruct((B, S, MLA_WIDTH), BF16),
        scratch_shapes=[pltpu.VMEM((tq, tq), F32) for _ in range(4)],
        compiler_params=_params("arbitrary", "arbitrary", "arbitrary"),
        name="mla_attention",
    )(qp, kp, vt)


def _odd_out_kernel(x_ref, hm_ref, ha_ref, wout_ref, xg_ref, wq_ref, kv_ref, wo_ref, o_ref):
    mix = jnp.concatenate([hm_ref[0], ha_ref[0]], axis=-1)
    x1 = x_ref[0] + _dot(mix, wout_ref[...])
    o_ref[0] = _xattn_body(x1, xg_ref, wq_ref, kv_ref, wo_ref)


def _odd_out(x, hm, ha, w_out, xg, wq_all, kv, wo_all, layer, *, ts):
    B, S, D = x.shape
    ts = min(ts, S)
    tok = lambda n: pl.BlockSpec((1, ts, n), lambda b, s: (b, s, 0))
    return pl.pallas_call(
        _odd_out_kernel,
        grid=(B, S // ts),
        in_specs=[tok(D), tok(ML_WIDTH), tok(MLA_WIDTH), _const_spec(w_out.shape)]
        + _xattn_specs(D, wq_all, kv, wo_all, layer),
        out_specs=tok(D),
        out_shape=jax.ShapeDtypeStruct(x.shape, F32),
        compiler_params=_params("arbitrary", "arbitrary"),
        name="odd_out",
    )(x, hm, ha, w_out, xg.reshape(1, D), wq_all, kv, wo_all)


def _pack_odd_weights(w_in, w_uq, w_ukv):
    D = w_in.shape[0]
    c = 4 * ML_WIDTH
    w_g = w_in[:, c:c + 2 * ML_HEADS]
    c += 2 * ML_HEADS
    w_cq = w_in[:, c:c + Q_LORA]
    c += Q_LORA
    w_ckv = w_in[:, c:c + KV_LORA]
    c += KV_LORA
    w_kr = w_in[:, c:c + QK_ROPE]
    half = QK_ROPE // 2
    w_kr_sw = jnp.concatenate([w_kr[:, half:], w_kr[:, :half]], axis=1)
    zl = jnp.zeros((D, QK_NOPE), w_in.dtype)
    zr = jnp.zeros((D, LANES - QK_NOPE - QK_ROPE), w_in.dtype)
    zg = jnp.zeros((D, LANES - 2 * ML_HEADS), w_in.dtype)
    w_main = jnp.concatenate([w_in[:, :4 * ML_WIDTH], w_cq, w_ckv,
                              zl, w_kr, zr, zl, w_kr_sw, zr, w_g, zg], axis=1).astype(BF16)
    w_gt = w_g.T.astype(BF16)

    uq = w_uq.reshape(Q_LORA, MLA_HEADS, QK_NOPE + QK_ROPE)
    uq_n, uq_r = uq[..., :QK_NOPE], uq[..., QK_NOPE:]
    uq_r_sw = jnp.concatenate([uq_r[..., half:], uq_r[..., :half]], axis=-1)
    zpad = jnp.zeros((Q_LORA, MLA_HEADS, LANES - QK_NOPE - QK_ROPE), w_uq.dtype)
    w_uq_p = jnp.concatenate([uq_n, uq_r, zpad], axis=-1).reshape(Q_LORA, MLA_HEADS * LANES)
    w_uq_s = jnp.concatenate([jnp.zeros_like(uq_n), uq_r_sw, zpad], axis=-1)
    w_uq_s = w_uq_s.reshape(Q_LORA, MLA_HEADS * LANES)

    ukv = w_ukv.reshape(KV_LORA, MLA_HEADS, QK_NOPE + V_DIM)
    uk, uv = ukv[..., :QK_NOPE], ukv[..., QK_NOPE:]
    w_uk_p = jnp.concatenate([uk, jnp.zeros((KV_LORA, MLA_HEADS, LANES - QK_NOPE), w_ukv.dtype)],
                             axis=-1).reshape(KV_LORA, MLA_HEADS * LANES)
    w_uvt = uv.reshape(KV_LORA, MLA_WIDTH).T
    return (w_main, w_gt, w_uq_p.astype(BF16), w_uq_s.astype(BF16), w_uk_p.astype(BF16),
            w_uvt.astype(BF16))


def _rope_inv_freq():
    half = QK_ROPE // 2
    inv = ROPE_THETA ** (-jnp.arange(half, dtype=F32) / half)
    return inv.reshape(half, 1)


TS_EVEN = 1024
TM_FFN = 1024
TH_FFN = 256
TS_ODD = 512
L_MLSTM = 256
TQ_MLA = 512
TS_OUT = 1024


def kernel(x, mem, positions, norm_mix_g, norm_xattn_g, mem_norm_g, xattn_wq, xattn_wkv, xattn_wo,
           norm_ffn_g, ffn_w_gate_up, ffn_w_down, ev_w_in, ev_conv_w, ev_pool_w, ev_pool_scale,
           ev_w_out, od_w_in, od_gate_bias, od_ml_norm_g, od_q_norm_g, od_kv_norm_g, od_w_uq,
           od_w_ukv, od_w_out, final_norm_g):
    depth = norm_mix_g.shape[0]
    B, S, D = x.shape
    pos3 = positions.reshape(B, 1, S)
    invf = _rope_inv_freq()
    wq_all, wkv_all, wo_all = (w.astype(BF16) for w in (xattn_wq, xattn_wkv, xattn_wo))
    w_gu_all, w_d_all = ffn_w_gate_up.astype(BF16), ffn_w_down.astype(BF16)
    for layer in range(depth):
        kv = _mem_kv(mem, mem_norm_g[layer], wkv_all, layer)
        xa = (norm_xattn_g[layer], wq_all, kv, wo_all, layer)
        if layer % 2 == 0:
            e = layer // 2
            x = _even_mixer(x, norm_mix_g[layer], ev_w_in[e].astype(BF16), ev_conv_w[e],
                            ev_pool_w[e].astype(BF16), ev_pool_scale[e], ev_w_out[e].astype(BF16),
                            *xa, ts=TS_EVEN)
        else:
            o = layer // 2
            w_main, w_gt, w_uq_p, w_uq_s, w_uk_p, w_uvt = _pack_odd_weights(
                od_w_in[o], od_w_uq[o], od_w_ukv[o])
            gbias_row = jnp.pad(od_gate_bias[o], (0, LANES - 2 * ML_HEADS)).reshape(1, LANES)
            qm, km, vm, om, gates_t, gates_c, qp, kp, vt = _odd_proj(
                x, pos3, norm_mix_g[layer], w_main, w_gt, od_gate_bias[o].reshape(2 * ML_HEADS, 1),
                gbias_row, od_q_norm_g[o].reshape(1, Q_LORA), od_kv_norm_g[o].reshape(1, KV_LORA),
                w_uq_p, w_uq_s, w_uk_p, w_uvt, invf, ts=TS_ODD)
            hm = _mlstm(qm, km, vm, om, gates_t, gates_c, od_ml_norm_g[o], L=L_MLSTM)
            ha = _mla_attention(qp, kp, vt, tq=TQ_MLA)
            x = _odd_out(x, hm, ha, od_w_out[o].astype(BF16), *xa, ts=TS_OUT)
        x = _swiglu(x, norm_ffn_g[layer], w_gu_all, w_d_all, final_norm_g, layer,
                    tm=TM_FFN, th=TH_FFN, final_norm=(layer == depth - 1))
    return x
```

```python
import functools

import jax
import jax.numpy as jnp
from jax import lax
from jax.experimental import pallas as pl
from jax.experimental.pallas import tpu as pltpu

F32 = jnp.float32
BF16 = jnp.bfloat16

EPS = 1e-6
ROPE_THETA = 10000.0
LANES = 128
SUBLANES = 8
BF16_ROWS = 16
VMEM_LIMIT_BYTES = 56 * 1024 * 1024
NEG_BIG = -1e30
LOG2_E = 1.4426950408889634

SC_WIDTH = 512
CONV_K = 3
POOL_WINDOWS = (2, 4, 8, 16)
POOL_GROUP = 128
HALO = 32
ML_HEADS = 4
ML_HEAD_DIM = 128
ML_WIDTH = ML_HEADS * ML_HEAD_DIM
MLA_HEADS = 8
QK_NOPE = 64
QK_ROPE = 32
V_DIM = 64
Q_LORA = 384
KV_LORA = 256
MLA_WIDTH = MLA_HEADS * V_DIM
XA_HEADS = 4
XA_HEAD_DIM = 128
XA_WIDTH = XA_HEADS * XA_HEAD_DIM


def _params(*sem):
    return pltpu.CompilerParams(dimension_semantics=sem, vmem_limit_bytes=VMEM_LIMIT_BYTES)


def _const_spec(shape):
    nd = len(shape)
    return pl.BlockSpec(shape, lambda *_: (0,) * nd, pipeline_mode=pl.Buffered(1))


def _layer_spec(stacked, layer):
    return pl.BlockSpec((None,) + stacked.shape[1:], lambda *_: (layer, 0, 0),
                        pipeline_mode=pl.Buffered(1))


def _rms(x, g):
    ms = jnp.mean(x * x, axis=-1, keepdims=True)
    return x * lax.rsqrt(ms + EPS) * g


def _dot(a, b):
    return jnp.dot(a, b, preferred_element_type=F32)


def _dot_nt(a, b):
    return lax.dot_general(a, b, (((1,), (1,)), ((), ())), preferred_element_type=F32)


def _even_kernel(x_ref, g_ref, win_ref, convw_ref, poolw_ref, pscale_ref, wout_ref,
                 xg_ref, wq_ref, kt_ref, v_ref, wo_ref, o_ref, ubuf, xbbuf, tmp_a, tmp_b, *, ts):
    si = pl.program_id(1)

    @pl.when(si == 0)
    def _():
        ubuf[0:HALO, :] = jnp.zeros((HALO, SC_WIDTH), F32)
        xbbuf[0:HALO, :] = jnp.zeros((HALO, SC_WIDTH), F32)

    x = x_ref[0]
    h = _rms(x, g_ref[...]).astype(BF16)
    z = _dot(h, win_ref[...])
    g_b = z[:, 0:SC_WIDTH]
    u = z[:, SC_WIDTH:2 * SC_WIDTH] * z[:, 2 * SC_WIDTH:3 * SC_WIDTH]
    xb = z[:, 3 * SC_WIDTH:4 * SC_WIDTH]
    ubuf[HALO:HALO + ts, :] = u
    xbbuf[HALO:HALO + ts, :] = xb

    cw = convw_ref[...]
    conv = cw[2:3, :] * u
    for k in range(CONV_K - 1):
        back = CONV_K - 1 - k
        conv = conv + cw[k:k + 1, :] * ubuf[HALO - back:HALO - back + ts, :]
    ya = g_b * conv

    def window_level(src, c0, dst, r0, shift):
        w2 = (src[r0:HALO + ts, c0:c0 + POOL_GROUP]
              + src[r0 - shift:HALO + ts - shift, c0:c0 + POOL_GROUP])
        if dst is not None:
            dst[r0:HALO + ts, :] = w2
        return w2

    t_idx = si * ts + lax.broadcasted_iota(jnp.int32, (ts, POOL_GROUP), 0)
    ys = [ya.astype(BF16)]
    for gi, w in enumerate(POOL_WINDOWS):
        lo = gi * POOL_GROUP
        xg = xb[:, lo:lo + POOL_GROUP]
        src, c0, shift, level = xbbuf, lo, 1, 1
        while 2 * shift < w:
            dst = tmp_a if src is not tmp_a else tmp_b
            window_level(src, c0, dst, SUBLANES * level, shift)
            src, c0, shift, level = dst, 0, 2 * shift, level + 1
        acc = window_level(src, c0, None, HALO, shift)
        cnt = jnp.minimum(t_idx + 1, w).astype(F32)
        d = (acc / cnt - xg).astype(BF16)
        yg = _dot(d, poolw_ref[gi]) * pscale_ref[:, lo:lo + POOL_GROUP]
        ys.append(yg.astype(BF16))

    ubuf[0:HALO, :] = ubuf[ts:ts + HALO, :]
    xbbuf[0:HALO, :] = xbbuf[ts:ts + HALO, :]

    mix = jnp.concatenate(ys, axis=-1)
    x1 = x + _dot(mix, wout_ref[...])
    o_ref[0] = _xattn_body(x1, xg_ref, wq_ref, kt_ref, v_ref, wo_ref)


def _even_mixer(x, g, w_in, conv_w, pool_w, pool_scale, w_out, xg, wq_all, kv, wo_all, layer, *, ts):
    B, S, D = x.shape
    ts = min(ts, S)
    kern = functools.partial(_even_kernel, ts=ts)
    return pl.pallas_call(
        kern,
        grid=(B, S // ts),
        in_specs=[
            pl.BlockSpec((1, ts, D), lambda b, s: (b, s, 0)),
            _const_spec((1, D)),
            _const_spec(w_in.shape),
            _const_spec(conv_w.shape),
            _const_spec(pool_w.shape),
            _const_spec((1, SC_WIDTH)),
            _const_spec(w_out.shape),
        ] + _xattn_specs(D, wq_all, kv, wo_all, layer),
        out_specs=pl.BlockSpec((1, ts, D), lambda b, s: (b, s, 0)),
        out_shape=jax.ShapeDtypeStruct(x.shape, F32),
        scratch_shapes=[pltpu.VMEM((HALO + ts, SC_WIDTH), F32),
                        pltpu.VMEM((HALO + ts, SC_WIDTH), F32),
                        pltpu.VMEM((HALO + ts, POOL_GROUP), F32),
                        pltpu.VMEM((HALO + ts, POOL_GROUP), F32)],
        compiler_params=_params("arbitrary", "arbitrary"),
        name="even_mixer",
    )(x, g.reshape(1, D), w_in, conv_w, pool_w, pool_scale.reshape(1, SC_WIDTH), w_out,
      xg.reshape(1, D), wq_all, *kv, wo_all)


def _memkv_kernel(mem_ref, g_ref, wkv_ref, kt_ref, v_ref):
    h = _rms(mem_ref[0], g_ref[...]).astype(BF16)
    kv = _dot(h, wkv_ref[...])
    kt_ref[0] = kv[:, :XA_WIDTH].T.astype(BF16)
    v_ref[0] = kv[:, XA_WIDTH:].astype(BF16)


def _mem_kv(mem, g, wkv_all, layer):
    B, M, D = mem.shape
    return pl.pallas_call(
        _memkv_kernel,
        grid=(B,),
        in_specs=[pl.BlockSpec((1, M, D), lambda b: (b, 0, 0)),
                  _const_spec((1, D)), _layer_spec(wkv_all, layer)],
        out_specs=(pl.BlockSpec((1, XA_WIDTH, M), lambda b: (b, 0, 0)),
                   pl.BlockSpec((1, M, XA_WIDTH), lambda b: (b, 0, 0))),
        out_shape=(jax.ShapeDtypeStruct((B, XA_WIDTH, M), BF16),
                   jax.ShapeDtypeStruct((B, M, XA_WIDTH), BF16)),
        compiler_params=_params("arbitrary"),
        name="mem_kv",
    )(mem, g.reshape(1, D), wkv_all)


def _xattn_body(x, g_ref, wq_ref, kt_ref, v_ref, wo_ref):
    h = _rms(x, g_ref[...]).astype(BF16)
    q = (_dot(h, wq_ref[...]) * (XA_HEAD_DIM ** -0.5)).astype(BF16)
    ones_v = jnp.ones((v_ref.shape[1], XA_HEAD_DIM), BF16)
    outs = []
    for hd in range(XA_HEADS):
        lo = hd * XA_HEAD_DIM
        s = _dot(q[:, lo:lo + XA_HEAD_DIM], kt_ref[0, lo:lo + XA_HEAD_DIM, :])
        p = jnp.exp(s - jnp.max(s, axis=-1, keepdims=True)).astype(BF16)
        v_aug = jnp.concatenate([v_ref[0, :, lo:lo + XA_HEAD_DIM], ones_v], axis=1)
        o = _dot(p, v_aug)
        outs.append((o[:, :XA_HEAD_DIM] / o[:, XA_HEAD_DIM:]).astype(BF16))
    a = jnp.concatenate(outs, axis=-1)
    return x + _dot(a, wo_ref[...])


def _xattn_specs(D, wq_all, kv, wo_all, layer):
    kt, v = kv
    return [_const_spec((1, D)), _layer_spec(wq_all, layer),
            pl.BlockSpec((1,) + kt.shape[1:], lambda b, s: (b, 0, 0)),
            pl.BlockSpec((1,) + v.shape[1:], lambda b, s: (b, 0, 0)), _layer_spec(wo_all, layer)]


def _swiglu_kernel(x_ref, g_ref, wgu_ref, wd_ref, fg_ref, o_ref, *, hidden, th, final_norm):
    x = x_ref[...]
    h = _rms(x, g_ref[...]).astype(BF16)
    acc = x
    for j in range(hidden // th):
        gate = _dot(h, wgu_ref[:, j * th:(j + 1) * th])
        up = _dot(h, wgu_ref[:, hidden + j * th:hidden + (j + 1) * th])
        act = (gate * jax.nn.sigmoid(gate) * up).astype(BF16)
        acc = acc + _dot(act, wd_ref[j * th:(j + 1) * th, :])
    if final_norm:
        acc = _rms(acc, fg_ref[...])
    o_ref[...] = acc


def _swiglu(x, g, w_gu_all, w_d_all, final_g, layer, *, tm, th, final_norm):
    B, S, D = x.shape
    T = B * S
    tm = min(tm, T)
    hidden = w_d_all.shape[1]
    kern = functools.partial(_swiglu_kernel, hidden=hidden, th=th, final_norm=final_norm)
    out = pl.pallas_call(
        kern,
        grid=(T // tm,),
        in_specs=[
            pl.BlockSpec((tm, D), lambda i: (i, 0)),
            _const_spec((1, D)),
            _layer_spec(w_gu_all, layer),
            _layer_spec(w_d_all, layer),
            _const_spec((1, D)),
        ],
        out_specs=pl.BlockSpec((tm, D), lambda i: (i, 0)),
        out_shape=jax.ShapeDtypeStruct((T, D), F32),
        compiler_params=_params("arbitrary"),
        name="swiglu",
    )(x.reshape(T, D), g.reshape(1, D), w_gu_all, w_d_all, final_g.reshape(1, D))
    return out.reshape(B, S, D)


def _odd_proj_kernel(x_ref, pos_ref, g_ref, wmain_ref, wgt_ref, gbias_ref, gbias_row_ref,
                     qg_ref, kvg_ref, wuq_ref, wuqs_ref, wuk_ref, wuvt_ref, invf_ref,
                     qm_ref, km_ref, vm_ref, om_ref, gates_ref, gates_c_ref, qp_ref, kp_ref, vt_ref):
    x = x_ref[0]
    h = _rms(x, g_ref[...]).astype(BF16)
    c0 = 4 * ML_WIDTH

    def ml_cols(i):
        return _dot(h, wmain_ref[:, i * ML_WIDTH:(i + 1) * ML_WIDTH])

    zl = _dot(h, wmain_ref[:, c0:])
    c_q = zl[:, 0:Q_LORA]
    c_kv = zl[:, Q_LORA:Q_LORA + KV_LORA]
    kr = zl[:, Q_LORA + KV_LORA:Q_LORA + KV_LORA + LANES]
    kr_sw = zl[:, Q_LORA + KV_LORA + LANES:Q_LORA + KV_LORA + 2 * LANES]
    c_g = Q_LORA + KV_LORA + 2 * LANES
    gates_c_ref[0] = zl[:, c_g:c_g + LANES] + gbias_row_ref[...]
    gates_ref[0] = _dot_nt(wgt_ref[...], h) + gbias_ref[...]

    half = QK_ROPE // 2
    ang_t = invf_ref[...] * pos_ref[0].astype(F32)
    cos_c, sin_c = jnp.cos(ang_t), jnp.sin(ang_t)
    ts = ang_t.shape[1]
    pad_lo = jnp.zeros((QK_NOPE, ts), F32)
    pad_hi = jnp.zeros((LANES - QK_NOPE - QK_ROPE, ts), F32)
    cos_t = jnp.concatenate([pad_lo + 1.0, cos_c, cos_c, pad_hi], axis=0).T
    sin_t = jnp.concatenate([pad_lo, -sin_c, sin_c, pad_hi], axis=0).T
    scale = (QK_NOPE + QK_ROPE) ** -0.5 * LOG2_E
    cos_q, sin_q = cos_t * scale, sin_t * scale
    qm_ref[0] = ml_cols(0).astype(BF16)
    km_ref[0] = (ml_cols(1) * (ML_HEAD_DIM ** -0.5)).astype(BF16)

    cqn = _rms(c_q, qg_ref[...]).astype(BF16)
    qa = _dot(cqn, wuq_ref[...])
    qb = _dot(cqn, wuqs_ref[...])
    ckn = _rms(c_kv, kvg_ref[...]).astype(BF16)
    kn = _dot(ckn, wuk_ref[...])
    vt_ref[0] = _dot_nt(wuvt_ref[...], ckn).astype(BF16)
    k_rope = kr * cos_t + kr_sw * sin_t

    def rope_heads(first, last):
        for hd in range(first, last):
            lo = hd * LANES
            qp_ref[0, :, lo:lo + LANES] = (qa[:, lo:lo + LANES] * cos_q
                                           + qb[:, lo:lo + LANES] * sin_q).astype(BF16)
            kp_ref[0, :, lo:lo + LANES] = (kn[:, lo:lo + LANES] + k_rope).astype(BF16)

    vm_ref[0] = ml_cols(2).astype(BF16)
    rope_heads(0, MLA_HEADS // 2)
    om_ref[0] = ml_cols(3).astype(BF16)
    rope_heads(MLA_HEADS // 2, MLA_HEADS)


def _odd_proj(x, pos3, g, w_main, w_gt, gbias, gbias_row, q_g, kv_g, w_uq_p, w_uq_s, w_uk_p, w_uvt,
              invf, *, ts):
    B, S, D = x.shape
    ts = min(ts, S)
    tok = lambda n: pl.BlockSpec((1, ts, n), lambda b, s: (b, s, 0))
    out_shapes = (
        jax.ShapeDtypeStruct((B, S, ML_WIDTH), BF16),
        jax.ShapeDtypeStruct((B, S, ML_WIDTH), BF16),
        jax.ShapeDtypeStruct((B, S, ML_WIDTH), BF16),
        jax.ShapeDtypeStruct((B, S, ML_WIDTH), BF16),
        jax.ShapeDtypeStruct((B, 2 * ML_HEADS, S), F32),
        jax.ShapeDtypeStruct((B, S, LANES), F32),
        jax.ShapeDtypeStruct((B, S, MLA_HEADS * LANES), BF16),
        jax.ShapeDtypeStruct((B, S, MLA_HEADS * LANES), BF16),
        jax.ShapeDtypeStruct((B, MLA_WIDTH, S), BF16),
    )
    return pl.pallas_call(
        _odd_proj_kernel,
        grid=(B, S // ts),
        in_specs=[
            tok(D),
            pl.BlockSpec((1, 1, ts), lambda b, s: (b, 0, s)),
            _const_spec((1, D)),
            _const_spec(w_main.shape),
            _const_spec(w_gt.shape),
            _const_spec(gbias.shape),
            _const_spec(gbias_row.shape),
            _const_spec(q_g.shape),
            _const_spec(kv_g.shape),
            _const_spec(w_uq_p.shape),
            _const_spec(w_uq_s.shape),
            _const_spec(w_uk_p.shape),
            _const_spec(w_uvt.shape),
            _const_spec(invf.shape),
        ],
        out_specs=(tok(ML_WIDTH), tok(ML_WIDTH), tok(ML_WIDTH), tok(ML_WIDTH),
                   pl.BlockSpec((1, 2 * ML_HEADS, ts), lambda b, s: (b, 0, s)), tok(LANES),
                   tok(MLA_HEADS * LANES), tok(MLA_HEADS * LANES),
                   pl.BlockSpec((1, MLA_WIDTH, ts), lambda b, s: (b, 0, s))),
        out_shape=out_shapes,
        compiler_params=_params("arbitrary", "arbitrary"),
        name="odd_proj",
    )(x, pos3, g.reshape(1, D), w_main, w_gt, gbias, gbias_row, q_g, kv_g, w_uq_p, w_uq_s, w_uk_p,
      w_uvt, invf)


def _split3(a):
    hi = a.astype(BF16)
    r1 = a - hi.astype(F32)
    mid = r1.astype(BF16)
    lo = (r1 - mid.astype(F32)).astype(BF16)
    return hi, mid, lo


def _lane_sum_mxu(a):
    ones = jnp.ones((a.shape[1], LANES), BF16)
    return sum(_dot(t, ones) for t in _split3(a))


def _log2_sigmoid(x):
    return (jnp.minimum(x, 0.0) - jnp.log(1.0 + jnp.exp(-jnp.abs(x)))) * LOG2_E


def _mlstm_kernel(q_ref, k_ref, v_ref, o_ref, gt_ref, gc_ref, ng_ref, h_ref, c_scr, m_scr, *, L):
    ci = pl.program_id(1)

    @pl.when(ci == 0)
    def _():
        c_scr[...] = jnp.zeros(c_scr.shape, F32)
        m_scr[...] = jnp.zeros(m_scr.shape, F32)

    row = lax.broadcasted_iota(jnp.int32, (L, L), 0)
    col = lax.broadcasted_iota(jnp.int32, (L, L), 1)
    causal = col <= row
    gt = gt_ref[0]
    ones_v = jnp.ones((L, LANES), BF16)
    reps = L // LANES

    def widen(a):
        return jnp.concatenate([a] * reps, axis=1)

    lf_t = _log2_sigmoid(gt)
    b_t = sum(_dot(t, (row <= col).astype(BF16)) for t in _split3(lf_t))
    tril = causal.astype(BF16)
    b_c = sum(_dot(tril, t) for t in _split3(_log2_sigmoid(gc_ref[0])))

    for hd in range(ML_HEADS):
        lo = hd * ML_HEAD_DIM
        fg = ML_HEADS + hd
        q = q_ref[0, :, lo:lo + ML_HEAD_DIM]
        k = k_ref[0, :, lo:lo + ML_HEAD_DIM]
        v_aug = jnp.concatenate([v_ref[0, :, lo:lo + ML_HEAD_DIM], ones_v], axis=1)
        g_row = gt[hd:hd + 1, :] * LOG2_E - b_t[fg:fg + 1, :]
        gtot = jnp.sum(lf_t[fg:fg + 1, :], axis=1, keepdims=True)
        b_col = jnp.broadcast_to(b_c[:, fg:fg + 1], (L, LANES))

        m_prev = m_scr[hd]
        c_prev = c_scr[hd]

        g_mask = jnp.where(causal, g_row, NEG_BIG)
        gmax = jnp.max(g_mask, axis=1, keepdims=True)
        h_col = jnp.maximum(m_prev, jnp.broadcast_to(gmax, (L, LANES)))
        w_inter = jnp.exp2(m_prev - h_col)
        sw = jnp.exp2(g_mask - widen(h_col)) * _dot_nt(q, k)
        out = (jnp.concatenate([w_inter, w_inter], axis=1) * _dot(q, c_prev.astype(BF16))
               + _dot(sw.astype(BF16), v_aug))
        den = jnp.maximum(jnp.abs(out[:, ML_HEAD_DIM:]), jnp.exp2(-(b_col + h_col)))
        hc = out[:, :ML_HEAD_DIM] / den

        og = jax.nn.sigmoid(o_ref[0, :, lo:lo + ML_HEAD_DIM].astype(F32))
        hg = og * hc
        ms = _lane_sum_mxu(hg * hg) * (1.0 / ML_HEAD_DIM)
        h_ref[0, :, lo:lo + ML_HEAD_DIM] = (
            hg * lax.rsqrt(ms + EPS) * ng_ref[:, lo:lo + ML_HEAD_DIM]).astype(BF16)

        m_prev1 = m_prev[:, 0:1]
        a_row = gtot + g_row
        m_loc = jnp.max(a_row, axis=1, keepdims=True)
        w_loc = jnp.exp2(a_row - m_loc)
        m_new = jnp.maximum(gtot + m_prev1, m_loc)
        s_old = jnp.exp2(gtot + m_prev1 - m_new)
        s_new = jnp.exp2(m_loc - m_new)
        kw_t = (k.T.astype(F32) * w_loc).astype(BF16)
        c_scr[hd] = s_old * c_prev + s_new * _dot(kw_t, v_aug)
        m_scr[hd] = jnp.broadcast_to(m_new, (1, LANES))


def _mlstm(q, k, v, o, gates_t, gates_c, norm_g, *, L):
    B, S, W = q.shape
    L = min(L, S)
    kern = functools.partial(_mlstm_kernel, L=L)
    tok = pl.BlockSpec((1, L, W), lambda b, c: (b, c, 0))
    return pl.pallas_call(
        kern,
        grid=(B, S // L),
        in_specs=[tok, tok, tok, tok,
                  pl.BlockSpec((1, 2 * ML_HEADS, L), lambda b, c: (b, 0, c)),
                  pl.BlockSpec((1, L, LANES), lambda b, c: (b, c, 0)),
                  _const_spec((1, W))],
        out_specs=tok,
        out_shape=jax.ShapeDtypeStruct((B, S, W), BF16),
        scratch_shapes=[pltpu.VMEM((ML_HEADS, ML_HEAD_DIM, 2 * ML_HEAD_DIM), F32),
                        pltpu.VMEM((ML_HEADS, 1, LANES), F32)],
        compiler_params=_params("arbitrary", "arbitrary"),
        name="mlstm",
    )(q, k, v, o, gates_t, gates_c, norm_g.reshape(1, W))


def _mla_kernel(q_ref, k_ref, vt_ref, o_ref, s00, s01, s10, s11, qt_scr, *, tq):
    qi = pl.program_id(2)
    heads = 2
    pieces = 2
    rows = tq // pieces
    s_scr = ((s00, s01), (s10, s11))
    for hh in range(heads):
        qt_scr[hh] = q_ref[0, :, hh * LANES:(hh + 1) * LANES].T
    ones_rows = jnp.ones((BF16_ROWS, tq), BF16)

    def score_piece(j, slot, hh, pc):
        start = pl.multiple_of(j * tq + pc * rows, rows)
        kb = k_ref[0, pl.ds(start, rows), hh * LANES:(hh + 1) * LANES]
        s_scr[slot][hh][pc * rows:(pc + 1) * rows, :] = _dot(kb, qt_scr[hh])

    def scores(j, slot):
        for hh in range(heads):
            for pc in range(pieces):
                score_piece(j, slot, hh, pc)

    def stage_max(slot, hh, m, masked):
        s = s_scr[slot][hh][...]
        if masked:
            kk = lax.broadcasted_iota(jnp.int32, s.shape, 0)
            qq = lax.broadcasted_iota(jnp.int32, s.shape, 1)
            s = jnp.where(kk <= qq, s, NEG_BIG)
            s_scr[slot][hh][...] = s
        return jnp.maximum(m, jnp.max(s, axis=0, keepdims=True))

    def stage_exp(j, slot, hh, m_new, acc, half):
        hk = tq // 2
        start = pl.multiple_of(j * tq + half * hk, hk)
        vtb = jnp.concatenate([vt_ref[0, hh * V_DIM:(hh + 1) * V_DIM, pl.ds(start, hk)],
                               ones_rows[:, :hk]], axis=0)
        p = jnp.exp2(s_scr[slot][hh][half * hk:(half + 1) * hk, :] - m_new).astype(BF16)
        return acc + _dot(vtb, p)

    def block(j, slot, carry, masked, nxt=None):
        def piece(hh, pc):
            if nxt is not None:
                score_piece(nxt[0], nxt[1], hh, pc)

        new = []
        for hh in range(heads):
            m, acc = carry[hh]
            piece(hh, 0)
            m_new = stage_max(slot, hh, m, masked)
            piece(hh, 1)
            acc = stage_exp(j, slot, hh, m_new, jnp.exp2(m - m_new) * acc, 0)
            acc = stage_exp(j, slot, hh, m_new, acc, 1)
            new.append((m_new, acc))
        return tuple(new)

    init = tuple((jnp.full((1, tq), NEG_BIG, F32), jnp.zeros((V_DIM + BF16_ROWS, tq), F32))
                 for _ in range(heads))
    scores(0, 0)

    def body(i, c):
        c = block(2 * i, 0, c, False, nxt=(2 * i + 1, 1))
        return block(2 * i + 1, 1, c, False, nxt=(2 * i + 2, 0))

    carry = lax.fori_loop(0, qi // 2, body, init)

    def finish(c):
        ot = jnp.concatenate([acc[:V_DIM] / acc[V_DIM:V_DIM + 1] for _, acc in c], axis=0)
        o_ref[0] = ot.T.astype(BF16)

    @pl.when(qi % 2 == 0)
    def _():
        finish(block(qi, 0, carry, True))

    @pl.when(qi % 2 == 1)
    def _():
        finish(block(qi, 1, block(qi - 1, 0, carry, False, nxt=(qi, 1)), True))


def _mla_attention(qp, kp, vt, *, tq):
    B, S, _ = qp.shape
    tq = min(tq, S)
    pairs = MLA_HEADS // 2
    kern = functools.partial(_mla_kernel, tq=tq)
    return pl.pallas_call(
        kern,
        grid=(B, pairs, S // tq),
        in_specs=[
            pl.BlockSpec((1, tq, 2 * LANES), lambda b, p, i: (b, i, p)),
            pl.BlockSpec((1, S, 2 * LANES), lambda b, p, i: (b, 0, p)),
            pl.BlockSpec((1, 2 * V_DIM, S), lambda b, p, i: (b, p, 0)),
        ],
        out_specs=pl.BlockSpec((1, tq, LANES), lambda b, p, i: (b, i, p)),
        out_shape=jax.ShapeDtypeStruct((B, S, MLA_WIDTH), BF16),
        scratch_shapes=[pltpu.VMEM((tq, tq), F32) for _ in range(4)]
        + [pltpu.VMEM((2, LANES, tq), BF16)],
        compiler_params=_params("arbitrary", "arbitrary", "arbitrary"),
        name="mla_attention",
    )(qp, kp, vt)


def _odd_out_kernel(x_ref, hm_ref, ha_ref, wout_ref, xg_ref, wq_ref, kt_ref, v_ref, wo_ref, o_ref):
    mix = jnp.concatenate([hm_ref[0], ha_ref[0]], axis=-1)
    x1 = x_ref[0] + _dot(mix, wout_ref[...])
    o_ref[0] = _xattn_body(x1, xg_ref, wq_ref, kt_ref, v_ref, wo_ref)


def _odd_out(x, hm, ha, w_out, xg, wq_all, kv, wo_all, layer, *, ts):
    B, S, D = x.shape
    ts = min(ts, S)
    tok = lambda n: pl.BlockSpec((1, ts, n), lambda b, s: (b, s, 0))
    return pl.pallas_call(
        _odd_out_kernel,
        grid=(B, S // ts),
        in_specs=[tok(D), tok(ML_WIDTH), tok(MLA_WIDTH), _const_spec(w_out.shape)]
        + _xattn_specs(D, wq_all, kv, wo_all, layer),
        out_specs=tok(D),
        out_shape=jax.ShapeDtypeStruct(x.shape, F32),
        compiler_params=_params("arbitrary", "arbitrary"),
        name="odd_out",
    )(x, hm, ha, w_out, xg.reshape(1, D), wq_all, *kv, wo_all)


def _pack_odd_weights(w_in, w_uq, w_ukv):
    D = w_in.shape[0]
    c = 4 * ML_WIDTH
    w_g = w_in[:, c:c + 2 * ML_HEADS]
    c += 2 * ML_HEADS
    w_cq = w_in[:, c:c + Q_LORA]
    c += Q_LORA
    w_ckv = w_in[:, c:c + KV_LORA]
    c += KV_LORA
    w_kr = w_in[:, c:c + QK_ROPE]
    half = QK_ROPE // 2
    w_kr_sw = jnp.concatenate([w_kr[:, half:], w_kr[:, :half]], axis=1)
    zl = jnp.zeros((D, QK_NOPE), w_in.dtype)
    zr = jnp.zeros((D, LANES - QK_NOPE - QK_ROPE), w_in.dtype)
    zg = jnp.zeros((D, LANES - 2 * ML_HEADS), w_in.dtype)
    w_main = jnp.concatenate([w_in[:, :4 * ML_WIDTH], w_cq, w_ckv,
                              zl, w_kr, zr, zl, w_kr_sw, zr, w_g, zg], axis=1).astype(BF16)
    w_gt = w_g.T.astype(BF16)

    uq = w_uq.reshape(Q_LORA, MLA_HEADS, QK_NOPE + QK_ROPE)
    uq_n, uq_r = uq[..., :QK_NOPE], uq[..., QK_NOPE:]
    uq_r_sw = jnp.concatenate([uq_r[..., half:], uq_r[..., :half]], axis=-1)
    zpad = jnp.zeros((Q_LORA, MLA_HEADS, LANES - QK_NOPE - QK_ROPE), w_uq.dtype)
    w_uq_p = jnp.concatenate([uq_n, uq_r, zpad], axis=-1).reshape(Q_LORA, MLA_HEADS * LANES)
    w_uq_s = jnp.concatenate([jnp.zeros_like(uq_n), uq_r_sw, zpad], axis=-1)
    w_uq_s = w_uq_s.reshape(Q_LORA, MLA_HEADS * LANES)

    ukv = w_ukv.reshape(KV_LORA, MLA_HEADS, QK_NOPE + V_DIM)
    uk, uv = ukv[..., :QK_NOPE], ukv[..., QK_NOPE:]
    w_uk_p = jnp.concatenate([uk, jnp.zeros((KV_LORA, MLA_HEADS, LANES - QK_NOPE), w_ukv.dtype)],
                             axis=-1).reshape(KV_LORA, MLA_HEADS * LANES)
    w_uvt = uv.reshape(KV_LORA, MLA_WIDTH).T
    return (w_main, w_gt, w_uq_p.astype(BF16), w_uq_s.astype(BF16), w_uk_p.astype(BF16),
            w_uvt.astype(BF16))


def _rope_inv_freq():
    half = QK_ROPE // 2
    inv = ROPE_THETA ** (-jnp.arange(half, dtype=F32) / half)
    return inv.reshape(half, 1)


TS_EVEN = 1024
TM_FFN = 1024
TH_FFN = 256
TS_ODD = 512
L_MLSTM = 256
TQ_MLA = 512
TS_OUT = 1024


def kernel(x, mem, positions, norm_mix_g, norm_xattn_g, mem_norm_g, xattn_wq, xattn_wkv, xattn_wo,
           norm_ffn_g, ffn_w_gate_up, ffn_w_down, ev_w_in, ev_conv_w, ev_pool_w, ev_pool_scale,
           ev_w_out, od_w_in, od_gate_bias, od_ml_norm_g, od_q_norm_g, od_kv_norm_g, od_w_uq,
           od_w_ukv, od_w_out, final_norm_g):
    depth = norm_mix_g.shape[0]
    B, S, D = x.shape
    pos3 = positions.reshape(B, 1, S)
    invf = _rope_inv_freq()
    wq_all, wkv_all, wo_all = (w.astype(BF16) for w in (xattn_wq, xattn_wkv, xattn_wo))
    w_gu_all, w_d_all = ffn_w_gate_up.astype(BF16), ffn_w_down.astype(BF16)
    for layer in range(depth):
        kv = _mem_kv(mem, mem_norm_g[layer], wkv_all, layer)
        xa = (norm_xattn_g[layer], wq_all, kv, wo_all, layer)
        if layer % 2 == 0:
            e = layer // 2
            x = _even_mixer(x, norm_mix_g[layer], ev_w_in[e].astype(BF16), ev_conv_w[e],
                            ev_pool_w[e].astype(BF16), ev_pool_scale[e], ev_w_out[e].astype(BF16),
                            *xa, ts=TS_EVEN)
        else:
            o = layer // 2
            w_main, w_gt, w_uq_p, w_uq_s, w_uk_p, w_uvt = _pack_odd_weights(
                od_w_in[o], od_w_uq[o], od_w_ukv[o])
            gbias_row = jnp.pad(od_gate_bias[o], (0, LANES - 2 * ML_HEADS)).reshape(1, LANES)
            qm, km, vm, om, gates_t, gates_c, qp, kp, vt = _odd_proj(
                x, pos3, norm_mix_g[layer], w_main, w_gt, od_gate_bias[o].reshape(2 * ML_HEADS, 1),
                gbias_row, od_q_norm_g[o].reshape(1, Q_LORA), od_kv_norm_g[o].reshape(1, KV_LORA),
                w_uq_p, w_uq_s, w_uk_p, w_uvt, invf, ts=TS_ODD)
            hm = _mlstm(qm, km, vm, om, gates_t, gates_c, od_ml_norm_g[o], L=L_MLSTM)
            ha = _mla_attention(qp, kp, vt, tq=TQ_MLA)
            x = _odd_out(x, hm, ha, od_w_out[o].astype(BF16), *xa, ts=TS_OUT)
        x = _swiglu(x, norm_ffn_g[layer], w_gu_all, w_d_all, final_norm_g, layer,
                    tm=TM_FFN, th=TH_FFN, final_norm=(layer == depth - 1))
    return x
```

```python
import functools

import jax
import jax.numpy as jnp
from jax import lax
from jax.experimental import pallas as pl
from jax.experimental.pallas import tpu as pltpu

F32 = jnp.float32
BF16 = jnp.bfloat16

EPS = 1e-6
ROPE_THETA = 10000.0
LANES = 128
SUBLANES = 8
BF16_ROWS = 16
VMEM_LIMIT_BYTES = 56 * 1024 * 1024
NEG_BIG = -1e30
LOG2_E = 1.4426950408889634

SC_WIDTH = 512
CONV_K = 3
POOL_WINDOWS = (2, 4, 8, 16)
POOL_GROUP = 128
HALO = 32
ML_HEADS = 4
ML_HEAD_DIM = 128
ML_WIDTH = ML_HEADS * ML_HEAD_DIM
MLA_HEADS = 8
QK_NOPE = 64
QK_ROPE = 32
V_DIM = 64
Q_LORA = 384
KV_LORA = 256
MLA_WIDTH = MLA_HEADS * V_DIM
XA_HEADS = 4
XA_HEAD_DIM = 128
XA_WIDTH = XA_HEADS * XA_HEAD_DIM


def _params(*sem):
    return pltpu.CompilerParams(dimension_semantics=sem, vmem_limit_bytes=VMEM_LIMIT_BYTES)


def _const_spec(shape):
    nd = len(shape)
    return pl.BlockSpec(shape, lambda *_: (0,) * nd, pipeline_mode=pl.Buffered(1))


def _layer_spec(stacked, layer):
    return pl.BlockSpec((None,) + stacked.shape[1:], lambda *_: (layer, 0, 0),
                        pipeline_mode=pl.Buffered(1))


def _rms(x, g):
    ms = jnp.mean(x * x, axis=-1, keepdims=True)
    return x * lax.rsqrt(ms + EPS) * g


def _dot(a, b):
    return jnp.dot(a, b, preferred_element_type=F32)


def _dot_nt(a, b):
    return lax.dot_general(a, b, (((1,), (1,)), ((), ())), preferred_element_type=F32)


def _even_kernel(x_ref, g_ref, win_ref, convw_ref, poolw_ref, pscale_ref, wout_ref,
                 xg_ref, wq_ref, kt_ref, v_ref, wo_ref, o_ref, ubuf, xbbuf, tmp_a, tmp_b, *, ts):
    si = pl.program_id(1)

    @pl.when(si == 0)
    def _():
        ubuf[0:HALO, :] = jnp.zeros((HALO, SC_WIDTH), F32)
        xbbuf[0:HALO, :] = jnp.zeros((HALO, SC_WIDTH), F32)

    x = x_ref[0]
    h = _rms(x, g_ref[...]).astype(BF16)
    z = _dot(h, win_ref[...])
    g_b = z[:, 0:SC_WIDTH]
    u = z[:, SC_WIDTH:2 * SC_WIDTH] * z[:, 2 * SC_WIDTH:3 * SC_WIDTH]
    xb = z[:, 3 * SC_WIDTH:4 * SC_WIDTH]
    ubuf[HALO:HALO + ts, :] = u
    xbbuf[HALO:HALO + ts, :] = xb

    cw = convw_ref[...]
    conv = cw[2:3, :] * u
    for k in range(CONV_K - 1):
        back = CONV_K - 1 - k
        conv = conv + cw[k:k + 1, :] * ubuf[HALO - back:HALO - back + ts, :]
    ya = g_b * conv

    def window_level(src, c0, dst, r0, shift):
        w2 = (src[r0:HALO + ts, c0:c0 + POOL_GROUP]
              + src[r0 - shift:HALO + ts - shift, c0:c0 + POOL_GROUP])
        if dst is not None:
            dst[r0:HALO + ts, :] = w2
        return w2

    t_idx = si * ts + lax.broadcasted_iota(jnp.int32, (ts, POOL_GROUP), 0)
    ys = [ya.astype(BF16)]
    for gi, w in enumerate(POOL_WINDOWS):
        lo = gi * POOL_GROUP
        xg = xb[:, lo:lo + POOL_GROUP]
        src, c0, shift, level = xbbuf, lo, 1, 1
        while 2 * shift < w:
            dst = tmp_a if src is not tmp_a else tmp_b
            window_level(src, c0, dst, SUBLANES * level, shift)
            src, c0, shift, level = dst, 0, 2 * shift, level + 1
        acc = window_level(src, c0, None, HALO, shift)
        cnt = jnp.minimum(t_idx + 1, w).astype(F32)
        d = (acc / cnt - xg).astype(BF16)
        yg = _dot(d, poolw_ref[gi]) * pscale_ref[:, lo:lo + POOL_GROUP]
        ys.append(yg.astype(BF16))

    ubuf[0:HALO, :] = ubuf[ts:ts + HALO, :]
    xbbuf[0:HALO, :] = xbbuf[ts:ts + HALO, :]

    mix = jnp.concatenate(ys, axis=-1)
    x1 = x + _dot(mix, wout_ref[...])
    o_ref[0] = _xattn_body(x1, xg_ref, wq_ref, kt_ref, v_ref, wo_ref)


def _even_mixer(x, g, w_in, conv_w, pool_w, pool_scale, w_out, xg, wq_all, kv, wo_all, layer, *, ts):
    B, S, D = x.shape
    ts = min(ts, S)
    kern = functools.partial(_even_kernel, ts=ts)
    return pl.pallas_call(
        kern,
        grid=(B, S // ts),
        in_specs=[
            pl.BlockSpec((1, ts, D), lambda b, s: (b, s, 0)),
            _const_spec((1, D)),
            _const_spec(w_in.shape),
            _const_spec(conv_w.shape),
            _const_spec(pool_w.shape),
            _const_spec((1, SC_WIDTH)),
            _const_spec(w_out.shape),
        ] + _xattn_specs(D, wq_all, kv, wo_all, layer),
        out_specs=pl.BlockSpec((1, ts, D), lambda b, s: (b, s, 0)),
        out_shape=jax.ShapeDtypeStruct(x.shape, F32),
        scratch_shapes=[pltpu.VMEM((HALO + ts, SC_WIDTH), F32),
                        pltpu.VMEM((HALO + ts, SC_WIDTH), F32),
                        pltpu.VMEM((HALO + ts, POOL_GROUP), F32),
                        pltpu.VMEM((HALO + ts, POOL_GROUP), F32)],
        compiler_params=_params("arbitrary", "arbitrary"),
        name="even_mixer",
    )(x, g.reshape(1, D), w_in, conv_w, pool_w, pool_scale.reshape(1, SC_WIDTH), w_out,
      xg.reshape(1, D), wq_all, *kv, wo_all)


def _memkv_kernel(mem_ref, g_ref, wkv_ref, kt_ref, v_ref):
    h = _rms(mem_ref[0], g_ref[...]).astype(BF16)
    kv = _dot(h, wkv_ref[...])
    kt_ref[0] = kv[:, :XA_WIDTH].T.astype(BF16)
    v_ref[0] = kv[:, XA_WIDTH:].astype(BF16)


def _mem_kv(mem, g, wkv_all, layer):
    B, M, D = mem.shape
    return pl.pallas_call(
        _memkv_kernel,
        grid=(B,),
        in_specs=[pl.BlockSpec((1, M, D), lambda b: (b, 0, 0)),
                  _const_spec((1, D)), _layer_spec(wkv_all, layer)],
        out_specs=(pl.BlockSpec((1, XA_WIDTH, M), lambda b: (b, 0, 0)),
                   pl.BlockSpec((1, M, XA_WIDTH), lambda b: (b, 0, 0))),
        out_shape=(jax.ShapeDtypeStruct((B, XA_WIDTH, M), BF16),
                   jax.ShapeDtypeStruct((B, M, XA_WIDTH), BF16)),
        compiler_params=_params("arbitrary"),
        name="mem_kv",
    )(mem, g.reshape(1, D), wkv_all)


def _xattn_body(x, g_ref, wq_ref, kt_ref, v_ref, wo_ref):
    h = _rms(x, g_ref[...]).astype(BF16)
    q = (_dot(h, wq_ref[...]) * (XA_HEAD_DIM ** -0.5)).astype(BF16)
    ones_v = jnp.ones((v_ref.shape[1], XA_HEAD_DIM), BF16)
    outs = []
    for hd in range(XA_HEADS):
        lo = hd * XA_HEAD_DIM
        s = _dot(q[:, lo:lo + XA_HEAD_DIM], kt_ref[0, lo:lo + XA_HEAD_DIM, :])
        p = jnp.exp(s - jnp.max(s, axis=-1, keepdims=True)).astype(BF16)
        v_aug = jnp.concatenate([v_ref[0, :, lo:lo + XA_HEAD_DIM], ones_v], axis=1)
        o = _dot(p, v_aug)
        outs.append((o[:, :XA_HEAD_DIM] / o[:, XA_HEAD_DIM:]).astype(BF16))
    a = jnp.concatenate(outs, axis=-1)
    return x + _dot(a, wo_ref[...])


def _xattn_specs(D, wq_all, kv, wo_all, layer):
    kt, v = kv
    return [_const_spec((1, D)), _layer_spec(wq_all, layer),
            pl.BlockSpec((1,) + kt.shape[1:], lambda b, s: (b, 0, 0)),
            pl.BlockSpec((1,) + v.shape[1:], lambda b, s: (b, 0, 0)), _layer_spec(wo_all, layer)]


def _swiglu_kernel(x_ref, g_ref, wgu_ref, wd_ref, fg_ref, o_ref, *, hidden, th, final_norm):
    x = x_ref[...]
    h = _rms(x, g_ref[...]).astype(BF16)
    acc = x
    for j in range(hidden // th):
        gate = _dot(h, wgu_ref[:, j * th:(j + 1) * th])
        up = _dot(h, wgu_ref[:, hidden + j * th:hidden + (j + 1) * th])
        act = (gate * jax.nn.sigmoid(gate) * up).astype(BF16)
        acc = acc + _dot(act, wd_ref[j * th:(j + 1) * th, :])
    if final_norm:
        acc = _rms(acc, fg_ref[...])
    o_ref[...] = acc


def _swiglu(x, g, w_gu_all, w_d_all, final_g, layer, *, tm, th, final_norm):
    B, S, D = x.shape
    T = B * S
    tm = min(tm, T)
    hidden = w_d_all.shape[1]
    kern = functools.partial(_swiglu_kernel, hidden=hidden, th=th, final_norm=final_norm)
    out = pl.pallas_call(
        kern,
        grid=(T // tm,),
        in_specs=[
            pl.BlockSpec((tm, D), lambda i: (i, 0)),
            _const_spec((1, D)),
            _layer_spec(w_gu_all, layer),
            _layer_spec(w_d_all, layer),
            _const_spec((1, D)),
        ],
        out_specs=pl.BlockSpec((tm, D), lambda i: (i, 0)),
        out_shape=jax.ShapeDtypeStruct((T, D), F32),
        compiler_params=_params("arbitrary"),
        name="swiglu",
    )(x.reshape(T, D), g.reshape(1, D), w_gu_all, w_d_all, final_g.reshape(1, D))
    return out.reshape(B, S, D)


def _odd_proj_kernel(x_ref, pos_ref, g_ref, wmain_ref, wgt_ref, gbias_ref,
                     qg_ref, kvg_ref, wuq_ref, wuqs_ref, wuk_ref, wuvt_ref, invf_ref,
                     qm_ref, km_ref, vm_ref, om_ref, gates_ref, qp_ref, kp_ref, vt_ref):
    x = x_ref[0]
    h = _rms(x, g_ref[...]).astype(BF16)
    c0 = 4 * ML_WIDTH

    def ml_cols(i):
        return _dot(h, wmain_ref[:, i * ML_WIDTH:(i + 1) * ML_WIDTH])

    zl = _dot(h, wmain_ref[:, c0:])
    c_q = zl[:, 0:Q_LORA]
    c_kv = zl[:, Q_LORA:Q_LORA + KV_LORA]
    kr = zl[:, Q_LORA + KV_LORA:Q_LORA + KV_LORA + LANES]
    kr_sw = zl[:, Q_LORA + KV_LORA + LANES:Q_LORA + KV_LORA + 2 * LANES]
    gates_ref[0] = _dot_nt(wgt_ref[...], h) + gbias_ref[...]

    half = QK_ROPE // 2
    ang_t = invf_ref[...] * pos_ref[0].astype(F32)
    cos_c, sin_c = jnp.cos(ang_t), jnp.sin(ang_t)
    ts = ang_t.shape[1]
    pad_lo = jnp.zeros((QK_NOPE, ts), F32)
    pad_hi = jnp.zeros((LANES - QK_NOPE - QK_ROPE, ts), F32)
    cos_t = jnp.concatenate([pad_lo + 1.0, cos_c, cos_c, pad_hi], axis=0).T
    sin_t = jnp.concatenate([pad_lo, -sin_c, sin_c, pad_hi], axis=0).T
    scale = (QK_NOPE + QK_ROPE) ** -0.5 * LOG2_E
    cos_q, sin_q = cos_t * scale, sin_t * scale
    qm_ref[0] = ml_cols(0).astype(BF16)
    km_ref[0] = (ml_cols(1) * (ML_HEAD_DIM ** -0.5)).astype(BF16)

    cqn = _rms(c_q, qg_ref[...]).astype(BF16)
    qa = _dot(cqn, wuq_ref[...])
    qb = _dot(cqn, wuqs_ref[...])
    ckn = _rms(c_kv, kvg_ref[...]).astype(BF16)
    kn = _dot(ckn, wuk_ref[...])
    vt_ref[0] = _dot_nt(wuvt_ref[...], ckn).astype(BF16)
    k_rope = kr * cos_t + kr_sw * sin_t

    def rope_heads(first, last):
        for hd in range(first, last):
            lo = hd * LANES
            qp_ref[0, :, lo:lo + LANES] = (qa[:, lo:lo + LANES] * cos_q
                                           + qb[:, lo:lo + LANES] * sin_q).astype(BF16)
            kp_ref[0, :, lo:lo + LANES] = (kn[:, lo:lo + LANES] + k_rope).astype(BF16)

    vm_ref[0] = ml_cols(2).astype(BF16)
    rope_heads(0, MLA_HEADS // 2)
    om_ref[0] = ml_cols(3).astype(BF16)
    rope_heads(MLA_HEADS // 2, MLA_HEADS)


def _odd_proj(x, pos3, g, w_main, w_gt, gbias, q_g, kv_g, w_uq_p, w_uq_s, w_uk_p, w_uvt, invf, *, ts):
    B, S, D = x.shape
    ts = min(ts, S)
    tok = lambda n: pl.BlockSpec((1, ts, n), lambda b, s: (b, s, 0))
    out_shapes = (
        jax.ShapeDtypeStruct((B, S, ML_WIDTH), BF16),
        jax.ShapeDtypeStruct((B, S, ML_WIDTH), BF16),
        jax.ShapeDtypeStruct((B, S, ML_WIDTH), BF16),
        jax.ShapeDtypeStruct((B, S, ML_WIDTH), BF16),
        jax.ShapeDtypeStruct((B, 2 * ML_HEADS, S), F32),
        jax.ShapeDtypeStruct((B, S, MLA_HEADS * LANES), BF16),
        jax.ShapeDtypeStruct((B, S, MLA_HEADS * LANES), BF16),
        jax.ShapeDtypeStruct((B, MLA_WIDTH, S), BF16),
    )
    return pl.pallas_call(
        _odd_proj_kernel,
        grid=(B, S // ts),
        in_specs=[
            tok(D),
            pl.BlockSpec((1, 1, ts), lambda b, s: (b, 0, s)),
            _const_spec((1, D)),
            _const_spec(w_main.shape),
            _const_spec(w_gt.shape),
            _const_spec(gbias.shape),
            _const_spec(q_g.shape),
            _const_spec(kv_g.shape),
            _const_spec(w_uq_p.shape),
            _const_spec(w_uq_s.shape),
            _const_spec(w_uk_p.shape),
            _const_spec(w_uvt.shape),
            _const_spec(invf.shape),
        ],
        out_specs=(tok(ML_WIDTH), tok(ML_WIDTH), tok(ML_WIDTH), tok(ML_WIDTH),
                   pl.BlockSpec((1, 2 * ML_HEADS, ts), lambda b, s: (b, 0, s)),
                   tok(MLA_HEADS * LANES), tok(MLA_HEADS * LANES),
                   pl.BlockSpec((1, MLA_WIDTH, ts), lambda b, s: (b, 0, s))),
        out_shape=out_shapes,
        compiler_params=_params("arbitrary", "arbitrary"),
        name="odd_proj",
    )(x, pos3, g.reshape(1, D), w_main, w_gt, gbias, q_g, kv_g, w_uq_p, w_uq_s, w_uk_p, w_uvt, invf)


def _split3(a):
    hi = a.astype(BF16)
    r1 = a - hi.astype(F32)
    mid = r1.astype(BF16)
    lo = (r1 - mid.astype(F32)).astype(BF16)
    return hi, mid, lo


def _log2_sigmoid(x):
    return (jnp.minimum(x, 0.0) - jnp.log(1.0 + jnp.exp(-jnp.abs(x)))) * LOG2_E


def _mlstm_kernel(q_ref, k_ref, v_ref, o_ref, gt_ref, ng_ref, h_ref, c_scr, m_scr, *, L):
    ci = pl.program_id(1)

    @pl.when(ci == 0)
    def _():
        c_scr[...] = jnp.zeros(c_scr.shape, F32)
        m_scr[...] = jnp.zeros(m_scr.shape, F32)

    row = lax.broadcasted_iota(jnp.int32, (L, L), 0)
    col = lax.broadcasted_iota(jnp.int32, (L, L), 1)
    causal = col <= row
    gt = gt_ref[0]
    ones_v = jnp.ones((L, LANES), BF16)
    reps = L // LANES

    def widen(a):
        return jnp.concatenate([a] * reps, axis=1)

    lf_t = _log2_sigmoid(gt)
    b_t = sum(_dot(t, (row <= col).astype(BF16)) for t in _split3(lf_t))
    b_c = jnp.concatenate([b_t, jnp.zeros((LANES - 2 * ML_HEADS, L), F32)], axis=0).T

    for hd in range(ML_HEADS):
        lo = hd * ML_HEAD_DIM
        fg = ML_HEADS + hd
        q = q_ref[0, :, lo:lo + ML_HEAD_DIM]
        k = k_ref[0, :, lo:lo + ML_HEAD_DIM]
        v_aug = jnp.concatenate([v_ref[0, :, lo:lo + ML_HEAD_DIM], ones_v], axis=1)
        g_row = gt[hd:hd + 1, :] * LOG2_E - b_t[fg:fg + 1, :]
        gtot = jnp.sum(lf_t[fg:fg + 1, :], axis=1, keepdims=True)
        b_col = jnp.broadcast_to(b_c[:, fg:fg + 1], (L, LANES))

        m_prev = m_scr[hd]
        c_prev = c_scr[hd]

        g_mask = jnp.where(causal, g_row, NEG_BIG)
        gmax = jnp.max(g_mask, axis=1, keepdims=True)
        h_col = jnp.maximum(m_prev, jnp.broadcast_to(gmax, (L, LANES)))
        w_inter = jnp.exp2(m_prev - h_col)
        sw = jnp.exp2(g_mask - widen(h_col)) * _dot_nt(q, k)
        out = (jnp.concatenate([w_inter, w_inter], axis=1) * _dot(q, c_prev.astype(BF16))
               + _dot(sw.astype(BF16), v_aug))
        den = jnp.maximum(jnp.abs(out[:, ML_HEAD_DIM:]), jnp.exp2(-(b_col + h_col)))
        hc = out[:, :ML_HEAD_DIM] / den

        og = jax.nn.sigmoid(o_ref[0, :, lo:lo + ML_HEAD_DIM].astype(F32))
        hg = og * hc
        ms = jnp.mean(hg * hg, axis=1, keepdims=True)
        h_ref[0, :, lo:lo + ML_HEAD_DIM] = (
            hg * lax.rsqrt(ms + EPS) * ng_ref[:, lo:lo + ML_HEAD_DIM]).astype(BF16)

        m_prev1 = m_prev[:, 0:1]
        a_row = gtot + g_row
        m_loc = jnp.max(a_row, axis=1, keepdims=True)
        m_new = jnp.maximum(gtot + m_prev1, m_loc)
        s_old = jnp.exp2(gtot + m_prev1 - m_new)
        w_loc = jnp.exp2(a_row - m_new)
        kw_t = (k.T.astype(F32) * w_loc).astype(BF16)
        c_scr[hd] = s_old * c_prev + _dot(kw_t, v_aug)
        m_scr[hd] = jnp.broadcast_to(m_new, (1, LANES))


def _mlstm(q, k, v, o, gates_t, norm_g, *, L):
    B, S, W = q.shape
    L = min(L, S)
    kern = functools.partial(_mlstm_kernel, L=L)
    tok = pl.BlockSpec((1, L, W), lambda b, c: (b, c, 0))
    return pl.pallas_call(
        kern,
        grid=(B, S // L),
        in_specs=[tok, tok, tok, tok,
                  pl.BlockSpec((1, 2 * ML_HEADS, L), lambda b, c: (b, 0, c)),
                  _const_spec((1, W))],
        out_specs=tok,
        out_shape=jax.ShapeDtypeStruct((B, S, W), BF16),
        scratch_shapes=[pltpu.VMEM((ML_HEADS, ML_HEAD_DIM, 2 * ML_HEAD_DIM), F32),
                        pltpu.VMEM((ML_HEADS, 1, LANES), F32)],
        compiler_params=_params("arbitrary", "arbitrary"),
        name="mlstm",
    )(q, k, v, o, gates_t, norm_g.reshape(1, W))


def _mla_kernel(q_ref, k_ref, vt_ref, o_ref, s00, s01, s10, s11, qt_scr, *, tq):
    qi = pl.program_id(2)
    heads = 2
    pieces = 2
    rows = tq // pieces
    s_scr = ((s00, s01), (s10, s11))
    for hh in range(heads):
        qt_scr[hh] = q_ref[0, :, hh * LANES:(hh + 1) * LANES].T
    ones_rows = jnp.ones((BF16_ROWS, tq), BF16)

    def score_piece(j, slot, hh, pc):
        start = pl.multiple_of(j * tq + pc * rows, rows)
        kb = k_ref[0, pl.ds(start, rows), hh * LANES:(hh + 1) * LANES]
        s_scr[slot][hh][pc * rows:(pc + 1) * rows, :] = _dot(kb, qt_scr[hh])

    def scores(j, slot):
        for hh in range(heads):
            for pc in range(pieces):
                score_piece(j, slot, hh, pc)

    def stage_max(slot, hh, m, masked):
        s = s_scr[slot][hh][...]
        if masked:
            kk = lax.broadcasted_iota(jnp.int32, s.shape, 0)
            qq = lax.broadcasted_iota(jnp.int32, s.shape, 1)
            s = jnp.where(kk <= qq, s, NEG_BIG)
            s_scr[slot][hh][...] = s
        return jnp.maximum(m, jnp.max(s, axis=0, keepdims=True))

    def stage_exp(j, slot, hh, m_new, acc, half):
        hk = tq // 2
        start = pl.multiple_of(j * tq + half * hk, hk)
        vtb = jnp.concatenate([vt_ref[0, hh * V_DIM:(hh + 1) * V_DIM, pl.ds(start, hk)],
                               ones_rows[:, :hk]], axis=0)
        p = jnp.exp2(s_scr[slot][hh][half * hk:(half + 1) * hk, :] - m_new).astype(BF16)
        return acc + _dot(vtb, p)

    def block(j, slot, carry, masked, nxt=None):
        def piece(hh, pc):
            if nxt is not None:
                score_piece(nxt[0], nxt[1], hh, pc)

        new = []
        for hh in range(heads):
            m, acc = carry[hh]
            piece(hh, 0)
            m_new = stage_max(slot, hh, m, masked)
            piece(hh, 1)
            acc = stage_exp(j, slot, hh, m_new, jnp.exp2(m - m_new) * acc, 0)
            acc = stage_exp(j, slot, hh, m_new, acc, 1)
            new.append((m_new, acc))
        return tuple(new)

    init = tuple((jnp.full((1, tq), NEG_BIG, F32), jnp.zeros((V_DIM + BF16_ROWS, tq), F32))
                 for _ in range(heads))
    scores(0, 0)

    def body(i, c):
        c = block(2 * i, 0, c, False, nxt=(2 * i + 1, 1))
        return block(2 * i + 1, 1, c, False, nxt=(2 * i + 2, 0))

    carry = lax.fori_loop(0, qi // 2, body, init)

    def finish(c):
        ot = jnp.concatenate([acc[:V_DIM] / acc[V_DIM:V_DIM + 1] for _, acc in c], axis=0)
        o_ref[0] = ot.T.astype(BF16)

    @pl.when(qi % 2 == 0)
    def _():
        finish(block(qi, 0, carry, True))

    @pl.when(qi % 2 == 1)
    def _():
        finish(block(qi, 1, block(qi - 1, 0, carry, False, nxt=(qi, 1)), True))


def _mla_attention(qp, kp, vt, *, tq):
    B, S, _ = qp.shape
    tq = min(tq, S)
    pairs = MLA_HEADS // 2
    kern = functools.partial(_mla_kernel, tq=tq)
    return pl.pallas_call(
        kern,
        grid=(B, pairs, S // tq),
        in_specs=[
            pl.BlockSpec((1, tq, 2 * LANES), lambda b, p, i: (b, i, p)),
            pl.BlockSpec((1, S, 2 * LANES), lambda b, p, i: (b, 0, p)),
            pl.BlockSpec((1, 2 * V_DIM, S), lambda b, p, i: (b, p, 0)),
        ],
        out_specs=pl.BlockSpec((1, tq, LANES), lambda b, p, i: (b, i, p)),
        out_shape=jax.ShapeDtypeStruct((B, S, MLA_WIDTH), BF16),
        scratch_shapes=[pltpu.VMEM((tq, tq), F32) for _ in range(4)]
        + [pltpu.VMEM((2, LANES, tq), BF16)],
        compiler_params=_params("arbitrary", "arbitrary", "arbitrary"),
        name="mla_attention",
    )(qp, kp, vt)


def _odd_out_kernel(x_ref, hm_ref, ha_ref, wout_ref, xg_ref, wq_ref, kt_ref, v_ref, wo_ref, o_ref):
    mix = jnp.concatenate([hm_ref[0], ha_ref[0]], axis=-1)
    x1 = x_ref[0] + _dot(mix, wout_ref[...])
    o_ref[0] = _xattn_body(x1, xg_ref, wq_ref, kt_ref, v_ref, wo_ref)


def _odd_out(x, hm, ha, w_out, xg, wq_all, kv, wo_all, layer, *, ts):
    B, S, D = x.shape
    ts = min(ts, S)
    tok = lambda n: pl.BlockSpec((1, ts, n), lambda b, s: (b, s, 0))
    return pl.pallas_call(
        _odd_out_kernel,
        grid=(B, S // ts),
        in_specs=[tok(D), tok(ML_WIDTH), tok(MLA_WIDTH), _const_spec(w_out.shape)]
        + _xattn_specs(D, wq_all, kv, wo_all, layer),
        out_specs=tok(D),
        out_shape=jax.ShapeDtypeStruct(x.shape, F32),
        compiler_params=_params("arbitrary", "arbitrary"),
        name="odd_out",
    )(x, hm, ha, w_out, xg.reshape(1, D), wq_all, *kv, wo_all)


def _pack_odd_weights(w_in, w_uq, w_ukv):
    D = w_in.shape[0]
    c = 4 * ML_WIDTH
    w_g = w_in[:, c:c + 2 * ML_HEADS]
    c += 2 * ML_HEADS
    w_cq = w_in[:, c:c + Q_LORA]
    c += Q_LORA
    w_ckv = w_in[:, c:c + KV_LORA]
    c += KV_LORA
    w_kr = w_in[:, c:c + QK_ROPE]
    half = QK_ROPE // 2
    w_kr_sw = jnp.concatenate([w_kr[:, half:], w_kr[:, :half]], axis=1)
    zl = jnp.zeros((D, QK_NOPE), w_in.dtype)
    zr = jnp.zeros((D, LANES - QK_NOPE - QK_ROPE), w_in.dtype)
    w_main = jnp.concatenate([w_in[:, :4 * ML_WIDTH], w_cq, w_ckv,
                              zl, w_kr, zr, zl, w_kr_sw, zr], axis=1).astype(BF16)
    w_gt = w_g.T.astype(BF16)

    uq = w_uq.reshape(Q_LORA, MLA_HEADS, QK_NOPE + QK_ROPE)
    uq_n, uq_r = uq[..., :QK_NOPE], uq[..., QK_NOPE:]
    uq_r_sw = jnp.concatenate([uq_r[..., half:], uq_r[..., :half]], axis=-1)
    zpad = jnp.zeros((Q_LORA, MLA_HEADS, LANES - QK_NOPE - QK_ROPE), w_uq.dtype)
    w_uq_p = jnp.concatenate([uq_n, uq_r, zpad], axis=-1).reshape(Q_LORA, MLA_HEADS * LANES)
    w_uq_s = jnp.concatenate([jnp.zeros_like(uq_n), uq_r_sw, zpad], axis=-1)
    w_uq_s = w_uq_s.reshape(Q_LORA, MLA_HEADS * LANES)

    ukv = w_ukv.reshape(KV_LORA, MLA_HEADS, QK_NOPE + V_DIM)
    uk, uv = ukv[..., :QK_NOPE], ukv[..., QK_NOPE:]
    w_uk_p = jnp.concatenate([uk, jnp.zeros((KV_LORA, MLA_HEADS, LANES - QK_NOPE), w_ukv.dtype)],
                             axis=-1).reshape(KV_LORA, MLA_HEADS * LANES)
    w_uvt = uv.reshape(KV_LORA, MLA_WIDTH).T
    return (w_main, w_gt, w_uq_p.astype(BF16), w_uq_s.astype(BF16), w_uk_p.astype(BF16),
            w_uvt.astype(BF16))


def _rope_inv_freq():
    half = QK_ROPE // 2
    inv = ROPE_THETA ** (-jnp.arange(half, dtype=F32) / half)
    return inv.reshape(half, 1)


TS_EVEN = 1024
TM_FFN = 1024
TH_FFN = 256
TS_ODD = 512
L_MLSTM = 256
TQ_MLA = 512
TS_OUT = 1024


def kernel(x, mem, positions, norm_mix_g, norm_xattn_g, mem_norm_g, xattn_wq, xattn_wkv, xattn_wo,
           norm_ffn_g, ffn_w_gate_up, ffn_w_down, ev_w_in, ev_conv_w, ev_pool_w, ev_pool_scale,
           ev_w_out, od_w_in, od_gate_bias, od_ml_norm_g, od_q_norm_g, od_kv_norm_g, od_w_uq,
           od_w_ukv, od_w_out, final_norm_g):
    depth = norm_mix_g.shape[0]
    B, S, D = x.shape
    pos3 = positions.reshape(B, 1, S)
    invf = _rope_inv_freq()
    wq_all, wkv_all, wo_all = (w.astype(BF16) for w in (xattn_wq, xattn_wkv, xattn_wo))
    w_gu_all, w_d_all = ffn_w_gate_up.astype(BF16), ffn_w_down.astype(BF16)
    for layer in range(depth):
        kv = _mem_kv(mem, mem_norm_g[layer], wkv_all, layer)
        xa = (norm_xattn_g[layer], wq_all, kv, wo_all, layer)
        if layer % 2 == 0:
            e = layer // 2
            x = _even_mixer(x, norm_mix_g[layer], ev_w_in[e].astype(BF16), ev_conv_w[e],
                            ev_pool_w[e].astype(BF16), ev_pool_scale[e], ev_w_out[e].astype(BF16),
                            *xa, ts=TS_EVEN)
        else:
            o = layer // 2
            w_main, w_gt, w_uq_p, w_uq_s, w_uk_p, w_uvt = _pack_odd_weights(
                od_w_in[o], od_w_uq[o], od_w_ukv[o])
            qm, km, vm, om, gates_t, qp, kp, vt = _odd_proj(
                x, pos3, norm_mix_g[layer], w_main, w_gt, od_gate_bias[o].reshape(2 * ML_HEADS, 1),
                od_q_norm_g[o].reshape(1, Q_LORA), od_kv_norm_g[o].reshape(1, KV_LORA),
                w_uq_p, w_uq_s, w_uk_p, w_uvt, invf, ts=TS_ODD)
            hm = _mlstm(qm, km, vm, om, gates_t, od_ml_norm_g[o], L=L_MLSTM)
            ha = _mla_attention(qp, kp, vt, tq=TQ_MLA)
            x = _odd_out(x, hm, ha, od_w_out[o].astype(BF16), *xa, ts=TS_OUT)
        x = _swiglu(x, norm_ffn_g[layer], w_gu_all, w_d_all, final_norm_g, layer,
                    tm=TM_FFN, th=TH_FFN, final_norm=(layer == depth - 1))
    return x
```

```python
import functools

import jax
import jax.numpy as jnp
from jax import lax
from jax.experimental import pallas as pl
from jax.experimental.pallas import tpu as pltpu

F32 = jnp.float32
BF16 = jnp.bfloat16

EPS = 1e-6
ROPE_THETA = 10000.0
LANES = 128
SUBLANES = 8
BF16_ROWS = 16
VMEM_LIMIT_BYTES = 56 * 1024 * 1024
NEG_BIG = -1e30
LOG2_E = 1.4426950408889634

SC_WIDTH = 512
CONV_K = 3
POOL_WINDOWS = (2, 4, 8, 16)
POOL_GROUP = 128
HALO = 32
ML_HEADS = 4
ML_HEAD_DIM = 128
ML_WIDTH = ML_HEADS * ML_HEAD_DIM
MLA_HEADS = 8
QK_NOPE = 64
QK_ROPE = 32
V_DIM = 64
Q_LORA = 384
KV_LORA = 256
MLA_WIDTH = MLA_HEADS * V_DIM
XA_HEADS = 4
XA_HEAD_DIM = 128
XA_WIDTH = XA_HEADS * XA_HEAD_DIM


def _params(*sem):
    return pltpu.CompilerParams(dimension_semantics=sem, vmem_limit_bytes=VMEM_LIMIT_BYTES)


def _const_spec(shape):
    nd = len(shape)
    return pl.BlockSpec(shape, lambda *_: (0,) * nd, pipeline_mode=pl.Buffered(1))


def _layer_spec(stacked, layer):
    return pl.BlockSpec((None,) + stacked.shape[1:], lambda *_: (layer, 0, 0),
                        pipeline_mode=pl.Buffered(1))


def _rms(x, g):
    ms = jnp.mean(x * x, axis=-1, keepdims=True)
    return x * lax.rsqrt(ms + EPS) * g


def _dot(a, b):
    return jnp.dot(a, b, preferred_element_type=F32)


def _dot_nt(a, b):
    return lax.dot_general(a, b, (((1,), (1,)), ((), ())), preferred_element_type=F32)


def _even_kernel(x_ref, g_ref, win_ref, convw_ref, poolw_ref, pscale_ref, wout_ref,
                 xg_ref, wq_ref, kt_ref, v_ref, wo_ref, o_ref, ubuf, xbbuf, tmp_a, tmp_b, *, ts):
    si = pl.program_id(1)

    @pl.when(si == 0)
    def _():
        ubuf[0:HALO, :] = jnp.zeros((HALO, SC_WIDTH), F32)
        xbbuf[0:HALO, :] = jnp.zeros((HALO, SC_WIDTH), F32)

    x = x_ref[0]
    h = _rms(x, g_ref[...]).astype(BF16)
    z = _dot(h, win_ref[...])
    g_b = z[:, 0:SC_WIDTH]
    u = z[:, SC_WIDTH:2 * SC_WIDTH] * z[:, 2 * SC_WIDTH:3 * SC_WIDTH]
    xb = z[:, 3 * SC_WIDTH:4 * SC_WIDTH]
    ubuf[HALO:HALO + ts, :] = u
    xbbuf[HALO:HALO + ts, :] = xb

    cw = convw_ref[...]
    conv = cw[2:3, :] * u
    for k in range(CONV_K - 1):
        back = CONV_K - 1 - k
        conv = conv + cw[k:k + 1, :] * ubuf[HALO - back:HALO - back + ts, :]
    ya = g_b * conv

    def window_level(src, c0, dst, r0, shift):
        w2 = (src[r0:HALO + ts, c0:c0 + POOL_GROUP]
              + src[r0 - shift:HALO + ts - shift, c0:c0 + POOL_GROUP])
        if dst is not None:
            dst[r0:HALO + ts, :] = w2
        return w2

    t_idx = si * ts + lax.broadcasted_iota(jnp.int32, (ts, POOL_GROUP), 0)
    ys = [ya.astype(BF16)]
    for gi, w in enumerate(POOL_WINDOWS):
        lo = gi * POOL_GROUP
        xg = xb[:, lo:lo + POOL_GROUP]
        src, c0, shift, level = xbbuf, lo, 1, 1
        while 2 * shift < w:
            dst = tmp_a if src is not tmp_a else tmp_b
            window_level(src, c0, dst, SUBLANES * level, shift)
            src, c0, shift, level = dst, 0, 2 * shift, level + 1
        acc = window_level(src, c0, None, HALO, shift)
        cnt = jnp.minimum(t_idx + 1, w).astype(F32)
        d = (acc / cnt - xg).astype(BF16)
        yg = _dot(d, poolw_ref[gi]) * pscale_ref[:, lo:lo + POOL_GROUP]
        ys.append(yg.astype(BF16))

    ubuf[0:HALO, :] = ubuf[ts:ts + HALO, :]
    xbbuf[0:HALO, :] = xbbuf[ts:ts + HALO, :]

    mix = jnp.concatenate(ys, axis=-1)
    x1 = x + _dot(mix, wout_ref[...])
    o_ref[0] = _xattn_body(x1, xg_ref, wq_ref, kt_ref, v_ref, wo_ref)


def _even_mixer(x, g, w_in, conv_w, pool_w, pool_scale, w_out, xg, wq_all, kv, wo_all, layer, *, ts):
    B, S, D = x.shape
    ts = min(ts, S)
    kern = functools.partial(_even_kernel, ts=ts)
    return pl.pallas_call(
        kern,
        grid=(B, S // ts),
        in_specs=[
            pl.BlockSpec((1, ts, D), lambda b, s: (b, s, 0)),
            _const_spec((1, D)),
            _const_spec(w_in.shape),
            _const_spec(conv_w.shape),
            _const_spec(pool_w.shape),
            _const_spec((1, SC_WIDTH)),
            _const_spec(w_out.shape),
        ] + _xattn_specs(D, wq_all, kv, wo_all, layer),
        out_specs=pl.BlockSpec((1, ts, D), lambda b, s: (b, s, 0)),
        out_shape=jax.ShapeDtypeStruct(x.shape, F32),
        scratch_shapes=[pltpu.VMEM((HALO + ts, SC_WIDTH), F32),
                        pltpu.VMEM((HALO + ts, SC_WIDTH), F32),
                        pltpu.VMEM((HALO + ts, POOL_GROUP), F32),
                        pltpu.VMEM((HALO + ts, POOL_GROUP), F32)],
        compiler_params=_params("arbitrary", "arbitrary"),
        name="even_mixer",
    )(x, g.reshape(1, D), w_in, conv_w, pool_w, pool_scale.reshape(1, SC_WIDTH), w_out,
      xg.reshape(1, D), wq_all, *kv, wo_all)


def _memkv_kernel(mem_ref, g_ref, wkv_ref, kt_ref, v_ref):
    h = _rms(mem_ref[0], g_ref[...]).astype(BF16)
    kv = _dot(h, wkv_ref[...])
    kt_ref[0] = kv[:, :XA_WIDTH].T.astype(BF16)
    v_ref[0] = kv[:, XA_WIDTH:].astype(BF16)


def _mem_kv(mem, g, wkv_all, layer):
    B, M, D = mem.shape
    return pl.pallas_call(
        _memkv_kernel,
        grid=(B,),
        in_specs=[pl.BlockSpec((1, M, D), lambda b: (b, 0, 0)),
                  _const_spec((1, D)), _layer_spec(wkv_all, layer)],
        out_specs=(pl.BlockSpec((1, XA_WIDTH, M), lambda b: (b, 0, 0)),
                   pl.BlockSpec((1, M, XA_WIDTH), lambda b: (b, 0, 0))),
        out_shape=(jax.ShapeDtypeStruct((B, XA_WIDTH, M), BF16),
                   jax.ShapeDtypeStruct((B, M, XA_WIDTH), BF16)),
        compiler_params=_params("arbitrary"),
        name="mem_kv",
    )(mem, g.reshape(1, D), wkv_all)


def _xattn_body(x, g_ref, wq_ref, kt_ref, v_ref, wo_ref):
    h = _rms(x, g_ref[...]).astype(BF16)
    q = (_dot(h, wq_ref[...]) * (XA_HEAD_DIM ** -0.5)).astype(BF16)
    ones_v = jnp.ones((v_ref.shape[1], XA_HEAD_DIM), BF16)
    outs = []
    for hd in range(XA_HEADS):
        lo = hd * XA_HEAD_DIM
        s = _dot(q[:, lo:lo + XA_HEAD_DIM], kt_ref[0, lo:lo + XA_HEAD_DIM, :])
        p = jnp.exp(s - jnp.max(s, axis=-1, keepdims=True)).astype(BF16)
        v_aug = jnp.concatenate([v_ref[0, :, lo:lo + XA_HEAD_DIM], ones_v], axis=1)
        o = _dot(p, v_aug)
        outs.append((o[:, :XA_HEAD_DIM] / o[:, XA_HEAD_DIM:]).astype(BF16))
    a = jnp.concatenate(outs, axis=-1)
    return x + _dot(a, wo_ref[...])


def _xattn_specs(D, wq_all, kv, wo_all, layer):
    kt, v = kv
    return [_const_spec((1, D)), _layer_spec(wq_all, layer),
            pl.BlockSpec((1,) + kt.shape[1:], lambda b, s: (b, 0, 0)),
            pl.BlockSpec((1,) + v.shape[1:], lambda b, s: (b, 0, 0)), _layer_spec(wo_all, layer)]


def _swiglu_kernel(x_ref, g_ref, wgu_ref, wd_ref, fg_ref, o_ref, *, hidden, th, final_norm):
    x = x_ref[...]
    h = _rms(x, g_ref[...]).astype(BF16)
    acc = x
    for j in range(hidden // th):
        gate = _dot(h, wgu_ref[:, j * th:(j + 1) * th])
        up = _dot(h, wgu_ref[:, hidden + j * th:hidden + (j + 1) * th])
        act = (gate * jax.nn.sigmoid(gate) * up).astype(BF16)
        acc = acc + _dot(act, wd_ref[j * th:(j + 1) * th, :])
    if final_norm:
        acc = _rms(acc, fg_ref[...])
    o_ref[...] = acc


def _swiglu(x, g, w_gu_all, w_d_all, final_g, layer, *, tm, th, final_norm):
    B, S, D = x.shape
    T = B * S
    tm = min(tm, T)
    hidden = w_d_all.shape[1]
    kern = functools.partial(_swiglu_kernel, hidden=hidden, th=th, final_norm=final_norm)
    out = pl.pallas_call(
        kern,
        grid=(T // tm,),
        in_specs=[
            pl.BlockSpec((tm, D), lambda i: (i, 0)),
            _const_spec((1, D)),
            _layer_spec(w_gu_all, layer),
            _layer_spec(w_d_all, layer),
            _const_spec((1, D)),
        ],
        out_specs=pl.BlockSpec((tm, D), lambda i: (i, 0)),
        out_shape=jax.ShapeDtypeStruct((T, D), F32),
        compiler_params=_params("arbitrary"),
        name="swiglu",
    )(x.reshape(T, D), g.reshape(1, D), w_gu_all, w_d_all, final_g.reshape(1, D))
    return out.reshape(B, S, D)


def _odd_proj_kernel(x_ref, pos_ref, g_ref, wmain_ref, wgt_ref, gbias_ref,
                     qg_ref, kvg_ref, wuq_ref, wuqs_ref, wuk_ref, wuvt_ref, invf_ref,
                     qm_ref, km_ref, vm_ref, om_ref, gates_ref, qp_ref, kp_ref, vt_ref):
    x = x_ref[0]
    h = _rms(x, g_ref[...]).astype(BF16)
    c0 = 4 * ML_WIDTH

    def ml_cols(i):
        return _dot(h, wmain_ref[:, i * ML_WIDTH:(i + 1) * ML_WIDTH])

    zl = _dot(h, wmain_ref[:, c0:])
    c_q = zl[:, 0:Q_LORA]
    c_kv = zl[:, Q_LORA:Q_LORA + KV_LORA]
    kr = zl[:, Q_LORA + KV_LORA:Q_LORA + KV_LORA + LANES]
    kr_sw = zl[:, Q_LORA + KV_LORA + LANES:Q_LORA + KV_LORA + 2 * LANES]
    gates_ref[0] = _dot_nt(wgt_ref[...], h) + gbias_ref[...]

    half = QK_ROPE // 2
    ang_t = invf_ref[...] * pos_ref[0].astype(F32)
    cos_c, sin_c = jnp.cos(ang_t), jnp.sin(ang_t)
    ts = ang_t.shape[1]
    pad_lo = jnp.zeros((QK_NOPE, ts), F32)
    pad_hi = jnp.zeros((LANES - QK_NOPE - QK_ROPE, ts), F32)
    cos_t = jnp.concatenate([pad_lo + 1.0, cos_c, cos_c, pad_hi], axis=0).T
    sin_t = jnp.concatenate([pad_lo, -sin_c, sin_c, pad_hi], axis=0).T
    scale = (QK_NOPE + QK_ROPE) ** -0.5 * LOG2_E
    cos_q, sin_q = cos_t * scale, sin_t * scale
    qm_ref[0] = ml_cols(0).astype(BF16)
    km_ref[0] = (ml_cols(1) * (ML_HEAD_DIM ** -0.5)).astype(BF16)

    cqn = _rms(c_q, qg_ref[...]).astype(BF16)
    qa = _dot(cqn, wuq_ref[...])
    qb = _dot(cqn, wuqs_ref[...])
    ckn = _rms(c_kv, kvg_ref[...]).astype(BF16)
    kn = _dot(ckn, wuk_ref[...])
    vt_ref[0] = _dot_nt(wuvt_ref[...], ckn).astype(BF16)
    k_rope = kr * cos_t + kr_sw * sin_t

    def rope_heads(first, last):
        for hd in range(first, last):
            lo = hd * LANES
            qp_ref[0, lo:lo + LANES, :] = (qa[:, lo:lo + LANES] * cos_q
                                           + qb[:, lo:lo + LANES] * sin_q).T.astype(BF16)
            kp_ref[0, :, lo:lo + LANES] = (kn[:, lo:lo + LANES] + k_rope).astype(BF16)

    vm_ref[0] = ml_cols(2).astype(BF16)
    rope_heads(0, MLA_HEADS // 2)
    om_ref[0] = ml_cols(3).astype(BF16)
    rope_heads(MLA_HEADS // 2, MLA_HEADS)


def _odd_proj(x, pos3, g, w_main, w_gt, gbias, q_g, kv_g, w_uq_p, w_uq_s, w_uk_p, w_uvt, invf, *, ts):
    B, S, D = x.shape
    ts = min(ts, S)
    tok = lambda n: pl.BlockSpec((1, ts, n), lambda b, s: (b, s, 0))
    out_shapes = (
        jax.ShapeDtypeStruct((B, S, ML_WIDTH), BF16),
        jax.ShapeDtypeStruct((B, S, ML_WIDTH), BF16),
        jax.ShapeDtypeStruct((B, S, ML_WIDTH), BF16),
        jax.ShapeDtypeStruct((B, S, ML_WIDTH), BF16),
        jax.ShapeDtypeStruct((B, 2 * ML_HEADS, S), F32),
        jax.ShapeDtypeStruct((B, MLA_HEADS * LANES, S), BF16),
        jax.ShapeDtypeStruct((B, S, MLA_HEADS * LANES), BF16),
        jax.ShapeDtypeStruct((B, MLA_WIDTH, S), BF16),
    )
    return pl.pallas_call(
        _odd_proj_kernel,
        grid=(B, S // ts),
        in_specs=[
            tok(D),
            pl.BlockSpec((1, 1, ts), lambda b, s: (b, 0, s)),
            _const_spec((1, D)),
            _const_spec(w_main.shape),
            _const_spec(w_gt.shape),
            _const_spec(gbias.shape),
            _const_spec(q_g.shape),
            _const_spec(kv_g.shape),
            _const_spec(w_uq_p.shape),
            _const_spec(w_uq_s.shape),
            _const_spec(w_uk_p.shape),
            _const_spec(w_uvt.shape),
            _const_spec(invf.shape),
        ],
        out_specs=(tok(ML_WIDTH), tok(ML_WIDTH), tok(ML_WIDTH), tok(ML_WIDTH),
                   pl.BlockSpec((1, 2 * ML_HEADS, ts), lambda b, s: (b, 0, s)),
                   pl.BlockSpec((1, MLA_HEADS * LANES, ts), lambda b, s: (b, 0, s)),
                   tok(MLA_HEADS * LANES),
                   pl.BlockSpec((1, MLA_WIDTH, ts), lambda b, s: (b, 0, s))),
        out_shape=out_shapes,
        compiler_params=_params("arbitrary", "arbitrary"),
        name="odd_proj",
    )(x, pos3, g.reshape(1, D), w_main, w_gt, gbias, q_g, kv_g, w_uq_p, w_uq_s, w_uk_p, w_uvt, invf)


def _split3(a):
    hi = a.astype(BF16)
    r1 = a - hi.astype(F32)
    mid = r1.astype(BF16)
    lo = (r1 - mid.astype(F32)).astype(BF16)
    return hi, mid, lo


def _log2_sigmoid(x):
    return (jnp.minimum(x, 0.0) - jnp.log(1.0 + jnp.exp(-jnp.abs(x)))) * LOG2_E


def _mlstm_kernel(q_ref, k_ref, v_ref, o_ref, gt_ref, ng_ref, h_ref, c_scr, m_scr, *, L):
    ci = pl.program_id(1)

    @pl.when(ci == 0)
    def _():
        c_scr[...] = jnp.zeros(c_scr.shape, F32)
        m_scr[...] = jnp.zeros(m_scr.shape, F32)

    row = lax.broadcasted_iota(jnp.int32, (L, L), 0)
    col = lax.broadcasted_iota(jnp.int32, (L, L), 1)
    causal = col <= row
    gt = gt_ref[0]
    ones_v = jnp.ones((L, LANES), BF16)
    reps = L // LANES

    def widen(a):
        return jnp.concatenate([a] * reps, axis=1)

    lf_t = _log2_sigmoid(gt)
    b_t = sum(_dot(t, (row <= col).astype(BF16)) for t in _split3(lf_t))
    b_c = jnp.concatenate([b_t, jnp.zeros((LANES - 2 * ML_HEADS, L), F32)], axis=0).T

    for hd in range(ML_HEADS):
        lo = hd * ML_HEAD_DIM
        fg = ML_HEADS + hd
        q = q_ref[0, :, lo:lo + ML_HEAD_DIM]
        k = k_ref[0, :, lo:lo + ML_HEAD_DIM]
        v_aug = jnp.concatenate([v_ref[0, :, lo:lo + ML_HEAD_DIM], ones_v], axis=1)
        g_row = gt[hd:hd + 1, :] * LOG2_E - b_t[fg:fg + 1, :]
        gtot = jnp.sum(lf_t[fg:fg + 1, :], axis=1, keepdims=True)
        b_col = jnp.broadcast_to(b_c[:, fg:fg + 1], (L, LANES))

        m_prev = m_scr[hd]
        c_prev = c_scr[hd]

        g_mask = jnp.where(causal, g_row, NEG_BIG)
        gmax = jnp.max(g_mask, axis=1, keepdims=True)
        h_col = jnp.maximum(m_prev, jnp.broadcast_to(gmax, (L, LANES)))
        w_inter = jnp.exp2(m_prev - h_col)
        sw = jnp.exp2(g_mask - widen(h_col)) * _dot_nt(q, k)
        out = (jnp.concatenate([w_inter, w_inter], axis=1) * _dot(q, c_prev.astype(BF16))
               + _dot(sw.astype(BF16), v_aug))
        den = jnp.maximum(jnp.abs(out[:, ML_HEAD_DIM:]), jnp.exp2(-(b_col + h_col)))
        hc = out[:, :ML_HEAD_DIM] / den

        og = jax.nn.sigmoid(o_ref[0, :, lo:lo + ML_HEAD_DIM].astype(F32))
        hg = og * hc
        ms = jnp.mean(hg * hg, axis=1, keepdims=True)
        h_ref[0, :, lo:lo + ML_HEAD_DIM] = (
            hg * lax.rsqrt(ms + EPS) * ng_ref[:, lo:lo + ML_HEAD_DIM]).astype(BF16)

        m_prev1 = m_prev[:, 0:1]
        a_row = gtot + g_row
        m_loc = jnp.max(a_row, axis=1, keepdims=True)
        m_new = jnp.maximum(gtot + m_prev1, m_loc)
        s_old = jnp.exp2(gtot + m_prev1 - m_new)
        w_loc = jnp.exp2(a_row - m_new)
        kw_t = (k.T.astype(F32) * w_loc).astype(BF16)
        c_scr[hd] = s_old * c_prev + _dot(kw_t, v_aug)
        m_scr[hd] = jnp.broadcast_to(m_new, (1, LANES))


def _mlstm(q, k, v, o, gates_t, norm_g, *, L):
    B, S, W = q.shape
    L = min(L, S)
    kern = functools.partial(_mlstm_kernel, L=L)
    tok = pl.BlockSpec((1, L, W), lambda b, c: (b, c, 0))
    return pl.pallas_call(
        kern,
        grid=(B, S // L),
        in_specs=[tok, tok, tok, tok,
                  pl.BlockSpec((1, 2 * ML_HEADS, L), lambda b, c: (b, 0, c)),
                  _const_spec((1, W))],
        out_specs=tok,
        out_shape=jax.ShapeDtypeStruct((B, S, W), BF16),
        scratch_shapes=[pltpu.VMEM((ML_HEADS, ML_HEAD_DIM, 2 * ML_HEAD_DIM), F32),
                        pltpu.VMEM((ML_HEADS, 1, LANES), F32)],
        compiler_params=_params("arbitrary", "arbitrary"),
        name="mlstm",
    )(q, k, v, o, gates_t, norm_g.reshape(1, W))


def _mla_kernel(qt_ref, k_ref, vt_ref, o_ref, s00, s01, s10, s11, *, tq):
    qi = pl.program_id(2)
    heads = 2
    pieces = 2
    rows = tq // pieces
    s_scr = ((s00, s01), (s10, s11))
    ones_rows = jnp.ones((BF16_ROWS, tq), BF16)

    def score_piece(j, slot, hh, pc):
        start = pl.multiple_of(j * tq + pc * rows, rows)
        kb = k_ref[0, pl.ds(start, rows), hh * LANES:(hh + 1) * LANES]
        qt = qt_ref[0, hh * LANES:(hh + 1) * LANES, :]
        s_scr[slot][hh][pc * rows:(pc + 1) * rows, :] = _dot(kb, qt)

    def scores(j, slot):
        for hh in range(heads):
            for pc in range(pieces):
                score_piece(j, slot, hh, pc)

    def stage_max(slot, hh, m):
        s = s_scr[slot][hh][...]
        return jnp.maximum(m, jnp.max(s, axis=0, keepdims=True))

    def stage_exp(j, slot, hh, m_new, acc, half):
        hk = tq // 2
        start = pl.multiple_of(j * tq + half * hk, hk)
        vtb = jnp.concatenate([vt_ref[0, hh * V_DIM:(hh + 1) * V_DIM, pl.ds(start, hk)],
                               ones_rows[:, :hk]], axis=0)
        p = jnp.exp2(s_scr[slot][hh][half * hk:(half + 1) * hk, :] - m_new).astype(BF16)
        return acc + _dot(vtb, p)

    def block(j, slot, carry, nxt):
        new = []
        for hh in range(heads):
            m, acc = carry[hh]
            score_piece(nxt[0], nxt[1], hh, 0)
            m_new = stage_max(slot, hh, m)
            score_piece(nxt[0], nxt[1], hh, 1)
            acc = stage_exp(j, slot, hh, m_new, jnp.exp2(m - m_new) * acc, 0)
            acc = stage_exp(j, slot, hh, m_new, acc, 1)
            new.append((m_new, acc))
        return tuple(new)

    def diag_block(slot, carry):
        hk = tq // 2
        start = pl.multiple_of(qi * tq, tq)
        kk = lax.broadcasted_iota(jnp.int32, (hk, tq), 0)
        qq = lax.broadcasted_iota(jnp.int32, (hk, tq), 1)
        kb = lax.broadcasted_iota(jnp.int32, (hk, hk), 0)
        qb = lax.broadcasted_iota(jnp.int32, (hk, hk), 1)
        hs = range(heads)
        top = [jnp.where(kk <= qq, s_scr[slot][hh][0:hk, :], NEG_BIG) for hh in hs]
        bot = [jnp.where(kb <= qb, s_scr[slot][hh][hk:, hk:], NEG_BIG) for hh in hs]
        neg = jnp.full((hk, hk), NEG_BIG, F32)
        m_new = [jnp.maximum(carry[hh][0], jnp.max(
            jnp.maximum(top[hh], jnp.concatenate([neg, bot[hh]], axis=1)), axis=0, keepdims=True))
            for hh in hs]
        vt_aug = [jnp.concatenate([vt_ref[0, hh * V_DIM:(hh + 1) * V_DIM, pl.ds(start, tq)],
                                   ones_rows], axis=0) for hh in hs]
        p_top = [jnp.exp2(top[hh] - m_new[hh]).astype(BF16) for hh in hs]
        p_bot = [jnp.exp2(bot[hh] - jnp.broadcast_to(m_new[hh], (hk, tq))[:, hk:]).astype(BF16)
                 for hh in hs]
        new = []
        for hh in hs:
            m, acc = carry[hh]
            acc = jnp.exp2(m - m_new[hh]) * acc + _dot(vt_aug[hh][:, :hk], p_top[hh])
            hi = acc[:, hk:] + _dot(vt_aug[hh][:, hk:], p_bot[hh])
            new.append((m_new[hh], jnp.concatenate([acc[:, :hk], hi], axis=1)))
        return tuple(new)

    init = tuple((jnp.full((1, tq), NEG_BIG, F32), jnp.zeros((V_DIM + BF16_ROWS, tq), F32))
                 for _ in range(heads))
    scores(0, 0)

    def body(i, c):
        c = block(2 * i, 0, c, (2 * i + 1, 1))
        return block(2 * i + 1, 1, c, (2 * i + 2, 0))

    carry = lax.fori_loop(0, qi // 2, body, init)

    def finish(c):
        ot = jnp.concatenate([acc[:V_DIM] / acc[V_DIM:V_DIM + 1] for _, acc in c], axis=0)
        o_ref[0] = ot.T.astype(BF16)

    @pl.when(qi % 2 == 0)
    def _():
        finish(diag_block(0, carry))

    @pl.when(qi % 2 == 1)
    def _():
        finish(diag_block(1, block(qi - 1, 0, carry, (qi, 1))))


def _mla_attention(qpt, kp, vt, *, tq):
    B, S, _ = kp.shape
    tq = min(tq, S)
    pairs = MLA_HEADS // 2
    kern = functools.partial(_mla_kernel, tq=tq)
    return pl.pallas_call(
        kern,
        grid=(B, pairs, S // tq),
        in_specs=[
            pl.BlockSpec((1, 2 * LANES, tq), lambda b, p, i: (b, p, i)),
            pl.BlockSpec((1, S, 2 * LANES), lambda b, p, i: (b, 0, p)),
            pl.BlockSpec((1, 2 * V_DIM, S), lambda b, p, i: (b, p, 0)),
        ],
        out_specs=pl.BlockSpec((1, tq, LANES), lambda b, p, i: (b, i, p)),
        out_shape=jax.ShapeDtypeStruct((B, S, MLA_WIDTH), BF16),
        scratch_shapes=[pltpu.VMEM((tq, tq), F32) for _ in range(4)],
        compiler_params=_params("arbitrary", "arbitrary", "arbitrary"),
        name="mla_attention",
    )(qpt, kp, vt)


def _odd_out_kernel(x_ref, hm_ref, ha_ref, wout_ref, xg_ref, wq_ref, kt_ref, v_ref, wo_ref, o_ref):
    mix = jnp.concatenate([hm_ref[0], ha_ref[0]], axis=-1)
    x1 = x_ref[0] + _dot(mix, wout_ref[...])
    o_ref[0] = _xattn_body(x1, xg_ref, wq_ref, kt_ref, v_ref, wo_ref)


def _odd_out(x, hm, ha, w_out, xg, wq_all, kv, wo_all, layer, *, ts):
    B, S, D = x.shape
    ts = min(ts, S)
    tok = lambda n: pl.BlockSpec((1, ts, n), lambda b, s: (b, s, 0))
    return pl.pallas_call(
        _odd_out_kernel,
        grid=(B, S // ts),
        in_specs=[tok(D), tok(ML_WIDTH), tok(MLA_WIDTH), _const_spec(w_out.shape)]
        + _xattn_specs(D, wq_all, kv, wo_all, layer),
        out_specs=tok(D),
        out_shape=jax.ShapeDtypeStruct(x.shape, F32),
        compiler_params=_params("arbitrary", "arbitrary"),
        name="odd_out",
    )(x, hm, ha, w_out, xg.reshape(1, D), wq_all, *kv, wo_all)


def _pack_odd_weights(w_in, w_uq, w_ukv):
    D = w_in.shape[0]
    c = 4 * ML_WIDTH
    w_g = w_in[:, c:c + 2 * ML_HEADS]
    c += 2 * ML_HEADS
    w_cq = w_in[:, c:c + Q_LORA]
    c += Q_LORA
    w_ckv = w_in[:, c:c + KV_LORA]
    c += KV_LORA
    w_kr = w_in[:, c:c + QK_ROPE]
    half = QK_ROPE // 2
    w_kr_sw = jnp.concatenate([w_kr[:, half:], w_kr[:, :half]], axis=1)
    zl = jnp.zeros((D, QK_NOPE), w_in.dtype)
    zr = jnp.zeros((D, LANES - QK_NOPE - QK_ROPE), w_in.dtype)
    w_main = jnp.concatenate([w_in[:, :4 * ML_WIDTH], w_cq, w_ckv,
                              zl, w_kr, zr, zl, w_kr_sw, zr], axis=1).astype(BF16)
    w_gt = w_g.T.astype(BF16)

    uq = w_uq.reshape(Q_LORA, MLA_HEADS, QK_NOPE + QK_ROPE)
    uq_n, uq_r = uq[..., :QK_NOPE], uq[..., QK_NOPE:]
    uq_r_sw = jnp.concatenate([uq_r[..., half:], uq_r[..., :half]], axis=-1)
    zpad = jnp.zeros((Q_LORA, MLA_HEADS, LANES - QK_NOPE - QK_ROPE), w_uq.dtype)
    w_uq_p = jnp.concatenate([uq_n, uq_r, zpad], axis=-1).reshape(Q_LORA, MLA_HEADS * LANES)
    w_uq_s = jnp.concatenate([jnp.zeros_like(uq_n), uq_r_sw, zpad], axis=-1)
    w_uq_s = w_uq_s.reshape(Q_LORA, MLA_HEADS * LANES)

    ukv = w_ukv.reshape(KV_LORA, MLA_HEADS, QK_NOPE + V_DIM)
    uk, uv = ukv[..., :QK_NOPE], ukv[..., QK_NOPE:]
    w_uk_p = jnp.concatenate([uk, jnp.zeros((KV_LORA, MLA_HEADS, LANES - QK_NOPE), w_ukv.dtype)],
                             axis=-1).reshape(KV_LORA, MLA_HEADS * LANES)
    w_uvt = uv.reshape(KV_LORA, MLA_WIDTH).T
    return (w_main, w_gt, w_uq_p.astype(BF16), w_uq_s.astype(BF16), w_uk_p.astype(BF16),
            w_uvt.astype(BF16))


def _rope_inv_freq():
    half = QK_ROPE // 2
    inv = ROPE_THETA ** (-jnp.arange(half, dtype=F32) / half)
    return inv.reshape(half, 1)


TS_EVEN = 1024
TM_FFN = 1024
TH_FFN = 256
TS_ODD = 512
L_MLSTM = 256
TQ_MLA = 512
TS_OUT = 1024


def kernel(x, mem, positions, norm_mix_g, norm_xattn_g, mem_norm_g, xattn_wq, xattn_wkv, xattn_wo,
           norm_ffn_g, ffn_w_gate_up, ffn_w_down, ev_w_in, ev_conv_w, ev_pool_w, ev_pool_scale,
           ev_w_out, od_w_in, od_gate_bias, od_ml_norm_g, od_q_norm_g, od_kv_norm_g, od_w_uq,
           od_w_ukv, od_w_out, final_norm_g):
    depth = norm_mix_g.shape[0]
    B, S, D = x.shape
    pos3 = positions.reshape(B, 1, S)
    invf = _rope_inv_freq()
    wq_all, wkv_all, wo_all = (w.astype(BF16) for w in (xattn_wq, xattn_wkv, xattn_wo))
    w_gu_all, w_d_all = ffn_w_gate_up.astype(BF16), ffn_w_down.astype(BF16)
    for layer in range(depth):
        kv = _mem_kv(mem, mem_norm_g[layer], wkv_all, layer)
        xa = (norm_xattn_g[layer], wq_all, kv, wo_all, layer)
        if layer % 2 == 0:
            e = layer // 2
            x = _even_mixer(x, norm_mix_g[layer], ev_w_in[e].astype(BF16), ev_conv_w[e],
                            ev_pool_w[e].astype(BF16), ev_pool_scale[e], ev_w_out[e].astype(BF16),
                            *xa, ts=TS_EVEN)
        else:
            o = layer // 2
            w_main, w_gt, w_uq_p, w_uq_s, w_uk_p, w_uvt = _pack_odd_weights(
                od_w_in[o], od_w_uq[o], od_w_ukv[o])
            qm, km, vm, om, gates_t, qp, kp, vt = _odd_proj(
                x, pos3, norm_mix_g[layer], w_main, w_gt, od_gate_bias[o].reshape(2 * ML_HEADS, 1),
                od_q_norm_g[o].reshape(1, Q_LORA), od_kv_norm_g[o].reshape(1, KV_LORA),
                w_uq_p, w_uq_s, w_uk_p, w_uvt, invf, ts=TS_ODD)
            hm = _mlstm(qm, km, vm, om, gates_t, od_ml_norm_g[o], L=L_MLSTM)
            ha = _mla_attention(qp, kp, vt, tq=TQ_MLA)
            x = _odd_out(x, hm, ha, od_w_out[o].astype(BF16), *xa, ts=TS_OUT)
        x = _swiglu(x, norm_ffn_g[layer], w_gu_all, w_d_all, final_norm_g, layer,
                    tm=TM_FFN, th=TH_FFN, final_norm=(layer == depth - 1))
    return x
```

```python
import functools

import jax
import jax.numpy as jnp
from jax import lax
from jax.experimental import pallas as pl
from jax.experimental.pallas import tpu as pltpu

F32 = jnp.float32
BF16 = jnp.bfloat16

EPS = 1e-6
ROPE_THETA = 10000.0
LANES = 128
SUBLANES = 8
BF16_ROWS = 16
VMEM_LIMIT_BYTES = 56 * 1024 * 1024
NEG_BIG = -1e30
LOG2_E = 1.4426950408889634

SC_WIDTH = 512
CONV_K = 3
POOL_WINDOWS = (2, 4, 8, 16)
POOL_GROUP = 128
HALO = 32
ML_HEADS = 4
ML_HEAD_DIM = 128
ML_WIDTH = ML_HEADS * ML_HEAD_DIM
MLA_HEADS = 8
QK_NOPE = 64
QK_ROPE = 32
V_DIM = 64
Q_LORA = 384
KV_LORA = 256
MLA_WIDTH = MLA_HEADS * V_DIM
XA_HEADS = 4
XA_HEAD_DIM = 128
XA_WIDTH = XA_HEADS * XA_HEAD_DIM


def _params(*sem):
    return pltpu.CompilerParams(dimension_semantics=sem, vmem_limit_bytes=VMEM_LIMIT_BYTES)


def _const_spec(shape):
    nd = len(shape)
    return pl.BlockSpec(shape, lambda *_: (0,) * nd, pipeline_mode=pl.Buffered(1))


def _layer_spec(stacked, layer):
    return pl.BlockSpec((None,) + stacked.shape[1:], lambda *_: (layer, 0, 0),
                        pipeline_mode=pl.Buffered(1))


def _rms(x, g):
    ms = jnp.mean(x * x, axis=-1, keepdims=True)
    return x * lax.rsqrt(ms + EPS) * g


def _dot(a, b):
    return jnp.dot(a, b, preferred_element_type=F32)


def _dot_nt(a, b):
    return lax.dot_general(a, b, (((1,), (1,)), ((), ())), preferred_element_type=F32)


def _even_kernel(x_ref, g_ref, win_ref, convw_ref, poolw_ref, pscale_ref, wout_ref,
                 xg_ref, wq_ref, kt_ref, v_ref, wo_ref, o_ref, ubuf, xbbuf, tmp_a, tmp_b, *, ts):
    si = pl.program_id(1)

    @pl.when(si == 0)
    def _():
        ubuf[0:HALO, :] = jnp.zeros((HALO, SC_WIDTH), F32)
        xbbuf[0:HALO, :] = jnp.zeros((HALO, SC_WIDTH), F32)

    x = x_ref[0]
    h = _rms(x, g_ref[...]).astype(BF16)
    z = _dot(h, win_ref[...])
    g_b = z[:, 0:SC_WIDTH]
    u = z[:, SC_WIDTH:2 * SC_WIDTH] * z[:, 2 * SC_WIDTH:3 * SC_WIDTH]
    xb = z[:, 3 * SC_WIDTH:4 * SC_WIDTH]
    ubuf[HALO:HALO + ts, :] = u
    xbbuf[HALO:HALO + ts, :] = xb

    cw = convw_ref[...]
    conv = cw[2:3, :] * u
    for k in range(CONV_K - 1):
        back = CONV_K - 1 - k
        conv = conv + cw[k:k + 1, :] * ubuf[HALO - back:HALO - back + ts, :]
    ya = g_b * conv

    def window_level(src, c0, dst, r0, shift):
        w2 = (src[r0:HALO + ts, c0:c0 + POOL_GROUP]
              + src[r0 - shift:HALO + ts - shift, c0:c0 + POOL_GROUP])
        if dst is not None:
            dst[r0:HALO + ts, :] = w2
        return w2

    t_idx = si * ts + lax.broadcasted_iota(jnp.int32, (ts, POOL_GROUP), 0)
    ys = [ya.astype(BF16)]
    for gi, w in enumerate(POOL_WINDOWS):
        lo = gi * POOL_GROUP
        xg = xb[:, lo:lo + POOL_GROUP]
        src, c0, shift, level = xbbuf, lo, 1, 1
        while 2 * shift < w:
            dst = tmp_a if src is not tmp_a else tmp_b
            window_level(src, c0, dst, SUBLANES * level, shift)
            src, c0, shift, level = dst, 0, 2 * shift, level + 1
        acc = window_level(src, c0, None, HALO, shift)
        cnt = jnp.minimum(t_idx + 1, w).astype(F32)
        d = (acc / cnt - xg).astype(BF16)
        yg = _dot(d, poolw_ref[gi]) * pscale_ref[:, lo:lo + POOL_GROUP]
        ys.append(yg.astype(BF16))

    ubuf[0:HALO, :] = ubuf[ts:ts + HALO, :]
    xbbuf[0:HALO, :] = xbbuf[ts:ts + HALO, :]

    mix = jnp.concatenate(ys, axis=-1)
    x1 = x + _dot(mix, wout_ref[...])
    o_ref[0] = _xattn_body(x1, xg_ref, wq_ref, kt_ref, v_ref, wo_ref)


def _even_mixer(x, g, w_in, conv_w, pool_w, pool_scale, w_out, xg, wq_all, kv, wo_all, layer, *, ts):
    B, S, D = x.shape
    ts = min(ts, S)
    kern = functools.partial(_even_kernel, ts=ts)
    return pl.pallas_call(
        kern,
        grid=(B, S // ts),
        in_specs=[
            pl.BlockSpec((1, ts, D), lambda b, s: (b, s, 0)),
            _const_spec((1, D)),
            _const_spec(w_in.shape),
            _const_spec(conv_w.shape),
            _const_spec(pool_w.shape),
            _const_spec((1, SC_WIDTH)),
            _const_spec(w_out.shape),
        ] + _xattn_specs(D, wq_all, kv, wo_all, layer),
        out_specs=pl.BlockSpec((1, ts, D), lambda b, s: (b, s, 0)),
        out_shape=jax.ShapeDtypeStruct(x.shape, F32),
        scratch_shapes=[pltpu.VMEM((HALO + ts, SC_WIDTH), F32),
                        pltpu.VMEM((HALO + ts, SC_WIDTH), F32),
                        pltpu.VMEM((HALO + ts, POOL_GROUP), F32),
                        pltpu.VMEM((HALO + ts, POOL_GROUP), F32)],
        compiler_params=_params("arbitrary", "arbitrary"),
        name="even_mixer",
    )(x, g.reshape(1, D), w_in, conv_w, pool_w, pool_scale.reshape(1, SC_WIDTH), w_out,
      xg.reshape(1, D), wq_all, *kv, wo_all)


def _memkv_kernel(mem_ref, g_ref, wkv_ref, kt_ref, v_ref):
    h = _rms(mem_ref[0], g_ref[...]).astype(BF16)
    kv = _dot(h, wkv_ref[...])
    kt_ref[0] = kv[:, :XA_WIDTH].T.astype(BF16)
    v_ref[0] = kv[:, XA_WIDTH:].astype(BF16)


def _mem_kv(mem, g, wkv_all, layer):
    B, M, D = mem.shape
    return pl.pallas_call(
        _memkv_kernel,
        grid=(B,),
        in_specs=[pl.BlockSpec((1, M, D), lambda b: (b, 0, 0)),
                  _const_spec((1, D)), _layer_spec(wkv_all, layer)],
        out_specs=(pl.BlockSpec((1, XA_WIDTH, M), lambda b: (b, 0, 0)),
                   pl.BlockSpec((1, M, XA_WIDTH), lambda b: (b, 0, 0))),
        out_shape=(jax.ShapeDtypeStruct((B, XA_WIDTH, M), BF16),
                   jax.ShapeDtypeStruct((B, M, XA_WIDTH), BF16)),
        compiler_params=_params("arbitrary"),
        name="mem_kv",
    )(mem, g.reshape(1, D), wkv_all)


def _xattn_body(x, g_ref, wq_ref, kt_ref, v_ref, wo_ref):
    h = _rms(x, g_ref[...]).astype(BF16)
    q = (_dot(h, wq_ref[...]) * (XA_HEAD_DIM ** -0.5)).astype(BF16)
    ones_v = jnp.ones((v_ref.shape[1], XA_HEAD_DIM), BF16)
    outs = []
    for hd in range(XA_HEADS):
        lo = hd * XA_HEAD_DIM
        s = _dot(q[:, lo:lo + XA_HEAD_DIM], kt_ref[0, lo:lo + XA_HEAD_DIM, :])
        p = jnp.exp(s - jnp.max(s, axis=-1, keepdims=True)).astype(BF16)
        v_aug = jnp.concatenate([v_ref[0, :, lo:lo + XA_HEAD_DIM], ones_v], axis=1)
        o = _dot(p, v_aug)
        outs.append((o[:, :XA_HEAD_DIM] / o[:, XA_HEAD_DIM:]).astype(BF16))
    a = jnp.concatenate(outs, axis=-1)
    return x + _dot(a, wo_ref[...])


def _xattn_specs(D, wq_all, kv, wo_all, layer):
    kt, v = kv
    return [_const_spec((1, D)), _layer_spec(wq_all, layer),
            pl.BlockSpec((1,) + kt.shape[1:], lambda b, s: (b, 0, 0)),
            pl.BlockSpec((1,) + v.shape[1:], lambda b, s: (b, 0, 0)), _layer_spec(wo_all, layer)]


def _swiglu_kernel(x_ref, g_ref, wgu_ref, wd_ref, fg_ref, o_ref, *, hidden, th, final_norm):
    x = x_ref[...]
    h = _rms(x, g_ref[...]).astype(BF16)
    acc = x
    for j in range(hidden // th):
        gate = _dot(h, wgu_ref[:, j * th:(j + 1) * th])
        up = _dot(h, wgu_ref[:, hidden + j * th:hidden + (j + 1) * th])
        act = (gate * jax.nn.sigmoid(gate) * up).astype(BF16)
        acc = acc + _dot(act, wd_ref[j * th:(j + 1) * th, :])
    if final_norm:
        acc = _rms(acc, fg_ref[...])
    o_ref[...] = acc


def _swiglu(x, g, w_gu_all, w_d_all, final_g, layer, *, tm, th, final_norm):
    B, S, D = x.shape
    T = B * S
    tm = min(tm, T)
    hidden = w_d_all.shape[1]
    kern = functools.partial(_swiglu_kernel, hidden=hidden, th=th, final_norm=final_norm)
    out = pl.pallas_call(
        kern,
        grid=(T // tm,),
        in_specs=[
            pl.BlockSpec((tm, D), lambda i: (i, 0)),
            _const_spec((1, D)),
            _layer_spec(w_gu_all, layer),
            _layer_spec(w_d_all, layer),
            _const_spec((1, D)),
        ],
        out_specs=pl.BlockSpec((tm, D), lambda i: (i, 0)),
        out_shape=jax.ShapeDtypeStruct((T, D), F32),
        compiler_params=_params("arbitrary"),
        name="swiglu",
    )(x.reshape(T, D), g.reshape(1, D), w_gu_all, w_d_all, final_g.reshape(1, D))
    return out.reshape(B, S, D)


def _odd_proj_kernel(x_ref, pos_ref, g_ref, wmain_ref, wgt_ref, gbias_ref,
                     qg_ref, kvg_ref, wuq_ref, wuqs_ref, wuk_ref, wuvt_ref, invf_ref,
                     qm_ref, km_ref, vm_ref, om_ref, gates_ref, qp_ref, kp_ref, vt_ref):
    x = x_ref[0]
    h = _rms(x, g_ref[...]).astype(BF16)
    c0 = 4 * ML_WIDTH

    def ml_cols(i):
        return _dot(h, wmain_ref[:, i * ML_WIDTH:(i + 1) * ML_WIDTH])

    zl = _dot(h, wmain_ref[:, c0:])
    c_q = zl[:, 0:Q_LORA]
    c_kv = zl[:, Q_LORA:Q_LORA + KV_LORA]
    kr = zl[:, Q_LORA + KV_LORA:Q_LORA + KV_LORA + LANES]
    kr_sw = zl[:, Q_LORA + KV_LORA + LANES:Q_LORA + KV_LORA + 2 * LANES]
    gates_ref[0] = _dot_nt(wgt_ref[...], h) + gbias_ref[...]

    half = QK_ROPE // 2
    ang_t = invf_ref[...] * pos_ref[0].astype(F32)
    cos_c, sin_c = jnp.cos(ang_t), jnp.sin(ang_t)
    ts = ang_t.shape[1]
    pad_lo = jnp.zeros((QK_NOPE, ts), F32)
    pad_hi = jnp.zeros((LANES - QK_NOPE - QK_ROPE, ts), F32)
    cos_t = jnp.concatenate([pad_lo + 1.0, cos_c, cos_c, pad_hi], axis=0).T
    sin_t = jnp.concatenate([pad_lo, -sin_c, sin_c, pad_hi], axis=0).T
    scale = (QK_NOPE + QK_ROPE) ** -0.5 * LOG2_E
    cos_q, sin_q = cos_t * scale, sin_t * scale
    qm_ref[0] = ml_cols(0).astype(BF16)
    km_ref[0] = (ml_cols(1) * (ML_HEAD_DIM ** -0.5)).astype(BF16)

    cqn = _rms(c_q, qg_ref[...]).astype(BF16)
    qa = _dot(cqn, wuq_ref[...])
    qb = _dot(cqn, wuqs_ref[...])
    ckn = _rms(c_kv, kvg_ref[...]).astype(BF16)
    kn = _dot(ckn, wuk_ref[...])
    vt_ref[0] = _dot_nt(wuvt_ref[...], ckn).astype(BF16)
    k_rope = kr * cos_t + kr_sw * sin_t

    def rope_heads(first, last):
        for hd in range(first, last):
            lo = hd * LANES
            qp_ref[0, lo:lo + LANES, :] = (qa[:, lo:lo + LANES] * cos_q
                                           + qb[:, lo:lo + LANES] * sin_q).T.astype(BF16)
            kp_ref[0, :, lo:lo + LANES] = (kn[:, lo:lo + LANES] + k_rope).astype(BF16)

    vm_ref[0] = ml_cols(2).astype(BF16)
    rope_heads(0, MLA_HEADS // 2)
    om_ref[0] = ml_cols(3).astype(BF16)
    rope_heads(MLA_HEADS // 2, MLA_HEADS)


def _odd_proj(x, pos3, g, w_main, w_gt, gbias, q_g, kv_g, w_uq_p, w_uq_s, w_uk_p, w_uvt, invf, *, ts):
    B, S, D = x.shape
    ts = min(ts, S)
    tok = lambda n: pl.BlockSpec((1, ts, n), lambda b, s: (b, s, 0))
    out_shapes = (
        jax.ShapeDtypeStruct((B, S, ML_WIDTH), BF16),
        jax.ShapeDtypeStruct((B, S, ML_WIDTH), BF16),
        jax.ShapeDtypeStruct((B, S, ML_WIDTH), BF16),
        jax.ShapeDtypeStruct((B, S, ML_WIDTH), BF16),
        jax.ShapeDtypeStruct((B, 2 * ML_HEADS, S), F32),
        jax.ShapeDtypeStruct((B, MLA_HEADS * LANES, S), BF16),
        jax.ShapeDtypeStruct((B, S, MLA_HEADS * LANES), BF16),
        jax.ShapeDtypeStruct((B, MLA_WIDTH, S), BF16),
    )
    return pl.pallas_call(
        _odd_proj_kernel,
        grid=(B, S // ts),
        in_specs=[
            tok(D),
            pl.BlockSpec((1, 1, ts), lambda b, s: (b, 0, s)),
            _const_spec((1, D)),
            _const_spec(w_main.shape),
            _const_spec(w_gt.shape),
            _const_spec(gbias.shape),
            _const_spec(q_g.shape),
            _const_spec(kv_g.shape),
            _const_spec(w_uq_p.shape),
            _const_spec(w_uq_s.shape),
            _const_spec(w_uk_p.shape),
            _const_spec(w_uvt.shape),
            _const_spec(invf.shape),
        ],
        out_specs=(tok(ML_WIDTH), tok(ML_WIDTH), tok(ML_WIDTH), tok(ML_WIDTH),
                   pl.BlockSpec((1, 2 * ML_HEADS, ts), lambda b, s: (b, 0, s)),
                   pl.BlockSpec((1, MLA_HEADS * LANES, ts), lambda b, s: (b, 0, s)),
                   tok(MLA_HEADS * LANES),
                   pl.BlockSpec((1, MLA_WIDTH, ts), lambda b, s: (b, 0, s))),
        out_shape=out_shapes,
        compiler_params=_params("arbitrary", "arbitrary"),
        name="odd_proj",
    )(x, pos3, g.reshape(1, D), w_main, w_gt, gbias, q_g, kv_g, w_uq_p, w_uq_s, w_uk_p, w_uvt, invf)


def _split3(a):
    hi = a.astype(BF16)
    r1 = a - hi.astype(F32)
    mid = r1.astype(BF16)
    lo = (r1 - mid.astype(F32)).astype(BF16)
    return hi, mid, lo


def _log2_sigmoid(x):
    return (jnp.minimum(x, 0.0) - jnp.log(1.0 + jnp.exp(-jnp.abs(x)))) * LOG2_E


def _mlstm_kernel(q_ref, k_ref, v_ref, o_ref, gt_ref, ng_ref, h_ref, c_scr, m_scr, *, L):
    ci = pl.program_id(1)

    @pl.when(ci == 0)
    def _():
        c_scr[...] = jnp.zeros(c_scr.shape, F32)
        m_scr[...] = jnp.zeros(m_scr.shape, F32)

    row = lax.broadcasted_iota(jnp.int32, (L, L), 0)
    col = lax.broadcasted_iota(jnp.int32, (L, L), 1)
    causal = col <= row
    gt = gt_ref[0]
    ones_v = jnp.ones((L, LANES), BF16)
    reps = L // LANES

    def widen(a):
        return jnp.concatenate([a] * reps, axis=1)

    lf_t = _log2_sigmoid(gt)
    b_t = sum(_dot(t, (row <= col).astype(BF16)) for t in _split3(lf_t))
    b_c = jnp.concatenate([b_t, jnp.zeros((LANES - 2 * ML_HEADS, L), F32)], axis=0).T

    for hd in range(ML_HEADS):
        lo = hd * ML_HEAD_DIM
        fg = ML_HEADS + hd
        q = q_ref[0, :, lo:lo + ML_HEAD_DIM]
        k = k_ref[0, :, lo:lo + ML_HEAD_DIM]
        v_aug = jnp.concatenate([v_ref[0, :, lo:lo + ML_HEAD_DIM], ones_v], axis=1)
        g_row = gt[hd:hd + 1, :] * LOG2_E - b_t[fg:fg + 1, :]
        gtot = jnp.sum(lf_t[fg:fg + 1, :], axis=1, keepdims=True)
        b_col = jnp.broadcast_to(b_c[:, fg:fg + 1], (L, LANES))

        m_prev = m_scr[hd]
        c_prev = c_scr[hd]

        g_mask = jnp.where(causal, g_row, NEG_BIG)
        gmax = jnp.max(g_mask, axis=1, keepdims=True)
        h_col = jnp.maximum(m_prev, jnp.broadcast_to(gmax, (L, LANES)))
        w_inter = jnp.exp2(m_prev - h_col)
        sw = jnp.exp2(g_mask - widen(h_col)) * _dot_nt(q, k)
        out = (jnp.concatenate([w_inter, w_inter], axis=1) * _dot(q, c_prev.astype(BF16))
               + _dot(sw.astype(BF16), v_aug))
        den = jnp.maximum(jnp.abs(out[:, ML_HEAD_DIM:]), jnp.exp2(-(b_col + h_col)))
        hc = out[:, :ML_HEAD_DIM] / den

        og = jax.nn.sigmoid(o_ref[0, :, lo:lo + ML_HEAD_DIM].astype(F32))
        hg = og * hc
        ms = jnp.mean(hg * hg, axis=1, keepdims=True)
        h_ref[0, :, lo:lo + ML_HEAD_DIM] = (
            hg * lax.rsqrt(ms + EPS) * ng_ref[:, lo:lo + ML_HEAD_DIM]).astype(BF16)

        m_prev1 = m_prev[:, 0:1]
        a_row = gtot + g_row
        m_loc = jnp.max(a_row, axis=1, keepdims=True)
        m_new = jnp.maximum(gtot + m_prev1, m_loc)
        s_old = jnp.exp2(gtot + m_prev1 - m_new)
        w_loc = jnp.exp2(a_row - m_new)
        kw_t = (k.T.astype(F32) * w_loc).astype(BF16)
        c_scr[hd] = s_old * c_prev + _dot(kw_t, v_aug)
        m_scr[hd] = jnp.broadcast_to(m_new, (1, LANES))


def _mlstm(q, k, v, o, gates_t, norm_g, *, L):
    B, S, W = q.shape
    L = min(L, S)
    kern = functools.partial(_mlstm_kernel, L=L)
    tok = pl.BlockSpec((1, L, W), lambda b, c: (b, c, 0))
    return pl.pallas_call(
        kern,
        grid=(B, S // L),
        in_specs=[tok, tok, tok, tok,
                  pl.BlockSpec((1, 2 * ML_HEADS, L), lambda b, c: (b, 0, c)),
                  _const_spec((1, W))],
        out_specs=tok,
        out_shape=jax.ShapeDtypeStruct((B, S, W), BF16),
        scratch_shapes=[pltpu.VMEM((ML_HEADS, ML_HEAD_DIM, 2 * ML_HEAD_DIM), F32),
                        pltpu.VMEM((ML_HEADS, 1, LANES), F32)],
        compiler_params=_params("arbitrary", "arbitrary"),
        name="mlstm",
    )(q, k, v, o, gates_t, norm_g.reshape(1, W))


def _mla_kernel(qt_ref, k_ref, vt_ref, o_ref, s00, s01, s10, s11, *, tq):
    qi = pl.program_id(2)
    heads = 2
    pieces = 2
    rows = tq // pieces
    s_scr = ((s00, s01), (s10, s11))
    ones_rows = jnp.ones((BF16_ROWS, tq), BF16)

    def score_piece(j, slot, hh, pc):
        start = pl.multiple_of(j * tq + pc * rows, rows)
        kb = k_ref[0, pl.ds(start, rows), hh * LANES:(hh + 1) * LANES]
        qt = qt_ref[0, hh * LANES:(hh + 1) * LANES, :]
        s_scr[slot][hh][pc * rows:(pc + 1) * rows, :] = _dot(kb, qt)

    def scores(j, slot):
        for hh in range(heads):
            for pc in range(pieces):
                score_piece(j, slot, hh, pc)

    def stage_max(slot, hh, m):
        s = s_scr[slot][hh][...]
        return jnp.maximum(m, jnp.max(s, axis=0, keepdims=True))

    def stage_exp(j, slot, hh, m_new, acc, half):
        hk = tq // 2
        start = pl.multiple_of(j * tq + half * hk, hk)
        vtb = jnp.concatenate([vt_ref[0, hh * V_DIM:(hh + 1) * V_DIM, pl.ds(start, hk)],
                               ones_rows[:, :hk]], axis=0)
        p = jnp.exp2(s_scr[slot][hh][half * hk:(half + 1) * hk, :] - m_new).astype(BF16)
        return acc + _dot(vtb, p)

    def block(j, slot, carry, nxt):
        new = []
        for hh in range(heads):
            m, acc = carry[hh]
            score_piece(nxt[0], nxt[1], hh, 0)
            m_new = stage_max(slot, hh, m)
            score_piece(nxt[0], nxt[1], hh, 1)
            acc = stage_exp(j, slot, hh, m_new, jnp.exp2(m - m_new) * acc, 0)
            acc = stage_exp(j, slot, hh, m_new, acc, 1)
            new.append((m_new, acc))
        return tuple(new)

    def diag_block(slot, carry):
        hk = tq // 2
        start = pl.multiple_of(qi * tq, tq)
        kk = lax.broadcasted_iota(jnp.int32, (hk, tq), 0)
        qq = lax.broadcasted_iota(jnp.int32, (hk, tq), 1)
        kb = lax.broadcasted_iota(jnp.int32, (hk, hk), 0)
        qb = lax.broadcasted_iota(jnp.int32, (hk, hk), 1)
        hs = range(heads)
        top = [jnp.where(kk <= qq, s_scr[slot][hh][0:hk, :], NEG_BIG) for hh in hs]
        bot = [jnp.where(kb <= qb, s_scr[slot][hh][hk:, hk:], NEG_BIG) for hh in hs]
        neg = jnp.full((hk, hk), NEG_BIG, F32)
        m_new = [jnp.maximum(carry[hh][0], jnp.max(
            jnp.maximum(top[hh], jnp.concatenate([neg, bot[hh]], axis=1)), axis=0, keepdims=True))
            for hh in hs]
        vt_aug = [jnp.concatenate([vt_ref[0, hh * V_DIM:(hh + 1) * V_DIM, pl.ds(start, tq)],
                                   ones_rows], axis=0) for hh in hs]
        p_top = [jnp.exp2(top[hh] - m_new[hh]).astype(BF16) for hh in hs]
        p_bot = [jnp.exp2(bot[hh] - jnp.broadcast_to(m_new[hh], (hk, tq))[:, hk:]).astype(BF16)
                 for hh in hs]
        new = []
        for hh in hs:
            m, acc = carry[hh]
            acc = jnp.exp2(m - m_new[hh]) * acc + _dot(vt_aug[hh][:, :hk], p_top[hh])
            hi = acc[:, hk:] + _dot(vt_aug[hh][:, hk:], p_bot[hh])
            new.append((m_new[hh], jnp.concatenate([acc[:, :hk], hi], axis=1)))
        return tuple(new)

    init = tuple((jnp.full((1, tq), NEG_BIG, F32), jnp.zeros((V_DIM + BF16_ROWS, tq), F32))
                 for _ in range(heads))
    scores(0, 0)

    def body(i, c):
        c = block(2 * i, 0, c, (2 * i + 1, 1))
        return block(2 * i + 1, 1, c, (2 * i + 2, 0))

    carry = lax.fori_loop(0, qi // 2, body, init)

    def finish(c):
        ot = jnp.concatenate([acc[:V_DIM] / acc[V_DIM:V_DIM + 1] for _, acc in c], axis=0)
        o_ref[0] = ot.astype(BF16)

    @pl.when(qi % 2 == 0)
    def _():
        finish(diag_block(0, carry))

    @pl.when(qi % 2 == 1)
    def _():
        finish(diag_block(1, block(qi - 1, 0, carry, (qi, 1))))


def _mla_attention(qpt, kp, vt, *, tq):
    B, S, _ = kp.shape
    tq = min(tq, S)
    pairs = MLA_HEADS // 2
    kern = functools.partial(_mla_kernel, tq=tq)
    return pl.pallas_call(
        kern,
        grid=(B, pairs, S // tq),
        in_specs=[
            pl.BlockSpec((1, 2 * LANES, tq), lambda b, p, i: (b, p, i)),
            pl.BlockSpec((1, S, 2 * LANES), lambda b, p, i: (b, 0, p)),
            pl.BlockSpec((1, 2 * V_DIM, S), lambda b, p, i: (b, p, 0)),
        ],
        out_specs=pl.BlockSpec((1, LANES, tq), lambda b, p, i: (b, p, i)),
        out_shape=jax.ShapeDtypeStruct((B, MLA_WIDTH, S), BF16),
        scratch_shapes=[pltpu.VMEM((tq, tq), F32) for _ in range(4)],
        compiler_params=_params("arbitrary", "arbitrary", "arbitrary"),
        name="mla_attention",
    )(qpt, kp, vt)


def _odd_out_kernel(x_ref, hm_ref, ha_ref, wout_ref, xg_ref, wq_ref, kt_ref, v_ref, wo_ref, o_ref):
    mix = jnp.concatenate([hm_ref[0], ha_ref[0].T], axis=-1)
    x1 = x_ref[0] + _dot(mix, wout_ref[...])
    o_ref[0] = _xattn_body(x1, xg_ref, wq_ref, kt_ref, v_ref, wo_ref)


def _odd_out(x, hm, ha, w_out, xg, wq_all, kv, wo_all, layer, *, ts):
    B, S, D = x.shape
    ts = min(ts, S)
    tok = lambda n: pl.BlockSpec((1, ts, n), lambda b, s: (b, s, 0))
    return pl.pallas_call(
        _odd_out_kernel,
        grid=(B, S // ts),
        in_specs=[tok(D), tok(ML_WIDTH), pl.BlockSpec((1, MLA_WIDTH, ts), lambda b, s: (b, 0, s)),
                  _const_spec(w_out.shape)]
        + _xattn_specs(D, wq_all, kv, wo_all, layer),
        out_specs=tok(D),
        out_shape=jax.ShapeDtypeStruct(x.shape, F32),
        compiler_params=_params("arbitrary", "arbitrary"),
        name="odd_out",
    )(x, hm, ha, w_out, xg.reshape(1, D), wq_all, *kv, wo_all)


def _pack_odd_weights(w_in, w_uq, w_ukv):
    D = w_in.shape[0]
    c = 4 * ML_WIDTH
    w_g = w_in[:, c:c + 2 * ML_HEADS]
    c += 2 * ML_HEADS
    w_cq = w_in[:, c:c + Q_LORA]
    c += Q_LORA
    w_ckv = w_in[:, c:c + KV_LORA]
    c += KV_LORA
    w_kr = w_in[:, c:c + QK_ROPE]
    half = QK_ROPE // 2
    w_kr_sw = jnp.concatenate([w_kr[:, half:], w_kr[:, :half]], axis=1)
    zl = jnp.zeros((D, QK_NOPE), w_in.dtype)
    zr = jnp.zeros((D, LANES - QK_NOPE - QK_ROPE), w_in.dtype)
    w_main = jnp.concatenate([w_in[:, :4 * ML_WIDTH], w_cq, w_ckv,
                              zl, w_kr, zr, zl, w_kr_sw, zr], axis=1).astype(BF16)
    w_gt = w_g.T.astype(BF16)

    uq = w_uq.reshape(Q_LORA, MLA_HEADS, QK_NOPE + QK_ROPE)
    uq_n, uq_r = uq[..., :QK_NOPE], uq[..., QK_NOPE:]
    uq_r_sw = jnp.concatenate([uq_r[..., half:], uq_r[..., :half]], axis=-1)
    zpad = jnp.zeros((Q_LORA, MLA_HEADS, LANES - QK_NOPE - QK_ROPE), w_uq.dtype)
    w_uq_p = jnp.concatenate([uq_n, uq_r, zpad], axis=-1).reshape(Q_LORA, MLA_HEADS * LANES)
    w_uq_s = jnp.concatenate([jnp.zeros_like(uq_n), uq_r_sw, zpad], axis=-1)
    w_uq_s = w_uq_s.reshape(Q_LORA, MLA_HEADS * LANES)

    ukv = w_ukv.reshape(KV_LORA, MLA_HEADS, QK_NOPE + V_DIM)
    uk, uv = ukv[..., :QK_NOPE], ukv[..., QK_NOPE:]
    w_uk_p = jnp.concatenate([uk, jnp.zeros((KV_LORA, MLA_HEADS, LANES - QK_NOPE), w_ukv.dtype)],
                             axis=-1).reshape(KV_LORA, MLA_HEADS * LANES)
    w_uvt = uv.reshape(KV_LORA, MLA_WIDTH).T
    return (w_main, w_gt, w_uq_p.astype(BF16), w_uq_s.astype(BF16), w_uk_p.astype(BF16),
            w_uvt.astype(BF16))


def _rope_inv_freq():
    half = QK_ROPE // 2
    inv = ROPE_THETA ** (-jnp.arange(half, dtype=F32) / half)
    return inv.reshape(half, 1)


TS_EVEN = 1024
TM_FFN = 1024
TH_FFN = 256
TS_ODD = 1024
L_MLSTM = 256
TQ_MLA = 512
TS_OUT = 1024


def kernel(x, mem, positions, norm_mix_g, norm_xattn_g, mem_norm_g, xattn_wq, xattn_wkv, xattn_wo,
           norm_ffn_g, ffn_w_gate_up, ffn_w_down, ev_w_in, ev_conv_w, ev_pool_w, ev_pool_scale,
           ev_w_out, od_w_in, od_gate_bias, od_ml_norm_g, od_q_norm_g, od_kv_norm_g, od_w_uq,
           od_w_ukv, od_w_out, final_norm_g):
    depth = norm_mix_g.shape[0]
    B, S, D = x.shape
    pos3 = positions.reshape(B, 1, S)
    invf = _rope_inv_freq()
    wq_all, wkv_all, wo_all = (w.astype(BF16) for w in (xattn_wq, xattn_wkv, xattn_wo))
    w_gu_all, w_d_all = ffn_w_gate_up.astype(BF16), ffn_w_down.astype(BF16)
    for layer in range(depth):
        kv = _mem_kv(mem, mem_norm_g[layer], wkv_all, layer)
        xa = (norm_xattn_g[layer], wq_all, kv, wo_all, layer)
        if layer % 2 == 0:
            e = layer // 2
            x = _even_mixer(x, norm_mix_g[layer], ev_w_in[e].astype(BF16), ev_conv_w[e],
                            ev_pool_w[e].astype(BF16), ev_pool_scale[e], ev_w_out[e].astype(BF16),
                            *xa, ts=TS_EVEN)
        else:
            o = layer // 2
            w_main, w_gt, w_uq_p, w_uq_s, w_uk_p, w_uvt = _pack_odd_weights(
                od_w_in[o], od_w_uq[o], od_w_ukv[o])
            qm, km, vm, om, gates_t, qp, kp, vt = _odd_proj(
                x, pos3, norm_mix_g[layer], w_main, w_gt, od_gate_bias[o].reshape(2 * ML_HEADS, 1),
                od_q_norm_g[o].reshape(1, Q_LORA), od_kv_norm_g[o].reshape(1, KV_LORA),
                w_uq_p, w_uq_s, w_uk_p, w_uvt, invf, ts=TS_ODD)
            hm = _mlstm(qm, km, vm, om, gates_t, od_ml_norm_g[o], L=L_MLSTM)
            ha = _mla_attention(qp, kp, vt, tq=TQ_MLA)
            x = _odd_out(x, hm, ha, od_w_out[o].astype(BF16), *xa, ts=TS_OUT)
        x = _swiglu(x, norm_ffn_g[layer], w_gu_all, w_d_all, final_norm_g, layer,
                    tm=TM_FFN, th=TH_FFN, final_norm=(layer == depth - 1))
    return x
```

```python
import functools

import jax
import jax.numpy as jnp
from jax import lax
from jax.experimental import pallas as pl
from jax.experimental.pallas import tpu as pltpu

F32 = jnp.float32
BF16 = jnp.bfloat16

EPS = 1e-6
ROPE_THETA = 10000.0
LANES = 128
SUBLANES = 8
BF16_ROWS = 16
VMEM_LIMIT_BYTES = 56 * 1024 * 1024
NEG_BIG = -1e30
LOG2_E = 1.4426950408889634

SC_WIDTH = 512
CONV_K = 3
POOL_WINDOWS = (2, 4, 8, 16)
POOL_GROUP = 128
HALO = 32
ML_HEADS = 4
ML_HEAD_DIM = 128
ML_WIDTH = ML_HEADS * ML_HEAD_DIM
MLA_HEADS = 8
QK_NOPE = 64
QK_ROPE = 32
V_DIM = 64
Q_LORA = 384
KV_LORA = 256
MLA_WIDTH = MLA_HEADS * V_DIM
XA_HEADS = 4
XA_HEAD_DIM = 128
XA_WIDTH = XA_HEADS * XA_HEAD_DIM


def _params(*sem):
    return pltpu.CompilerParams(dimension_semantics=sem, vmem_limit_bytes=VMEM_LIMIT_BYTES)


def _const_spec(shape):
    nd = len(shape)
    return pl.BlockSpec(shape, lambda *_: (0,) * nd, pipeline_mode=pl.Buffered(1))


def _layer_spec(stacked, layer):
    return pl.BlockSpec((None,) + stacked.shape[1:], lambda *_: (layer, 0, 0),
                        pipeline_mode=pl.Buffered(1))


def _rms(x, g):
    ms = jnp.mean(x * x, axis=-1, keepdims=True)
    return x * lax.rsqrt(ms + EPS) * g


def _dot(a, b):
    return jnp.dot(a, b, preferred_element_type=F32)


def _dot_nt(a, b):
    return lax.dot_general(a, b, (((1,), (1,)), ((), ())), preferred_element_type=F32)


def _even_kernel(x_ref, g_ref, win_ref, convw_ref, poolw_ref, pscale_ref, wout_ref,
                 xg_ref, wq_ref, kt_ref, v_ref, wo_ref, o_ref, ubuf, xbbuf, tmp_a, tmp_b, *, ts):
    si = pl.program_id(1)

    @pl.when(si == 0)
    def _():
        ubuf[0:HALO, :] = jnp.zeros((HALO, SC_WIDTH), F32)
        xbbuf[0:HALO, :] = jnp.zeros((HALO, SC_WIDTH), F32)

    x = x_ref[0]
    h = _rms(x, g_ref[...]).astype(BF16)
    z = _dot(h, win_ref[...])
    g_b = z[:, 0:SC_WIDTH]
    u = z[:, SC_WIDTH:2 * SC_WIDTH] * z[:, 2 * SC_WIDTH:3 * SC_WIDTH]
    xb = z[:, 3 * SC_WIDTH:4 * SC_WIDTH]
    ubuf[HALO:HALO + ts, :] = u
    xbbuf[HALO:HALO + ts, :] = xb

    cw = convw_ref[...]
    conv = cw[2:3, :] * u
    for k in range(CONV_K - 1):
        back = CONV_K - 1 - k
        conv = conv + cw[k:k + 1, :] * ubuf[HALO - back:HALO - back + ts, :]
    ya = g_b * conv

    def window_level(src, c0, dst, r0, shift):
        w2 = (src[r0:HALO + ts, c0:c0 + POOL_GROUP]
              + src[r0 - shift:HALO + ts - shift, c0:c0 + POOL_GROUP])
        if dst is not None:
            dst[r0:HALO + ts, :] = w2
        return w2

    t_idx = si * ts + lax.broadcasted_iota(jnp.int32, (ts, POOL_GROUP), 0)
    ys = [ya.astype(BF16)]
    for gi, w in enumerate(POOL_WINDOWS):
        lo = gi * POOL_GROUP
        xg = xb[:, lo:lo + POOL_GROUP]
        src, c0, shift, level = xbbuf, lo, 1, 1
        while 2 * shift < w:
            dst = tmp_a if src is not tmp_a else tmp_b
            window_level(src, c0, dst, SUBLANES * level, shift)
            src, c0, shift, level = dst, 0, 2 * shift, level + 1
        acc = window_level(src, c0, None, HALO, shift)
        cnt = jnp.minimum(t_idx + 1, w).astype(F32)
        d = (acc / cnt - xg).astype(BF16)
        yg = _dot(d, poolw_ref[gi]) * pscale_ref[:, lo:lo + POOL_GROUP]
        ys.append(yg.astype(BF16))

    ubuf[0:HALO, :] = ubuf[ts:ts + HALO, :]
    xbbuf[0:HALO, :] = xbbuf[ts:ts + HALO, :]

    mix = jnp.concatenate(ys, axis=-1)
    x1 = x + _dot(mix, wout_ref[...])
    o_ref[0] = _xattn_body(x1, xg_ref, wq_ref, kt_ref, v_ref, wo_ref)


def _even_mixer(x, g, w_in, conv_w, pool_w, pool_scale, w_out, xg, wq_all, kv, wo_all, layer, *, ts):
    B, S, D = x.shape
    ts = min(ts, S)
    kern = functools.partial(_even_kernel, ts=ts)
    return pl.pallas_call(
        kern,
        grid=(B, S // ts),
        in_specs=[
            pl.BlockSpec((1, ts, D), lambda b, s: (b, s, 0)),
            _const_spec((1, D)),
            _const_spec(w_in.shape),
            _const_spec(conv_w.shape),
            _const_spec(pool_w.shape),
            _const_spec((1, SC_WIDTH)),
            _const_spec(w_out.shape),
        ] + _xattn_specs(D, wq_all, kv, wo_all, layer),
        out_specs=pl.BlockSpec((1, ts, D), lambda b, s: (b, s, 0)),
        out_shape=jax.ShapeDtypeStruct(x.shape, F32),
        scratch_shapes=[pltpu.VMEM((HALO + ts, SC_WIDTH), F32),
                        pltpu.VMEM((HALO + ts, SC_WIDTH), F32),
                        pltpu.VMEM((HALO + ts, POOL_GROUP), F32),
                        pltpu.VMEM((HALO + ts, POOL_GROUP), F32)],
        compiler_params=_params("arbitrary", "arbitrary"),
        name="even_mixer",
    )(x, g.reshape(1, D), w_in, conv_w, pool_w, pool_scale.reshape(1, SC_WIDTH), w_out,
      xg.reshape(1, D), wq_all, *kv, wo_all)


def _memkv_kernel(mem_ref, g_ref, wkv_ref, kt_ref, v_ref):
    h = _rms(mem_ref[0], g_ref[...]).astype(BF16)
    kv = _dot(h, wkv_ref[...])
    kt_ref[0] = kv[:, :XA_WIDTH].T.astype(BF16)
    v_ref[0] = kv[:, XA_WIDTH:].astype(BF16)


def _mem_kv(mem, g, wkv_all, layer):
    B, M, D = mem.shape
    return pl.pallas_call(
        _memkv_kernel,
        grid=(B,),
        in_specs=[pl.BlockSpec((1, M, D), lambda b: (b, 0, 0)),
                  _const_spec((1, D)), _layer_spec(wkv_all, layer)],
        out_specs=(pl.BlockSpec((1, XA_WIDTH, M), lambda b: (b, 0, 0)),
                   pl.BlockSpec((1, M, XA_WIDTH), lambda b: (b, 0, 0))),
        out_shape=(jax.ShapeDtypeStruct((B, XA_WIDTH, M), BF16),
                   jax.ShapeDtypeStruct((B, M, XA_WIDTH), BF16)),
        compiler_params=_params("arbitrary"),
        name="mem_kv",
    )(mem, g.reshape(1, D), wkv_all)


def _xattn_body(x, g_ref, wq_ref, kt_ref, v_ref, wo_ref):
    h = _rms(x, g_ref[...]).astype(BF16)
    q = (_dot(h, wq_ref[...]) * (XA_HEAD_DIM ** -0.5)).astype(BF16)
    ones_v = jnp.ones((v_ref.shape[1], XA_HEAD_DIM), BF16)
    outs = []
    for hd in range(XA_HEADS):
        lo = hd * XA_HEAD_DIM
        s = _dot(q[:, lo:lo + XA_HEAD_DIM], kt_ref[0, lo:lo + XA_HEAD_DIM, :])
        p = jnp.exp(s - jnp.max(s, axis=-1, keepdims=True)).astype(BF16)
        v_aug = jnp.concatenate([v_ref[0, :, lo:lo + XA_HEAD_DIM], ones_v], axis=1)
        o = _dot(p, v_aug)
        outs.append((o[:, :XA_HEAD_DIM] / o[:, XA_HEAD_DIM:]).astype(BF16))
    a = jnp.concatenate(outs, axis=-1)
    return x + _dot(a, wo_ref[...])


def _xattn_specs(D, wq_all, kv, wo_all, layer):
    kt, v = kv
    return [_const_spec((1, D)), _layer_spec(wq_all, layer),
            pl.BlockSpec((1,) + kt.shape[1:], lambda b, s: (b, 0, 0)),
            pl.BlockSpec((1,) + v.shape[1:], lambda b, s: (b, 0, 0)), _layer_spec(wo_all, layer)]


def _swiglu_kernel(x_ref, g_ref, wgu_ref, wd_ref, fg_ref, o_ref, *, hidden, th, final_norm):
    x = x_ref[...]
    h = _rms(x, g_ref[...]).astype(BF16)
    acc = x
    for j in range(hidden // th):
        gate = _dot(h, wgu_ref[:, j * th:(j + 1) * th])
        up = _dot(h, wgu_ref[:, hidden + j * th:hidden + (j + 1) * th])
        act = (gate * jax.nn.sigmoid(gate) * up).astype(BF16)
        acc = acc + _dot(act, wd_ref[j * th:(j + 1) * th, :])
    if final_norm:
        acc = _rms(acc, fg_ref[...])
    o_ref[...] = acc


def _swiglu(x, g, w_gu_all, w_d_all, final_g, layer, *, tm, th, final_norm):
    B, S, D = x.shape
    T = B * S
    tm = min(tm, T)
    hidden = w_d_all.shape[1]
    kern = functools.partial(_swiglu_kernel, hidden=hidden, th=th, final_norm=final_norm)
    out = pl.pallas_call(
        kern,
        grid=(T // tm,),
        in_specs=[
            pl.BlockSpec((tm, D), lambda i: (i, 0)),
            _const_spec((1, D)),
            _layer_spec(w_gu_all, layer),
            _layer_spec(w_d_all, layer),
            _const_spec((1, D)),
        ],
        out_specs=pl.BlockSpec((tm, D), lambda i: (i, 0)),
        out_shape=jax.ShapeDtypeStruct((T, D), F32),
        compiler_params=_params("arbitrary"),
        name="swiglu",
    )(x.reshape(T, D), g.reshape(1, D), w_gu_all, w_d_all, final_g.reshape(1, D))
    return out.reshape(B, S, D)


def _odd_proj_kernel(x_ref, pos_ref, g_ref, wmain_ref, wgt_ref, gbias_ref,
                     qg_ref, kvg_ref, wuq_ref, wuqs_ref, wuk_ref, wuvt_ref, invf_ref,
                     qm_ref, km_ref, vm_ref, om_ref, gates_ref, qp_ref, kp_ref, vt_ref):
    x = x_ref[0]
    h = _rms(x, g_ref[...]).astype(BF16)
    c0 = 4 * ML_WIDTH

    def ml_cols(i):
        return _dot(h, wmain_ref[:, i * ML_WIDTH:(i + 1) * ML_WIDTH])

    zl = _dot(h, wmain_ref[:, c0:])
    c_q = zl[:, 0:Q_LORA]
    c_kv = zl[:, Q_LORA:Q_LORA + KV_LORA]
    kr = zl[:, Q_LORA + KV_LORA:Q_LORA + KV_LORA + LANES]
    kr_sw = zl[:, Q_LORA + KV_LORA + LANES:Q_LORA + KV_LORA + 2 * LANES]
    gates_ref[0] = _dot_nt(wgt_ref[...], h) + gbias_ref[...]

    half = QK_ROPE // 2
    ang_t = invf_ref[...] * pos_ref[0].astype(F32)
    cos_c, sin_c = jnp.cos(ang_t), jnp.sin(ang_t)
    ts = ang_t.shape[1]
    pad_lo = jnp.zeros((QK_NOPE, ts), F32)
    pad_hi = jnp.zeros((LANES - QK_NOPE - QK_ROPE, ts), F32)
    cos_t = jnp.concatenate([pad_lo + 1.0, cos_c, cos_c, pad_hi], axis=0).T
    sin_t = jnp.concatenate([pad_lo, -sin_c, sin_c, pad_hi], axis=0).T
    scale = (QK_NOPE + QK_ROPE) ** -0.5 * LOG2_E
    cos_q, sin_q = cos_t * scale, sin_t * scale
    qm_ref[0] = ml_cols(0).astype(BF16)
    km_ref[0] = (ml_cols(1) * (ML_HEAD_DIM ** -0.5)).astype(BF16)

    cqn = _rms(c_q, qg_ref[...]).astype(BF16)
    qa = _dot(cqn, wuq_ref[...])
    qb = _dot(cqn, wuqs_ref[...])
    ckn = _rms(c_kv, kvg_ref[...]).astype(BF16)
    kn = _dot(ckn, wuk_ref[...])
    vt_ref[0] = _dot_nt(wuvt_ref[...], ckn).astype(BF16)
    k_rope = kr * cos_t + kr_sw * sin_t

    def rope_heads(first, last):
        for hd in range(first, last):
            lo = hd * LANES
            qp_ref[0, lo:lo + LANES, :] = (qa[:, lo:lo + LANES] * cos_q
                                           + qb[:, lo:lo + LANES] * sin_q).T.astype(BF16)
            kp_ref[0, :, lo:lo + LANES] = (kn[:, lo:lo + LANES] + k_rope).astype(BF16)

    vm_ref[0] = ml_cols(2).astype(BF16)
    rope_heads(0, MLA_HEADS // 2)
    om_ref[0] = ml_cols(3).astype(BF16)
    rope_heads(MLA_HEADS // 2, MLA_HEADS)


def _odd_proj(x, pos3, g, w_main, w_gt, gbias, q_g, kv_g, w_uq_p, w_uq_s, w_uk_p, w_uvt, invf, *, ts):
    B, S, D = x.shape
    ts = min(ts, S)
    tok = lambda n: pl.BlockSpec((1, ts, n), lambda b, s: (b, s, 0))
    out_shapes = (
        jax.ShapeDtypeStruct((B, S, ML_WIDTH), BF16),
        jax.ShapeDtypeStruct((B, S, ML_WIDTH), BF16),
        jax.ShapeDtypeStruct((B, S, ML_WIDTH), BF16),
        jax.ShapeDtypeStruct((B, S, ML_WIDTH), BF16),
        jax.ShapeDtypeStruct((B, 2 * ML_HEADS, S), F32),
        jax.ShapeDtypeStruct((B, MLA_HEADS * LANES, S), BF16),
        jax.ShapeDtypeStruct((B, S, MLA_HEADS * LANES), BF16),
        jax.ShapeDtypeStruct((B, MLA_WIDTH, S), BF16),
    )
    return pl.pallas_call(
        _odd_proj_kernel,
        grid=(B, S // ts),
        in_specs=[
            tok(D),
            pl.BlockSpec((1, 1, ts), lambda b, s: (b, 0, s)),
            _const_spec((1, D)),
            _const_spec(w_main.shape),
            _const_spec(w_gt.shape),
            _const_spec(gbias.shape),
            _const_spec(q_g.shape),
            _const_spec(kv_g.shape),
            _const_spec(w_uq_p.shape),
            _const_spec(w_uq_s.shape),
            _const_spec(w_uk_p.shape),
            _const_spec(w_uvt.shape),
            _const_spec(invf.shape),
        ],
        out_specs=(tok(ML_WIDTH), tok(ML_WIDTH), tok(ML_WIDTH), tok(ML_WIDTH),
                   pl.BlockSpec((1, 2 * ML_HEADS, ts), lambda b, s: (b, 0, s)),
                   pl.BlockSpec((1, MLA_HEADS * LANES, ts), lambda b, s: (b, 0, s)),
                   tok(MLA_HEADS * LANES),
                   pl.BlockSpec((1, MLA_WIDTH, ts), lambda b, s: (b, 0, s))),
        out_shape=out_shapes,
        compiler_params=_params("arbitrary", "arbitrary"),
        name="odd_proj",
    )(x, pos3, g.reshape(1, D), w_main, w_gt, gbias, q_g, kv_g, w_uq_p, w_uq_s, w_uk_p, w_uvt, invf)


def _split3(a):
    hi = a.astype(BF16)
    r1 = a - hi.astype(F32)
    mid = r1.astype(BF16)
    lo = (r1 - mid.astype(F32)).astype(BF16)
    return hi, mid, lo


def _log2_sigmoid(x):
    return (jnp.minimum(x, 0.0) - jnp.log(1.0 + jnp.exp(-jnp.abs(x)))) * LOG2_E


def _mlstm_kernel(q_ref, k_ref, v_ref, o_ref, gt_ref, ng_ref, h_ref, c_scr, m_scr, *, L):
    ci = pl.program_id(1)

    @pl.when(ci == 0)
    def _():
        c_scr[...] = jnp.zeros(c_scr.shape, F32)
        m_scr[...] = jnp.zeros(m_scr.shape, F32)

    row = lax.broadcasted_iota(jnp.int32, (L, L), 0)
    col = lax.broadcasted_iota(jnp.int32, (L, L), 1)
    causal = col <= row
    gt = gt_ref[0]
    ones_v = jnp.ones((L, LANES), BF16)
    reps = L // LANES

    def widen(a):
        return jnp.concatenate([a] * reps, axis=1)

    lf_t = _log2_sigmoid(gt)
    b_t = sum(_dot(t, (row <= col).astype(BF16)) for t in _split3(lf_t))
    b_c = jnp.concatenate([b_t, jnp.zeros((LANES - 2 * ML_HEADS, L), F32)], axis=0).T

    for hd in range(ML_HEADS):
        lo = hd * ML_HEAD_DIM
        fg = ML_HEADS + hd
        q = q_ref[0, :, lo:lo + ML_HEAD_DIM]
        k = k_ref[0, :, lo:lo + ML_HEAD_DIM]
        v_aug = jnp.concatenate([v_ref[0, :, lo:lo + ML_HEAD_DIM], ones_v], axis=1)
        g_row = gt[hd:hd + 1, :] * LOG2_E - b_t[fg:fg + 1, :]
        gtot = jnp.sum(lf_t[fg:fg + 1, :], axis=1, keepdims=True)
        b_col = jnp.broadcast_to(b_c[:, fg:fg + 1], (L, LANES))

        m_prev = m_scr[hd]
        c_prev = c_scr[hd]

        g_mask = jnp.where(causal, g_row, NEG_BIG)
        gmax = jnp.max(g_mask, axis=1, keepdims=True)
        h_col = jnp.maximum(m_prev, jnp.broadcast_to(gmax, (L, LANES)))
        w_inter = jnp.exp2(m_prev - h_col)
        sw = jnp.exp2(g_mask - widen(h_col)) * _dot_nt(q, k)
        out = (jnp.concatenate([w_inter, w_inter], axis=1) * _dot(q, c_prev.astype(BF16))
               + _dot(sw.astype(BF16), v_aug))
        den = jnp.maximum(jnp.abs(out[:, ML_HEAD_DIM:]), jnp.exp2(-(b_col + h_col)))
        hc = out[:, :ML_HEAD_DIM] / den

        og = jax.nn.sigmoid(o_ref[0, :, lo:lo + ML_HEAD_DIM].astype(F32))
        hg = og * hc
        ms = jnp.mean(hg * hg, axis=1, keepdims=True)
        h_ref[0, :, lo:lo + ML_HEAD_DIM] = (
            hg * lax.rsqrt(ms + EPS) * ng_ref[:, lo:lo + ML_HEAD_DIM]).astype(BF16)

        m_prev1 = m_prev[:, 0:1]
        a_row = gtot + g_row
        m_loc = jnp.max(a_row, axis=1, keepdims=True)
        m_new = jnp.maximum(gtot + m_prev1, m_loc)
        s_old = jnp.exp2(gtot + m_prev1 - m_new)
        w_loc = jnp.exp2(a_row - m_new)
        kw_t = (k.T.astype(F32) * w_loc).astype(BF16)
        c_scr[hd] = s_old * c_prev + _dot(kw_t, v_aug)
        m_scr[hd] = jnp.broadcast_to(m_new, (1, LANES))


def _mlstm(q, k, v, o, gates_t, norm_g, *, L):
    B, S, W = q.shape
    L = min(L, S)
    kern = functools.partial(_mlstm_kernel, L=L)
    tok = pl.BlockSpec((1, L, W), lambda b, c: (b, c, 0))
    return pl.pallas_call(
        kern,
        grid=(B, S // L),
        in_specs=[tok, tok, tok, tok,
                  pl.BlockSpec((1, 2 * ML_HEADS, L), lambda b, c: (b, 0, c)),
                  _const_spec((1, W))],
        out_specs=tok,
        out_shape=jax.ShapeDtypeStruct((B, S, W), BF16),
        scratch_shapes=[pltpu.VMEM((ML_HEADS, ML_HEAD_DIM, 2 * ML_HEAD_DIM), F32),
                        pltpu.VMEM((ML_HEADS, 1, LANES), F32)],
        compiler_params=_params("arbitrary", "arbitrary"),
        name="mlstm",
    )(q, k, v, o, gates_t, norm_g.reshape(1, W))


def _mla_kernel(qt_ref, k_ref, vt_ref, o_ref, s00, s01, s10, s11, *, tq):
    g = pl.program_id(2)
    heads = 2
    pieces = 2
    rows = tq // pieces
    hk = tq // 2
    s_scr = ((s00, s01), (s10, s11))
    ones_rows = jnp.ones((BF16_ROWS, tq), BF16)

    def run_tile(t, qi, sa, sb):
        lanes = slice(t * tq, (t + 1) * tq)

        def score_piece(j, slot, hh, pc):
            start = pl.multiple_of(j * tq + pc * rows, rows)
            kb = k_ref[0, pl.ds(start, rows), hh * LANES:(hh + 1) * LANES]
            qt = qt_ref[0, hh * LANES:(hh + 1) * LANES, lanes]
            s_scr[slot][hh][pc * rows:(pc + 1) * rows, :] = _dot(kb, qt)

        def stage_exp(j, slot, hh, m_new, acc, half):
            start = pl.multiple_of(j * tq + half * hk, hk)
            vtb = jnp.concatenate([vt_ref[0, hh * V_DIM:(hh + 1) * V_DIM, pl.ds(start, hk)],
                                   ones_rows[:, :hk]], axis=0)
            p = jnp.exp2(s_scr[slot][hh][half * hk:(half + 1) * hk, :] - m_new).astype(BF16)
            return acc + _dot(vtb, p)

        def block(j, slot, carry, nxt):
            new = []
            for hh in range(heads):
                m, acc = carry[hh]
                score_piece(nxt[0], nxt[1], hh, 0)
                m_new = jnp.maximum(m, jnp.max(s_scr[slot][hh][...], axis=0, keepdims=True))
                score_piece(nxt[0], nxt[1], hh, 1)
                acc = stage_exp(j, slot, hh, m_new, jnp.exp2(m - m_new) * acc, 0)
                acc = stage_exp(j, slot, hh, m_new, acc, 1)
                new.append((m_new, acc))
            return tuple(new)

        def diag_block(slot, carry):
            start = pl.multiple_of(qi * tq, tq)
            kk = lax.broadcasted_iota(jnp.int32, (hk, tq), 0)
            qq = lax.broadcasted_iota(jnp.int32, (hk, tq), 1)
            kb = lax.broadcasted_iota(jnp.int32, (hk, hk), 0)
            qb = lax.broadcasted_iota(jnp.int32, (hk, hk), 1)
            hs = range(heads)
            top = [jnp.where(kk <= qq, s_scr[slot][hh][0:hk, :], NEG_BIG) for hh in hs]
            bot = [jnp.where(kb <= qb, s_scr[slot][hh][hk:, hk:], NEG_BIG) for hh in hs]
            neg = jnp.full((hk, hk), NEG_BIG, F32)
            m_new = [jnp.maximum(carry[hh][0], jnp.max(
                jnp.maximum(top[hh], jnp.concatenate([neg, bot[hh]], axis=1)), axis=0, keepdims=True))
                for hh in hs]
            vt_aug = [jnp.concatenate([vt_ref[0, hh * V_DIM:(hh + 1) * V_DIM, pl.ds(start, tq)],
                                       ones_rows], axis=0) for hh in hs]
            p_top = [jnp.exp2(top[hh] - m_new[hh]).astype(BF16) for hh in hs]
            p_bot = [jnp.exp2(bot[hh] - jnp.broadcast_to(m_new[hh], (hk, tq))[:, hk:]).astype(BF16)
                     for hh in hs]
            outs = []
            for hh in hs:
                m, acc = carry[hh]
                acc = jnp.exp2(m - m_new[hh]) * acc + _dot(vt_aug[hh][:, :hk], p_top[hh])
                hi = acc[:, hk:] + _dot(vt_aug[hh][:, hk:], p_bot[hh])
                acc = jnp.concatenate([acc[:, :hk], hi], axis=1)
                outs.append(acc[:V_DIM] / acc[V_DIM:V_DIM + 1])
            return jnp.concatenate(outs, axis=0)

        for hh in range(heads):
            for pc in range(pieces):
                score_piece(0, sa, hh, pc)

        def body(i, c):
            c = block(2 * i, sa, c, (2 * i + 1, sb))
            return block(2 * i + 1, sb, c, (2 * i + 2, sa))

        init = tuple((jnp.full((1, tq), NEG_BIG, F32), jnp.zeros((V_DIM + BF16_ROWS, tq), F32))
                     for _ in range(heads))
        carry = lax.fori_loop(0, g, body, init)
        if t == 0:
            ot = diag_block(sa, carry)
        else:
            ot = diag_block(sb, block(qi - 1, sa, carry, (qi, sb)))
        o_ref[0, :, lanes] = ot.astype(BF16)

    run_tile(0, 2 * g, 0, 1)
    run_tile(1, 2 * g + 1, 1, 0)


def _mla_attention(qpt, kp, vt, *, tq):
    B, S, _ = kp.shape
    tq = min(tq, S // 2)
    pairs = MLA_HEADS // 2
    kern = functools.partial(_mla_kernel, tq=tq)
    return pl.pallas_call(
        kern,
        grid=(B, pairs, S // (2 * tq)),
        in_specs=[
            pl.BlockSpec((1, 2 * LANES, 2 * tq), lambda b, p, i: (b, p, i)),
            pl.BlockSpec((1, S, 2 * LANES), lambda b, p, i: (b, 0, p)),
            pl.BlockSpec((1, 2 * V_DIM, S), lambda b, p, i: (b, p, 0)),
        ],
        out_specs=pl.BlockSpec((1, LANES, 2 * tq), lambda b, p, i: (b, p, i)),
        out_shape=jax.ShapeDtypeStruct((B, MLA_WIDTH, S), BF16),
        scratch_shapes=[pltpu.VMEM((tq, tq), F32) for _ in range(4)],
        compiler_params=_params("arbitrary", "arbitrary", "arbitrary"),
        name="mla_attention",
    )(qpt, kp, vt)


def _odd_out_kernel(x_ref, hm_ref, ha_ref, wout_ref, xg_ref, wq_ref, kt_ref, v_ref, wo_ref, o_ref):
    mix = jnp.concatenate([hm_ref[0], ha_ref[0].T], axis=-1)
    x1 = x_ref[0] + _dot(mix, wout_ref[...])
    o_ref[0] = _xattn_body(x1, xg_ref, wq_ref, kt_ref, v_ref, wo_ref)


def _odd_out(x, hm, ha, w_out, xg, wq_all, kv, wo_all, layer, *, ts):
    B, S, D = x.shape
    ts = min(ts, S)
    tok = lambda n: pl.BlockSpec((1, ts, n), lambda b, s: (b, s, 0))
    return pl.pallas_call(
        _odd_out_kernel,
        grid=(B, S // ts),
        in_specs=[tok(D), tok(ML_WIDTH), pl.BlockSpec((1, MLA_WIDTH, ts), lambda b, s: (b, 0, s)),
                  _const_spec(w_out.shape)]
        + _xattn_specs(D, wq_all, kv, wo_all, layer),
        out_specs=tok(D),
        out_shape=jax.ShapeDtypeStruct(x.shape, F32),
        compiler_params=_params("arbitrary", "arbitrary"),
        name="odd_out",
    )(x, hm, ha, w_out, xg.reshape(1, D), wq_all, *kv, wo_all)


def _pack_odd_weights(w_in, w_uq, w_ukv):
    D = w_in.shape[0]
    c = 4 * ML_WIDTH
    w_g = w_in[:, c:c + 2 * ML_HEADS]
    c += 2 * ML_HEADS
    w_cq = w_in[:, c:c + Q_LORA]
    c += Q_LORA
    w_ckv = w_in[:, c:c + KV_LORA]
    c += KV_LORA
    w_kr = w_in[:, c:c + QK_ROPE]
    half = QK_ROPE // 2
    w_kr_sw = jnp.concatenate([w_kr[:, half:], w_kr[:, :half]], axis=1)
    zl = jnp.zeros((D, QK_NOPE), w_in.dtype)
    zr = jnp.zeros((D, LANES - QK_NOPE - QK_ROPE), w_in.dtype)
    w_main = jnp.concatenate([w_in[:, :4 * ML_WIDTH], w_cq, w_ckv,
                              zl, w_kr, zr, zl, w_kr_sw, zr], axis=1).astype(BF16)
    w_gt = w_g.T.astype(BF16)

    uq = w_uq.reshape(Q_LORA, MLA_HEADS, QK_NOPE + QK_ROPE)
    uq_n, uq_r = uq[..., :QK_NOPE], uq[..., QK_NOPE:]
    uq_r_sw = jnp.concatenate([uq_r[..., half:], uq_r[..., :half]], axis=-1)
    zpad = jnp.zeros((Q_LORA, MLA_HEADS, LANES - QK_NOPE - QK_ROPE), w_uq.dtype)
    w_uq_p = jnp.concatenate([uq_n, uq_r, zpad], axis=-1).reshape(Q_LORA, MLA_HEADS * LANES)
    w_uq_s = jnp.concatenate([jnp.zeros_like(uq_n), uq_r_sw, zpad], axis=-1)
    w_uq_s = w_uq_s.reshape(Q_LORA, MLA_HEADS * LANES)

    ukv = w_ukv.reshape(KV_LORA, MLA_HEADS, QK_NOPE + V_DIM)
    uk, uv = ukv[..., :QK_NOPE], ukv[..., QK_NOPE:]
    w_uk_p = jnp.concatenate([uk, jnp.zeros((KV_LORA, MLA_HEADS, LANES - QK_NOPE), w_ukv.dtype)],
                             axis=-1).reshape(KV_LORA, MLA_HEADS * LANES)
    w_uvt = uv.reshape(KV_LORA, MLA_WIDTH).T
    return (w_main, w_gt, w_uq_p.astype(BF16), w_uq_s.astype(BF16), w_uk_p.astype(BF16),
            w_uvt.astype(BF16))


def _rope_inv_freq():
    half = QK_ROPE // 2
    inv = ROPE_THETA ** (-jnp.arange(half, dtype=F32) / half)
    return inv.reshape(half, 1)


TS_EVEN = 1024
TM_FFN = 1024
TH_FFN = 256
TS_ODD = 1024
L_MLSTM = 256
TQ_MLA = 512
TS_OUT = 1024


def kernel(x, mem, positions, norm_mix_g, norm_xattn_g, mem_norm_g, xattn_wq, xattn_wkv, xattn_wo,
           norm_ffn_g, ffn_w_gate_up, ffn_w_down, ev_w_in, ev_conv_w, ev_pool_w, ev_pool_scale,
           ev_w_out, od_w_in, od_gate_bias, od_ml_norm_g, od_q_norm_g, od_kv_norm_g, od_w_uq,
           od_w_ukv, od_w_out, final_norm_g):
    depth = norm_mix_g.shape[0]
    B, S, D = x.shape
    pos3 = positions.reshape(B, 1, S)
    invf = _rope_inv_freq()
    wq_all, wkv_all, wo_all = (w.astype(BF16) for w in (xattn_wq, xattn_wkv, xattn_wo))
    w_gu_all, w_d_all = ffn_w_gate_up.astype(BF16), ffn_w_down.astype(BF16)
    for layer in range(depth):
        kv = _mem_kv(mem, mem_norm_g[layer], wkv_all, layer)
        xa = (norm_xattn_g[layer], wq_all, kv, wo_all, layer)
        if layer % 2 == 0:
            e = layer // 2
            x = _even_mixer(x, norm_mix_g[layer], ev_w_in[e].astype(BF16), ev_conv_w[e],
                            ev_pool_w[e].astype(BF16), ev_pool_scale[e], ev_w_out[e].astype(BF16),
                            *xa, ts=TS_EVEN)
        else:
            o = layer // 2
            w_main, w_gt, w_uq_p, w_uq_s, w_uk_p, w_uvt = _pack_odd_weights(
                od_w_in[o], od_w_uq[o], od_w_ukv[o])
            qm, km, vm, om, gates_t, qp, kp, vt = _odd_proj(
                x, pos3, norm_mix_g[layer], w_main, w_gt, od_gate_bias[o].reshape(2 * ML_HEADS, 1),
                od_q_norm_g[o].reshape(1, Q_LORA), od_kv_norm_g[o].reshape(1, KV_LORA),
                w_uq_p, w_uq_s, w_uk_p, w_uvt, invf, ts=TS_ODD)
            hm = _mlstm(qm, km, vm, om, gates_t, od_ml_norm_g[o], L=L_MLSTM)
            ha = _mla_attention(qp, kp, vt, tq=TQ_MLA)
            x = _odd_out(x, hm, ha, od_w_out[o].astype(BF16), *xa, ts=TS_OUT)
        x = _swiglu(x, norm_ffn_g[layer], w_gu_all, w_d_all, final_norm_g, layer,
                    tm=TM_FFN, th=TH_FFN, final_norm=(layer == depth - 1))
    return x
```

```python
import functools

import jax
import jax.numpy as jnp
from jax import lax
from jax.experimental import pallas as pl
from jax.experimental.pallas import tpu as pltpu

F32 = jnp.float32
BF16 = jnp.bfloat16

EPS = 1e-6
ROPE_THETA = 10000.0
LANES = 128
SUBLANES = 8
BF16_ROWS = 16
VMEM_LIMIT_BYTES = 56 * 1024 * 1024
NEG_BIG = -1e30
LOG2_E = 1.4426950408889634

SC_WIDTH = 512
CONV_K = 3
POOL_WINDOWS = (2, 4, 8, 16)
POOL_GROUP = 128
HALO = 32
ML_HEADS = 4
ML_HEAD_DIM = 128
ML_WIDTH = ML_HEADS * ML_HEAD_DIM
MLA_HEADS = 8
QK_NOPE = 64
QK_ROPE = 32
V_DIM = 64
Q_LORA = 384
KV_LORA = 256
MLA_WIDTH = MLA_HEADS * V_DIM
XA_HEADS = 4
XA_HEAD_DIM = 128
XA_WIDTH = XA_HEADS * XA_HEAD_DIM


def _params(*sem):
    return pltpu.CompilerParams(dimension_semantics=sem, vmem_limit_bytes=VMEM_LIMIT_BYTES)


def _const_spec(shape):
    nd = len(shape)
    return pl.BlockSpec(shape, lambda *_: (0,) * nd, pipeline_mode=pl.Buffered(1))


def _layer_spec(stacked, layer):
    return pl.BlockSpec((None,) + stacked.shape[1:], lambda *_: (layer, 0, 0),
                        pipeline_mode=pl.Buffered(1))


def _rms(x, g):
    ms = jnp.mean(x * x, axis=-1, keepdims=True)
    return x * lax.rsqrt(ms + EPS) * g


def _dot(a, b):
    return jnp.dot(a, b, preferred_element_type=F32)


def _dot_nt(a, b):
    return lax.dot_general(a, b, (((1,), (1,)), ((), ())), preferred_element_type=F32)


def _even_kernel(x_ref, g_ref, win_ref, convw_ref, poolw_ref, pscale_ref, wout_ref,
                 xg_ref, wq_ref, kt_ref, v_ref, wo_ref, o_ref, ubuf, xbbuf, tmp_a, tmp_b, *, ts):
    si = pl.program_id(1)

    @pl.when(si == 0)
    def _():
        ubuf[0:HALO, :] = jnp.zeros((HALO, SC_WIDTH), F32)
        xbbuf[0:HALO, :] = jnp.zeros((HALO, SC_WIDTH), F32)

    x = x_ref[0]
    h = _rms(x, g_ref[...]).astype(BF16)
    z = _dot(h, win_ref[...])
    g_b = z[:, 0:SC_WIDTH]
    u = z[:, SC_WIDTH:2 * SC_WIDTH] * z[:, 2 * SC_WIDTH:3 * SC_WIDTH]
    xb = z[:, 3 * SC_WIDTH:4 * SC_WIDTH]
    ubuf[HALO:HALO + ts, :] = u
    xbbuf[HALO:HALO + ts, :] = xb

    cw = convw_ref[...]
    conv = cw[2:3, :] * u
    for k in range(CONV_K - 1):
        back = CONV_K - 1 - k
        conv = conv + cw[k:k + 1, :] * ubuf[HALO - back:HALO - back + ts, :]
    ya = g_b * conv

    def window_level(src, c0, dst, r0, shift):
        w2 = (src[r0:HALO + ts, c0:c0 + POOL_GROUP]
              + src[r0 - shift:HALO + ts - shift, c0:c0 + POOL_GROUP])
        if dst is not None:
            dst[r0:HALO + ts, :] = w2
        return w2

    t_idx = si * ts + lax.broadcasted_iota(jnp.int32, (ts, POOL_GROUP), 0)
    ys = [ya.astype(BF16)]
    for gi, w in enumerate(POOL_WINDOWS):
        lo = gi * POOL_GROUP
        xg = xb[:, lo:lo + POOL_GROUP]
        src, c0, shift, level = xbbuf, lo, 1, 1
        while 2 * shift < w:
            dst = tmp_a if src is not tmp_a else tmp_b
            window_level(src, c0, dst, SUBLANES * level, shift)
            src, c0, shift, level = dst, 0, 2 * shift, level + 1
        acc = window_level(src, c0, None, HALO, shift)
        cnt = jnp.minimum(t_idx + 1, w).astype(F32)
        d = (acc / cnt - xg).astype(BF16)
        yg = _dot(d, poolw_ref[gi]) * pscale_ref[:, lo:lo + POOL_GROUP]
        ys.append(yg.astype(BF16))

    ubuf[0:HALO, :] = ubuf[ts:ts + HALO, :]
    xbbuf[0:HALO, :] = xbbuf[ts:ts + HALO, :]

    mix = jnp.concatenate(ys, axis=-1)
    x1 = x + _dot(mix, wout_ref[...])
    o_ref[0] = _xattn_body(x1, xg_ref, wq_ref, kt_ref, v_ref, wo_ref)


def _even_mixer(x, g, w_in, conv_w, pool_w, pool_scale, w_out, xg, wq_all, kv, wo_all, layer, *, ts):
    B, S, D = x.shape
    ts = min(ts, S)
    kern = functools.partial(_even_kernel, ts=ts)
    return pl.pallas_call(
        kern,
        grid=(B, S // ts),
        in_specs=[
            pl.BlockSpec((1, ts, D), lambda b, s: (b, s, 0)),
            _const_spec((1, D)),
            _const_spec(w_in.shape),
            _const_spec(conv_w.shape),
            _const_spec(pool_w.shape),
            _const_spec((1, SC_WIDTH)),
            _const_spec(w_out.shape),
        ] + _xattn_specs(D, wq_all, kv, wo_all, layer),
        out_specs=pl.BlockSpec((1, ts, D), lambda b, s: (b, s, 0)),
        out_shape=jax.ShapeDtypeStruct(x.shape, F32),
        scratch_shapes=[pltpu.VMEM((HALO + ts, SC_WIDTH), F32),
                        pltpu.VMEM((HALO + ts, SC_WIDTH), F32),
                        pltpu.VMEM((HALO + ts, POOL_GROUP), F32),
                        pltpu.VMEM((HALO + ts, POOL_GROUP), F32)],
        compiler_params=_params("arbitrary", "arbitrary"),
        name="even_mixer",
    )(x, g.reshape(1, D), w_in, conv_w, pool_w, pool_scale.reshape(1, SC_WIDTH), w_out,
      xg.reshape(1, D), wq_all, *kv, wo_all)


def _memkv_kernel(mem_ref, g_ref, wkv_ref, kt_ref, v_ref):
    h = _rms(mem_ref[0], g_ref[...]).astype(BF16)
    kv = _dot(h, wkv_ref[...])
    kt_ref[0] = kv[:, :XA_WIDTH].T.astype(BF16)
    v_ref[0] = kv[:, XA_WIDTH:].astype(BF16)


def _mem_kv(mem, g, wkv_all, layer):
    B, M, D = mem.shape
    return pl.pallas_call(
        _memkv_kernel,
        grid=(B,),
        in_specs=[pl.BlockSpec((1, M, D), lambda b: (b, 0, 0)),
                  _const_spec((1, D)), _layer_spec(wkv_all, layer)],
        out_specs=(pl.BlockSpec((1, XA_WIDTH, M), lambda b: (b, 0, 0)),
                   pl.BlockSpec((1, M, XA_WIDTH), lambda b: (b, 0, 0))),
        out_shape=(jax.ShapeDtypeStruct((B, XA_WIDTH, M), BF16),
                   jax.ShapeDtypeStruct((B, M, XA_WIDTH), BF16)),
        compiler_params=_params("arbitrary"),
        name="mem_kv",
    )(mem, g.reshape(1, D), wkv_all)


def _xattn_body(x, g_ref, wq_ref, kt_ref, v_ref, wo_ref):
    h = _rms(x, g_ref[...]).astype(BF16)
    q = (_dot(h, wq_ref[...]) * (XA_HEAD_DIM ** -0.5)).astype(BF16)
    ones_v = jnp.ones((v_ref.shape[1], XA_HEAD_DIM), BF16)
    outs = []
    for hd in range(XA_HEADS):
        lo = hd * XA_HEAD_DIM
        s = _dot(q[:, lo:lo + XA_HEAD_DIM], kt_ref[0, lo:lo + XA_HEAD_DIM, :])
        p = jnp.exp(s - jnp.max(s, axis=-1, keepdims=True)).astype(BF16)
        v_aug = jnp.concatenate([v_ref[0, :, lo:lo + XA_HEAD_DIM], ones_v], axis=1)
        o = _dot(p, v_aug)
        outs.append((o[:, :XA_HEAD_DIM] / o[:, XA_HEAD_DIM:]).astype(BF16))
    a = jnp.concatenate(outs, axis=-1)
    return x + _dot(a, wo_ref[...])


def _xattn_specs(D, wq_all, kv, wo_all, layer):
    kt, v = kv
    return [_const_spec((1, D)), _layer_spec(wq_all, layer),
            pl.BlockSpec((1,) + kt.shape[1:], lambda b, s: (b, 0, 0)),
            pl.BlockSpec((1,) + v.shape[1:], lambda b, s: (b, 0, 0)), _layer_spec(wo_all, layer)]


def _swiglu_kernel(x_ref, g_ref, wgu_ref, wd_ref, fg_ref, o_ref, *, hidden, th, final_norm):
    x = x_ref[...]
    h = _rms(x, g_ref[...]).astype(BF16)
    acc = x
    for j in range(hidden // th):
        gate = _dot(h, wgu_ref[:, j * th:(j + 1) * th])
        up = _dot(h, wgu_ref[:, hidden + j * th:hidden + (j + 1) * th])
        act = (gate * jax.nn.sigmoid(gate) * up).astype(BF16)
        acc = acc + _dot(act, wd_ref[j * th:(j + 1) * th, :])
    if final_norm:
        acc = _rms(acc, fg_ref[...])
    o_ref[...] = acc


def _swiglu(x, g, w_gu_all, w_d_all, final_g, layer, *, tm, th, final_norm):
    B, S, D = x.shape
    T = B * S
    tm = min(tm, T)
    hidden = w_d_all.shape[1]
    kern = functools.partial(_swiglu_kernel, hidden=hidden, th=th, final_norm=final_norm)
    out = pl.pallas_call(
        kern,
        grid=(T // tm,),
        in_specs=[
            pl.BlockSpec((tm, D), lambda i: (i, 0)),
            _const_spec((1, D)),
            _layer_spec(w_gu_all, layer),
            _layer_spec(w_d_all, layer),
            _const_spec((1, D)),
        ],
        out_specs=pl.BlockSpec((tm, D), lambda i: (i, 0)),
        out_shape=jax.ShapeDtypeStruct((T, D), F32),
        compiler_params=_params("arbitrary"),
        name="swiglu",
    )(x.reshape(T, D), g.reshape(1, D), w_gu_all, w_d_all, final_g.reshape(1, D))
    return out.reshape(B, S, D)


def _odd_proj_kernel(x_ref, pos_ref, g_ref, wmain_ref, wgt_ref, gbias_ref,
                     qg_ref, kvg_ref, wuq_ref, wuqs_ref, wuk_ref, wuvt_ref, invf_ref,
                     qm_ref, km_ref, vm_ref, om_ref, gates_ref, qp_ref, kp_ref, vt_ref):
    x = x_ref[0]
    h = _rms(x, g_ref[...]).astype(BF16)
    c0 = 4 * ML_WIDTH

    def ml_cols(i):
        return _dot(h, wmain_ref[:, i * ML_WIDTH:(i + 1) * ML_WIDTH])

    zl = _dot(h, wmain_ref[:, c0:])
    c_q = zl[:, 0:Q_LORA]
    c_kv = zl[:, Q_LORA:Q_LORA + KV_LORA]
    kr = zl[:, Q_LORA + KV_LORA:Q_LORA + KV_LORA + LANES]
    kr_sw = zl[:, Q_LORA + KV_LORA + LANES:Q_LORA + KV_LORA + 2 * LANES]
    gates_ref[0] = _dot_nt(wgt_ref[...], h) + gbias_ref[...]

    half = QK_ROPE // 2
    ang_t = invf_ref[...] * pos_ref[0].astype(F32)
    cos_c, sin_c = jnp.cos(ang_t), jnp.sin(ang_t)
    ts = ang_t.shape[1]
    pad_lo = jnp.zeros((QK_NOPE, ts), F32)
    pad_hi = jnp.zeros((LANES - QK_NOPE - QK_ROPE, ts), F32)
    cos_t = jnp.concatenate([pad_lo + 1.0, cos_c, cos_c, pad_hi], axis=0).T
    sin_t = jnp.concatenate([pad_lo, -sin_c, sin_c, pad_hi], axis=0).T
    scale = (QK_NOPE + QK_ROPE) ** -0.5 * LOG2_E
    cos_q, sin_q = cos_t * scale, sin_t * scale
    qm_ref[0] = ml_cols(0).astype(BF16)
    km_ref[0] = (ml_cols(1) * (ML_HEAD_DIM ** -0.5)).astype(BF16)

    cqn = _rms(c_q, qg_ref[...]).astype(BF16)
    qa = _dot(cqn, wuq_ref[...])
    qb = _dot(cqn, wuqs_ref[...])
    ckn = _rms(c_kv, kvg_ref[...]).astype(BF16)
    kn = _dot(ckn, wuk_ref[...])
    vt_ref[0] = _dot_nt(wuvt_ref[...], ckn).astype(BF16)
    k_rope = kr * cos_t + kr_sw * sin_t

    def rope_heads(first, last):
        for hd in range(first, last):
            lo = hd * LANES
            qp_ref[0, lo:lo + LANES, :] = (qa[:, lo:lo + LANES] * cos_q
                                           + qb[:, lo:lo + LANES] * sin_q).T.astype(BF16)
            kp_ref[0, :, lo:lo + LANES] = (kn[:, lo:lo + LANES] + k_rope).astype(BF16)

    vm_ref[0] = ml_cols(2).astype(BF16)
    rope_heads(0, MLA_HEADS // 2)
    om_ref[0] = ml_cols(3).astype(BF16)
    rope_heads(MLA_HEADS // 2, MLA_HEADS)


def _odd_proj(x, pos3, g, w_main, w_gt, gbias, q_g, kv_g, w_uq_p, w_uq_s, w_uk_p, w_uvt, invf, *, ts):
    B, S, D = x.shape
    ts = min(ts, S)
    tok = lambda n: pl.BlockSpec((1, ts, n), lambda b, s: (b, s, 0))
    out_shapes = (
        jax.ShapeDtypeStruct((B, S, ML_WIDTH), BF16),
        jax.ShapeDtypeStruct((B, S, ML_WIDTH), BF16),
        jax.ShapeDtypeStruct((B, S, ML_WIDTH), BF16),
        jax.ShapeDtypeStruct((B, S, ML_WIDTH), BF16),
        jax.ShapeDtypeStruct((B, 2 * ML_HEADS, S), F32),
        jax.ShapeDtypeStruct((B, MLA_HEADS * LANES, S), BF16),
        jax.ShapeDtypeStruct((B, S, MLA_HEADS * LANES), BF16),
        jax.ShapeDtypeStruct((B, MLA_WIDTH, S), BF16),
    )
    return pl.pallas_call(
        _odd_proj_kernel,
        grid=(B, S // ts),
        in_specs=[
            tok(D),
            pl.BlockSpec((1, 1, ts), lambda b, s: (b, 0, s)),
            _const_spec((1, D)),
            _const_spec(w_main.shape),
            _const_spec(w_gt.shape),
            _const_spec(gbias.shape),
            _const_spec(q_g.shape),
            _const_spec(kv_g.shape),
            _const_spec(w_uq_p.shape),
            _const_spec(w_uq_s.shape),
            _const_spec(w_uk_p.shape),
            _const_spec(w_uvt.shape),
            _const_spec(invf.shape),
        ],
        out_specs=(tok(ML_WIDTH), tok(ML_WIDTH), tok(ML_WIDTH), tok(ML_WIDTH),
                   pl.BlockSpec((1, 2 * ML_HEADS, ts), lambda b, s: (b, 0, s)),
                   pl.BlockSpec((1, MLA_HEADS * LANES, ts), lambda b, s: (b, 0, s)),
                   tok(MLA_HEADS * LANES),
                   pl.BlockSpec((1, MLA_WIDTH, ts), lambda b, s: (b, 0, s))),
        out_shape=out_shapes,
        compiler_params=_params("arbitrary", "arbitrary"),
        name="odd_proj",
    )(x, pos3, g.reshape(1, D), w_main, w_gt, gbias, q_g, kv_g, w_uq_p, w_uq_s, w_uk_p, w_uvt, invf)


def _split3(a):
    hi = a.astype(BF16)
    r1 = a - hi.astype(F32)
    mid = r1.astype(BF16)
    lo = (r1 - mid.astype(F32)).astype(BF16)
    return hi, mid, lo


def _log2_sigmoid(x):
    return (jnp.minimum(x, 0.0) - jnp.log(1.0 + jnp.exp(-jnp.abs(x)))) * LOG2_E


def _mlstm_kernel(q_ref, k_ref, v_ref, o_ref, gt_ref, ng_ref, h_ref, c_scr, m_scr, *, L):
    ci = pl.program_id(1)

    @pl.when(ci == 0)
    def _():
        c_scr[...] = jnp.zeros(c_scr.shape, F32)
        m_scr[...] = jnp.zeros(m_scr.shape, F32)

    row = lax.broadcasted_iota(jnp.int32, (L, L), 0)
    col = lax.broadcasted_iota(jnp.int32, (L, L), 1)
    causal = col <= row
    gt = gt_ref[0]
    ones_v = jnp.ones((L, LANES), BF16)
    reps = L // LANES

    def widen(a):
        return jnp.concatenate([a] * reps, axis=1)

    lf_t = _log2_sigmoid(gt)
    b_t = sum(_dot(t, (row <= col).astype(BF16)) for t in _split3(lf_t))
    b_c = jnp.concatenate([b_t, jnp.zeros((LANES - 2 * ML_HEADS, L), F32)], axis=0).T

    for hd in range(ML_HEADS):
        lo = hd * ML_HEAD_DIM
        fg = ML_HEADS + hd
        q = q_ref[0, :, lo:lo + ML_HEAD_DIM]
        k = k_ref[0, :, lo:lo + ML_HEAD_DIM]
        v_aug = jnp.concatenate([v_ref[0, :, lo:lo + ML_HEAD_DIM], ones_v], axis=1)
        g_row = gt[hd:hd + 1, :] * LOG2_E - b_t[fg:fg + 1, :]
        gtot = jnp.sum(lf_t[fg:fg + 1, :], axis=1, keepdims=True)
        b_col = jnp.broadcast_to(b_c[:, fg:fg + 1], (L, LANES))

        m_prev = m_scr[hd]
        c_prev = c_scr[hd]

        g_mask = jnp.where(causal, g_row, NEG_BIG)
        gmax = jnp.max(g_mask, axis=1, keepdims=True)
        h_col = jnp.maximum(m_prev, jnp.broadcast_to(gmax, (L, LANES)))
        w_inter = jnp.exp2(m_prev - h_col)
        sw = jnp.exp2(g_mask - widen(h_col)) * _dot_nt(q, k)
        out = (jnp.concatenate([w_inter, w_inter], axis=1) * _dot(q, c_prev.astype(BF16))
               + _dot(sw.astype(BF16), v_aug))
        den = jnp.maximum(jnp.abs(out[:, ML_HEAD_DIM:]), jnp.exp2(-(b_col + h_col)))
        hc = out[:, :ML_HEAD_DIM] / den

        og = jax.nn.sigmoid(o_ref[0, :, lo:lo + ML_HEAD_DIM].astype(F32))
        hg = og * hc
        ms = jnp.mean(hg * hg, axis=1, keepdims=True)
        h_ref[0, :, lo:lo + ML_HEAD_DIM] = (
            hg * lax.rsqrt(ms + EPS) * ng_ref[:, lo:lo + ML_HEAD_DIM]).astype(BF16)

        m_prev1 = m_prev[:, 0:1]
        a_row = gtot + g_row
        m_loc = jnp.max(a_row, axis=1, keepdims=True)
        m_new = jnp.maximum(gtot + m_prev1, m_loc)
        s_old = jnp.exp2(gtot + m_prev1 - m_new)
        w_loc = jnp.exp2(a_row - m_new)
        kw_t = (k.T.astype(F32) * w_loc).astype(BF16)
        c_scr[hd] = s_old * c_prev + _dot(kw_t, v_aug)
        m_scr[hd] = jnp.broadcast_to(m_new, (1, LANES))


def _mlstm(q, k, v, o, gates_t, norm_g, *, L):
    B, S, W = q.shape
    L = min(L, S)
    kern = functools.partial(_mlstm_kernel, L=L)
    tok = pl.BlockSpec((1, L, W), lambda b, c: (b, c, 0))
    return pl.pallas_call(
        kern,
        grid=(B, S // L),
        in_specs=[tok, tok, tok, tok,
                  pl.BlockSpec((1, 2 * ML_HEADS, L), lambda b, c: (b, 0, c)),
                  _const_spec((1, W))],
        out_specs=tok,
        out_shape=jax.ShapeDtypeStruct((B, S, W), BF16),
        scratch_shapes=[pltpu.VMEM((ML_HEADS, ML_HEAD_DIM, 2 * ML_HEAD_DIM), F32),
                        pltpu.VMEM((ML_HEADS, 1, LANES), F32)],
        compiler_params=_params("arbitrary", "arbitrary"),
        name="mlstm",
    )(q, k, v, o, gates_t, norm_g.reshape(1, W))


def _mla_kernel(qt_ref, k_ref, vt_ref, o_ref, s00, s01, s10, s11, *, tq, nt):
    g = pl.program_id(2)
    heads = 2
    pieces = 2
    rows = tq // pieces
    hk = tq // 2
    s_scr = ((s00, s01), (s10, s11))
    ones_rows = jnp.ones((BF16_ROWS, tq), BF16)

    def run_tile(t, qi, sa, sb):
        lanes = slice(t * tq, (t + 1) * tq)

        def score_piece(j, slot, hh, pc):
            start = pl.multiple_of(j * tq + pc * rows, rows)
            kb = k_ref[0, pl.ds(start, rows), hh * LANES:(hh + 1) * LANES]
            qt = qt_ref[0, hh * LANES:(hh + 1) * LANES, lanes]
            s_scr[slot][hh][pc * rows:(pc + 1) * rows, :] = _dot(kb, qt)

        def stage_exp(j, slot, hh, m_new, acc, half):
            start = pl.multiple_of(j * tq + half * hk, hk)
            vtb = jnp.concatenate([vt_ref[0, hh * V_DIM:(hh + 1) * V_DIM, pl.ds(start, hk)],
                                   ones_rows[:, :hk]], axis=0)
            p = jnp.exp2(s_scr[slot][hh][half * hk:(half + 1) * hk, :] - m_new).astype(BF16)
            return acc + _dot(vtb, p)

        def block(j, slot, carry, nxt):
            new = []
            for hh in range(heads):
                m, acc = carry[hh]
                score_piece(nxt[0], nxt[1], hh, 0)
                m_new = jnp.maximum(m, jnp.max(s_scr[slot][hh][...], axis=0, keepdims=True))
                score_piece(nxt[0], nxt[1], hh, 1)
                acc = stage_exp(j, slot, hh, m_new, jnp.exp2(m - m_new) * acc, 0)
                acc = stage_exp(j, slot, hh, m_new, acc, 1)
                new.append((m_new, acc))
            return tuple(new)

        def diag_block(slot, carry):
            start = pl.multiple_of(qi * tq, tq)
            kk = lax.broadcasted_iota(jnp.int32, (hk, tq), 0)
            qq = lax.broadcasted_iota(jnp.int32, (hk, tq), 1)
            kb = lax.broadcasted_iota(jnp.int32, (hk, hk), 0)
            qb = lax.broadcasted_iota(jnp.int32, (hk, hk), 1)
            hs = range(heads)
            top = [jnp.where(kk <= qq, s_scr[slot][hh][0:hk, :], NEG_BIG) for hh in hs]
            bot = [jnp.where(kb <= qb, s_scr[slot][hh][hk:, hk:], NEG_BIG) for hh in hs]
            neg = jnp.full((hk, hk), NEG_BIG, F32)
            m_new = [jnp.maximum(carry[hh][0], jnp.max(
                jnp.maximum(top[hh], jnp.concatenate([neg, bot[hh]], axis=1)), axis=0, keepdims=True))
                for hh in hs]
            vt_aug = [jnp.concatenate([vt_ref[0, hh * V_DIM:(hh + 1) * V_DIM, pl.ds(start, tq)],
                                       ones_rows], axis=0) for hh in hs]
            p_top = [jnp.exp2(top[hh] - m_new[hh]).astype(BF16) for hh in hs]
            p_bot = [jnp.exp2(bot[hh] - jnp.broadcast_to(m_new[hh], (hk, tq))[:, hk:]).astype(BF16)
                     for hh in hs]
            outs = []
            for hh in hs:
                m, acc = carry[hh]
                acc = jnp.exp2(m - m_new[hh]) * acc + _dot(vt_aug[hh][:, :hk], p_top[hh])
                hi = acc[:, hk:] + _dot(vt_aug[hh][:, hk:], p_bot[hh])
                acc = jnp.concatenate([acc[:, :hk], hi], axis=1)
                outs.append(acc[:V_DIM] / acc[V_DIM:V_DIM + 1])
            return jnp.concatenate(outs, axis=0)

        for hh in range(heads):
            for pc in range(pieces):
                score_piece(0, sa, hh, pc)

        def body(i, c):
            c = block(2 * i, sa, c, (2 * i + 1, sb))
            return block(2 * i + 1, sb, c, (2 * i + 2, sa))

        init = tuple((jnp.full((1, tq), NEG_BIG, F32), jnp.zeros((V_DIM + BF16_ROWS, tq), F32))
                     for _ in range(heads))
        carry = lax.fori_loop(0, qi // 2, body, init)
        if t % 2 == 0:
            ot = diag_block(sa, carry)
        else:
            ot = diag_block(sb, block(qi - 1, sa, carry, (qi, sb)))
        o_ref[0, :, lanes] = ot.astype(BF16)

    slot_pairs = ((0, 1), (1, 0), (1, 0), (0, 1))
    for t in range(nt):
        run_tile(t, nt * g + t, *slot_pairs[t % 4])


def _mla_attention(qpt, kp, vt, *, tq, nt):
    B, S, _ = kp.shape
    tq = min(tq, S // 2)
    nt = min(nt, S // tq)
    assert nt % 2 == 0 and S % (nt * tq) == 0
    pairs = MLA_HEADS // 2
    kern = functools.partial(_mla_kernel, tq=tq, nt=nt)
    return pl.pallas_call(
        kern,
        grid=(B, pairs, S // (nt * tq)),
        in_specs=[
            pl.BlockSpec((1, 2 * LANES, nt * tq), lambda b, p, i: (b, p, i)),
            pl.BlockSpec((1, S, 2 * LANES), lambda b, p, i: (b, 0, p)),
            pl.BlockSpec((1, 2 * V_DIM, S), lambda b, p, i: (b, p, 0)),
        ],
        out_specs=pl.BlockSpec((1, LANES, nt * tq), lambda b, p, i: (b, p, i)),
        out_shape=jax.ShapeDtypeStruct((B, MLA_WIDTH, S), BF16),
        scratch_shapes=[pltpu.VMEM((tq, tq), F32) for _ in range(4)],
        compiler_params=_params("arbitrary", "arbitrary", "arbitrary"),
        name="mla_attention",
    )(qpt, kp, vt)


def _odd_out_kernel(x_ref, hm_ref, ha_ref, wout_ref, xg_ref, wq_ref, kt_ref, v_ref, wo_ref, o_ref):
    mix = jnp.concatenate([hm_ref[0], ha_ref[0].T], axis=-1)
    x1 = x_ref[0] + _dot(mix, wout_ref[...])
    o_ref[0] = _xattn_body(x1, xg_ref, wq_ref, kt_ref, v_ref, wo_ref)


def _odd_out(x, hm, ha, w_out, xg, wq_all, kv, wo_all, layer, *, ts):
    B, S, D = x.shape
    ts = min(ts, S)
    tok = lambda n: pl.BlockSpec((1, ts, n), lambda b, s: (b, s, 0))
    return pl.pallas_call(
        _odd_out_kernel,
        grid=(B, S // ts),
        in_specs=[tok(D), tok(ML_WIDTH), pl.BlockSpec((1, MLA_WIDTH, ts), lambda b, s: (b, 0, s)),
                  _const_spec(w_out.shape)]
        + _xattn_specs(D, wq_all, kv, wo_all, layer),
        out_specs=tok(D),
        out_shape=jax.ShapeDtypeStruct(x.shape, F32),
        compiler_params=_params("arbitrary", "arbitrary"),
        name="odd_out",
    )(x, hm, ha, w_out, xg.reshape(1, D), wq_all, *kv, wo_all)


def _pack_odd_weights(w_in, w_uq, w_ukv):
    D = w_in.shape[0]
    c = 4 * ML_WIDTH
    w_g = w_in[:, c:c + 2 * ML_HEADS]
    c += 2 * ML_HEADS
    w_cq = w_in[:, c:c + Q_LORA]
    c += Q_LORA
    w_ckv = w_in[:, c:c + KV_LORA]
    c += KV_LORA
    w_kr = w_in[:, c:c + QK_ROPE]
    half = QK_ROPE // 2
    w_kr_sw = jnp.concatenate([w_kr[:, half:], w_kr[:, :half]], axis=1)
    zl = jnp.zeros((D, QK_NOPE), w_in.dtype)
    zr = jnp.zeros((D, LANES - QK_NOPE - QK_ROPE), w_in.dtype)
    w_main = jnp.concatenate([w_in[:, :4 * ML_WIDTH], w_cq, w_ckv,
                              zl, w_kr, zr, zl, w_kr_sw, zr], axis=1).astype(BF16)
    w_gt = w_g.T.astype(BF16)

    uq = w_uq.reshape(Q_LORA, MLA_HEADS, QK_NOPE + QK_ROPE)
    uq_n, uq_r = uq[..., :QK_NOPE], uq[..., QK_NOPE:]
    uq_r_sw = jnp.concatenate([uq_r[..., half:], uq_r[..., :half]], axis=-1)
    zpad = jnp.zeros((Q_LORA, MLA_HEADS, LANES - QK_NOPE - QK_ROPE), w_uq.dtype)
    w_uq_p = jnp.concatenate([uq_n, uq_r, zpad], axis=-1).reshape(Q_LORA, MLA_HEADS * LANES)
    w_uq_s = jnp.concatenate([jnp.zeros_like(uq_n), uq_r_sw, zpad], axis=-1)
    w_uq_s = w_uq_s.reshape(Q_LORA, MLA_HEADS * LANES)

    ukv = w_ukv.reshape(KV_LORA, MLA_HEADS, QK_NOPE + V_DIM)
    uk, uv = ukv[..., :QK_NOPE], ukv[..., QK_NOPE:]
    w_uk_p = jnp.concatenate([uk, jnp.zeros((KV_LORA, MLA_HEADS, LANES - QK_NOPE), w_ukv.dtype)],
                             axis=-1).reshape(KV_LORA, MLA_HEADS * LANES)
    w_uvt = uv.reshape(KV_LORA, MLA_WIDTH).T
    return (w_main, w_gt, w_uq_p.astype(BF16), w_uq_s.astype(BF16), w_uk_p.astype(BF16),
            w_uvt.astype(BF16))


def _rope_inv_freq():
    half = QK_ROPE // 2
    inv = ROPE_THETA ** (-jnp.arange(half, dtype=F32) / half)
    return inv.reshape(half, 1)


TS_EVEN = 1024
TM_FFN = 1024
TH_FFN = 256
TS_ODD = 1024
L_MLSTM = 256
TQ_MLA = 512
NT_MLA = 8
TS_OUT = 1024


def kernel(x, mem, positions, norm_mix_g, norm_xattn_g, mem_norm_g, xattn_wq, xattn_wkv, xattn_wo,
           norm_ffn_g, ffn_w_gate_up, ffn_w_down, ev_w_in, ev_conv_w, ev_pool_w, ev_pool_scale,
           ev_w_out, od_w_in, od_gate_bias, od_ml_norm_g, od_q_norm_g, od_kv_norm_g, od_w_uq,
           od_w_ukv, od_w_out, final_norm_g):
    depth = norm_mix_g.shape[0]
    B, S, D = x.shape
    pos3 = positions.reshape(B, 1, S)
    invf = _rope_inv_freq()
    wq_all, wkv_all, wo_all = (w.astype(BF16) for w in (xattn_wq, xattn_wkv, xattn_wo))
    w_gu_all, w_d_all = ffn_w_gate_up.astype(BF16), ffn_w_down.astype(BF16)
    for layer in range(depth):
        kv = _mem_kv(mem, mem_norm_g[layer], wkv_all, layer)
        xa = (norm_xattn_g[layer], wq_all, kv, wo_all, layer)
        if layer % 2 == 0:
            e = layer // 2
            x = _even_mixer(x, norm_mix_g[layer], ev_w_in[e].astype(BF16), ev_conv_w[e],
                            ev_pool_w[e].astype(BF16), ev_pool_scale[e], ev_w_out[e].astype(BF16),
                            *xa, ts=TS_EVEN)
        else:
            o = layer // 2
            w_main, w_gt, w_uq_p, w_uq_s, w_uk_p, w_uvt = _pack_odd_weights(
                od_w_in[o], od_w_uq[o], od_w_ukv[o])
            qm, km, vm, om, gates_t, qp, kp, vt = _odd_proj(
                x, pos3, norm_mix_g[layer], w_main, w_gt, od_gate_bias[o].reshape(2 * ML_HEADS, 1),
                od_q_norm_g[o].reshape(1, Q_LORA), od_kv_norm_g[o].reshape(1, KV_LORA),
                w_uq_p, w_uq_s, w_uk_p, w_uvt, invf, ts=TS_ODD)
            hm = _mlstm(qm, km, vm, om, gates_t, od_ml_norm_g[o], L=L_MLSTM)
            ha = _mla_attention(qp, kp, vt, tq=TQ_MLA, nt=NT_MLA)
            x = _odd_out(x, hm, ha, od_w_out[o].astype(BF16), *xa, ts=TS_OUT)
        x = _swiglu(x, norm_ffn_g[layer], w_gu_all, w_d_all, final_norm_g, layer,
                    tm=TM_FFN, th=TH_FFN, final_norm=(layer == depth - 1))
    return x
```

```python
import functools

import jax
import jax.numpy as jnp
from jax import lax
from jax.experimental import pallas as pl
from jax.experimental.pallas import tpu as pltpu

F32 = jnp.float32
BF16 = jnp.bfloat16

EPS = 1e-6
ROPE_THETA = 10000.0
LANES = 128
SUBLANES = 8
BF16_ROWS = 16
VMEM_LIMIT_BYTES = 56 * 1024 * 1024
NEG_BIG = -1e30
LOG2_E = 1.4426950408889634

SC_WIDTH = 512
CONV_K = 3
POOL_WINDOWS = (2, 4, 8, 16)
POOL_GROUP = 128
HALO = 32
ML_HEADS = 4
ML_HEAD_DIM = 128
ML_WIDTH = ML_HEADS * ML_HEAD_DIM
MLA_HEADS = 8
QK_NOPE = 64
QK_ROPE = 32
V_DIM = 64
Q_LORA = 384
KV_LORA = 256
MLA_WIDTH = MLA_HEADS * V_DIM
XA_HEADS = 4
XA_HEAD_DIM = 128
XA_WIDTH = XA_HEADS * XA_HEAD_DIM


def _params(*sem):
    return pltpu.CompilerParams(dimension_semantics=sem, vmem_limit_bytes=VMEM_LIMIT_BYTES)


def _const_spec(shape):
    nd = len(shape)
    return pl.BlockSpec(shape, lambda *_: (0,) * nd, pipeline_mode=pl.Buffered(1))


def _layer_spec(stacked, layer):
    return pl.BlockSpec((None,) + stacked.shape[1:], lambda *_: (layer, 0, 0),
                        pipeline_mode=pl.Buffered(1))


def _rms(x, g):
    ms = jnp.mean(x * x, axis=-1, keepdims=True)
    return x * lax.rsqrt(ms + EPS) * g


def _dot(a, b):
    return jnp.dot(a, b, preferred_element_type=F32)


def _dot_nt(a, b):
    return lax.dot_general(a, b, (((1,), (1,)), ((), ())), preferred_element_type=F32)


def _even_kernel(x_ref, g_ref, win_ref, convw_ref, poolw_ref, pscale_ref, wout_ref,
                 xg_ref, wq_ref, kt_ref, v_ref, wo_ref, o_ref, ubuf, xbbuf, tmp_a, tmp_b, *, ts):
    si = pl.program_id(1)

    @pl.when(si == 0)
    def _():
        ubuf[0:HALO, :] = jnp.zeros((HALO, SC_WIDTH), F32)
        xbbuf[0:HALO, :] = jnp.zeros((HALO, SC_WIDTH), F32)

    x = x_ref[0]
    h = _rms(x, g_ref[...]).astype(BF16)
    z = _dot(h, win_ref[...])
    g_b = z[:, 0:SC_WIDTH]
    u = z[:, SC_WIDTH:2 * SC_WIDTH] * z[:, 2 * SC_WIDTH:3 * SC_WIDTH]
    xb = z[:, 3 * SC_WIDTH:4 * SC_WIDTH]
    ubuf[HALO:HALO + ts, :] = u
    xbbuf[HALO:HALO + ts, :] = xb

    cw = convw_ref[...]
    conv = cw[2:3, :] * u
    for k in range(CONV_K - 1):
        back = CONV_K - 1 - k
        conv = conv + cw[k:k + 1, :] * ubuf[HALO - back:HALO - back + ts, :]
    ya = g_b * conv

    def window_level(src, c0, dst, r0, shift):
        w2 = (src[r0:HALO + ts, c0:c0 + POOL_GROUP]
              + src[r0 - shift:HALO + ts - shift, c0:c0 + POOL_GROUP])
        if dst is not None:
            dst[r0:HALO + ts, :] = w2
        return w2

    t_idx = si * ts + lax.broadcasted_iota(jnp.int32, (ts, POOL_GROUP), 0)
    ys = [ya.astype(BF16)]
    for gi, w in enumerate(POOL_WINDOWS):
        lo = gi * POOL_GROUP
        xg = xb[:, lo:lo + POOL_GROUP]
        src, c0, shift, level = xbbuf, lo, 1, 1
        while 2 * shift < w:
            dst = tmp_a if src is not tmp_a else tmp_b
            window_level(src, c0, dst, SUBLANES * level, shift)
            src, c0, shift, level = dst, 0, 2 * shift, level + 1
        acc = window_level(src, c0, None, HALO, shift)
        cnt = jnp.minimum(t_idx + 1, w).astype(F32)
        d = (acc / cnt - xg).astype(BF16)
        yg = _dot(d, poolw_ref[gi]) * pscale_ref[:, lo:lo + POOL_GROUP]
        ys.append(yg.astype(BF16))

    ubuf[0:HALO, :] = ubuf[ts:ts + HALO, :]
    xbbuf[0:HALO, :] = xbbuf[ts:ts + HALO, :]

    mix = jnp.concatenate(ys, axis=-1)
    x1 = x + _dot(mix, wout_ref[...])
    o_ref[0] = _xattn_body(x1, xg_ref, wq_ref, kt_ref, v_ref, wo_ref)


def _even_mixer(x, g, w_in, conv_w, pool_w, pool_scale, w_out, xg, wq_all, kv, wo_all, layer, *, ts):
    B, S, D = x.shape
    ts = min(ts, S)
    kern = functools.partial(_even_kernel, ts=ts)
    return pl.pallas_call(
        kern,
        grid=(B, S // ts),
        in_specs=[
            pl.BlockSpec((1, ts, D), lambda b, s: (b, s, 0)),
            _const_spec((1, D)),
            _const_spec(w_in.shape),
            _const_spec(conv_w.shape),
            _const_spec(pool_w.shape),
            _const_spec((1, SC_WIDTH)),
            _const_spec(w_out.shape),
        ] + _xattn_specs(D, wq_all, kv, wo_all, layer),
        out_specs=pl.BlockSpec((1, ts, D), lambda b, s: (b, s, 0)),
        out_shape=jax.ShapeDtypeStruct(x.shape, F32),
        scratch_shapes=[pltpu.VMEM((HALO + ts, SC_WIDTH), F32),
                        pltpu.VMEM((HALO + ts, SC_WIDTH), F32),
                        pltpu.VMEM((HALO + ts, POOL_GROUP), F32),
                        pltpu.VMEM((HALO + ts, POOL_GROUP), F32)],
        compiler_params=_params("arbitrary", "arbitrary"),
        name="even_mixer",
    )(x, g.reshape(1, D), w_in, conv_w, pool_w, pool_scale.reshape(1, SC_WIDTH), w_out,
      xg.reshape(1, D), wq_all, *kv, wo_all)


def _memkv_kernel(mem_ref, g_ref, wkv_ref, kt_ref, v_ref):
    h = _rms(mem_ref[0], g_ref[...]).astype(BF16)
    kv = _dot(h, wkv_ref[...])
    kt_ref[0] = kv[:, :XA_WIDTH].T.astype(BF16)
    v_ref[0] = kv[:, XA_WIDTH:].astype(BF16)


def _mem_kv(mem, g, wkv_all, layer):
    B, M, D = mem.shape
    return pl.pallas_call(
        _memkv_kernel,
        grid=(B,),
        in_specs=[pl.BlockSpec((1, M, D), lambda b: (b, 0, 0)),
                  _const_spec((1, D)), _layer_spec(wkv_all, layer)],
        out_specs=(pl.BlockSpec((1, XA_WIDTH, M), lambda b: (b, 0, 0)),
                   pl.BlockSpec((1, M, XA_WIDTH), lambda b: (b, 0, 0))),
        out_shape=(jax.ShapeDtypeStruct((B, XA_WIDTH, M), BF16),
                   jax.ShapeDtypeStruct((B, M, XA_WIDTH), BF16)),
        compiler_params=_params("arbitrary"),
        name="mem_kv",
    )(mem, g.reshape(1, D), wkv_all)


def _xattn_body(x, g_ref, wq_ref, kt_ref, v_ref, wo_ref):
    h = _rms(x, g_ref[...]).astype(BF16)
    q = (_dot(h, wq_ref[...]) * (XA_HEAD_DIM ** -0.5)).astype(BF16)
    ones_v = jnp.ones((v_ref.shape[1], XA_HEAD_DIM), BF16)
    outs = []
    for hd in range(XA_HEADS):
        lo = hd * XA_HEAD_DIM
        s = _dot(q[:, lo:lo + XA_HEAD_DIM], kt_ref[0, lo:lo + XA_HEAD_DIM, :])
        p = jnp.exp(s - jnp.max(s, axis=-1, keepdims=True)).astype(BF16)
        v_aug = jnp.concatenate([v_ref[0, :, lo:lo + XA_HEAD_DIM], ones_v], axis=1)
        o = _dot(p, v_aug)
        outs.append((o[:, :XA_HEAD_DIM] / o[:, XA_HEAD_DIM:]).astype(BF16))
    a = jnp.concatenate(outs, axis=-1)
    return x + _dot(a, wo_ref[...])


def _xattn_specs(D, wq_all, kv, wo_all, layer):
    kt, v = kv
    return [_const_spec((1, D)), _layer_spec(wq_all, layer),
            pl.BlockSpec((1,) + kt.shape[1:], lambda b, s: (b, 0, 0)),
            pl.BlockSpec((1,) + v.shape[1:], lambda b, s: (b, 0, 0)), _layer_spec(wo_all, layer)]


def _swiglu_kernel(x_ref, g_ref, wgu_ref, wd_ref, fg_ref, o_ref, *, hidden, th, final_norm):
    x = x_ref[...]
    h = _rms(x, g_ref[...]).astype(BF16)
    acc = x
    for j in range(hidden // th):
        gate = _dot(h, wgu_ref[:, j * th:(j + 1) * th])
        up = _dot(h, wgu_ref[:, hidden + j * th:hidden + (j + 1) * th])
        act = (gate * jax.nn.sigmoid(gate) * up).astype(BF16)
        acc = acc + _dot(act, wd_ref[j * th:(j + 1) * th, :])
    if final_norm:
        acc = _rms(acc, fg_ref[...])
    o_ref[...] = acc


def _swiglu(x, g, w_gu_all, w_d_all, final_g, layer, *, tm, th, final_norm):
    B, S, D = x.shape
    T = B * S
    tm = min(tm, T)
    hidden = w_d_all.shape[1]
    kern = functools.partial(_swiglu_kernel, hidden=hidden, th=th, final_norm=final_norm)
    out = pl.pallas_call(
        kern,
        grid=(T // tm,),
        in_specs=[
            pl.BlockSpec((tm, D), lambda i: (i, 0)),
            _const_spec((1, D)),
            _layer_spec(w_gu_all, layer),
            _layer_spec(w_d_all, layer),
            _const_spec((1, D)),
        ],
        out_specs=pl.BlockSpec((tm, D), lambda i: (i, 0)),
        out_shape=jax.ShapeDtypeStruct((T, D), F32),
        compiler_params=_params("arbitrary"),
        name="swiglu",
    )(x.reshape(T, D), g.reshape(1, D), w_gu_all, w_d_all, final_g.reshape(1, D))
    return out.reshape(B, S, D)


def _odd_proj_kernel(x_ref, pos_ref, g_ref, wmain_ref, wgt_ref, gbias_ref,
                     qg_ref, kvg_ref, wuq_ref, wuqs_ref, wuk_ref, wuvt_ref, invf_ref,
                     qm_ref, km_ref, vm_ref, om_ref, gates_ref, qp_ref, kp_ref, vt_ref):
    x = x_ref[0]
    h = _rms(x, g_ref[...]).astype(BF16)
    c0 = 4 * ML_WIDTH

    def ml_cols(i):
        return _dot(h, wmain_ref[:, i * ML_WIDTH:(i + 1) * ML_WIDTH])

    zl = _dot(h, wmain_ref[:, c0:])
    c_q = zl[:, 0:Q_LORA]
    c_kv = zl[:, Q_LORA:Q_LORA + KV_LORA]
    kr = zl[:, Q_LORA + KV_LORA:Q_LORA + KV_LORA + LANES]
    kr_sw = zl[:, Q_LORA + KV_LORA + LANES:Q_LORA + KV_LORA + 2 * LANES]
    gates_ref[0] = _dot_nt(wgt_ref[...], h) + gbias_ref[...]

    half = QK_ROPE // 2
    ang_t = invf_ref[...] * pos_ref[0].astype(F32)
    cos_c, sin_c = jnp.cos(ang_t), jnp.sin(ang_t)
    ts = ang_t.shape[1]
    pad_lo = jnp.zeros((QK_NOPE, ts), F32)
    pad_hi = jnp.zeros((LANES - QK_NOPE - QK_ROPE, ts), F32)
    cos_t = jnp.concatenate([pad_lo + 1.0, cos_c, cos_c, pad_hi], axis=0).T
    sin_t = jnp.concatenate([pad_lo, -sin_c, sin_c, pad_hi], axis=0).T
    scale = (QK_NOPE + QK_ROPE) ** -0.5 * LOG2_E
    cos_q, sin_q = cos_t * scale, sin_t * scale
    qm_ref[0] = ml_cols(0).astype(BF16)
    km_ref[0] = (ml_cols(1) * (ML_HEAD_DIM ** -0.5)).astype(BF16)

    cqn = _rms(c_q, qg_ref[...]).astype(BF16)
    qa = _dot(cqn, wuq_ref[...])
    qb = _dot(cqn, wuqs_ref[...])
    ckn = _rms(c_kv, kvg_ref[...]).astype(BF16)
    kn = _dot(ckn, wuk_ref[...])
    vt_ref[0] = _dot_nt(wuvt_ref[...], ckn).astype(BF16)
    k_rope = kr * cos_t + kr_sw * sin_t

    def rope_heads(first, last):
        for hd in range(first, last):
            lo = hd * LANES
            qp_ref[0, lo:lo + LANES, :] = (qa[:, lo:lo + LANES] * cos_q
                                           + qb[:, lo:lo + LANES] * sin_q).T.astype(BF16)
            kp_ref[0, :, lo:lo + LANES] = (kn[:, lo:lo + LANES] + k_rope).astype(BF16)

    vm_ref[0] = ml_cols(2).astype(BF16)
    rope_heads(0, MLA_HEADS // 2)
    om_ref[0] = ml_cols(3).astype(BF16)
    rope_heads(MLA_HEADS // 2, MLA_HEADS)


def _odd_proj(x, pos3, g, w_main, w_gt, gbias, q_g, kv_g, w_uq_p, w_uq_s, w_uk_p, w_uvt, invf, *, ts):
    B, S, D = x.shape
    ts = min(ts, S)
    tok = lambda n: pl.BlockSpec((1, ts, n), lambda b, s: (b, s, 0))
    out_shapes = (
        jax.ShapeDtypeStruct((B, S, ML_WIDTH), BF16),
        jax.ShapeDtypeStruct((B, S, ML_WIDTH), BF16),
        jax.ShapeDtypeStruct((B, S, ML_WIDTH), BF16),
        jax.ShapeDtypeStruct((B, S, ML_WIDTH), BF16),
        jax.ShapeDtypeStruct((B, 2 * ML_HEADS, S), F32),
        jax.ShapeDtypeStruct((B, MLA_HEADS * LANES, S), BF16),
        jax.ShapeDtypeStruct((B, S, MLA_HEADS * LANES), BF16),
        jax.ShapeDtypeStruct((B, MLA_WIDTH, S), BF16),
    )
    return pl.pallas_call(
        _odd_proj_kernel,
        grid=(B, S // ts),
        in_specs=[
            tok(D),
            pl.BlockSpec((1, 1, ts), lambda b, s: (b, 0, s)),
            _const_spec((1, D)),
            _const_spec(w_main.shape),
            _const_spec(w_gt.shape),
            _const_spec(gbias.shape),
            _const_spec(q_g.shape),
            _const_spec(kv_g.shape),
            _const_spec(w_uq_p.shape),
            _const_spec(w_uq_s.shape),
            _const_spec(w_uk_p.shape),
            _const_spec(w_uvt.shape),
            _const_spec(invf.shape),
        ],
        out_specs=(tok(ML_WIDTH), tok(ML_WIDTH), tok(ML_WIDTH), tok(ML_WIDTH),
                   pl.BlockSpec((1, 2 * ML_HEADS, ts), lambda b, s: (b, 0, s)),
                   pl.BlockSpec((1, MLA_HEADS * LANES, ts), lambda b, s: (b, 0, s)),
                   tok(MLA_HEADS * LANES),
                   pl.BlockSpec((1, MLA_WIDTH, ts), lambda b, s: (b, 0, s))),
        out_shape=out_shapes,
        compiler_params=_params("arbitrary", "arbitrary"),
        name="odd_proj",
    )(x, pos3, g.reshape(1, D), w_main, w_gt, gbias, q_g, kv_g, w_uq_p, w_uq_s, w_uk_p, w_uvt, invf)


def _split3(a):
    hi = a.astype(BF16)
    r1 = a - hi.astype(F32)
    mid = r1.astype(BF16)
    lo = (r1 - mid.astype(F32)).astype(BF16)
    return hi, mid, lo


def _log2_sigmoid(x):
    return (jnp.minimum(x, 0.0) - jnp.log(1.0 + jnp.exp(-jnp.abs(x)))) * LOG2_E


def _mlstm_kernel(q_ref, k_ref, v_ref, o_ref, gt_ref, ng_ref, h_ref, c_scr, m_scr, *, L):
    ci = pl.program_id(1)

    @pl.when(ci == 0)
    def _():
        c_scr[...] = jnp.zeros(c_scr.shape, F32)
        m_scr[...] = jnp.zeros(m_scr.shape, F32)

    row = lax.broadcasted_iota(jnp.int32, (L, L), 0)
    col = lax.broadcasted_iota(jnp.int32, (L, L), 1)
    causal = col <= row
    gt = gt_ref[0]
    ones_v = jnp.ones((L, LANES), BF16)
    reps = L // LANES

    def widen(a):
        return jnp.concatenate([a] * reps, axis=1)

    lf_t = _log2_sigmoid(gt)
    b_t = sum(_dot(t, (row <= col).astype(BF16)) for t in _split3(lf_t))
    b_c = jnp.concatenate([b_t, jnp.zeros((LANES - 2 * ML_HEADS, L), F32)], axis=0).T

    for hd in range(ML_HEADS):
        lo = hd * ML_HEAD_DIM
        fg = ML_HEADS + hd
        q = q_ref[0, :, lo:lo + ML_HEAD_DIM]
        k = k_ref[0, :, lo:lo + ML_HEAD_DIM]
        v_aug = jnp.concatenate([v_ref[0, :, lo:lo + ML_HEAD_DIM], ones_v], axis=1)
        g_row = gt[hd:hd + 1, :] * LOG2_E - b_t[fg:fg + 1, :]
        gtot = jnp.sum(lf_t[fg:fg + 1, :], axis=1, keepdims=True)
        b_col = jnp.broadcast_to(b_c[:, fg:fg + 1], (L, LANES))

        m_prev = m_scr[hd]
        c_prev = c_scr[hd]

        g_mask = jnp.where(causal, g_row, NEG_BIG)
        gmax = jnp.max(g_mask, axis=1, keepdims=True)
        h_col = jnp.maximum(m_prev, jnp.broadcast_to(gmax, (L, LANES)))
        w_inter = jnp.exp2(m_prev - h_col)
        sw = jnp.exp2(g_mask - widen(h_col)) * _dot_nt(q, k)
        out = (jnp.concatenate([w_inter, w_inter], axis=1) * _dot(q, c_prev.astype(BF16))
               + _dot(sw.astype(BF16), v_aug))
        den = jnp.maximum(jnp.abs(out[:, ML_HEAD_DIM:]), jnp.exp2(-(b_col + h_col)))
        hc = out[:, :ML_HEAD_DIM] / den

        og = jax.nn.sigmoid(o_ref[0, :, lo:lo + ML_HEAD_DIM].astype(F32))
        hg = og * hc
        ms = jnp.mean(hg * hg, axis=1, keepdims=True)
        h_ref[0, :, lo:lo + ML_HEAD_DIM] = (
            hg * lax.rsqrt(ms + EPS) * ng_ref[:, lo:lo + ML_HEAD_DIM]).astype(BF16)

        m_prev1 = m_prev[:, 0:1]
        a_row = gtot + g_row
        m_loc = jnp.max(a_row, axis=1, keepdims=True)
        m_new = jnp.maximum(gtot + m_prev1, m_loc)
        s_old = jnp.exp2(gtot + m_prev1 - m_new)
        w_loc = jnp.exp2(a_row - m_new)
        kw_t = (k.T.astype(F32) * w_loc).astype(BF16)
        c_scr[hd] = s_old * c_prev + _dot(kw_t, v_aug)
        m_scr[hd] = jnp.broadcast_to(m_new, (1, LANES))


def _mlstm(q, k, v, o, gates_t, norm_g, *, L):
    B, S, W = q.shape
    L = min(L, S)
    kern = functools.partial(_mlstm_kernel, L=L)
    tok = pl.BlockSpec((1, L, W), lambda b, c: (b, c, 0))
    return pl.pallas_call(
        kern,
        grid=(B, S // L),
        in_specs=[tok, tok, tok, tok,
                  pl.BlockSpec((1, 2 * ML_HEADS, L), lambda b, c: (b, 0, c)),
                  _const_spec((1, W))],
        out_specs=tok,
        out_shape=jax.ShapeDtypeStruct((B, S, W), BF16),
        scratch_shapes=[pltpu.VMEM((ML_HEADS, ML_HEAD_DIM, 2 * ML_HEAD_DIM), F32),
                        pltpu.VMEM((ML_HEADS, 1, LANES), F32)],
        compiler_params=_params("arbitrary", "arbitrary"),
        name="mlstm",
    )(q, k, v, o, gates_t, norm_g.reshape(1, W))


def _mla_kernel(qt_ref, k_ref, vt_ref, o_ref, s00, s01, s10, s11, *, tq, nt):
    g = pl.program_id(2)
    heads = 2
    pieces = 2
    rows = tq // pieces
    hk = tq // 2
    s_scr = ((s00, s01), (s10, s11))
    ones_rows = jnp.ones((BF16_ROWS, tq), BF16)

    def run_tile(t, qi, sa, sb):
        lanes = slice(t * tq, (t + 1) * tq)

        def score_piece(j, slot, hh, pc):
            start = pl.multiple_of(j * tq + pc * rows, rows)
            kb = k_ref[0, pl.ds(start, rows), hh * LANES:(hh + 1) * LANES]
            qt = qt_ref[0, hh * LANES:(hh + 1) * LANES, lanes]
            s_scr[slot][hh][pc * rows:(pc + 1) * rows, 0:tq] = _dot(kb, qt)

        def stage_exp(j, slot, hh, m_new, acc, half):
            start = pl.multiple_of(j * tq + half * hk, hk)
            vtb = jnp.concatenate([vt_ref[0, hh * V_DIM:(hh + 1) * V_DIM, pl.ds(start, hk)],
                                   ones_rows[:, :hk]], axis=0)
            p = jnp.exp2(s_scr[slot][hh][half * hk:(half + 1) * hk, 0:tq] - m_new).astype(BF16)
            return acc + _dot(vtb, p)

        def block(j, slot, carry, nxt):
            new = []
            for hh in range(heads):
                m, acc = carry[hh]
                score_piece(nxt[0], nxt[1], hh, 0)
                m_new = jnp.maximum(m, jnp.max(s_scr[slot][hh][:, 0:tq], axis=0, keepdims=True))
                score_piece(nxt[0], nxt[1], hh, 1)
                acc = stage_exp(j, slot, hh, m_new, jnp.exp2(m - m_new) * acc, 0)
                acc = stage_exp(j, slot, hh, m_new, acc, 1)
                new.append((m_new, acc))
            return tuple(new)

        def diag_block(slot, carry):
            start = pl.multiple_of(qi * tq, tq)
            kk = lax.broadcasted_iota(jnp.int32, (hk, tq), 0)
            qq = lax.broadcasted_iota(jnp.int32, (hk, tq), 1)
            kb = lax.broadcasted_iota(jnp.int32, (hk, hk), 0)
            qb = lax.broadcasted_iota(jnp.int32, (hk, hk), 1)
            hs = range(heads)
            top = [jnp.where(kk <= qq, s_scr[slot][hh][0:hk, 0:tq], NEG_BIG) for hh in hs]
            bot = [jnp.where(kb <= qb, s_scr[slot][hh][hk:, hk:tq], NEG_BIG) for hh in hs]
            neg = jnp.full((hk, hk), NEG_BIG, F32)
            m_new = [jnp.maximum(carry[hh][0], jnp.max(
                jnp.maximum(top[hh], jnp.concatenate([neg, bot[hh]], axis=1)), axis=0, keepdims=True))
                for hh in hs]
            vt_aug = [jnp.concatenate([vt_ref[0, hh * V_DIM:(hh + 1) * V_DIM, pl.ds(start, tq)],
                                       ones_rows], axis=0) for hh in hs]
            p_top = [jnp.exp2(top[hh] - m_new[hh]).astype(BF16) for hh in hs]
            p_bot = [jnp.exp2(bot[hh] - jnp.broadcast_to(m_new[hh], (hk, tq))[:, hk:]).astype(BF16)
                     for hh in hs]
            outs = []
            for hh in hs:
                m, acc = carry[hh]
                acc = jnp.exp2(m - m_new[hh]) * acc + _dot(vt_aug[hh][:, :hk], p_top[hh])
                hi = acc[:, hk:] + _dot(vt_aug[hh][:, hk:], p_bot[hh])
                acc = jnp.concatenate([acc[:, :hk], hi], axis=1)
                outs.append(acc[:V_DIM] / acc[V_DIM:V_DIM + 1])
            return jnp.concatenate(outs, axis=0)

        for hh in range(heads):
            for pc in range(pieces):
                score_piece(0, sa, hh, pc)

        def body(i, c):
            c = block(2 * i, sa, c, (2 * i + 1, sb))
            return block(2 * i + 1, sb, c, (2 * i + 2, sa))

        init = tuple((jnp.full((1, tq), NEG_BIG, F32), jnp.zeros((V_DIM + BF16_ROWS, tq), F32))
                     for _ in range(heads))
        carry = lax.fori_loop(0, qi // 2, body, init)
        if t % 2 == 0:
            ot = diag_block(sa, carry)
        else:
            ot = diag_block(sb, block(qi - 1, sa, carry, (qi, sb)))
        o_ref[0, :, lanes] = ot.astype(BF16)

    slot_pairs = ((0, 1), (1, 0), (1, 0), (0, 1))
    for t in range(nt):
        run_tile(t, nt * g + t, *slot_pairs[t % 4])


def _mla_attention(qpt, kp, vt, *, tq, nt):
    B, S, _ = kp.shape
    tq = min(tq, S // 2)
    nt = min(nt, S // tq)
    assert nt % 2 == 0 and S % (nt * tq) == 0
    pairs = MLA_HEADS // 2
    kern = functools.partial(_mla_kernel, tq=tq, nt=nt)
    return pl.pallas_call(
        kern,
        grid=(B, pairs, S // (nt * tq)),
        in_specs=[
            pl.BlockSpec((1, 2 * LANES, nt * tq), lambda b, p, i: (b, p, i)),
            pl.BlockSpec((1, S, 2 * LANES), lambda b, p, i: (b, 0, p)),
            pl.BlockSpec((1, 2 * V_DIM, S), lambda b, p, i: (b, p, 0)),
        ],
        out_specs=pl.BlockSpec((1, LANES, nt * tq), lambda b, p, i: (b, p, i)),
        out_shape=jax.ShapeDtypeStruct((B, MLA_WIDTH, S), BF16),
        scratch_shapes=[pltpu.VMEM((tq, tq + LANES), F32) for _ in range(4)],
        compiler_params=_params("arbitrary", "arbitrary", "arbitrary"),
        name="mla_attention",
    )(qpt, kp, vt)


def _odd_out_kernel(x_ref, hm_ref, ha_ref, wout_ref, xg_ref, wq_ref, kt_ref, v_ref, wo_ref, o_ref):
    mix = jnp.concatenate([hm_ref[0], ha_ref[0].T], axis=-1)
    x1 = x_ref[0] + _dot(mix, wout_ref[...])
    o_ref[0] = _xattn_body(x1, xg_ref, wq_ref, kt_ref, v_ref, wo_ref)


def _odd_out(x, hm, ha, w_out, xg, wq_all, kv, wo_all, layer, *, ts):
    B, S, D = x.shape
    ts = min(ts, S)
    tok = lambda n: pl.BlockSpec((1, ts, n), lambda b, s: (b, s, 0))
    return pl.pallas_call(
        _odd_out_kernel,
        grid=(B, S // ts),
        in_specs=[tok(D), tok(ML_WIDTH), pl.BlockSpec((1, MLA_WIDTH, ts), lambda b, s: (b, 0, s)),
                  _const_spec(w_out.shape)]
        + _xattn_specs(D, wq_all, kv, wo_all, layer),
        out_specs=tok(D),
        out_shape=jax.ShapeDtypeStruct(x.shape, F32),
        compiler_params=_params("arbitrary", "arbitrary"),
        name="odd_out",
    )(x, hm, ha, w_out, xg.reshape(1, D), wq_all, *kv, wo_all)


def _pack_odd_weights(w_in, w_uq, w_ukv):
    D = w_in.shape[0]
    c = 4 * ML_WIDTH
    w_g = w_in[:, c:c + 2 * ML_HEADS]
    c += 2 * ML_HEADS
    w_cq = w_in[:, c:c + Q_LORA]
    c += Q_LORA
    w_ckv = w_in[:, c:c + KV_LORA]
    c += KV_LORA
    w_kr = w_in[:, c:c + QK_ROPE]
    half = QK_ROPE // 2
    w_kr_sw = jnp.concatenate([w_kr[:, half:], w_kr[:, :half]], axis=1)
    zl = jnp.zeros((D, QK_NOPE), w_in.dtype)
    zr = jnp.zeros((D, LANES - QK_NOPE - QK_ROPE), w_in.dtype)
    w_main = jnp.concatenate([w_in[:, :4 * ML_WIDTH], w_cq, w_ckv,
                              zl, w_kr, zr, zl, w_kr_sw, zr], axis=1).astype(BF16)
    w_gt = w_g.T.astype(BF16)

    uq = w_uq.reshape(Q_LORA, MLA_HEADS, QK_NOPE + QK_ROPE)
    uq_n, uq_r = uq[..., :QK_NOPE], uq[..., QK_NOPE:]
    uq_r_sw = jnp.concatenate([uq_r[..., half:], uq_r[..., :half]], axis=-1)
    zpad = jnp.zeros((Q_LORA, MLA_HEADS, LANES - QK_NOPE - QK_ROPE), w_uq.dtype)
    w_uq_p = jnp.concatenate([uq_n, uq_r, zpad], axis=-1).reshape(Q_LORA, MLA_HEADS * LANES)
    w_uq_s = jnp.concatenate([jnp.zeros_like(uq_n), uq_r_sw, zpad], axis=-1)
    w_uq_s = w_uq_s.reshape(Q_LORA, MLA_HEADS * LANES)

    ukv = w_ukv.reshape(KV_LORA, MLA_HEADS, QK_NOPE + V_DIM)
    uk, uv = ukv[..., :QK_NOPE], ukv[..., QK_NOPE:]
    w_uk_p = jnp.concatenate([uk, jnp.zeros((KV_LORA, MLA_HEADS, LANES - QK_NOPE), w_ukv.dtype)],
                             axis=-1).reshape(KV_LORA, MLA_HEADS * LANES)
    w_uvt = uv.reshape(KV_LORA, MLA_WIDTH).T
    return (w_main, w_gt, w_uq_p.astype(BF16), w_uq_s.astype(BF16), w_uk_p.astype(BF16),
            w_uvt.astype(BF16))


def _rope_inv_freq():
    half = QK_ROPE // 2
    inv = ROPE_THETA ** (-jnp.arange(half, dtype=F32) / half)
    return inv.reshape(half, 1)


TS_EVEN = 1024
TM_FFN = 1024
TH_FFN = 256
TS_ODD = 1024
L_MLSTM = 256
TQ_MLA = 512
NT_MLA = 8
TS_OUT = 1024


def kernel(x, mem, positions, norm_mix_g, norm_xattn_g, mem_norm_g, xattn_wq, xattn_wkv, xattn_wo,
           norm_ffn_g, ffn_w_gate_up, ffn_w_down, ev_w_in, ev_conv_w, ev_pool_w, ev_pool_scale,
           ev_w_out, od_w_in, od_gate_bias, od_ml_norm_g, od_q_norm_g, od_kv_norm_g, od_w_uq,
           od_w_ukv, od_w_out, final_norm_g):
    depth = norm_mix_g.shape[0]
    B, S, D = x.shape
    pos3 = positions.reshape(B, 1, S)
    invf = _rope_inv_freq()
    wq_all, wkv_all, wo_all = (w.astype(BF16) for w in (xattn_wq, xattn_wkv, xattn_wo))
    w_gu_all, w_d_all = ffn_w_gate_up.astype(BF16), ffn_w_down.astype(BF16)
    for layer in range(depth):
        kv = _mem_kv(mem, mem_norm_g[layer], wkv_all, layer)
        xa = (norm_xattn_g[layer], wq_all, kv, wo_all, layer)
        if layer % 2 == 0:
            e = layer // 2
            x = _even_mixer(x, norm_mix_g[layer], ev_w_in[e].astype(BF16), ev_conv_w[e],
                            ev_pool_w[e].astype(BF16), ev_pool_scale[e], ev_w_out[e].astype(BF16),
                            *xa, ts=TS_EVEN)
        else:
            o = layer // 2
            w_main, w_gt, w_uq_p, w_uq_s, w_uk_p, w_uvt = _pack_odd_weights(
                od_w_in[o], od_w_uq[o], od_w_ukv[o])
            qm, km, vm, om, gates_t, qp, kp, vt = _odd_proj(
                x, pos3, norm_mix_g[layer], w_main, w_gt, od_gate_bias[o].reshape(2 * ML_HEADS, 1),
                od_q_norm_g[o].reshape(1, Q_LORA), od_kv_norm_g[o].reshape(1, KV_LORA),
                w_uq_p, w_uq_s, w_uk_p, w_uvt, invf, ts=TS_ODD)
            hm = _mlstm(qm, km, vm, om, gates_t, od_ml_norm_g[o], L=L_MLSTM)
            ha = _mla_attention(qp, kp, vt, tq=TQ_MLA, nt=NT_MLA)
            x = _odd_out(x, hm, ha, od_w_out[o].astype(BF16), *xa, ts=TS_OUT)
        x = _swiglu(x, norm_ffn_g[layer], w_gu_all, w_d_all, final_norm_g, layer,
                    tm=TM_FFN, th=TH_FFN, final_norm=(layer == depth - 1))
    return x
```

```python
import functools

import jax
import jax.numpy as jnp
from jax import lax
from jax.experimental import pallas as pl
from jax.experimental.pallas import tpu as pltpu

F32 = jnp.float32
BF16 = jnp.bfloat16

EPS = 1e-6
ROPE_THETA = 10000.0
LANES = 128
SUBLANES = 8
BF16_ROWS = 16
VMEM_LIMIT_BYTES = 56 * 1024 * 1024
NEG_BIG = -1e30
LOG2_E = 1.4426950408889634

SC_WIDTH = 512
CONV_K = 3
POOL_WINDOWS = (2, 4, 8, 16)
POOL_GROUP = 128
HALO = 32
ML_HEADS = 4
ML_HEAD_DIM = 128
ML_WIDTH = ML_HEADS * ML_HEAD_DIM
MLA_HEADS = 8
QK_NOPE = 64
QK_ROPE = 32
V_DIM = 64
Q_LORA = 384
KV_LORA = 256
MLA_WIDTH = MLA_HEADS * V_DIM
XA_HEADS = 4
XA_HEAD_DIM = 128
XA_WIDTH = XA_HEADS * XA_HEAD_DIM


def _params(*sem):
    return pltpu.CompilerParams(dimension_semantics=sem, vmem_limit_bytes=VMEM_LIMIT_BYTES)


def _const_spec(shape):
    nd = len(shape)
    return pl.BlockSpec(shape, lambda *_: (0,) * nd, pipeline_mode=pl.Buffered(1))


def _layer_spec(stacked, layer):
    return pl.BlockSpec((None,) + stacked.shape[1:], lambda *_: (layer, 0, 0),
                        pipeline_mode=pl.Buffered(1))


def _rms(x, g):
    ms = jnp.mean(x * x, axis=-1, keepdims=True)
    return x * lax.rsqrt(ms + EPS) * g


def _dot(a, b):
    return jnp.dot(a, b, preferred_element_type=F32)


def _dot_nt(a, b):
    return lax.dot_general(a, b, (((1,), (1,)), ((), ())), preferred_element_type=F32)


def _even_kernel(x_ref, g_ref, win_ref, convw_ref, poolw_ref, pscale_ref, wout_ref,
                 xg_ref, wq_ref, kt_ref, v_ref, wo_ref, o_ref, ubuf, xbbuf, tmp_a, tmp_b, *, ts):
    si = pl.program_id(1)

    @pl.when(si == 0)
    def _():
        ubuf[0:HALO, :] = jnp.zeros((HALO, SC_WIDTH), F32)
        xbbuf[0:HALO, :] = jnp.zeros((HALO, SC_WIDTH), F32)

    x = x_ref[0]
    h = _rms(x, g_ref[...]).astype(BF16)
    z = _dot(h, win_ref[...])
    g_b = z[:, 0:SC_WIDTH]
    u = z[:, SC_WIDTH:2 * SC_WIDTH] * z[:, 2 * SC_WIDTH:3 * SC_WIDTH]
    xb = z[:, 3 * SC_WIDTH:4 * SC_WIDTH]
    ubuf[HALO:HALO + ts, :] = u
    xbbuf[HALO:HALO + ts, :] = xb

    cw = convw_ref[...]
    conv = cw[2:3, :] * u
    for k in range(CONV_K - 1):
        back = CONV_K - 1 - k
        conv = conv + cw[k:k + 1, :] * ubuf[HALO - back:HALO - back + ts, :]
    ya = g_b * conv

    def window_level(src, c0, dst, r0, shift):
        w2 = (src[r0:HALO + ts, c0:c0 + POOL_GROUP]
              + src[r0 - shift:HALO + ts - shift, c0:c0 + POOL_GROUP])
        if dst is not None:
            dst[r0:HALO + ts, :] = w2
        return w2

    t_idx = si * ts + lax.broadcasted_iota(jnp.int32, (ts, POOL_GROUP), 0)
    ys = [ya.astype(BF16)]
    for gi, w in enumerate(POOL_WINDOWS):
        lo = gi * POOL_GROUP
        xg = xb[:, lo:lo + POOL_GROUP]
        src, c0, shift, level = xbbuf, lo, 1, 1
        while 2 * shift < w:
            dst = tmp_a if src is not tmp_a else tmp_b
            window_level(src, c0, dst, SUBLANES * level, shift)
            src, c0, shift, level = dst, 0, 2 * shift, level + 1
        acc = window_level(src, c0, None, HALO, shift)
        cnt = jnp.minimum(t_idx + 1, w).astype(F32)
        d = (acc / cnt - xg).astype(BF16)
        yg = _dot(d, poolw_ref[gi]) * pscale_ref[:, lo:lo + POOL_GROUP]
        ys.append(yg.astype(BF16))

    ubuf[0:HALO, :] = ubuf[ts:ts + HALO, :]
    xbbuf[0:HALO, :] = xbbuf[ts:ts + HALO, :]

    mix = jnp.concatenate(ys, axis=-1)
    x1 = x + _dot(mix, wout_ref[...])
    o_ref[0] = _xattn_body(x1, xg_ref, wq_ref, kt_ref, v_ref, wo_ref)


def _even_mixer(x, g, w_in, conv_w, pool_w, pool_scale, w_out, xg, wq_all, kv, wo_all, layer, *, ts):
    B, S, D = x.shape
    ts = min(ts, S)
    kern = functools.partial(_even_kernel, ts=ts)
    return pl.pallas_call(
        kern,
        grid=(B, S // ts),
        in_specs=[
            pl.BlockSpec((1, ts, D), lambda b, s: (b, s, 0)),
            _const_spec((1, D)),
            _const_spec(w_in.shape),
            _const_spec(conv_w.shape),
            _const_spec(pool_w.shape),
            _const_spec((1, SC_WIDTH)),
            _const_spec(w_out.shape),
        ] + _xattn_specs(D, wq_all, kv, wo_all, layer),
        out_specs=pl.BlockSpec((1, ts, D), lambda b, s: (b, s, 0)),
        out_shape=jax.ShapeDtypeStruct(x.shape, F32),
        scratch_shapes=[pltpu.VMEM((HALO + ts, SC_WIDTH), F32),
                        pltpu.VMEM((HALO + ts, SC_WIDTH), F32),
                        pltpu.VMEM((HALO + ts, POOL_GROUP), F32),
                        pltpu.VMEM((HALO + ts, POOL_GROUP), F32)],
        compiler_params=_params("arbitrary", "arbitrary"),
        name="even_mixer",
    )(x, g.reshape(1, D), w_in, conv_w, pool_w, pool_scale.reshape(1, SC_WIDTH), w_out,
      xg.reshape(1, D), wq_all, *kv, wo_all)


def _memkv_kernel(mem_ref, g_ref, wkv_ref, kt_ref, v_ref):
    h = _rms(mem_ref[0], g_ref[...]).astype(BF16)
    kv = _dot(h, wkv_ref[...])
    kt_ref[0] = kv[:, :XA_WIDTH].T.astype(BF16)
    v_ref[0] = kv[:, XA_WIDTH:].astype(BF16)


def _mem_kv(mem, g, wkv_all, layer):
    B, M, D = mem.shape
    return pl.pallas_call(
        _memkv_kernel,
        grid=(B,),
        in_specs=[pl.BlockSpec((1, M, D), lambda b: (b, 0, 0)),
                  _const_spec((1, D)), _layer_spec(wkv_all, layer)],
        out_specs=(pl.BlockSpec((1, XA_WIDTH, M), lambda b: (b, 0, 0)),
                   pl.BlockSpec((1, M, XA_WIDTH), lambda b: (b, 0, 0))),
        out_shape=(jax.ShapeDtypeStruct((B, XA_WIDTH, M), BF16),
                   jax.ShapeDtypeStruct((B, M, XA_WIDTH), BF16)),
        compiler_params=_params("arbitrary"),
        name="mem_kv",
    )(mem, g.reshape(1, D), wkv_all)


def _xattn_body(x, g_ref, wq_ref, kt_ref, v_ref, wo_ref):
    h = _rms(x, g_ref[...]).astype(BF16)
    q = (_dot(h, wq_ref[...]) * (XA_HEAD_DIM ** -0.5)).astype(BF16)
    ones_v = jnp.ones((v_ref.shape[1], XA_HEAD_DIM), BF16)
    outs = []
    for hd in range(XA_HEADS):
        lo = hd * XA_HEAD_DIM
        s = _dot(q[:, lo:lo + XA_HEAD_DIM], kt_ref[0, lo:lo + XA_HEAD_DIM, :])
        p = jnp.exp(s - jnp.max(s, axis=-1, keepdims=True)).astype(BF16)
        v_aug = jnp.concatenate([v_ref[0, :, lo:lo + XA_HEAD_DIM], ones_v], axis=1)
        o = _dot(p, v_aug)
        outs.append((o[:, :XA_HEAD_DIM] / o[:, XA_HEAD_DIM:]).astype(BF16))
    a = jnp.concatenate(outs, axis=-1)
    return x + _dot(a, wo_ref[...])


def _xattn_specs(D, wq_all, kv, wo_all, layer):
    kt, v = kv
    return [_const_spec((1, D)), _layer_spec(wq_all, layer),
            pl.BlockSpec((1,) + kt.shape[1:], lambda b, s: (b, 0, 0)),
            pl.BlockSpec((1,) + v.shape[1:], lambda b, s: (b, 0, 0)), _layer_spec(wo_all, layer)]


def _swiglu_kernel(x_ref, g_ref, wgu_ref, wd_ref, fg_ref, o_ref, *, hidden, th, final_norm):
    x = x_ref[...]
    h = _rms(x, g_ref[...]).astype(BF16)
    acc = x
    for j in range(hidden // th):
        gate = _dot(h, wgu_ref[:, j * th:(j + 1) * th])
        up = _dot(h, wgu_ref[:, hidden + j * th:hidden + (j + 1) * th])
        act = (gate * jax.nn.sigmoid(gate) * up).astype(BF16)
        acc = acc + _dot(act, wd_ref[j * th:(j + 1) * th, :])
    if final_norm:
        acc = _rms(acc, fg_ref[...])
    o_ref[...] = acc


def _swiglu(x, g, w_gu_all, w_d_all, final_g, layer, *, tm, th, final_norm):
    B, S, D = x.shape
    T = B * S
    tm = min(tm, T)
    hidden = w_d_all.shape[1]
    kern = functools.partial(_swiglu_kernel, hidden=hidden, th=th, final_norm=final_norm)
    out = pl.pallas_call(
        kern,
        grid=(T // tm,),
        in_specs=[
            pl.BlockSpec((tm, D), lambda i: (i, 0)),
            _const_spec((1, D)),
            _layer_spec(w_gu_all, layer),
            _layer_spec(w_d_all, layer),
            _const_spec((1, D)),
        ],
        out_specs=pl.BlockSpec((tm, D), lambda i: (i, 0)),
        out_shape=jax.ShapeDtypeStruct((T, D), F32),
        compiler_params=_params("arbitrary"),
        name="swiglu",
    )(x.reshape(T, D), g.reshape(1, D), w_gu_all, w_d_all, final_g.reshape(1, D))
    return out.reshape(B, S, D)


def _odd_proj_kernel(x_ref, pos_ref, g_ref, wmain_ref, wgt_ref, gbias_ref,
                     qg_ref, kvg_ref, wuq_ref, wuqs_ref, wuk_ref, wuvt_ref, invf_ref,
                     qm_ref, km_ref, vm_ref, om_ref, gates_ref, qp_ref, kp_ref, vt_ref):
    x = x_ref[0]
    h = _rms(x, g_ref[...]).astype(BF16)
    c0 = 4 * ML_WIDTH

    def ml_cols(i):
        return _dot(h, wmain_ref[:, i * ML_WIDTH:(i + 1) * ML_WIDTH])

    zl = _dot(h, wmain_ref[:, c0:])
    c_q = zl[:, 0:Q_LORA]
    c_kv = zl[:, Q_LORA:Q_LORA + KV_LORA]
    kr = zl[:, Q_LORA + KV_LORA:Q_LORA + KV_LORA + LANES]
    kr_sw = zl[:, Q_LORA + KV_LORA + LANES:Q_LORA + KV_LORA + 2 * LANES]
    gates_ref[0] = _dot_nt(wgt_ref[...], h) + gbias_ref[...]

    half = QK_ROPE // 2
    ang_t = invf_ref[...] * pos_ref[0].astype(F32)
    cos_c, sin_c = jnp.cos(ang_t), jnp.sin(ang_t)
    ts = ang_t.shape[1]
    pad_lo = jnp.zeros((QK_NOPE, ts), F32)
    pad_hi = jnp.zeros((LANES - QK_NOPE - QK_ROPE, ts), F32)
    cos_t = jnp.concatenate([pad_lo + 1.0, cos_c, cos_c, pad_hi], axis=0).T
    sin_t = jnp.concatenate([pad_lo, -sin_c, sin_c, pad_hi], axis=0).T
    scale = (QK_NOPE + QK_ROPE) ** -0.5 * LOG2_E
    cos_q, sin_q = cos_t * scale, sin_t * scale
    qm_ref[0] = ml_cols(0).astype(BF16)
    km_ref[0] = (ml_cols(1) * (ML_HEAD_DIM ** -0.5)).astype(BF16)

    cqn = _rms(c_q, qg_ref[...]).astype(BF16)
    qa = _dot(cqn, wuq_ref[...])
    qb = _dot(cqn, wuqs_ref[...])
    ckn = _rms(c_kv, kvg_ref[...]).astype(BF16)
    kn = _dot(ckn, wuk_ref[...])
    vt_ref[0] = _dot_nt(wuvt_ref[...], ckn).astype(BF16)
    k_rope = kr * cos_t + kr_sw * sin_t

    def rope_heads(first, last):
        for hd in range(first, last):
            lo = hd * LANES
            qp_ref[0, lo:lo + LANES, :] = (qa[:, lo:lo + LANES] * cos_q
                                           + qb[:, lo:lo + LANES] * sin_q).T.astype(BF16)
            kp_ref[0, :, lo:lo + LANES] = (kn[:, lo:lo + LANES] + k_rope).astype(BF16)

    vm_ref[0] = ml_cols(2).astype(BF16)
    rope_heads(0, MLA_HEADS // 2)
    om_ref[0] = ml_cols(3).astype(BF16)
    rope_heads(MLA_HEADS // 2, MLA_HEADS)


def _odd_proj(x, pos3, g, w_main, w_gt, gbias, q_g, kv_g, w_uq_p, w_uq_s, w_uk_p, w_uvt, invf, *, ts):
    B, S, D = x.shape
    ts = min(ts, S)
    tok = lambda n: pl.BlockSpec((1, ts, n), lambda b, s: (b, s, 0))
    out_shapes = (
        jax.ShapeDtypeStruct((B, S, ML_WIDTH), BF16),
        jax.ShapeDtypeStruct((B, S, ML_WIDTH), BF16),
        jax.ShapeDtypeStruct((B, S, ML_WIDTH), BF16),
        jax.ShapeDtypeStruct((B, S, ML_WIDTH), BF16),
        jax.ShapeDtypeStruct((B, 2 * ML_HEADS, S), F32),
        jax.ShapeDtypeStruct((B, MLA_HEADS * LANES, S), BF16),
        jax.ShapeDtypeStruct((B, S, MLA_HEADS * LANES), BF16),
        jax.ShapeDtypeStruct((B, MLA_WIDTH, S), BF16),
    )
    return pl.pallas_call(
        _odd_proj_kernel,
        grid=(B, S // ts),
        in_specs=[
            tok(D),
            pl.BlockSpec((1, 1, ts), lambda b, s: (b, 0, s)),
            _const_spec((1, D)),
            _const_spec(w_main.shape),
            _const_spec(w_gt.shape),
            _const_spec(gbias.shape),
            _const_spec(q_g.shape),
            _const_spec(kv_g.shape),
            _const_spec(w_uq_p.shape),
            _const_spec(w_uq_s.shape),
            _const_spec(w_uk_p.shape),
            _const_spec(w_uvt.shape),
            _const_spec(invf.shape),
        ],
        out_specs=(tok(ML_WIDTH), tok(ML_WIDTH), tok(ML_WIDTH), tok(ML_WIDTH),
                   pl.BlockSpec((1, 2 * ML_HEADS, ts), lambda b, s: (b, 0, s)),
                   pl.BlockSpec((1, MLA_HEADS * LANES, ts), lambda b, s: (b, 0, s)),
                   tok(MLA_HEADS * LANES),
                   pl.BlockSpec((1, MLA_WIDTH, ts), lambda b, s: (b, 0, s))),
        out_shape=out_shapes,
        compiler_params=_params("arbitrary", "arbitrary"),
        name="odd_proj",
    )(x, pos3, g.reshape(1, D), w_main, w_gt, gbias, q_g, kv_g, w_uq_p, w_uq_s, w_uk_p, w_uvt, invf)


def _split3(a):
    hi = a.astype(BF16)
    r1 = a - hi.astype(F32)
    mid = r1.astype(BF16)
    lo = (r1 - mid.astype(F32)).astype(BF16)
    return hi, mid, lo


def _log2_sigmoid(x):
    return (jnp.minimum(x, 0.0) - jnp.log(1.0 + jnp.exp(-jnp.abs(x)))) * LOG2_E


def _mlstm_kernel(q_ref, k_ref, v_ref, o_ref, gt_ref, ng_ref, h_ref, c_scr, m_scr, *, L, nc):
    ci = pl.program_id(1)

    @pl.when(ci == 0)
    def _():
        c_scr[...] = jnp.zeros(c_scr.shape, F32)
        m_scr[...] = jnp.zeros(m_scr.shape, F32)

    row = lax.broadcasted_iota(jnp.int32, (L, L), 0)
    col = lax.broadcasted_iota(jnp.int32, (L, L), 1)
    causal = col <= row
    ones_v = jnp.ones((L, LANES), BF16)
    reps = L // LANES

    def widen(a):
        return jnp.concatenate([a] * reps, axis=1)

    for ch, hd in [(ch, hd) for ch in range(nc) for hd in range(ML_HEADS)]:
        tok = slice(ch * L, (ch + 1) * L)
        if hd == 0:
            gt = gt_ref[0, :, tok]
            lf_t = _log2_sigmoid(gt)
            b_t = sum(_dot(t, (row <= col).astype(BF16)) for t in _split3(lf_t))
            b_c = jnp.concatenate([b_t, jnp.zeros((LANES - 2 * ML_HEADS, L), F32)], axis=0).T
        lo = hd * ML_HEAD_DIM
        fg = ML_HEADS + hd
        q = q_ref[0, tok, lo:lo + ML_HEAD_DIM]
        k = k_ref[0, tok, lo:lo + ML_HEAD_DIM]
        v_aug = jnp.concatenate([v_ref[0, tok, lo:lo + ML_HEAD_DIM], ones_v], axis=1)
        g_row = gt[hd:hd + 1, :] * LOG2_E - b_t[fg:fg + 1, :]
        gtot = jnp.sum(lf_t[fg:fg + 1, :], axis=1, keepdims=True)
        b_col = jnp.broadcast_to(b_c[:, fg:fg + 1], (L, LANES))

        m_prev = m_scr[hd]
        c_prev = c_scr[hd]

        g_mask = jnp.where(causal, g_row, NEG_BIG)
        gmax = jnp.max(g_mask, axis=1, keepdims=True)
        h_col = jnp.maximum(m_prev, jnp.broadcast_to(gmax, (L, LANES)))
        w_inter = jnp.exp2(m_prev - h_col)
        sw = jnp.exp2(g_mask - widen(h_col)) * _dot_nt(q, k)
        out = (jnp.concatenate([w_inter, w_inter], axis=1) * _dot(q, c_prev.astype(BF16))
               + _dot(sw.astype(BF16), v_aug))
        den = jnp.maximum(jnp.abs(out[:, ML_HEAD_DIM:]), jnp.exp2(-(b_col + h_col)))
        hc = out[:, :ML_HEAD_DIM] / den

        og = jax.nn.sigmoid(o_ref[0, tok, lo:lo + ML_HEAD_DIM].astype(F32))
        hg = og * hc
        ms = jnp.mean(hg * hg, axis=1, keepdims=True)
        h_ref[0, tok, lo:lo + ML_HEAD_DIM] = (
            hg * lax.rsqrt(ms + EPS) * ng_ref[:, lo:lo + ML_HEAD_DIM]).astype(BF16)

        m_prev1 = m_prev[:, 0:1]
        a_row = gtot + g_row
        m_loc = jnp.max(a_row, axis=1, keepdims=True)
        m_new = jnp.maximum(gtot + m_prev1, m_loc)
        s_old = jnp.exp2(gtot + m_prev1 - m_new)
        w_loc = jnp.exp2(a_row - m_new)
        kw_t = (k.T.astype(F32) * w_loc).astype(BF16)
        c_scr[hd] = s_old * c_prev + _dot(kw_t, v_aug)
        m_scr[hd] = jnp.broadcast_to(m_new, (1, LANES))


def _mlstm(q, k, v, o, gates_t, norm_g, *, L, nc):
    B, S, W = q.shape
    L = min(L, S)
    nc = min(nc, S // L)
    kern = functools.partial(_mlstm_kernel, L=L, nc=nc)
    tok = pl.BlockSpec((1, nc * L, W), lambda b, c: (b, c, 0))
    return pl.pallas_call(
        kern,
        grid=(B, S // (nc * L)),
        in_specs=[tok, tok, tok, tok,
                  pl.BlockSpec((1, 2 * ML_HEADS, nc * L), lambda b, c: (b, 0, c)),
                  _const_spec((1, W))],
        out_specs=tok,
        out_shape=jax.ShapeDtypeStruct((B, S, W), BF16),
        scratch_shapes=[pltpu.VMEM((ML_HEADS, ML_HEAD_DIM, 2 * ML_HEAD_DIM), F32),
                        pltpu.VMEM((ML_HEADS, 1, LANES), F32)],
        compiler_params=_params("arbitrary", "arbitrary"),
        name="mlstm",
    )(q, k, v, o, gates_t, norm_g.reshape(1, W))


def _mla_kernel(qt_ref, k_ref, vt_ref, o_ref, s00, s01, s10, s11, *, tq, nt):
    g = pl.program_id(2)
    heads = 2
    pieces = 2
    rows = tq // pieces
    hk = tq // 2
    s_scr = ((s00, s01), (s10, s11))
    ones_rows = jnp.ones((BF16_ROWS, tq), BF16)

    def run_tile(t, qi, sa, sb):
        lanes = slice(t * tq, (t + 1) * tq)

        def score_piece(j, slot, hh, pc):
            start = pl.multiple_of(j * tq + pc * rows, rows)
            kb = k_ref[0, pl.ds(start, rows), hh * LANES:(hh + 1) * LANES]
            qt = qt_ref[0, hh * LANES:(hh + 1) * LANES, lanes]
            s_scr[slot][hh][pc * rows:(pc + 1) * rows, :] = _dot(kb, qt)

        def stage_exp(j, slot, hh, m_new, acc, half):
            start = pl.multiple_of(j * tq + half * hk, hk)
            vtb = jnp.concatenate([vt_ref[0, hh * V_DIM:(hh + 1) * V_DIM, pl.ds(start, hk)],
                                   ones_rows[:, :hk]], axis=0)
            p = jnp.exp2(s_scr[slot][hh][half * hk:(half + 1) * hk, :] - m_new).astype(BF16)
            return acc + _dot(vtb, p)

        def block(j, slot, carry, nxt):
            new = []
            for hh in range(heads):
                m, acc = carry[hh]
                score_piece(nxt[0], nxt[1], hh, 0)
                m_new = jnp.maximum(m, jnp.max(s_scr[slot][hh][...], axis=0, keepdims=True))
                score_piece(nxt[0], nxt[1], hh, 1)
                acc = stage_exp(j, slot, hh, m_new, jnp.exp2(m - m_new) * acc, 0)
                acc = stage_exp(j, slot, hh, m_new, acc, 1)
                new.append((m_new, acc))
            return tuple(new)

        def diag_block(slot, carry):
            start = pl.multiple_of(qi * tq, tq)
            kk = lax.broadcasted_iota(jnp.int32, (hk, tq), 0)
            qq = lax.broadcasted_iota(jnp.int32, (hk, tq), 1)
            kb = lax.broadcasted_iota(jnp.int32, (hk, hk), 0)
            qb = lax.broadcasted_iota(jnp.int32, (hk, hk), 1)
            hs = range(heads)
            top = [jnp.where(kk <= qq, s_scr[slot][hh][0:hk, :], NEG_BIG) for hh in hs]
            bot = [jnp.where(kb <= qb, s_scr[slot][hh][hk:, hk:], NEG_BIG) for hh in hs]
            neg = jnp.full((hk, hk), NEG_BIG, F32)
            m_new = [jnp.maximum(carry[hh][0], jnp.max(
                jnp.maximum(top[hh], jnp.concatenate([neg, bot[hh]], axis=1)), axis=0, keepdims=True))
                for hh in hs]
            vt_aug = [jnp.concatenate([vt_ref[0, hh * V_DIM:(hh + 1) * V_DIM, pl.ds(start, tq)],
                                       ones_rows], axis=0) for hh in hs]
            p_top = [jnp.exp2(top[hh] - m_new[hh]).astype(BF16) for hh in hs]
            p_bot = [jnp.exp2(bot[hh] - jnp.broadcast_to(m_new[hh], (hk, tq))[:, hk:]).astype(BF16)
                     for hh in hs]
            outs = []
            for hh in hs:
                m, acc = carry[hh]
                acc = jnp.exp2(m - m_new[hh]) * acc + _dot(vt_aug[hh][:, :hk], p_top[hh])
                hi = acc[:, hk:] + _dot(vt_aug[hh][:, hk:], p_bot[hh])
                acc = jnp.concatenate([acc[:, :hk], hi], axis=1)
                outs.append(acc[:V_DIM] / acc[V_DIM:V_DIM + 1])
            return jnp.concatenate(outs, axis=0)

        for hh in range(heads):
            for pc in range(pieces):
                score_piece(0, sa, hh, pc)

        def body(i, c):
            c = block(2 * i, sa, c, (2 * i + 1, sb))
            return block(2 * i + 1, sb, c, (2 * i + 2, sa))

        init = tuple((jnp.full((1, tq), NEG_BIG, F32), jnp.zeros((V_DIM + BF16_ROWS, tq), F32))
                     for _ in range(heads))
        carry = lax.fori_loop(0, qi // 2, body, init)
        if t % 2 == 0:
            ot = diag_block(sa, carry)
        else:
            ot = diag_block(sb, block(qi - 1, sa, carry, (qi, sb)))
        o_ref[0, :, lanes] = ot.astype(BF16)

    slot_pairs = ((0, 1), (1, 0), (1, 0), (0, 1))
    for t in range(nt):
        run_tile(t, nt * g + t, *slot_pairs[t % 4])


def _mla_attention(qpt, kp, vt, *, tq, nt):
    B, S, _ = kp.shape
    tq = min(tq, S // 2)
    nt = min(nt, S // tq)
    assert nt % 2 == 0 and S % (nt * tq) == 0
    pairs = MLA_HEADS // 2
    kern = functools.partial(_mla_kernel, tq=tq, nt=nt)
    return pl.pallas_call(
        kern,
        grid=(B, pairs, S // (nt * tq)),
        in_specs=[
            pl.BlockSpec((1, 2 * LANES, nt * tq), lambda b, p, i: (b, p, i)),
            pl.BlockSpec((1, S, 2 * LANES), lambda b, p, i: (b, 0, p)),
            pl.BlockSpec((1, 2 * V_DIM, S), lambda b, p, i: (b, p, 0)),
        ],
        out_specs=pl.BlockSpec((1, LANES, nt * tq), lambda b, p, i: (b, p, i)),
        out_shape=jax.ShapeDtypeStruct((B, MLA_WIDTH, S), BF16),
        scratch_shapes=[pltpu.VMEM((tq, tq), F32) for _ in range(4)],
        compiler_params=_params("arbitrary", "arbitrary", "arbitrary"),
        name="mla_attention",
    )(qpt, kp, vt)


def _odd_out_kernel(x_ref, hm_ref, ha_ref, wout_ref, xg_ref, wq_ref, kt_ref, v_ref, wo_ref, o_ref):
    mix = jnp.concatenate([hm_ref[0], ha_ref[0].T], axis=-1)
    x1 = x_ref[0] + _dot(mix, wout_ref[...])
    o_ref[0] = _xattn_body(x1, xg_ref, wq_ref, kt_ref, v_ref, wo_ref)


def _odd_out(x, hm, ha, w_out, xg, wq_all, kv, wo_all, layer, *, ts):
    B, S, D = x.shape
    ts = min(ts, S)
    tok = lambda n: pl.BlockSpec((1, ts, n), lambda b, s: (b, s, 0))
    return pl.pallas_call(
        _odd_out_kernel,
        grid=(B, S // ts),
        in_specs=[tok(D), tok(ML_WIDTH), pl.BlockSpec((1, MLA_WIDTH, ts), lambda b, s: (b, 0, s)),
                  _const_spec(w_out.shape)]
        + _xattn_specs(D, wq_all, kv, wo_all, layer),
        out_specs=tok(D),
        out_shape=jax.ShapeDtypeStruct(x.shape, F32),
        compiler_params=_params("arbitrary", "arbitrary"),
        name="odd_out",
    )(x, hm, ha, w_out, xg.reshape(1, D), wq_all, *kv, wo_all)


def _pack_odd_weights(w_in, w_uq, w_ukv):
    D = w_in.shape[0]
    c = 4 * ML_WIDTH
    w_g = w_in[:, c:c + 2 * ML_HEADS]
    c += 2 * ML_HEADS
    w_cq = w_in[:, c:c + Q_LORA]
    c += Q_LORA
    w_ckv = w_in[:, c:c + KV_LORA]
    c += KV_LORA
    w_kr = w_in[:, c:c + QK_ROPE]
    half = QK_ROPE // 2
    w_kr_sw = jnp.concatenate([w_kr[:, half:], w_kr[:, :half]], axis=1)
    zl = jnp.zeros((D, QK_NOPE), w_in.dtype)
    zr = jnp.zeros((D, LANES - QK_NOPE - QK_ROPE), w_in.dtype)
    w_main = jnp.concatenate([w_in[:, :4 * ML_WIDTH], w_cq, w_ckv,
                              zl, w_kr, zr, zl, w_kr_sw, zr], axis=1).astype(BF16)
    w_gt = w_g.T.astype(BF16)

    uq = w_uq.reshape(Q_LORA, MLA_HEADS, QK_NOPE + QK_ROPE)
    uq_n, uq_r = uq[..., :QK_NOPE], uq[..., QK_NOPE:]
    uq_r_sw = jnp.concatenate([uq_r[..., half:], uq_r[..., :half]], axis=-1)
    zpad = jnp.zeros((Q_LORA, MLA_HEADS, LANES - QK_NOPE - QK_ROPE), w_uq.dtype)
    w_uq_p = jnp.concatenate([uq_n, uq_r, zpad], axis=-1).reshape(Q_LORA, MLA_HEADS * LANES)
    w_uq_s = jnp.concatenate([jnp.zeros_like(uq_n), uq_r_sw, zpad], axis=-1)
    w_uq_s = w_uq_s.reshape(Q_LORA, MLA_HEADS * LANES)

    ukv = w_ukv.reshape(KV_LORA, MLA_HEADS, QK_NOPE + V_DIM)
    uk, uv = ukv[..., :QK_NOPE], ukv[..., QK_NOPE:]
    w_uk_p = jnp.concatenate([uk, jnp.zeros((KV_LORA, MLA_HEADS, LANES - QK_NOPE), w_ukv.dtype)],
                             axis=-1).reshape(KV_LORA, MLA_HEADS * LANES)
    w_uvt = uv.reshape(KV_LORA, MLA_WIDTH).T
    return (w_main, w_gt, w_uq_p.astype(BF16), w_uq_s.astype(BF16), w_uk_p.astype(BF16),
            w_uvt.astype(BF16))


def _rope_inv_freq():
    half = QK_ROPE // 2
    inv = ROPE_THETA ** (-jnp.arange(half, dtype=F32) / half)
    return inv.reshape(half, 1)


TS_EVEN = 1024
TM_FFN = 1024
TH_FFN = 256
TS_ODD = 1024
L_MLSTM = 256
NC_MLSTM = 4
TQ_MLA = 512
NT_MLA = 8
TS_OUT = 1024


def kernel(x, mem, positions, norm_mix_g, norm_xattn_g, mem_norm_g, xattn_wq, xattn_wkv, xattn_wo,
           norm_ffn_g, ffn_w_gate_up, ffn_w_down, ev_w_in, ev_conv_w, ev_pool_w, ev_pool_scale,
           ev_w_out, od_w_in, od_gate_bias, od_ml_norm_g, od_q_norm_g, od_kv_norm_g, od_w_uq,
           od_w_ukv, od_w_out, final_norm_g):
    depth = norm_mix_g.shape[0]
    B, S, D = x.shape
    pos3 = positions.reshape(B, 1, S)
    invf = _rope_inv_freq()
    wq_all, wkv_all, wo_all = (w.astype(BF16) for w in (xattn_wq, xattn_wkv, xattn_wo))
    w_gu_all, w_d_all = ffn_w_gate_up.astype(BF16), ffn_w_down.astype(BF16)
    for layer in range(depth):
        kv = _mem_kv(mem, mem_norm_g[layer], wkv_all, layer)
        xa = (norm_xattn_g[layer], wq_all, kv, wo_all, layer)
        if layer % 2 == 0:
            e = layer // 2
            x = _even_mixer(x, norm_mix_g[layer], ev_w_in[e].astype(BF16), ev_conv_w[e],
                            ev_pool_w[e].astype(BF16), ev_pool_scale[e], ev_w_out[e].astype(BF16),
                            *xa, ts=TS_EVEN)
        else:
            o = layer // 2
            w_main, w_gt, w_uq_p, w_uq_s, w_uk_p, w_uvt = _pack_odd_weights(
                od_w_in[o], od_w_uq[o], od_w_ukv[o])
            qm, km, vm, om, gates_t, qp, kp, vt = _odd_proj(
                x, pos3, norm_mix_g[layer], w_main, w_gt, od_gate_bias[o].reshape(2 * ML_HEADS, 1),
                od_q_norm_g[o].reshape(1, Q_LORA), od_kv_norm_g[o].reshape(1, KV_LORA),
                w_uq_p, w_uq_s, w_uk_p, w_uvt, invf, ts=TS_ODD)
            hm = _mlstm(qm, km, vm, om, gates_t, od_ml_norm_g[o], L=L_MLSTM, nc=NC_MLSTM)
            ha = _mla_attention(qp, kp, vt, tq=TQ_MLA, nt=NT_MLA)
            x = _odd_out(x, hm, ha, od_w_out[o].astype(BF16), *xa, ts=TS_OUT)
        x = _swiglu(x, norm_ffn_g[layer], w_gu_all, w_d_all, final_norm_g, layer,
                    tm=TM_FFN, th=TH_FFN, final_norm=(layer == depth - 1))
    return x
```

```python
import functools

import jax
import jax.numpy as jnp
from jax import lax
from jax.experimental import pallas as pl
from jax.experimental.pallas import tpu as pltpu

F32 = jnp.float32
BF16 = jnp.bfloat16

EPS = 1e-6
ROPE_THETA = 10000.0
LANES = 128
SUBLANES = 8
BF16_ROWS = 16
VMEM_LIMIT_BYTES = 56 * 1024 * 1024
NEG_BIG = -1e30
LOG2_E = 1.4426950408889634

SC_WIDTH = 512
CONV_K = 3
POOL_WINDOWS = (2, 4, 8, 16)
POOL_GROUP = 128
HALO = 2 * max(POOL_WINDOWS)
ML_HEADS = 4
ML_HEAD_DIM = 128
ML_WIDTH = ML_HEADS * ML_HEAD_DIM
MLA_HEADS = 8
QK_NOPE = 64
QK_ROPE = 32
V_DIM = 64
Q_LORA = 384
KV_LORA = 256
MLA_WIDTH = MLA_HEADS * V_DIM
XA_HEADS = 4
XA_HEAD_DIM = 128
XA_WIDTH = XA_HEADS * XA_HEAD_DIM


def _params(*sem):
    return pltpu.CompilerParams(dimension_semantics=sem, vmem_limit_bytes=VMEM_LIMIT_BYTES)


def _const_spec(shape):
    nd = len(shape)
    return pl.BlockSpec(shape, lambda *_: (0,) * nd, pipeline_mode=pl.Buffered(1))


def _layer_spec(stacked, layer):
    return pl.BlockSpec((None,) + stacked.shape[1:], lambda *_: (layer, 0, 0),
                        pipeline_mode=pl.Buffered(1))


def _rms(x, g):
    ms = jnp.mean(x * x, axis=-1, keepdims=True)
    return x * lax.rsqrt(ms + EPS) * g


def _dot(a, b):
    return jnp.dot(a, b, preferred_element_type=F32)


def _dot_nt(a, b):
    return lax.dot_general(a, b, (((1,), (1,)), ((), ())), preferred_element_type=F32)


def _even_kernel(x_ref, g_ref, win_ref, convw_ref, poolw_ref, pscale_ref, wout_ref,
                 xg_ref, wq_ref, kt_ref, v_ref, wo_ref, o_ref, ubuf, xbbuf, tmp_a, tmp_b, *, ts):
    si = pl.program_id(1)

    @pl.when(si == 0)
    def _():
        ubuf[0:HALO, :] = jnp.zeros((HALO, SC_WIDTH), F32)
        xbbuf[0:HALO, :] = jnp.zeros((HALO, SC_WIDTH), F32)

    x = x_ref[0]
    h = _rms(x, g_ref[...]).astype(BF16)
    z = _dot(h, win_ref[...])
    g_b = z[:, 0:SC_WIDTH]
    u = z[:, SC_WIDTH:2 * SC_WIDTH] * z[:, 2 * SC_WIDTH:3 * SC_WIDTH]
    xb = z[:, 3 * SC_WIDTH:4 * SC_WIDTH]
    ubuf[HALO:HALO + ts, :] = u
    xbbuf[HALO:HALO + ts, :] = xb

    cw = convw_ref[...]
    conv = cw[2:3, :] * u
    for k in range(CONV_K - 1):
        back = CONV_K - 1 - k
        conv = conv + cw[k:k + 1, :] * ubuf[HALO - back:HALO - back + ts, :]
    ya = g_b * conv

    def window_level(src, c0, dst, r0, shift):
        w2 = (src[r0:HALO + ts, c0:c0 + POOL_GROUP]
              + src[r0 - shift:HALO + ts - shift, c0:c0 + POOL_GROUP])
        if dst is not None:
            dst[r0:HALO + ts, :] = w2
        return w2

    t_idx = si * ts + lax.broadcasted_iota(jnp.int32, (ts, POOL_GROUP), 0)
    ys = [ya.astype(BF16)]
    for gi, w in enumerate(POOL_WINDOWS):
        lo = gi * POOL_GROUP
        xg = xb[:, lo:lo + POOL_GROUP]
        src, c0, shift, level = xbbuf, lo, 1, 1
        while 2 * shift < w:
            dst = tmp_a if src is not tmp_a else tmp_b
            window_level(src, c0, dst, SUBLANES * level, shift)
            src, c0, shift, level = dst, 0, 2 * shift, level + 1
        acc = window_level(src, c0, None, HALO, shift)
        cnt = jnp.minimum(t_idx + 1, w).astype(F32)
        d = (acc / cnt - xg).astype(BF16)
        yg = _dot(d, poolw_ref[gi]) * pscale_ref[:, lo:lo + POOL_GROUP]
        ys.append(yg.astype(BF16))

    ubuf[0:HALO, :] = ubuf[ts:ts + HALO, :]
    xbbuf[0:HALO, :] = xbbuf[ts:ts + HALO, :]

    mix = jnp.concatenate(ys, axis=-1)
    x1 = x + _dot(mix, wout_ref[...])
    o_ref[0] = _xattn_body(x1, xg_ref, wq_ref, kt_ref, v_ref, wo_ref)


def _even_mixer(x, g, w_in, conv_w, pool_w, pool_scale, w_out, xg, wq_all, kv, wo_all, layer, *, ts):
    B, S, D = x.shape
    ts = min(ts, S)
    assert S % ts == 0 and D == 2 * SC_WIDTH and w_in.shape == (D, 4 * SC_WIDTH)
    kern = functools.partial(_even_kernel, ts=ts)
    return pl.pallas_call(
        kern,
        grid=(B, S // ts),
        in_specs=[
            pl.BlockSpec((1, ts, D), lambda b, s: (b, s, 0)),
            _const_spec((1, D)),
            _const_spec(w_in.shape),
            _const_spec(conv_w.shape),
            _const_spec(pool_w.shape),
            _const_spec((1, SC_WIDTH)),
            _const_spec(w_out.shape),
        ] + _xattn_specs(D, wq_all, kv, wo_all, layer),
        out_specs=pl.BlockSpec((1, ts, D), lambda b, s: (b, s, 0)),
        out_shape=jax.ShapeDtypeStruct(x.shape, F32),
        scratch_shapes=[pltpu.VMEM((HALO + ts, SC_WIDTH), F32),
                        pltpu.VMEM((HALO + ts, SC_WIDTH), F32),
                        pltpu.VMEM((HALO + ts, POOL_GROUP), F32),
                        pltpu.VMEM((HALO + ts, POOL_GROUP), F32)],
        compiler_params=_params("arbitrary", "arbitrary"),
        name="even_mixer",
    )(x, g.reshape(1, D), w_in, conv_w, pool_w, pool_scale.reshape(1, SC_WIDTH), w_out,
      xg.reshape(1, D), wq_all, *kv, wo_all)


def _memkv_kernel(mem_ref, g_ref, wkv_ref, kt_ref, v_ref):
    h = _rms(mem_ref[0], g_ref[...]).astype(BF16)
    kv = _dot(h, wkv_ref[...])
    kt_ref[0] = kv[:, :XA_WIDTH].T.astype(BF16)
    v_ref[0] = kv[:, XA_WIDTH:].astype(BF16)


def _mem_kv(mem, g, wkv_all, layer):
    B, M, D = mem.shape
    return pl.pallas_call(
        _memkv_kernel,
        grid=(B,),
        in_specs=[pl.BlockSpec((1, M, D), lambda b: (b, 0, 0)),
                  _const_spec((1, D)), _layer_spec(wkv_all, layer)],
        out_specs=(pl.BlockSpec((1, XA_WIDTH, M), lambda b: (b, 0, 0)),
                   pl.BlockSpec((1, M, XA_WIDTH), lambda b: (b, 0, 0))),
        out_shape=(jax.ShapeDtypeStruct((B, XA_WIDTH, M), BF16),
                   jax.ShapeDtypeStruct((B, M, XA_WIDTH), BF16)),
        compiler_params=_params("arbitrary"),
        name="mem_kv",
    )(mem, g.reshape(1, D), wkv_all)


def _xattn_body(x, g_ref, wq_ref, kt_ref, v_ref, wo_ref):
    h = _rms(x, g_ref[...]).astype(BF16)
    q = (_dot(h, wq_ref[...]) * (XA_HEAD_DIM ** -0.5)).astype(BF16)
    ones_v = jnp.ones((v_ref.shape[1], XA_HEAD_DIM), BF16)
    outs = []
    for hd in range(XA_HEADS):
        lo = hd * XA_HEAD_DIM
        s = _dot(q[:, lo:lo + XA_HEAD_DIM], kt_ref[0, lo:lo + XA_HEAD_DIM, :])
        p = jnp.exp(s - jnp.max(s, axis=-1, keepdims=True)).astype(BF16)
        v_aug = jnp.concatenate([v_ref[0, :, lo:lo + XA_HEAD_DIM], ones_v], axis=1)
        o = _dot(p, v_aug)
        outs.append((o[:, :XA_HEAD_DIM] / o[:, XA_HEAD_DIM:]).astype(BF16))
    a = jnp.concatenate(outs, axis=-1)
    return x + _dot(a, wo_ref[...])


def _xattn_specs(D, wq_all, kv, wo_all, layer):
    kt, v = kv
    return [_const_spec((1, D)), _layer_spec(wq_all, layer),
            pl.BlockSpec((1,) + kt.shape[1:], lambda b, s: (b, 0, 0)),
            pl.BlockSpec((1,) + v.shape[1:], lambda b, s: (b, 0, 0)), _layer_spec(wo_all, layer)]


def _swiglu_kernel(x_ref, g_ref, wgu_ref, wd_ref, fg_ref, o_ref, *, hidden, th, final_norm):
    x = x_ref[...]
    h = _rms(x, g_ref[...]).astype(BF16)
    acc = x
    for j in range(hidden // th):
        gate = _dot(h, wgu_ref[:, j * th:(j + 1) * th])
        up = _dot(h, wgu_ref[:, hidden + j * th:hidden + (j + 1) * th])
        act = (gate * jax.nn.sigmoid(gate) * up).astype(BF16)
        acc = acc + _dot(act, wd_ref[j * th:(j + 1) * th, :])
    if final_norm:
        acc = _rms(acc, fg_ref[...])
    o_ref[...] = acc


def _swiglu(x, g, w_gu_all, w_d_all, final_g, layer, *, tm, th, final_norm):
    B, S, D = x.shape
    T = B * S
    tm = min(tm, T)
    hidden = w_d_all.shape[1]
    assert T % tm == 0 and hidden % th == 0 and w_gu_all.shape[2] == 2 * hidden
    kern = functools.partial(_swiglu_kernel, hidden=hidden, th=th, final_norm=final_norm)
    out = pl.pallas_call(
        kern,
        grid=(T // tm,),
        in_specs=[
            pl.BlockSpec((tm, D), lambda i: (i, 0)),
            _const_spec((1, D)),
            _layer_spec(w_gu_all, layer),
            _layer_spec(w_d_all, layer),
            _const_spec((1, D)),
        ],
        out_specs=pl.BlockSpec((tm, D), lambda i: (i, 0)),
        out_shape=jax.ShapeDtypeStruct((T, D), F32),
        compiler_params=_params("arbitrary"),
        name="swiglu",
    )(x.reshape(T, D), g.reshape(1, D), w_gu_all, w_d_all, final_g.reshape(1, D))
    return out.reshape(B, S, D)


def _odd_proj_kernel(x_ref, pos_ref, g_ref, wmain_ref, wgt_ref, gbias_ref,
                     qg_ref, kvg_ref, wuq_ref, wuqs_ref, wuk_ref, wuvt_ref, invf_ref,
                     qm_ref, km_ref, vm_ref, om_ref, gates_ref, qp_ref, kp_ref, vt_ref):
    x = x_ref[0]
    h = _rms(x, g_ref[...]).astype(BF16)
    c0 = 4 * ML_WIDTH

    def ml_cols(i):
        return _dot(h, wmain_ref[:, i * ML_WIDTH:(i + 1) * ML_WIDTH])

    zl = _dot(h, wmain_ref[:, c0:])
    c_q = zl[:, 0:Q_LORA]
    c_kv = zl[:, Q_LORA:Q_LORA + KV_LORA]
    kr = zl[:, Q_LORA + KV_LORA:Q_LORA + KV_LORA + LANES]
    kr_sw = zl[:, Q_LORA + KV_LORA + LANES:Q_LORA + KV_LORA + 2 * LANES]
    gates_ref[0] = _dot_nt(wgt_ref[...], h) + gbias_ref[...]

    half = QK_ROPE // 2
    ang_t = invf_ref[...] * pos_ref[0].astype(F32)
    cos_c, sin_c = jnp.cos(ang_t), jnp.sin(ang_t)
    ts = ang_t.shape[1]
    pad_lo = jnp.zeros((QK_NOPE, ts), F32)
    pad_hi = jnp.zeros((LANES - QK_NOPE - QK_ROPE, ts), F32)
    cos_t = jnp.concatenate([pad_lo + 1.0, cos_c, cos_c, pad_hi], axis=0).T
    sin_t = jnp.concatenate([pad_lo, -sin_c, sin_c, pad_hi], axis=0).T
    scale = (QK_NOPE + QK_ROPE) ** -0.5 * LOG2_E
    cos_q, sin_q = cos_t * scale, sin_t * scale
    qm_ref[0] = ml_cols(0).astype(BF16)
    km_ref[0] = (ml_cols(1) * (ML_HEAD_DIM ** -0.5)).astype(BF16)

    cqn = _rms(c_q, qg_ref[...]).astype(BF16)
    qa = _dot(cqn, wuq_ref[...])
    qb = _dot(cqn, wuqs_ref[...])
    ckn = _rms(c_kv, kvg_ref[...]).astype(BF16)
    kn = _dot(ckn, wuk_ref[...])
    vt_ref[0] = _dot_nt(wuvt_ref[...], ckn).astype(BF16)
    k_rope = kr * cos_t + kr_sw * sin_t

    def rope_heads(first, last):
        for hd in range(first, last):
            lo = hd * LANES
            qp_ref[0, lo:lo + LANES, :] = (qa[:, lo:lo + LANES] * cos_q
                                           + qb[:, lo:lo + LANES] * sin_q).T.astype(BF16)
            kp_ref[0, :, lo:lo + LANES] = (kn[:, lo:lo + LANES] + k_rope).astype(BF16)

    vm_ref[0] = ml_cols(2).astype(BF16)
    rope_heads(0, MLA_HEADS // 2)
    om_ref[0] = ml_cols(3).astype(BF16)
    rope_heads(MLA_HEADS // 2, MLA_HEADS)


def _odd_proj(x, pos3, g, w_main, w_gt, gbias, q_g, kv_g, w_uq_p, w_uq_s, w_uk_p, w_uvt, invf, *, ts):
    B, S, D = x.shape
    ts = min(ts, S)
    assert S % ts == 0
    tok = lambda n: pl.BlockSpec((1, ts, n), lambda b, s: (b, s, 0))
    out_shapes = (
        jax.ShapeDtypeStruct((B, S, ML_WIDTH), BF16),
        jax.ShapeDtypeStruct((B, S, ML_WIDTH), BF16),
        jax.ShapeDtypeStruct((B, S, ML_WIDTH), BF16),
        jax.ShapeDtypeStruct((B, S, ML_WIDTH), BF16),
        jax.ShapeDtypeStruct((B, 2 * ML_HEADS, S), F32),
        jax.ShapeDtypeStruct((B, MLA_HEADS * LANES, S), BF16),
        jax.ShapeDtypeStruct((B, S, MLA_HEADS * LANES), BF16),
        jax.ShapeDtypeStruct((B, MLA_WIDTH, S), BF16),
    )
    return pl.pallas_call(
        _odd_proj_kernel,
        grid=(B, S // ts),
        in_specs=[
            tok(D),
            pl.BlockSpec((1, 1, ts), lambda b, s: (b, 0, s)),
            _const_spec((1, D)),
            _const_spec(w_main.shape),
            _const_spec(w_gt.shape),
            _const_spec(gbias.shape),
            _const_spec(q_g.shape),
            _const_spec(kv_g.shape),
            _const_spec(w_uq_p.shape),
            _const_spec(w_uq_s.shape),
            _const_spec(w_uk_p.shape),
            _const_spec(w_uvt.shape),
            _const_spec(invf.shape),
        ],
        out_specs=(tok(ML_WIDTH), tok(ML_WIDTH), tok(ML_WIDTH), tok(ML_WIDTH),
                   pl.BlockSpec((1, 2 * ML_HEADS, ts), lambda b, s: (b, 0, s)),
                   pl.BlockSpec((1, MLA_HEADS * LANES, ts), lambda b, s: (b, 0, s)),
                   tok(MLA_HEADS * LANES),
                   pl.BlockSpec((1, MLA_WIDTH, ts), lambda b, s: (b, 0, s))),
        out_shape=out_shapes,
        compiler_params=_params("arbitrary", "arbitrary"),
        name="odd_proj",
    )(x, pos3, g.reshape(1, D), w_main, w_gt, gbias, q_g, kv_g, w_uq_p, w_uq_s, w_uk_p, w_uvt, invf)


def _split3(a):
    hi = a.astype(BF16)
    r1 = a - hi.astype(F32)
    mid = r1.astype(BF16)
    lo = (r1 - mid.astype(F32)).astype(BF16)
    return hi, mid, lo


def _log2_sigmoid(x):
    return (jnp.minimum(x, 0.0) - jnp.log(1.0 + jnp.exp(-jnp.abs(x)))) * LOG2_E


def _mlstm_kernel(q_ref, k_ref, v_ref, o_ref, gt_ref, ng_ref, h_ref, c_scr, m_scr, *, L, nc):
    ci = pl.program_id(1)

    @pl.when(ci == 0)
    def _():
        c_scr[...] = jnp.zeros(c_scr.shape, F32)
        m_scr[...] = jnp.zeros(m_scr.shape, F32)

    row = lax.broadcasted_iota(jnp.int32, (L, L), 0)
    col = lax.broadcasted_iota(jnp.int32, (L, L), 1)
    causal = col <= row
    ones_v = jnp.ones((L, LANES), BF16)
    reps = L // LANES

    def widen(a):
        return jnp.concatenate([a] * reps, axis=1)

    for ch, hd in [(ch, hd) for ch in range(nc) for hd in range(ML_HEADS)]:
        tok = slice(ch * L, (ch + 1) * L)
        if hd == 0:
            gt = gt_ref[0, :, tok]
            lf_t = _log2_sigmoid(gt)
            b_t = sum(_dot(t, (row <= col).astype(BF16)) for t in _split3(lf_t))
            b_c = jnp.concatenate([b_t, jnp.zeros((LANES - 2 * ML_HEADS, L), F32)], axis=0).T
        lo = hd * ML_HEAD_DIM
        fg = ML_HEADS + hd
        q = q_ref[0, tok, lo:lo + ML_HEAD_DIM]
        k = k_ref[0, tok, lo:lo + ML_HEAD_DIM]
        v_aug = jnp.concatenate([v_ref[0, tok, lo:lo + ML_HEAD_DIM], ones_v], axis=1)
        g_row = gt[hd:hd + 1, :] * LOG2_E - b_t[fg:fg + 1, :]
        gtot = jnp.sum(lf_t[fg:fg + 1, :], axis=1, keepdims=True)
        b_col = jnp.broadcast_to(b_c[:, fg:fg + 1], (L, LANES))

        m_prev = m_scr[hd]
        c_prev = c_scr[hd]

        g_mask = jnp.where(causal, g_row, NEG_BIG)
        gmax = jnp.max(g_mask, axis=1, keepdims=True)
        h_col = jnp.maximum(m_prev, jnp.broadcast_to(gmax, (L, LANES)))
        w_inter = jnp.exp2(m_prev - h_col)
        sw = jnp.exp2(g_mask - widen(h_col)) * _dot_nt(q, k)
        out = (jnp.concatenate([w_inter, w_inter], axis=1) * _dot(q, c_prev.astype(BF16))
               + _dot(sw.astype(BF16), v_aug))
        den = jnp.maximum(jnp.abs(out[:, ML_HEAD_DIM:]), jnp.exp2(-(b_col + h_col)))
        hc = out[:, :ML_HEAD_DIM] / den

        og = jax.nn.sigmoid(o_ref[0, tok, lo:lo + ML_HEAD_DIM].astype(F32))
        hg = og * hc
        ms = jnp.mean(hg * hg, axis=1, keepdims=True)
        h_ref[0, tok, lo:lo + ML_HEAD_DIM] = (
            hg * lax.rsqrt(ms + EPS) * ng_ref[:, lo:lo + ML_HEAD_DIM]).astype(BF16)

        m_prev1 = m_prev[:, 0:1]
        a_row = gtot + g_row
        m_loc = jnp.max(a_row, axis=1, keepdims=True)
        m_new = jnp.maximum(gtot + m_prev1, m_loc)
        s_old = jnp.exp2(gtot + m_prev1 - m_new)
        w_loc = jnp.exp2(a_row - m_new)
        kw_t = (k.T.astype(F32) * w_loc).astype(BF16)
        c_scr[hd] = s_old * c_prev + _dot(kw_t, v_aug)
        m_scr[hd] = jnp.broadcast_to(m_new, (1, LANES))


def _mlstm(q, k, v, o, gates_t, norm_g, *, L, nc):
    B, S, W = q.shape
    L = min(L, S)
    nc = min(nc, S // L)
    assert S % (nc * L) == 0 and L % LANES == 0 and W == ML_WIDTH
    kern = functools.partial(_mlstm_kernel, L=L, nc=nc)
    tok = pl.BlockSpec((1, nc * L, W), lambda b, c: (b, c, 0))
    return pl.pallas_call(
        kern,
        grid=(B, S // (nc * L)),
        in_specs=[tok, tok, tok, tok,
                  pl.BlockSpec((1, 2 * ML_HEADS, nc * L), lambda b, c: (b, 0, c)),
                  _const_spec((1, W))],
        out_specs=tok,
        out_shape=jax.ShapeDtypeStruct((B, S, W), BF16),
        scratch_shapes=[pltpu.VMEM((ML_HEADS, ML_HEAD_DIM, 2 * ML_HEAD_DIM), F32),
                        pltpu.VMEM((ML_HEADS, 1, LANES), F32)],
        compiler_params=_params("arbitrary", "arbitrary"),
        name="mlstm",
    )(q, k, v, o, gates_t, norm_g.reshape(1, W))


def _mla_kernel(qt_ref, k_ref, vt_ref, o_ref, s00, s01, s10, s11, *, tq, nt):
    g = pl.program_id(2)
    heads = 2
    pieces = 2
    rows = tq // pieces
    hk = tq // 2
    s_scr = ((s00, s01), (s10, s11))
    ones_rows = jnp.ones((BF16_ROWS, tq), BF16)

    def run_tile(t, qi, sa, sb):
        lanes = slice(t * tq, (t + 1) * tq)

        def score_piece(j, slot, hh, pc):
            start = pl.multiple_of(j * tq + pc * rows, rows)
            kb = k_ref[0, pl.ds(start, rows), hh * LANES:(hh + 1) * LANES]
            qt = qt_ref[0, hh * LANES:(hh + 1) * LANES, lanes]
            s_scr[slot][hh][pc * rows:(pc + 1) * rows, :] = _dot(kb, qt)

        def stage_exp(j, slot, hh, m_new, acc, half):
            start = pl.multiple_of(j * tq + half * hk, hk)
            vtb = jnp.concatenate([vt_ref[0, hh * V_DIM:(hh + 1) * V_DIM, pl.ds(start, hk)],
                                   ones_rows[:, :hk]], axis=0)
            p = jnp.exp2(s_scr[slot][hh][half * hk:(half + 1) * hk, :] - m_new).astype(BF16)
            return acc + _dot(vtb, p)

        def block(j, slot, carry, nxt):
            new = []
            for hh in range(heads):
                m, acc = carry[hh]
                score_piece(nxt[0], nxt[1], hh, 0)
                m_new = jnp.maximum(m, jnp.max(s_scr[slot][hh][...], axis=0, keepdims=True))
                score_piece(nxt[0], nxt[1], hh, 1)
                acc = stage_exp(j, slot, hh, m_new, jnp.exp2(m - m_new) * acc, 0)
                acc = stage_exp(j, slot, hh, m_new, acc, 1)
                new.append((m_new, acc))
            return tuple(new)

        def diag_block(slot, carry):
            start = pl.multiple_of(qi * tq, tq)
            kk = lax.broadcasted_iota(jnp.int32, (hk, tq), 0)
            qq = lax.broadcasted_iota(jnp.int32, (hk, tq), 1)
            kb = lax.broadcasted_iota(jnp.int32, (hk, hk), 0)
            qb = lax.broadcasted_iota(jnp.int32, (hk, hk), 1)
            hs = range(heads)
            top = [jnp.where(kk <= qq, s_scr[slot][hh][0:hk, :], NEG_BIG) for hh in hs]
            bot = [jnp.where(kb <= qb, s_scr[slot][hh][hk:, hk:], NEG_BIG) for hh in hs]
            neg = jnp.full((hk, hk), NEG_BIG, F32)
            m_new = [jnp.maximum(carry[hh][0], jnp.max(
                jnp.maximum(top[hh], jnp.concatenate([neg, bot[hh]], axis=1)), axis=0, keepdims=True))
                for hh in hs]
            vt_aug = [jnp.concatenate([vt_ref[0, hh * V_DIM:(hh + 1) * V_DIM, pl.ds(start, tq)],
                                       ones_rows], axis=0) for hh in hs]
            p_top = [jnp.exp2(top[hh] - m_new[hh]).astype(BF16) for hh in hs]
            p_bot = [jnp.exp2(bot[hh] - jnp.broadcast_to(m_new[hh], (hk, tq))[:, hk:]).astype(BF16)
                     for hh in hs]
            outs = []
            for hh in hs:
                m, acc = carry[hh]
                acc = jnp.exp2(m - m_new[hh]) * acc + _dot(vt_aug[hh][:, :hk], p_top[hh])
                hi = acc[:, hk:] + _dot(vt_aug[hh][:, hk:], p_bot[hh])
                acc = jnp.concatenate([acc[:, :hk], hi], axis=1)
                outs.append(acc[:V_DIM] / acc[V_DIM:V_DIM + 1])
            return jnp.concatenate(outs, axis=0)

        for hh in range(heads):
            for pc in range(pieces):
                score_piece(0, sa, hh, pc)

        def body(i, c):
            c = block(2 * i, sa, c, (2 * i + 1, sb))
            return block(2 * i + 1, sb, c, (2 * i + 2, sa))

        init = tuple((jnp.full((1, tq), NEG_BIG, F32), jnp.zeros((V_DIM + BF16_ROWS, tq), F32))
                     for _ in range(heads))
        carry = lax.fori_loop(0, qi // 2, body, init)
        if t % 2 == 0:
            ot = diag_block(sa, carry)
        else:
            ot = diag_block(sb, block(qi - 1, sa, carry, (qi, sb)))
        o_ref[0, :, lanes] = ot.astype(BF16)

    slot_pairs = ((0, 1), (1, 0), (1, 0), (0, 1))
    for t in range(nt):
        run_tile(t, nt * g + t, *slot_pairs[t % 4])


def _mla_attention(qpt, kp, vt, *, tq, nt):
    B, S, _ = kp.shape
    tq = min(tq, S // 2)
    nt = min(nt, S // tq)
    assert nt % 2 == 0 and S % (nt * tq) == 0
    pairs = MLA_HEADS // 2
    kern = functools.partial(_mla_kernel, tq=tq, nt=nt)
    return pl.pallas_call(
        kern,
        grid=(B, pairs, S // (nt * tq)),
        in_specs=[
            pl.BlockSpec((1, 2 * LANES, nt * tq), lambda b, p, i: (b, p, i)),
            pl.BlockSpec((1, S, 2 * LANES), lambda b, p, i: (b, 0, p)),
            pl.BlockSpec((1, 2 * V_DIM, S), lambda b, p, i: (b, p, 0)),
        ],
        out_specs=pl.BlockSpec((1, LANES, nt * tq), lambda b, p, i: (b, p, i)),
        out_shape=jax.ShapeDtypeStruct((B, MLA_WIDTH, S), BF16),
        scratch_shapes=[pltpu.VMEM((tq, tq), F32) for _ in range(4)],
        compiler_params=_params("arbitrary", "arbitrary", "arbitrary"),
        name="mla_attention",
    )(qpt, kp, vt)


def _odd_out_kernel(x_ref, hm_ref, ha_ref, wout_ref, xg_ref, wq_ref, kt_ref, v_ref, wo_ref, o_ref):
    mix = jnp.concatenate([hm_ref[0], ha_ref[0].T], axis=-1)
    x1 = x_ref[0] + _dot(mix, wout_ref[...])
    o_ref[0] = _xattn_body(x1, xg_ref, wq_ref, kt_ref, v_ref, wo_ref)


def _odd_out(x, hm, ha, w_out, xg, wq_all, kv, wo_all, layer, *, ts):
    B, S, D = x.shape
    ts = min(ts, S)
    assert S % ts == 0
    tok = lambda n: pl.BlockSpec((1, ts, n), lambda b, s: (b, s, 0))
    return pl.pallas_call(
        _odd_out_kernel,
        grid=(B, S // ts),
        in_specs=[tok(D), tok(ML_WIDTH), pl.BlockSpec((1, MLA_WIDTH, ts), lambda b, s: (b, 0, s)),
                  _const_spec(w_out.shape)]
        + _xattn_specs(D, wq_all, kv, wo_all, layer),
        out_specs=tok(D),
        out_shape=jax.ShapeDtypeStruct(x.shape, F32),
        compiler_params=_params("arbitrary", "arbitrary"),
        name="odd_out",
    )(x, hm, ha, w_out, xg.reshape(1, D), wq_all, *kv, wo_all)


def _pack_odd_weights(w_in, w_uq, w_ukv):
    D = w_in.shape[0]
    c = 4 * ML_WIDTH
    w_g = w_in[:, c:c + 2 * ML_HEADS]
    c += 2 * ML_HEADS
    w_cq = w_in[:, c:c + Q_LORA]
    c += Q_LORA
    w_ckv = w_in[:, c:c + KV_LORA]
    c += KV_LORA
    w_kr = w_in[:, c:c + QK_ROPE]
    half = QK_ROPE // 2
    w_kr_sw = jnp.concatenate([w_kr[:, half:], w_kr[:, :half]], axis=1)
    zl = jnp.zeros((D, QK_NOPE), w_in.dtype)
    zr = jnp.zeros((D, LANES - QK_NOPE - QK_ROPE), w_in.dtype)
    w_main = jnp.concatenate([w_in[:, :4 * ML_WIDTH], w_cq, w_ckv,
                              zl, w_kr, zr, zl, w_kr_sw, zr], axis=1).astype(BF16)
    w_gt = w_g.T.astype(BF16)

    uq = w_uq.reshape(Q_LORA, MLA_HEADS, QK_NOPE + QK_ROPE)
    uq_n, uq_r = uq[..., :QK_NOPE], uq[..., QK_NOPE:]
    uq_r_sw = jnp.concatenate([uq_r[..., half:], uq_r[..., :half]], axis=-1)
    zpad = jnp.zeros((Q_LORA, MLA_HEADS, LANES - QK_NOPE - QK_ROPE), w_uq.dtype)
    w_uq_p = jnp.concatenate([uq_n, uq_r, zpad], axis=-1).reshape(Q_LORA, MLA_HEADS * LANES)
    w_uq_s = jnp.concatenate([jnp.zeros_like(uq_n), uq_r_sw, zpad], axis=-1)
    w_uq_s = w_uq_s.reshape(Q_LORA, MLA_HEADS * LANES)

    ukv = w_ukv.reshape(KV_LORA, MLA_HEADS, QK_NOPE + V_DIM)
    uk, uv = ukv[..., :QK_NOPE], ukv[..., QK_NOPE:]
    w_uk_p = jnp.concatenate([uk, jnp.zeros((KV_LORA, MLA_HEADS, LANES - QK_NOPE), w_ukv.dtype)],
                             axis=-1).reshape(KV_LORA, MLA_HEADS * LANES)
    w_uvt = uv.reshape(KV_LORA, MLA_WIDTH).T
    return (w_main, w_gt, w_uq_p.astype(BF16), w_uq_s.astype(BF16), w_uk_p.astype(BF16),
            w_uvt.astype(BF16))


def _rope_inv_freq():
    half = QK_ROPE // 2
    inv = ROPE_THETA ** (-jnp.arange(half, dtype=F32) / half)
    return inv.reshape(half, 1)


TS_EVEN = 1024
TM_FFN = 1024
TH_FFN = 256
TS_ODD = 1024
L_MLSTM = 256
NC_MLSTM = 4
TQ_MLA = 512
NT_MLA = 8
TS_OUT = 1024


def kernel(x, mem, positions, norm_mix_g, norm_xattn_g, mem_norm_g, xattn_wq, xattn_wkv, xattn_wo,
           norm_ffn_g, ffn_w_gate_up, ffn_w_down, ev_w_in, ev_conv_w, ev_pool_w, ev_pool_scale,
           ev_w_out, od_w_in, od_gate_bias, od_ml_norm_g, od_q_norm_g, od_kv_norm_g, od_w_uq,
           od_w_ukv, od_w_out, final_norm_g):
    depth = norm_mix_g.shape[0]
    B, S, D = x.shape
    pos3 = positions.reshape(B, 1, S)
    invf = _rope_inv_freq()
    wq_all, wkv_all, wo_all = (w.astype(BF16) for w in (xattn_wq, xattn_wkv, xattn_wo))
    w_gu_all, w_d_all = ffn_w_gate_up.astype(BF16), ffn_w_down.astype(BF16)
    for layer in range(depth):
        kv = _mem_kv(mem, mem_norm_g[layer], wkv_all, layer)
        xa = (norm_xattn_g[layer], wq_all, kv, wo_all, layer)
        if layer % 2 == 0:
            e = layer // 2
            x = _even_mixer(x, norm_mix_g[layer], ev_w_in[e].astype(BF16), ev_conv_w[e],
                            ev_pool_w[e].astype(BF16), ev_pool_scale[e], ev_w_out[e].astype(BF16),
                            *xa, ts=TS_EVEN)
        else:
            o = layer // 2
            w_main, w_gt, w_uq_p, w_uq_s, w_uk_p, w_uvt = _pack_odd_weights(
                od_w_in[o], od_w_uq[o], od_w_ukv[o])
            qm, km, vm, om, gates_t, qp, kp, vt = _odd_proj(
                x, pos3, norm_mix_g[layer], w_main, w_gt, od_gate_bias[o].reshape(2 * ML_HEADS, 1),
                od_q_norm_g[o].reshape(1, Q_LORA), od_kv_norm_g[o].reshape(1, KV_LORA),
                w_uq_p, w_uq_s, w_uk_p, w_uvt, invf, ts=TS_ODD)
            hm = _mlstm(qm, km, vm, om, gates_t, od_ml_norm_g[o], L=L_MLSTM, nc=NC_MLSTM)
            ha = _mla_attention(qp, kp, vt, tq=TQ_MLA, nt=NT_MLA)
            x = _odd_out(x, hm, ha, od_w_out[o].astype(BF16), *xa, ts=TS_OUT)
        x = _swiglu(x, norm_ffn_g[layer], w_gu_all, w_d_all, final_norm_g, layer,
                    tm=TM_FFN, th=TH_FFN, final_norm=(layer == depth - 1))
    return x
```

```python
import functools

import jax
import jax.numpy as jnp
from jax import lax
from jax.experimental import pallas as pl
from jax.experimental.pallas import tpu as pltpu

F32 = jnp.float32
BF16 = jnp.bfloat16

EPS = 1e-6
ROPE_THETA = 10000.0
LANES = 128
SUBLANES = 8
BF16_ROWS = 16
VMEM_LIMIT_BYTES = 56 * 1024 * 1024
NEG_BIG = -1e30
LOG2_E = 1.4426950408889634

SC_WIDTH = 512
CONV_K = 3
POOL_WINDOWS = (2, 4, 8, 16)
POOL_GROUP = 128
HALO = 2 * max(POOL_WINDOWS)
ML_HEADS = 4
ML_HEAD_DIM = 128
ML_WIDTH = ML_HEADS * ML_HEAD_DIM
MLA_HEADS = 8
QK_NOPE = 64
QK_ROPE = 32
V_DIM = 64
Q_LORA = 384
KV_LORA = 256
MLA_WIDTH = MLA_HEADS * V_DIM
XA_HEADS = 4
XA_HEAD_DIM = 128
XA_WIDTH = XA_HEADS * XA_HEAD_DIM


def _params(*sem):
    return pltpu.CompilerParams(dimension_semantics=sem, vmem_limit_bytes=VMEM_LIMIT_BYTES)


def _const_spec(shape):
    nd = len(shape)
    return pl.BlockSpec(shape, lambda *_: (0,) * nd, pipeline_mode=pl.Buffered(1))


def _layer_spec(stacked, layer):
    return pl.BlockSpec((None,) + stacked.shape[1:], lambda *_: (layer, 0, 0),
                        pipeline_mode=pl.Buffered(1))


def _rms(x, g):
    ms = jnp.mean(x * x, axis=-1, keepdims=True)
    return x * lax.rsqrt(ms + EPS) * g


def _dot(a, b):
    return jnp.dot(a, b, preferred_element_type=F32)


def _dot_nt(a, b):
    return lax.dot_general(a, b, (((1,), (1,)), ((), ())), preferred_element_type=F32)


def _even_kernel(x_ref, g_ref, win_ref, convw_ref, poolw_ref, pscale_ref, wout_ref,
                 xg_ref, wq_ref, kt_ref, v_ref, wo_ref, o_ref, ubuf, xbbuf, tmp_a, tmp_b, *, ts):
    si = pl.program_id(1)

    @pl.when(si == 0)
    def _():
        ubuf[0:HALO, :] = jnp.zeros((HALO, SC_WIDTH), F32)
        xbbuf[0:HALO, :] = jnp.zeros((HALO, SC_WIDTH), F32)

    x = x_ref[0]
    h = _rms(x, g_ref[...]).astype(BF16)
    z = _dot(h, win_ref[...])
    g_b = z[:, 0:SC_WIDTH]
    u = z[:, SC_WIDTH:2 * SC_WIDTH] * z[:, 2 * SC_WIDTH:3 * SC_WIDTH]
    xb = z[:, 3 * SC_WIDTH:4 * SC_WIDTH]
    ubuf[HALO:HALO + ts, :] = u
    xbbuf[HALO:HALO + ts, :] = xb

    cw = convw_ref[...]
    conv = cw[2:3, :] * u
    for k in range(CONV_K - 1):
        back = CONV_K - 1 - k
        conv = conv + cw[k:k + 1, :] * ubuf[HALO - back:HALO - back + ts, :]
    ya = g_b * conv

    def window_level(src, c0, dst, r0, shift):
        w2 = (src[r0:HALO + ts, c0:c0 + POOL_GROUP]
              + src[r0 - shift:HALO + ts - shift, c0:c0 + POOL_GROUP])
        if dst is not None:
            dst[r0:HALO + ts, :] = w2
        return w2

    t_idx = si * ts + lax.broadcasted_iota(jnp.int32, (ts, POOL_GROUP), 0)
    ys = [ya.astype(BF16)]
    for gi, w in enumerate(POOL_WINDOWS):
        lo = gi * POOL_GROUP
        xg = xb[:, lo:lo + POOL_GROUP]
        src, c0, shift, level = xbbuf, lo, 1, 1
        while 2 * shift < w:
            dst = tmp_a if src is not tmp_a else tmp_b
            window_level(src, c0, dst, SUBLANES * level, shift)
            src, c0, shift, level = dst, 0, 2 * shift, level + 1
        acc = window_level(src, c0, None, HALO, shift)
        cnt = jnp.minimum(t_idx + 1, w).astype(F32)
        d = (acc / cnt - xg).astype(BF16)
        yg = _dot(d, poolw_ref[gi]) * pscale_ref[:, lo:lo + POOL_GROUP]
        ys.append(yg.astype(BF16))

    ubuf[0:HALO, :] = ubuf[ts:ts + HALO, :]
    xbbuf[0:HALO, :] = xbbuf[ts:ts + HALO, :]

    mix = jnp.concatenate(ys, axis=-1)
    x1 = x + _dot(mix, wout_ref[...])
    o_ref[0] = _xattn_body(x1, xg_ref, wq_ref, kt_ref, v_ref, wo_ref)


def _even_mixer(x, g, w_in, conv_w, pool_w, pool_scale, w_out, xg, wq_all, kv, wo_all, layer, *, ts):
    B, S, D = x.shape
    ts = min(ts, S)
    assert S % ts == 0 and D == 2 * SC_WIDTH and w_in.shape == (D, 4 * SC_WIDTH)
    kern = functools.partial(_even_kernel, ts=ts)
    return pl.pallas_call(
        kern,
        grid=(B, S // ts),
        in_specs=[
            pl.BlockSpec((1, ts, D), lambda b, s: (b, s, 0)),
            _const_spec((1, D)),
            _const_spec(w_in.shape),
            _const_spec(conv_w.shape),
            _const_spec(pool_w.shape),
            _const_spec((1, SC_WIDTH)),
            _const_spec(w_out.shape),
        ] + _xattn_specs(D, wq_all, kv, wo_all, layer),
        out_specs=pl.BlockSpec((1, ts, D), lambda b, s: (b, s, 0)),
        out_shape=jax.ShapeDtypeStruct(x.shape, F32),
        scratch_shapes=[pltpu.VMEM((HALO + ts, SC_WIDTH), F32),
                        pltpu.VMEM((HALO + ts, SC_WIDTH), F32),
                        pltpu.VMEM((HALO + ts, POOL_GROUP), F32),
                        pltpu.VMEM((HALO + ts, POOL_GROUP), F32)],
        compiler_params=_params("arbitrary", "arbitrary"),
        name="even_mixer",
    )(x, g.reshape(1, D), w_in, conv_w, pool_w, pool_scale.reshape(1, SC_WIDTH), w_out,
      xg.reshape(1, D), wq_all, *kv, wo_all)


def _memkv_kernel(mem_ref, g_ref, wkv_ref, kt_ref, v_ref):
    h = _rms(mem_ref[0], g_ref[...]).astype(BF16)
    kv = _dot(h, wkv_ref[...])
    kt_ref[0] = kv[:, :XA_WIDTH].T.astype(BF16)
    v_ref[0] = kv[:, XA_WIDTH:].astype(BF16)


def _mem_kv(mem, g, wkv_all, layer):
    B, M, D = mem.shape
    return pl.pallas_call(
        _memkv_kernel,
        grid=(B,),
        in_specs=[pl.BlockSpec((1, M, D), lambda b: (b, 0, 0)),
                  _const_spec((1, D)), _layer_spec(wkv_all, layer)],
        out_specs=(pl.BlockSpec((1, XA_WIDTH, M), lambda b: (b, 0, 0)),
                   pl.BlockSpec((1, M, XA_WIDTH), lambda b: (b, 0, 0))),
        out_shape=(jax.ShapeDtypeStruct((B, XA_WIDTH, M), BF16),
                   jax.ShapeDtypeStruct((B, M, XA_WIDTH), BF16)),
        compiler_params=_params("arbitrary"),
        name="mem_kv",
    )(mem, g.reshape(1, D), wkv_all)


def _xattn_body(x, g_ref, wq_ref, kt_ref, v_ref, wo_ref):
    h = _rms(x, g_ref[...]).astype(BF16)
    q = (_dot(h, wq_ref[...]) * (XA_HEAD_DIM ** -0.5)).astype(BF16)
    ones_v = jnp.ones((v_ref.shape[1], XA_HEAD_DIM), BF16)
    outs = []
    for hd in range(XA_HEADS):
        lo = hd * XA_HEAD_DIM
        s = _dot(q[:, lo:lo + XA_HEAD_DIM], kt_ref[0, lo:lo + XA_HEAD_DIM, :])
        p = jnp.exp(s - jnp.max(s, axis=-1, keepdims=True)).astype(BF16)
        v_aug = jnp.concatenate([v_ref[0, :, lo:lo + XA_HEAD_DIM], ones_v], axis=1)
        o = _dot(p, v_aug)
        outs.append((o[:, :XA_HEAD_DIM] / o[:, XA_HEAD_DIM:]).astype(BF16))
    a = jnp.concatenate(outs, axis=-1)
    return x + _dot(a, wo_ref[...])


def _xattn_specs(D, wq_all, kv, wo_all, layer):
    kt, v = kv
    return [_const_spec((1, D)), _layer_spec(wq_all, layer),
            pl.BlockSpec((1,) + kt.shape[1:], lambda b, s: (b, 0, 0)),
            pl.BlockSpec((1,) + v.shape[1:], lambda b, s: (b, 0, 0)), _layer_spec(wo_all, layer)]


def _swiglu_kernel(x_ref, g_ref, wgu_ref, wd_ref, fg_ref, o_ref, *, hidden, th, final_norm):
    x = x_ref[...]
    h = _rms(x, g_ref[...]).astype(BF16)
    acc = x
    for j in range(hidden // th):
        gate = _dot(h, wgu_ref[:, j * th:(j + 1) * th])
        up = _dot(h, wgu_ref[:, hidden + j * th:hidden + (j + 1) * th])
        act = (gate * jax.nn.sigmoid(gate) * up).astype(BF16)
        acc = acc + _dot(act, wd_ref[j * th:(j + 1) * th, :])
    if final_norm:
        acc = _rms(acc, fg_ref[...])
    o_ref[...] = acc


def _swiglu(x, g, w_gu_all, w_d_all, final_g, layer, *, tm, th, final_norm):
    B, S, D = x.shape
    T = B * S
    tm = min(tm, T)
    hidden = w_d_all.shape[1]
    assert T % tm == 0 and hidden % th == 0 and w_gu_all.shape[2] == 2 * hidden
    kern = functools.partial(_swiglu_kernel, hidden=hidden, th=th, final_norm=final_norm)
    out = pl.pallas_call(
        kern,
        grid=(T // tm,),
        in_specs=[
            pl.BlockSpec((tm, D), lambda i: (i, 0)),
            _const_spec((1, D)),
            _layer_spec(w_gu_all, layer),
            _layer_spec(w_d_all, layer),
            _const_spec((1, D)),
        ],
        out_specs=pl.BlockSpec((tm, D), lambda i: (i, 0)),
        out_shape=jax.ShapeDtypeStruct((T, D), F32),
        compiler_params=_params("arbitrary"),
        name="swiglu",
    )(x.reshape(T, D), g.reshape(1, D), w_gu_all, w_d_all, final_g.reshape(1, D))
    return out.reshape(B, S, D)


def _odd_proj_kernel(x_ref, pos_ref, g_ref, wmain_ref, wgt_ref, gbias_ref,
                     qg_ref, kvg_ref, wuq_ref, wuk_ref, wuvt_ref, invf_ref,
                     qm_ref, km_ref, vm_ref, om_ref, gates_ref, qp_ref, kp_ref, vt_ref):
    x = x_ref[0]
    h = _rms(x, g_ref[...]).astype(BF16)
    c0 = 4 * ML_WIDTH

    def ml_cols(i):
        return _dot(h, wmain_ref[:, i * ML_WIDTH:(i + 1) * ML_WIDTH])

    zl = _dot(h, wmain_ref[:, c0:])
    c_q = zl[:, 0:Q_LORA]
    c_kv = zl[:, Q_LORA:Q_LORA + KV_LORA]
    kr = zl[:, Q_LORA + KV_LORA:Q_LORA + KV_LORA + LANES]
    kr_sw = zl[:, Q_LORA + KV_LORA + LANES:Q_LORA + KV_LORA + 2 * LANES]
    gates_ref[0] = _dot_nt(wgt_ref[...], h) + gbias_ref[...]

    half = QK_ROPE // 2
    ang_t = invf_ref[...] * pos_ref[0].astype(F32)
    cos_c, sin_c = jnp.cos(ang_t), jnp.sin(ang_t)
    ts = ang_t.shape[1]
    pad_lo = jnp.zeros((QK_NOPE, ts), F32)
    pad_hi = jnp.zeros((LANES - QK_NOPE - QK_ROPE, ts), F32)
    cos_tt = jnp.concatenate([pad_lo + 1.0, cos_c, cos_c, pad_hi], axis=0)
    sin_tt = jnp.concatenate([pad_lo, -sin_c, sin_c, pad_hi], axis=0)
    cos_t, sin_t = cos_tt.T, sin_tt.T
    scale = (QK_NOPE + QK_ROPE) ** -0.5 * LOG2_E
    cos_q, sin_q = cos_tt * scale, sin_tt * scale
    qm_ref[0] = ml_cols(0).astype(BF16)
    km_ref[0] = (ml_cols(1) * (ML_HEAD_DIM ** -0.5)).astype(BF16)

    cqn = _rms(c_q, qg_ref[...]).astype(BF16)
    qa = _dot(cqn, wuq_ref[...])
    ckn = _rms(c_kv, kvg_ref[...]).astype(BF16)
    kn = _dot(ckn, wuk_ref[...])
    vt_ref[0] = _dot_nt(wuvt_ref[...], ckn).astype(BF16)
    k_rope = kr * cos_t + kr_sw * sin_t

    def rope_heads(first, last):
        for hd in range(first, last):
            lo = hd * LANES
            qt = qa[:, lo:lo + LANES].T
            r0, r1, r2 = QK_NOPE, QK_NOPE + half, QK_NOPE + QK_ROPE
            qt_sw = jnp.concatenate([qt[:r0], qt[r1:r2], qt[r0:r1], qt[r2:]], axis=0)
            qp_ref[0, lo:lo + LANES, :] = (qt * cos_q + qt_sw * sin_q).astype(BF16)
            kp_ref[0, :, lo:lo + LANES] = (kn[:, lo:lo + LANES] + k_rope).astype(BF16)

    vm_ref[0] = ml_cols(2).astype(BF16)
    rope_heads(0, MLA_HEADS // 2)
    om_ref[0] = ml_cols(3).astype(BF16)
    rope_heads(MLA_HEADS // 2, MLA_HEADS)


def _odd_proj(x, pos3, g, w_main, w_gt, gbias, q_g, kv_g, w_uq_p, w_uk_p, w_uvt, invf, *, ts):
    B, S, D = x.shape
    ts = min(ts, S)
    assert S % ts == 0
    tok = lambda n: pl.BlockSpec((1, ts, n), lambda b, s: (b, s, 0))
    out_shapes = (
        jax.ShapeDtypeStruct((B, S, ML_WIDTH), BF16),
        jax.ShapeDtypeStruct((B, S, ML_WIDTH), BF16),
        jax.ShapeDtypeStruct((B, S, ML_WIDTH), BF16),
        jax.ShapeDtypeStruct((B, S, ML_WIDTH), BF16),
        jax.ShapeDtypeStruct((B, 2 * ML_HEADS, S), F32),
        jax.ShapeDtypeStruct((B, MLA_HEADS * LANES, S), BF16),
        jax.ShapeDtypeStruct((B, S, MLA_HEADS * LANES), BF16),
        jax.ShapeDtypeStruct((B, MLA_WIDTH, S), BF16),
    )
    return pl.pallas_call(
        _odd_proj_kernel,
        grid=(B, S // ts),
        in_specs=[
            tok(D),
            pl.BlockSpec((1, 1, ts), lambda b, s: (b, 0, s)),
            _const_spec((1, D)),
            _const_spec(w_main.shape),
            _const_spec(w_gt.shape),
            _const_spec(gbias.shape),
            _const_spec(q_g.shape),
            _const_spec(kv_g.shape),
            _const_spec(w_uq_p.shape),
            _const_spec(w_uk_p.shape),
            _const_spec(w_uvt.shape),
            _const_spec(invf.shape),
        ],
        out_specs=(tok(ML_WIDTH), tok(ML_WIDTH), tok(ML_WIDTH), tok(ML_WIDTH),
                   pl.BlockSpec((1, 2 * ML_HEADS, ts), lambda b, s: (b, 0, s)),
                   pl.BlockSpec((1, MLA_HEADS * LANES, ts), lambda b, s: (b, 0, s)),
                   tok(MLA_HEADS * LANES),
                   pl.BlockSpec((1, MLA_WIDTH, ts), lambda b, s: (b, 0, s))),
        out_shape=out_shapes,
        compiler_params=_params("arbitrary", "arbitrary"),
        name="odd_proj",
    )(x, pos3, g.reshape(1, D), w_main, w_gt, gbias, q_g, kv_g, w_uq_p, w_uk_p, w_uvt, invf)


def _split3(a):
    hi = a.astype(BF16)
    r1 = a - hi.astype(F32)
    mid = r1.astype(BF16)
    lo = (r1 - mid.astype(F32)).astype(BF16)
    return hi, mid, lo


def _log2_sigmoid(x):
    return (jnp.minimum(x, 0.0) - jnp.log(1.0 + jnp.exp(-jnp.abs(x)))) * LOG2_E


def _mlstm_kernel(q_ref, k_ref, v_ref, o_ref, gt_ref, ng_ref, h_ref, c_scr, m_scr, *, L, nc):
    ci = pl.program_id(1)

    @pl.when(ci == 0)
    def _():
        c_scr[...] = jnp.zeros(c_scr.shape, F32)
        m_scr[...] = jnp.zeros(m_scr.shape, F32)

    row = lax.broadcasted_iota(jnp.int32, (L, L), 0)
    col = lax.broadcasted_iota(jnp.int32, (L, L), 1)
    causal = col <= row
    ones_v = jnp.ones((L, LANES), BF16)
    reps = L // LANES

    def widen(a):
        return jnp.concatenate([a] * reps, axis=1)

    for ch, hd in [(ch, hd) for ch in range(nc) for hd in range(ML_HEADS)]:
        tok = slice(ch * L, (ch + 1) * L)
        if hd == 0:
            gt = gt_ref[0, :, tok]
            lf_t = _log2_sigmoid(gt)
            b_t = sum(_dot(t, (row <= col).astype(BF16)) for t in _split3(lf_t))
            b_c = jnp.concatenate([b_t, jnp.zeros((LANES - 2 * ML_HEADS, L), F32)], axis=0).T
        lo = hd * ML_HEAD_DIM
        fg = ML_HEADS + hd
        q = q_ref[0, tok, lo:lo + ML_HEAD_DIM]
        k = k_ref[0, tok, lo:lo + ML_HEAD_DIM]
        v_aug = jnp.concatenate([v_ref[0, tok, lo:lo + ML_HEAD_DIM], ones_v], axis=1)
        g_row = gt[hd:hd + 1, :] * LOG2_E - b_t[fg:fg + 1, :]
        gtot = jnp.sum(lf_t[fg:fg + 1, :], axis=1, keepdims=True)
        b_col = jnp.broadcast_to(b_c[:, fg:fg + 1], (L, LANES))

        m_prev = m_scr[hd]
        c_prev = c_scr[hd]

        g_mask = jnp.where(causal, g_row, NEG_BIG)
        gmax = jnp.max(g_mask, axis=1, keepdims=True)
        h_col = jnp.maximum(m_prev, jnp.broadcast_to(gmax, (L, LANES)))
        w_inter = jnp.exp2(m_prev - h_col)
        sw = jnp.exp2(g_mask - widen(h_col)) * _dot_nt(q, k)
        out = (jnp.concatenate([w_inter, w_inter], axis=1) * _dot(q, c_prev.astype(BF16))
               + _dot(sw.astype(BF16), v_aug))
        den = jnp.maximum(jnp.abs(out[:, ML_HEAD_DIM:]), jnp.exp2(-(b_col + h_col)))
        hc = out[:, :ML_HEAD_DIM] / den

        og = jax.nn.sigmoid(o_ref[0, tok, lo:lo + ML_HEAD_DIM].astype(F32))
        hg = og * hc
        ms = jnp.mean(hg * hg, axis=1, keepdims=True)
        h_ref[0, tok, lo:lo + ML_HEAD_DIM] = (
            hg * lax.rsqrt(ms + EPS) * ng_ref[:, lo:lo + ML_HEAD_DIM]).astype(BF16)

        m_prev1 = m_prev[:, 0:1]
        a_row = gtot + g_row
        m_loc = jnp.max(a_row, axis=1, keepdims=True)
        m_new = jnp.maximum(gtot + m_prev1, m_loc)
        s_old = jnp.exp2(gtot + m_prev1 - m_new)
        w_loc = jnp.exp2(a_row - m_new)
        kw_t = (k.T.astype(F32) * w_loc).astype(BF16)
        c_scr[hd] = s_old * c_prev + _dot(kw_t, v_aug)
        m_scr[hd] = jnp.broadcast_to(m_new, (1, LANES))


def _mlstm(q, k, v, o, gates_t, norm_g, *, L, nc):
    B, S, W = q.shape
    L = min(L, S)
    nc = min(nc, S // L)
    assert S % (nc * L) == 0 and L % LANES == 0 and W == ML_WIDTH
    kern = functools.partial(_mlstm_kernel, L=L, nc=nc)
    tok = pl.BlockSpec((1, nc * L, W), lambda b, c: (b, c, 0))
    return pl.pallas_call(
        kern,
        grid=(B, S // (nc * L)),
        in_specs=[tok, tok, tok, tok,
                  pl.BlockSpec((1, 2 * ML_HEADS, nc * L), lambda b, c: (b, 0, c)),
                  _const_spec((1, W))],
        out_specs=tok,
        out_shape=jax.ShapeDtypeStruct((B, S, W), BF16),
        scratch_shapes=[pltpu.VMEM((ML_HEADS, ML_HEAD_DIM, 2 * ML_HEAD_DIM), F32),
                        pltpu.VMEM((ML_HEADS, 1, LANES), F32)],
        compiler_params=_params("arbitrary", "arbitrary"),
        name="mlstm",
    )(q, k, v, o, gates_t, norm_g.reshape(1, W))


def _mla_kernel(qt_ref, k_ref, vt_ref, o_ref, s00, s01, s10, s11, *, tq, nt):
    g = pl.program_id(2)
    heads = 2
    pieces = 2
    rows = tq // pieces
    hk = tq // 2
    s_scr = ((s00, s01), (s10, s11))
    ones_rows = jnp.ones((BF16_ROWS, tq), BF16)

    def run_tile(t, qi, sa, sb):
        lanes = slice(t * tq, (t + 1) * tq)

        def score_piece(j, slot, hh, pc):
            start = pl.multiple_of(j * tq + pc * rows, rows)
            kb = k_ref[0, pl.ds(start, rows), hh * LANES:(hh + 1) * LANES]
            qt = qt_ref[0, hh * LANES:(hh + 1) * LANES, lanes]
            s_scr[slot][hh][pc * rows:(pc + 1) * rows, :] = _dot(kb, qt)

        def stage_exp(j, slot, hh, m_new, acc, half):
            start = pl.multiple_of(j * tq + half * hk, hk)
            vtb = jnp.concatenate([vt_ref[0, hh * V_DIM:(hh + 1) * V_DIM, pl.ds(start, hk)],
                                   ones_rows[:, :hk]], axis=0)
            p = jnp.exp2(s_scr[slot][hh][half * hk:(half + 1) * hk, :] - m_new).astype(BF16)
            return acc + _dot(vtb, p)

        def block(j, slot, carry, nxt):
            new = []
            for hh in range(heads):
                m, acc = carry[hh]
                score_piece(nxt[0], nxt[1], hh, 0)
                m_new = jnp.maximum(m, jnp.max(s_scr[slot][hh][...], axis=0, keepdims=True))
                score_piece(nxt[0], nxt[1], hh, 1)
                acc = stage_exp(j, slot, hh, m_new, jnp.exp2(m - m_new) * acc, 0)
                acc = stage_exp(j, slot, hh, m_new, acc, 1)
                new.append((m_new, acc))
            return tuple(new)

        def diag_block(slot, carry):
            start = pl.multiple_of(qi * tq, tq)
            kk = lax.broadcasted_iota(jnp.int32, (hk, tq), 0)
            qq = lax.broadcasted_iota(jnp.int32, (hk, tq), 1)
            kb = lax.broadcasted_iota(jnp.int32, (hk, hk), 0)
            qb = lax.broadcasted_iota(jnp.int32, (hk, hk), 1)
            hs = range(heads)
            top = [jnp.where(kk <= qq, s_scr[slot][hh][0:hk, :], NEG_BIG) for hh in hs]
            bot = [jnp.where(kb <= qb, s_scr[slot][hh][hk:, hk:], NEG_BIG) for hh in hs]
            neg = jnp.full((hk, hk), NEG_BIG, F32)
            m_new = [jnp.maximum(carry[hh][0], jnp.max(
                jnp.maximum(top[hh], jnp.concatenate([neg, bot[hh]], axis=1)), axis=0, keepdims=True))
                for hh in hs]
            vt_aug = [jnp.concatenate([vt_ref[0, hh * V_DIM:(hh + 1) * V_DIM, pl.ds(start, tq)],
                                       ones_rows], axis=0) for hh in hs]
            p_top = [jnp.exp2(top[hh] - m_new[hh]).astype(BF16) for hh in hs]
            p_bot = [jnp.exp2(bot[hh] - jnp.broadcast_to(m_new[hh], (hk, tq))[:, hk:]).astype(BF16)
                     for hh in hs]
            outs = []
            for hh in hs:
                m, acc = carry[hh]
                acc = jnp.exp2(m - m_new[hh]) * acc + _dot(vt_aug[hh][:, :hk], p_top[hh])
                hi = acc[:, hk:] + _dot(vt_aug[hh][:, hk:], p_bot[hh])
                acc = jnp.concatenate([acc[:, :hk], hi], axis=1)
                outs.append(acc[:V_DIM] / acc[V_DIM:V_DIM + 1])
            return jnp.concatenate(outs, axis=0)

        for hh in range(heads):
            for pc in range(pieces):
                score_piece(0, sa, hh, pc)

        def body(i, c):
            c = block(2 * i, sa, c, (2 * i + 1, sb))
            return block(2 * i + 1, sb, c, (2 * i + 2, sa))

        init = tuple((jnp.full((1, tq), NEG_BIG, F32), jnp.zeros((V_DIM + BF16_ROWS, tq), F32))
                     for _ in range(heads))
        carry = lax.fori_loop(0, qi // 2, body, init)
        if t % 2 == 0:
            ot = diag_block(sa, carry)
        else:
            ot = diag_block(sb, block(qi - 1, sa, carry, (qi, sb)))
        o_ref[0, :, lanes] = ot.astype(BF16)

    slot_pairs = ((0, 1), (1, 0), (1, 0), (0, 1))
    for t in range(nt):
        run_tile(t, nt * g + t, *slot_pairs[t % 4])


def _mla_attention(qpt, kp, vt, *, tq, nt):
    B, S, _ = kp.shape
    tq = min(tq, S // 2)
    nt = min(nt, S // tq)
    assert nt % 2 == 0 and S % (nt * tq) == 0
    pairs = MLA_HEADS // 2
    kern = functools.partial(_mla_kernel, tq=tq, nt=nt)
    return pl.pallas_call(
        kern,
        grid=(B, pairs, S // (nt * tq)),
        in_specs=[
            pl.BlockSpec((1, 2 * LANES, nt * tq), lambda b, p, i: (b, p, i)),
            pl.BlockSpec((1, S, 2 * LANES), lambda b, p, i: (b, 0, p)),
            pl.BlockSpec((1, 2 * V_DIM, S), lambda b, p, i: (b, p, 0)),
        ],
        out_specs=pl.BlockSpec((1, LANES, nt * tq), lambda b, p, i: (b, p, i)),
        out_shape=jax.ShapeDtypeStruct((B, MLA_WIDTH, S), BF16),
        scratch_shapes=[pltpu.VMEM((tq, tq), F32) for _ in range(4)],
        compiler_params=_params("arbitrary", "arbitrary", "arbitrary"),
        name="mla_attention",
    )(qpt, kp, vt)


def _odd_out_kernel(x_ref, hm_ref, ha_ref, wout_ref, xg_ref, wq_ref, kt_ref, v_ref, wo_ref, o_ref):
    mix = jnp.concatenate([hm_ref[0], ha_ref[0].T], axis=-1)
    x1 = x_ref[0] + _dot(mix, wout_ref[...])
    o_ref[0] = _xattn_body(x1, xg_ref, wq_ref, kt_ref, v_ref, wo_ref)


def _odd_out(x, hm, ha, w_out, xg, wq_all, kv, wo_all, layer, *, ts):
    B, S, D = x.shape
    ts = min(ts, S)
    assert S % ts == 0
    tok = lambda n: pl.BlockSpec((1, ts, n), lambda b, s: (b, s, 0))
    return pl.pallas_call(
        _odd_out_kernel,
        grid=(B, S // ts),
        in_specs=[tok(D), tok(ML_WIDTH), pl.BlockSpec((1, MLA_WIDTH, ts), lambda b, s: (b, 0, s)),
                  _const_spec(w_out.shape)]
        + _xattn_specs(D, wq_all, kv, wo_all, layer),
        out_specs=tok(D),
        out_shape=jax.ShapeDtypeStruct(x.shape, F32),
        compiler_params=_params("arbitrary", "arbitrary"),
        name="odd_out",
    )(x, hm, ha, w_out, xg.reshape(1, D), wq_all, *kv, wo_all)


def _pack_odd_weights(w_in, w_uq, w_ukv):
    D = w_in.shape[0]
    c = 4 * ML_WIDTH
    w_g = w_in[:, c:c + 2 * ML_HEADS]
    c += 2 * ML_HEADS
    w_cq = w_in[:, c:c + Q_LORA]
    c += Q_LORA
    w_ckv = w_in[:, c:c + KV_LORA]
    c += KV_LORA
    w_kr = w_in[:, c:c + QK_ROPE]
    half = QK_ROPE // 2
    w_kr_sw = jnp.concatenate([w_kr[:, half:], w_kr[:, :half]], axis=1)
    zl = jnp.zeros((D, QK_NOPE), w_in.dtype)
    zr = jnp.zeros((D, LANES - QK_NOPE - QK_ROPE), w_in.dtype)
    w_main = jnp.concatenate([w_in[:, :4 * ML_WIDTH], w_cq, w_ckv,
                              zl, w_kr, zr, zl, w_kr_sw, zr], axis=1).astype(BF16)
    w_gt = w_g.T.astype(BF16)

    uq = w_uq.reshape(Q_LORA, MLA_HEADS, QK_NOPE + QK_ROPE)
    uq_n, uq_r = uq[..., :QK_NOPE], uq[..., QK_NOPE:]
    zpad = jnp.zeros((Q_LORA, MLA_HEADS, LANES - QK_NOPE - QK_ROPE), w_uq.dtype)
    w_uq_p = jnp.concatenate([uq_n, uq_r, zpad], axis=-1).reshape(Q_LORA, MLA_HEADS * LANES)

    ukv = w_ukv.reshape(KV_LORA, MLA_HEADS, QK_NOPE + V_DIM)
    uk, uv = ukv[..., :QK_NOPE], ukv[..., QK_NOPE:]
    w_uk_p = jnp.concatenate([uk, jnp.zeros((KV_LORA, MLA_HEADS, LANES - QK_NOPE), w_ukv.dtype)],
                             axis=-1).reshape(KV_LORA, MLA_HEADS * LANES)
    w_uvt = uv.reshape(KV_LORA, MLA_WIDTH).T
    return (w_main, w_gt, w_uq_p.astype(BF16), w_uk_p.astype(BF16),
            w_uvt.astype(BF16))


def _rope_inv_freq():
    half = QK_ROPE // 2
    inv = ROPE_THETA ** (-jnp.arange(half, dtype=F32) / half)
    return inv.reshape(half, 1)


TS_EVEN = 1024
TM_FFN = 1024
TH_FFN = 256
TS_ODD = 1024
L_MLSTM = 256
NC_MLSTM = 4
TQ_MLA = 512
NT_MLA = 8
TS_OUT = 1024


def kernel(x, mem, positions, norm_mix_g, norm_xattn_g, mem_norm_g, xattn_wq, xattn_wkv, xattn_wo,
           norm_ffn_g, ffn_w_gate_up, ffn_w_down, ev_w_in, ev_conv_w, ev_pool_w, ev_pool_scale,
           ev_w_out, od_w_in, od_gate_bias, od_ml_norm_g, od_q_norm_g, od_kv_norm_g, od_w_uq,
           od_w_ukv, od_w_out, final_norm_g):
    depth = norm_mix_g.shape[0]
    B, S, D = x.shape
    pos3 = positions.reshape(B, 1, S)
    invf = _rope_inv_freq()
    wq_all, wkv_all, wo_all = (w.astype(BF16) for w in (xattn_wq, xattn_wkv, xattn_wo))
    w_gu_all, w_d_all = ffn_w_gate_up.astype(BF16), ffn_w_down.astype(BF16)
    for layer in range(depth):
        kv = _mem_kv(mem, mem_norm_g[layer], wkv_all, layer)
        xa = (norm_xattn_g[layer], wq_all, kv, wo_all, layer)
        if layer % 2 == 0:
            e = layer // 2
            x = _even_mixer(x, norm_mix_g[layer], ev_w_in[e].astype(BF16), ev_conv_w[e],
                            ev_pool_w[e].astype(BF16), ev_pool_scale[e], ev_w_out[e].astype(BF16),
                            *xa, ts=TS_EVEN)
        else:
            o = layer // 2
            w_main, w_gt, w_uq_p, w_uk_p, w_uvt = _pack_odd_weights(
                od_w_in[o], od_w_uq[o], od_w_ukv[o])
            qm, km, vm, om, gates_t, qp, kp, vt = _odd_proj(
                x, pos3, norm_mix_g[layer], w_main, w_gt, od_gate_bias[o].reshape(2 * ML_HEADS, 1),
                od_q_norm_g[o].reshape(1, Q_LORA), od_kv_norm_g[o].reshape(1, KV_LORA),
                w_uq_p, w_uk_p, w_uvt, invf, ts=TS_ODD)
            hm = _mlstm(qm, km, vm, om, gates_t, od_ml_norm_g[o], L=L_MLSTM, nc=NC_MLSTM)
            ha = _mla_attention(qp, kp, vt, tq=TQ_MLA, nt=NT_MLA)
            x = _odd_out(x, hm, ha, od_w_out[o].astype(BF16), *xa, ts=TS_OUT)
        x = _swiglu(x, norm_ffn_g[layer], w_gu_all, w_d_all, final_norm_g, layer,
                    tm=TM_FFN, th=TH_FFN, final_norm=(layer == depth - 1))
    return x
```

```python
import functools

import jax
import jax.numpy as jnp
from jax import lax
from jax.experimental import pallas as pl
from jax.experimental.pallas import tpu as pltpu

F32 = jnp.float32
BF16 = jnp.bfloat16

EPS = 1e-6
ROPE_THETA = 10000.0
LANES = 128
SUBLANES = 8
BF16_ROWS = 16
VMEM_LIMIT_BYTES = 56 * 1024 * 1024
NEG_BIG = -1e30
LOG2_E = 1.4426950408889634

SC_WIDTH = 512
CONV_K = 3
POOL_WINDOWS = (2, 4, 8, 16)
POOL_GROUP = 128
HALO = 2 * max(POOL_WINDOWS)
ML_HEADS = 4
ML_HEAD_DIM = 128
ML_WIDTH = ML_HEADS * ML_HEAD_DIM
MLA_HEADS = 8
QK_NOPE = 64
QK_ROPE = 32
V_DIM = 64
Q_LORA = 384
KV_LORA = 256
MLA_WIDTH = MLA_HEADS * V_DIM
XA_HEADS = 4
XA_HEAD_DIM = 128
XA_WIDTH = XA_HEADS * XA_HEAD_DIM


def _params(*sem):
    return pltpu.CompilerParams(dimension_semantics=sem, vmem_limit_bytes=VMEM_LIMIT_BYTES)


def _const_spec(shape):
    nd = len(shape)
    return pl.BlockSpec(shape, lambda *_: (0,) * nd, pipeline_mode=pl.Buffered(1))


def _layer_spec(stacked, layer):
    return pl.BlockSpec((None,) + stacked.shape[1:], lambda *_: (layer, 0, 0),
                        pipeline_mode=pl.Buffered(1))


def _rms(x, g):
    ms = jnp.mean(x * x, axis=-1, keepdims=True)
    return x * lax.rsqrt(ms + EPS) * g


def _dot(a, b):
    return jnp.dot(a, b, preferred_element_type=F32)


def _dot_nt(a, b):
    return lax.dot_general(a, b, (((1,), (1,)), ((), ())), preferred_element_type=F32)


def _even_kernel(x_ref, g_ref, win_ref, convw_ref, poolw_ref, pscale_ref, wout_ref,
                 xg_ref, wq_ref, kt_ref, v_ref, wo_ref, o_ref, ubuf, xbbuf, tmp_a, tmp_b, *, ts):
    si = pl.program_id(1)

    @pl.when(si == 0)
    def _():
        ubuf[0:HALO, :] = jnp.zeros((HALO, SC_WIDTH), F32)
        xbbuf[0:HALO, :] = jnp.zeros((HALO, SC_WIDTH), F32)

    x = x_ref[0]
    h = _rms(x, g_ref[...]).astype(BF16)
    z = _dot(h, win_ref[...])
    g_b = z[:, 0:SC_WIDTH]
    u = z[:, SC_WIDTH:2 * SC_WIDTH] * z[:, 2 * SC_WIDTH:3 * SC_WIDTH]
    xb = z[:, 3 * SC_WIDTH:4 * SC_WIDTH]
    ubuf[HALO:HALO + ts, :] = u
    xbbuf[HALO:HALO + ts, :] = xb

    cw = convw_ref[...]
    conv = cw[2:3, :] * u
    for k in range(CONV_K - 1):
        back = CONV_K - 1 - k
        conv = conv + cw[k:k + 1, :] * ubuf[HALO - back:HALO - back + ts, :]
    ya = g_b * conv

    def window_level(src, c0, dst, r0, shift):
        w2 = (src[r0:HALO + ts, c0:c0 + POOL_GROUP]
              + src[r0 - shift:HALO + ts - shift, c0:c0 + POOL_GROUP])
        if dst is not None:
            dst[r0:HALO + ts, :] = w2
        return w2

    t_idx = si * ts + lax.broadcasted_iota(jnp.int32, (ts, POOL_GROUP), 0)
    ys = [ya.astype(BF16)]
    for gi, w in enumerate(POOL_WINDOWS):
        lo = gi * POOL_GROUP
        xg = xb[:, lo:lo + POOL_GROUP]
        src, c0, shift, level = xbbuf, lo, 1, 1
        while 2 * shift < w:
            dst = tmp_a if src is not tmp_a else tmp_b
            window_level(src, c0, dst, SUBLANES * level, shift)
            src, c0, shift, level = dst, 0, 2 * shift, level + 1
        acc = window_level(src, c0, None, HALO, shift)
        cnt = jnp.minimum(t_idx + 1, w).astype(F32)
        d = (acc / cnt - xg).astype(BF16)
        yg = _dot(d, poolw_ref[gi]) * pscale_ref[:, lo:lo + POOL_GROUP]
        ys.append(yg.astype(BF16))

    ubuf[0:HALO, :] = ubuf[ts:ts + HALO, :]
    xbbuf[0:HALO, :] = xbbuf[ts:ts + HALO, :]

    mix = jnp.concatenate(ys, axis=-1)
    x1 = x + _dot(mix, wout_ref[...])
    o_ref[0] = _xattn_body(x1, xg_ref, wq_ref, kt_ref, v_ref, wo_ref)


def _even_mixer(x, g, w_in, conv_w, pool_w, pool_scale, w_out, xg, wq_all, kv, wo_all, layer, *, ts):
    B, S, D = x.shape
    ts = min(ts, S)
    assert S % ts == 0 and D == 2 * SC_WIDTH and w_in.shape == (D, 4 * SC_WIDTH)
    kern = functools.partial(_even_kernel, ts=ts)
    return pl.pallas_call(
        kern,
        grid=(B, S // ts),
        in_specs=[
            pl.BlockSpec((1, ts, D), lambda b, s: (b, s, 0)),
            _const_spec((1, D)),
            _const_spec(w_in.shape),
            _const_spec(conv_w.shape),
            _const_spec(pool_w.shape),
            _const_spec((1, SC_WIDTH)),
            _const_spec(w_out.shape),
        ] + _xattn_specs(D, wq_all, kv, wo_all, layer),
        out_specs=pl.BlockSpec((1, ts, D), lambda b, s: (b, s, 0)),
        out_shape=jax.ShapeDtypeStruct(x.shape, F32),
        scratch_shapes=[pltpu.VMEM((HALO + ts, SC_WIDTH), F32),
                        pltpu.VMEM((HALO + ts, SC_WIDTH), F32),
                        pltpu.VMEM((HALO + ts, POOL_GROUP), F32),
                        pltpu.VMEM((HALO + ts, POOL_GROUP), F32)],
        compiler_params=_params("arbitrary", "arbitrary"),
        name="even_mixer",
    )(x, g.reshape(1, D), w_in, conv_w, pool_w, pool_scale.reshape(1, SC_WIDTH), w_out,
      xg.reshape(1, D), wq_all, *kv, wo_all)


def _memkv_kernel(mem_ref, g_ref, wkv_ref, kt_ref, v_ref):
    h = _rms(mem_ref[0], g_ref[...]).astype(BF16)
    kv = _dot(h, wkv_ref[...])
    kt_ref[0] = kv[:, :XA_WIDTH].T.astype(BF16)
    v_ref[0] = kv[:, XA_WIDTH:].astype(BF16)


def _mem_kv(mem, g, wkv_all, layer):
    B, M, D = mem.shape
    return pl.pallas_call(
        _memkv_kernel,
        grid=(B,),
        in_specs=[pl.BlockSpec((1, M, D), lambda b: (b, 0, 0)),
                  _const_spec((1, D)), _layer_spec(wkv_all, layer)],
        out_specs=(pl.BlockSpec((1, XA_WIDTH, M), lambda b: (b, 0, 0)),
                   pl.BlockSpec((1, M, XA_WIDTH), lambda b: (b, 0, 0))),
        out_shape=(jax.ShapeDtypeStruct((B, XA_WIDTH, M), BF16),
                   jax.ShapeDtypeStruct((B, M, XA_WIDTH), BF16)),
        compiler_params=_params("arbitrary"),
        name="mem_kv",
    )(mem, g.reshape(1, D), wkv_all)


def _xattn_body(x, g_ref, wq_ref, kt_ref, v_ref, wo_ref):
    h = _rms(x, g_ref[...]).astype(BF16)
    q = (_dot(h, wq_ref[...]) * (XA_HEAD_DIM ** -0.5)).astype(BF16)
    ones_v = jnp.ones((v_ref.shape[1], XA_HEAD_DIM), BF16)
    outs = []
    for hd in range(XA_HEADS):
        lo = hd * XA_HEAD_DIM
        s = _dot(q[:, lo:lo + XA_HEAD_DIM], kt_ref[0, lo:lo + XA_HEAD_DIM, :])
        p = jnp.exp(s - jnp.max(s, axis=-1, keepdims=True)).astype(BF16)
        v_aug = jnp.concatenate([v_ref[0, :, lo:lo + XA_HEAD_DIM], ones_v], axis=1)
        o = _dot(p, v_aug)
        outs.append((o[:, :XA_HEAD_DIM] / o[:, XA_HEAD_DIM:]).astype(BF16))
    a = jnp.concatenate(outs, axis=-1)
    return x + _dot(a, wo_ref[...])


def _xattn_specs(D, wq_all, kv, wo_all, layer):
    kt, v = kv
    return [_const_spec((1, D)), _layer_spec(wq_all, layer),
            pl.BlockSpec((1,) + kt.shape[1:], lambda b, s: (b, 0, 0)),
            pl.BlockSpec((1,) + v.shape[1:], lambda b, s: (b, 0, 0)), _layer_spec(wo_all, layer)]


def _swiglu_kernel(x_ref, g_ref, wgu_ref, wd_ref, fg_ref, o_ref, *, hidden, th, final_norm):
    x = x_ref[...]
    h = _rms(x, g_ref[...]).astype(BF16)
    acc = x
    for j in range(hidden // th):
        gate = _dot(h, wgu_ref[:, j * th:(j + 1) * th])
        up = _dot(h, wgu_ref[:, hidden + j * th:hidden + (j + 1) * th])
        act = (gate * jax.nn.sigmoid(gate) * up).astype(BF16)
        acc = acc + _dot(act, wd_ref[j * th:(j + 1) * th, :])
    if final_norm:
        acc = _rms(acc, fg_ref[...])
    o_ref[...] = acc


def _swiglu(x, g, w_gu_all, w_d_all, final_g, layer, *, tm, th, final_norm):
    B, S, D = x.shape
    T = B * S
    tm = min(tm, T)
    hidden = w_d_all.shape[1]
    assert T % tm == 0 and hidden % th == 0 and w_gu_all.shape[2] == 2 * hidden
    kern = functools.partial(_swiglu_kernel, hidden=hidden, th=th, final_norm=final_norm)
    out = pl.pallas_call(
        kern,
        grid=(T // tm,),
        in_specs=[
            pl.BlockSpec((tm, D), lambda i: (i, 0)),
            _const_spec((1, D)),
            _layer_spec(w_gu_all, layer),
            _layer_spec(w_d_all, layer),
            _const_spec((1, D)),
        ],
        out_specs=pl.BlockSpec((tm, D), lambda i: (i, 0)),
        out_shape=jax.ShapeDtypeStruct((T, D), F32),
        compiler_params=_params("arbitrary"),
        name="swiglu",
    )(x.reshape(T, D), g.reshape(1, D), w_gu_all, w_d_all, final_g.reshape(1, D))
    return out.reshape(B, S, D)


def _odd_proj_kernel(x_ref, pos_ref, g_ref, wmain_ref, wgt_ref, gbias_ref,
                     qg_ref, kvg_ref, wuq_ref, wuk_ref, wuvt_ref, invf_ref,
                     qm_ref, km_ref, vm_ref, om_ref, gates_ref, qp_ref, kp_ref, vt_ref):
    x = x_ref[0]
    h = _rms(x, g_ref[...]).astype(BF16)
    c0 = 4 * ML_WIDTH

    def ml_cols(i):
        return _dot(h, wmain_ref[:, i * ML_WIDTH:(i + 1) * ML_WIDTH])

    zl = _dot(h, wmain_ref[:, c0:])
    c_q = zl[:, 0:Q_LORA]
    c_kv = zl[:, Q_LORA:Q_LORA + KV_LORA]
    kr = zl[:, Q_LORA + KV_LORA:Q_LORA + KV_LORA + LANES]
    gates_ref[0] = _dot_nt(wgt_ref[...], h) + gbias_ref[...]

    half = QK_ROPE // 2
    ang_t = invf_ref[...] * pos_ref[0].astype(F32)
    cos_c, sin_c = jnp.cos(ang_t), jnp.sin(ang_t)
    ts = ang_t.shape[1]
    pad_lo = jnp.zeros((QK_NOPE, ts), F32)
    pad_hi = jnp.zeros((LANES - QK_NOPE - QK_ROPE, ts), F32)
    cos_tt = jnp.concatenate([pad_lo + 1.0, cos_c, cos_c, pad_hi], axis=0)
    sin_tt = jnp.concatenate([pad_lo, -sin_c, sin_c, pad_hi], axis=0)
    scale = (QK_NOPE + QK_ROPE) ** -0.5 * LOG2_E
    cos_q, sin_q = cos_tt * scale, sin_tt * scale

    def rotate(t, cos_tab, sin_tab):
        r0, r1, r2 = QK_NOPE, QK_NOPE + half, QK_NOPE + QK_ROPE
        t_sw = jnp.concatenate([t[:r0], t[r1:r2], t[r0:r1], t[r2:]], axis=0)
        return t * cos_tab + t_sw * sin_tab

    qm_ref[0] = ml_cols(0).astype(BF16)
    km_ref[0] = (ml_cols(1) * (ML_HEAD_DIM ** -0.5)).astype(BF16)

    cqn = _rms(c_q, qg_ref[...]).astype(BF16)
    qa = _dot(cqn, wuq_ref[...])
    ckn = _rms(c_kv, kvg_ref[...]).astype(BF16)
    kn = _dot(ckn, wuk_ref[...])
    vt_ref[0] = _dot_nt(wuvt_ref[...], ckn).astype(BF16)
    k_rope = rotate(kr.T, cos_tt, sin_tt).T

    def rope_heads(first, last):
        for hd in range(first, last):
            lo = hd * LANES
            qp_ref[0, lo:lo + LANES, :] = rotate(qa[:, lo:lo + LANES].T, cos_q, sin_q).astype(BF16)
            kp_ref[0, :, lo:lo + LANES] = (kn[:, lo:lo + LANES] + k_rope).astype(BF16)

    vm_ref[0] = ml_cols(2).astype(BF16)
    rope_heads(0, MLA_HEADS // 2)
    om_ref[0] = ml_cols(3).astype(BF16)
    rope_heads(MLA_HEADS // 2, MLA_HEADS)


def _odd_proj(x, pos3, g, w_main, w_gt, gbias, q_g, kv_g, w_uq_p, w_uk_p, w_uvt, invf, *, ts):
    B, S, D = x.shape
    ts = min(ts, S)
    assert S % ts == 0
    tok = lambda n: pl.BlockSpec((1, ts, n), lambda b, s: (b, s, 0))
    out_shapes = (
        jax.ShapeDtypeStruct((B, S, ML_WIDTH), BF16),
        jax.ShapeDtypeStruct((B, S, ML_WIDTH), BF16),
        jax.ShapeDtypeStruct((B, S, ML_WIDTH), BF16),
        jax.ShapeDtypeStruct((B, S, ML_WIDTH), BF16),
        jax.ShapeDtypeStruct((B, 2 * ML_HEADS, S), F32),
        jax.ShapeDtypeStruct((B, MLA_HEADS * LANES, S), BF16),
        jax.ShapeDtypeStruct((B, S, MLA_HEADS * LANES), BF16),
        jax.ShapeDtypeStruct((B, MLA_WIDTH, S), BF16),
    )
    return pl.pallas_call(
        _odd_proj_kernel,
        grid=(B, S // ts),
        in_specs=[
            tok(D),
            pl.BlockSpec((1, 1, ts), lambda b, s: (b, 0, s)),
            _const_spec((1, D)),
            _const_spec(w_main.shape),
            _const_spec(w_gt.shape),
            _const_spec(gbias.shape),
            _const_spec(q_g.shape),
            _const_spec(kv_g.shape),
            _const_spec(w_uq_p.shape),
            _const_spec(w_uk_p.shape),
            _const_spec(w_uvt.shape),
            _const_spec(invf.shape),
        ],
        out_specs=(tok(ML_WIDTH), tok(ML_WIDTH), tok(ML_WIDTH), tok(ML_WIDTH),
                   pl.BlockSpec((1, 2 * ML_HEADS, ts), lambda b, s: (b, 0, s)),
                   pl.BlockSpec((1, MLA_HEADS * LANES, ts), lambda b, s: (b, 0, s)),
                   tok(MLA_HEADS * LANES),
                   pl.BlockSpec((1, MLA_WIDTH, ts), lambda b, s: (b, 0, s))),
        out_shape=out_shapes,
        compiler_params=_params("arbitrary", "arbitrary"),
        name="odd_proj",
    )(x, pos3, g.reshape(1, D), w_main, w_gt, gbias, q_g, kv_g, w_uq_p, w_uk_p, w_uvt, invf)


def _split3(a):
    hi = a.astype(BF16)
    r1 = a - hi.astype(F32)
    mid = r1.astype(BF16)
    lo = (r1 - mid.astype(F32)).astype(BF16)
    return hi, mid, lo


def _log2_sigmoid(x):
    return (jnp.minimum(x, 0.0) - jnp.log(1.0 + jnp.exp(-jnp.abs(x)))) * LOG2_E


def _mlstm_kernel(q_ref, k_ref, v_ref, o_ref, gt_ref, ng_ref, h_ref, c_scr, m_scr, *, L, nc):
    ci = pl.program_id(1)

    @pl.when(ci == 0)
    def _():
        c_scr[...] = jnp.zeros(c_scr.shape, F32)
        m_scr[...] = jnp.zeros(m_scr.shape, F32)

    row = lax.broadcasted_iota(jnp.int32, (L, L), 0)
    col = lax.broadcasted_iota(jnp.int32, (L, L), 1)
    causal = col <= row
    ones_v = jnp.ones((L, LANES), BF16)
    reps = L // LANES

    def widen(a):
        return jnp.concatenate([a] * reps, axis=1)

    for ch, hd in [(ch, hd) for ch in range(nc) for hd in range(ML_HEADS)]:
        tok = slice(ch * L, (ch + 1) * L)
        if hd == 0:
            gt = gt_ref[0, :, tok]
            lf_t = _log2_sigmoid(gt)
            b_t = sum(_dot(t, (row <= col).astype(BF16)) for t in _split3(lf_t))
            b_c = jnp.concatenate([b_t, jnp.zeros((LANES - 2 * ML_HEADS, L), F32)], axis=0).T
        lo = hd * ML_HEAD_DIM
        fg = ML_HEADS + hd
        q = q_ref[0, tok, lo:lo + ML_HEAD_DIM]
        k = k_ref[0, tok, lo:lo + ML_HEAD_DIM]
        v_aug = jnp.concatenate([v_ref[0, tok, lo:lo + ML_HEAD_DIM], ones_v], axis=1)
        g_row = gt[hd:hd + 1, :] * LOG2_E - b_t[fg:fg + 1, :]
        gtot = jnp.sum(lf_t[fg:fg + 1, :], axis=1, keepdims=True)
        b_col = jnp.broadcast_to(b_c[:, fg:fg + 1], (L, LANES))

        m_prev = m_scr[hd]
        c_prev = c_scr[hd]

        g_mask = jnp.where(causal, g_row, NEG_BIG)
        gmax = jnp.max(g_mask, axis=1, keepdims=True)
        h_col = jnp.maximum(m_prev, jnp.broadcast_to(gmax, (L, LANES)))
        w_inter = jnp.exp2(m_prev - h_col)
        sw = jnp.exp2(g_mask - widen(h_col)) * _dot_nt(q, k)
        out = (jnp.concatenate([w_inter, w_inter], axis=1) * _dot(q, c_prev.astype(BF16))
               + _dot(sw.astype(BF16), v_aug))
        den = jnp.maximum(jnp.abs(out[:, ML_HEAD_DIM:]), jnp.exp2(-(b_col + h_col)))
        hc = out[:, :ML_HEAD_DIM] / den

        og = jax.nn.sigmoid(o_ref[0, tok, lo:lo + ML_HEAD_DIM].astype(F32))
        hg = og * hc
        ms = jnp.mean(hg * hg, axis=1, keepdims=True)
        h_ref[0, tok, lo:lo + ML_HEAD_DIM] = (
            hg * lax.rsqrt(ms + EPS) * ng_ref[:, lo:lo + ML_HEAD_DIM]).astype(BF16)

        m_prev1 = m_prev[:, 0:1]
        a_row = gtot + g_row
        m_loc = jnp.max(a_row, axis=1, keepdims=True)
        m_new = jnp.maximum(gtot + m_prev1, m_loc)
        s_old = jnp.exp2(gtot + m_prev1 - m_new)
        w_loc = jnp.exp2(a_row - m_new)
        kw_t = (k.T.astype(F32) * w_loc).astype(BF16)
        c_scr[hd] = s_old * c_prev + _dot(kw_t, v_aug)
        m_scr[hd] = jnp.broadcast_to(m_new, (1, LANES))


def _mlstm(q, k, v, o, gates_t, norm_g, *, L, nc):
    B, S, W = q.shape
    L = min(L, S)
    nc = min(nc, S // L)
    assert S % (nc * L) == 0 and L % LANES == 0 and W == ML_WIDTH
    kern = functools.partial(_mlstm_kernel, L=L, nc=nc)
    tok = pl.BlockSpec((1, nc * L, W), lambda b, c: (b, c, 0))
    return pl.pallas_call(
        kern,
        grid=(B, S // (nc * L)),
        in_specs=[tok, tok, tok, tok,
                  pl.BlockSpec((1, 2 * ML_HEADS, nc * L), lambda b, c: (b, 0, c)),
                  _const_spec((1, W))],
        out_specs=tok,
        out_shape=jax.ShapeDtypeStruct((B, S, W), BF16),
        scratch_shapes=[pltpu.VMEM((ML_HEADS, ML_HEAD_DIM, 2 * ML_HEAD_DIM), F32),
                        pltpu.VMEM((ML_HEADS, 1, LANES), F32)],
        compiler_params=_params("arbitrary", "arbitrary"),
        name="mlstm",
    )(q, k, v, o, gates_t, norm_g.reshape(1, W))


def _mla_kernel(qt_ref, k_ref, vt_ref, o_ref, s00, s01, s10, s11, *, tq, nt):
    g = pl.program_id(2)
    heads = 2
    pieces = 2
    rows = tq // pieces
    hk = tq // 2
    s_scr = ((s00, s01), (s10, s11))
    ones_rows = jnp.ones((BF16_ROWS, tq), BF16)

    def run_tile(t, qi, sa, sb):
        lanes = slice(t * tq, (t + 1) * tq)

        def score_piece(j, slot, hh, pc):
            start = pl.multiple_of(j * tq + pc * rows, rows)
            kb = k_ref[0, pl.ds(start, rows), hh * LANES:(hh + 1) * LANES]
            qt = qt_ref[0, hh * LANES:(hh + 1) * LANES, lanes]
            s_scr[slot][hh][pc * rows:(pc + 1) * rows, :] = _dot(kb, qt)

        def stage_exp(j, slot, hh, m_new, acc, half):
            start = pl.multiple_of(j * tq + half * hk, hk)
            vtb = jnp.concatenate([vt_ref[0, hh * V_DIM:(hh + 1) * V_DIM, pl.ds(start, hk)],
                                   ones_rows[:, :hk]], axis=0)
            p = jnp.exp2(s_scr[slot][hh][half * hk:(half + 1) * hk, :] - m_new).astype(BF16)
            return acc + _dot(vtb, p)

        def block(j, slot, carry, nxt):
            new = []
            for hh in range(heads):
                m, acc = carry[hh]
                score_piece(nxt[0], nxt[1], hh, 0)
                m_new = jnp.maximum(m, jnp.max(s_scr[slot][hh][...], axis=0, keepdims=True))
                score_piece(nxt[0], nxt[1], hh, 1)
                acc = stage_exp(j, slot, hh, m_new, jnp.exp2(m - m_new) * acc, 0)
                acc = stage_exp(j, slot, hh, m_new, acc, 1)
                new.append((m_new, acc))
            return tuple(new)

        def diag_block(slot, carry):
            start = pl.multiple_of(qi * tq, tq)
            kk = lax.broadcasted_iota(jnp.int32, (hk, tq), 0)
            qq = lax.broadcasted_iota(jnp.int32, (hk, tq), 1)
            kb = lax.broadcasted_iota(jnp.int32, (hk, hk), 0)
            qb = lax.broadcasted_iota(jnp.int32, (hk, hk), 1)
            hs = range(heads)
            top = [jnp.where(kk <= qq, s_scr[slot][hh][0:hk, :], NEG_BIG) for hh in hs]
            bot = [jnp.where(kb <= qb, s_scr[slot][hh][hk:, hk:], NEG_BIG) for hh in hs]
            neg = jnp.full((hk, hk), NEG_BIG, F32)
            m_new = [jnp.maximum(carry[hh][0], jnp.max(
                jnp.maximum(top[hh], jnp.concatenate([neg, bot[hh]], axis=1)), axis=0, keepdims=True))
                for hh in hs]
            vt_aug = [jnp.concatenate([vt_ref[0, hh * V_DIM:(hh + 1) * V_DIM, pl.ds(start, tq)],
                                       ones_rows], axis=0) for hh in hs]
            p_top = [jnp.exp2(top[hh] - m_new[hh]).astype(BF16) for hh in hs]
            p_bot = [jnp.exp2(bot[hh] - jnp.broadcast_to(m_new[hh], (hk, tq))[:, hk:]).astype(BF16)
                     for hh in hs]
            outs = []
            for hh in hs:
                m, acc = carry[hh]
                acc = jnp.exp2(m - m_new[hh]) * acc + _dot(vt_aug[hh][:, :hk], p_top[hh])
                hi = acc[:, hk:] + _dot(vt_aug[hh][:, hk:], p_bot[hh])
                acc = jnp.concatenate([acc[:, :hk], hi], axis=1)
                outs.append(acc[:V_DIM] / acc[V_DIM:V_DIM + 1])
            return jnp.concatenate(outs, axis=0)

        for hh in range(heads):
            for pc in range(pieces):
                score_piece(0, sa, hh, pc)

        def body(i, c):
            c = block(2 * i, sa, c, (2 * i + 1, sb))
            return block(2 * i + 1, sb, c, (2 * i + 2, sa))

        init = tuple((jnp.full((1, tq), NEG_BIG, F32), jnp.zeros((V_DIM + BF16_ROWS, tq), F32))
                     for _ in range(heads))
        carry = lax.fori_loop(0, qi // 2, body, init)
        if t % 2 == 0:
            ot = diag_block(sa, carry)
        else:
            ot = diag_block(sb, block(qi - 1, sa, carry, (qi, sb)))
        o_ref[0, :, lanes] = ot.astype(BF16)

    slot_pairs = ((0, 1), (1, 0), (1, 0), (0, 1))
    for t in range(nt):
        run_tile(t, nt * g + t, *slot_pairs[t % 4])


def _mla_attention(qpt, kp, vt, *, tq, nt):
    B, S, _ = kp.shape
    tq = min(tq, S // 2)
    nt = min(nt, S // tq)
    assert nt % 2 == 0 and S % (nt * tq) == 0
    pairs = MLA_HEADS // 2
    kern = functools.partial(_mla_kernel, tq=tq, nt=nt)
    return pl.pallas_call(
        kern,
        grid=(B, pairs, S // (nt * tq)),
        in_specs=[
            pl.BlockSpec((1, 2 * LANES, nt * tq), lambda b, p, i: (b, p, i)),
            pl.BlockSpec((1, S, 2 * LANES), lambda b, p, i: (b, 0, p)),
            pl.BlockSpec((1, 2 * V_DIM, S), lambda b, p, i: (b, p, 0)),
        ],
        out_specs=pl.BlockSpec((1, LANES, nt * tq), lambda b, p, i: (b, p, i)),
        out_shape=jax.ShapeDtypeStruct((B, MLA_WIDTH, S), BF16),
        scratch_shapes=[pltpu.VMEM((tq, tq), F32) for _ in range(4)],
        compiler_params=_params("arbitrary", "arbitrary", "arbitrary"),
        name="mla_attention",
    )(qpt, kp, vt)


def _odd_out_kernel(x_ref, hm_ref, ha_ref, wout_ref, xg_ref, wq_ref, kt_ref, v_ref, wo_ref, o_ref):
    mix = jnp.concatenate([hm_ref[0], ha_ref[0].T], axis=-1)
    x1 = x_ref[0] + _dot(mix, wout_ref[...])
    o_ref[0] = _xattn_body(x1, xg_ref, wq_ref, kt_ref, v_ref, wo_ref)


def _odd_out(x, hm, ha, w_out, xg, wq_all, kv, wo_all, layer, *, ts):
    B, S, D = x.shape
    ts = min(ts, S)
    assert S % ts == 0
    tok = lambda n: pl.BlockSpec((1, ts, n), lambda b, s: (b, s, 0))
    return pl.pallas_call(
        _odd_out_kernel,
        grid=(B, S // ts),
        in_specs=[tok(D), tok(ML_WIDTH), pl.BlockSpec((1, MLA_WIDTH, ts), lambda b, s: (b, 0, s)),
                  _const_spec(w_out.shape)]
        + _xattn_specs(D, wq_all, kv, wo_all, layer),
        out_specs=tok(D),
        out_shape=jax.ShapeDtypeStruct(x.shape, F32),
        compiler_params=_params("arbitrary", "arbitrary"),
        name="odd_out",
    )(x, hm, ha, w_out, xg.reshape(1, D), wq_all, *kv, wo_all)


def _pack_odd_weights(w_in, w_uq, w_ukv):
    D = w_in.shape[0]
    c = 4 * ML_WIDTH
    w_g = w_in[:, c:c + 2 * ML_HEADS]
    c += 2 * ML_HEADS
    w_cq = w_in[:, c:c + Q_LORA]
    c += Q_LORA
    w_ckv = w_in[:, c:c + KV_LORA]
    c += KV_LORA
    w_kr = w_in[:, c:c + QK_ROPE]
    zl = jnp.zeros((D, QK_NOPE), w_in.dtype)
    zr = jnp.zeros((D, LANES - QK_NOPE - QK_ROPE), w_in.dtype)
    w_main = jnp.concatenate([w_in[:, :4 * ML_WIDTH], w_cq, w_ckv,
                              zl, w_kr, zr], axis=1).astype(BF16)
    w_gt = w_g.T.astype(BF16)

    uq = w_uq.reshape(Q_LORA, MLA_HEADS, QK_NOPE + QK_ROPE)
    uq_n, uq_r = uq[..., :QK_NOPE], uq[..., QK_NOPE:]
    zpad = jnp.zeros((Q_LORA, MLA_HEADS, LANES - QK_NOPE - QK_ROPE), w_uq.dtype)
    w_uq_p = jnp.concatenate([uq_n, uq_r, zpad], axis=-1).reshape(Q_LORA, MLA_HEADS * LANES)

    ukv = w_ukv.reshape(KV_LORA, MLA_HEADS, QK_NOPE + V_DIM)
    uk, uv = ukv[..., :QK_NOPE], ukv[..., QK_NOPE:]
    w_uk_p = jnp.concatenate([uk, jnp.zeros((KV_LORA, MLA_HEADS, LANES - QK_NOPE), w_ukv.dtype)],
                             axis=-1).reshape(KV_LORA, MLA_HEADS * LANES)
    w_uvt = uv.reshape(KV_LORA, MLA_WIDTH).T
    return (w_main, w_gt, w_uq_p.astype(BF16), w_uk_p.astype(BF16),
            w_uvt.astype(BF16))


def _rope_inv_freq():
    half = QK_ROPE // 2
    inv = ROPE_THETA ** (-jnp.arange(half, dtype=F32) / half)
    return inv.reshape(half, 1)


TS_EVEN = 1024
TM_FFN = 1024
TH_FFN = 256
TS_ODD = 1024
L_MLSTM = 256
NC_MLSTM = 4
TQ_MLA = 512
NT_MLA = 8
TS_OUT = 1024


def kernel(x, mem, positions, norm_mix_g, norm_xattn_g, mem_norm_g, xattn_wq, xattn_wkv, xattn_wo,
           norm_ffn_g, ffn_w_gate_up, ffn_w_down, ev_w_in, ev_conv_w, ev_pool_w, ev_pool_scale,
           ev_w_out, od_w_in, od_gate_bias, od_ml_norm_g, od_q_norm_g, od_kv_norm_g, od_w_uq,
           od_w_ukv, od_w_out, final_norm_g):
    depth = norm_mix_g.shape[0]
    B, S, D = x.shape
    pos3 = positions.reshape(B, 1, S)
    invf = _rope_inv_freq()
    wq_all, wkv_all, wo_all = (w.astype(BF16) for w in (xattn_wq, xattn_wkv, xattn_wo))
    w_gu_all, w_d_all = ffn_w_gate_up.astype(BF16), ffn_w_down.astype(BF16)
    for layer in range(depth):
        kv = _mem_kv(mem, mem_norm_g[layer], wkv_all, layer)
        xa = (norm_xattn_g[layer], wq_all, kv, wo_all, layer)
        if layer % 2 == 0:
            e = layer // 2
            x = _even_mixer(x, norm_mix_g[layer], ev_w_in[e].astype(BF16), ev_conv_w[e],
                            ev_pool_w[e].astype(BF16), ev_pool_scale[e], ev_w_out[e].astype(BF16),
                            *xa, ts=TS_EVEN)
        else:
            o = layer // 2
            w_main, w_gt, w_uq_p, w_uk_p, w_uvt = _pack_odd_weights(
                od_w_in[o], od_w_uq[o], od_w_ukv[o])
            qm, km, vm, om, gates_t, qp, kp, vt = _odd_proj(
                x, pos3, norm_mix_g[layer], w_main, w_gt, od_gate_bias[o].reshape(2 * ML_HEADS, 1),
                od_q_norm_g[o].reshape(1, Q_LORA), od_kv_norm_g[o].reshape(1, KV_LORA),
                w_uq_p, w_uk_p, w_uvt, invf, ts=TS_ODD)
            hm = _mlstm(qm, km, vm, om, gates_t, od_ml_norm_g[o], L=L_MLSTM, nc=NC_MLSTM)
            ha = _mla_attention(qp, kp, vt, tq=TQ_MLA, nt=NT_MLA)
            x = _odd_out(x, hm, ha, od_w_out[o].astype(BF16), *xa, ts=TS_OUT)
        x = _swiglu(x, norm_ffn_g[layer], w_gu_all, w_d_all, final_norm_g, layer,
                    tm=TM_FFN, th=TH_FFN, final_norm=(layer == depth - 1))
    return x
```

```python
import functools

import jax
import jax.numpy as jnp
from jax import lax
from jax.experimental import pallas as pl
from jax.experimental.pallas import tpu as pltpu

F32 = jnp.float32
BF16 = jnp.bfloat16

EPS = 1e-6
ROPE_THETA = 10000.0
LANES = 128
SUBLANES = 8
BF16_ROWS = 16
VMEM_LIMIT_BYTES = 56 * 1024 * 1024
NEG_BIG = -1e30
LOG2_E = 1.4426950408889634

SC_WIDTH = 512
CONV_K = 3
POOL_WINDOWS = (2, 4, 8, 16)
POOL_GROUP = 128
HALO = 2 * max(POOL_WINDOWS)
ML_HEADS = 4
ML_HEAD_DIM = 128
ML_WIDTH = ML_HEADS * ML_HEAD_DIM
MLA_HEADS = 8
QK_NOPE = 64
QK_ROPE = 32
V_DIM = 64
Q_LORA = 384
KV_LORA = 256
MLA_WIDTH = MLA_HEADS * V_DIM
XA_HEADS = 4
XA_HEAD_DIM = 128
XA_WIDTH = XA_HEADS * XA_HEAD_DIM


def _params(*sem):
    return pltpu.CompilerParams(dimension_semantics=sem, vmem_limit_bytes=VMEM_LIMIT_BYTES)


def _const_spec(shape):
    nd = len(shape)
    return pl.BlockSpec(shape, lambda *_: (0,) * nd, pipeline_mode=pl.Buffered(1))


def _layer_spec(stacked, layer):
    return pl.BlockSpec((None,) + stacked.shape[1:], lambda *_: (layer, 0, 0),
                        pipeline_mode=pl.Buffered(1))


def _rms(x, g):
    ms = jnp.mean(x * x, axis=-1, keepdims=True)
    return x * lax.rsqrt(ms + EPS) * g


def _dot(a, b):
    return jnp.dot(a, b, preferred_element_type=F32)


def _dot_nt(a, b):
    return lax.dot_general(a, b, (((1,), (1,)), ((), ())), preferred_element_type=F32)


def _even_kernel(x_ref, g_ref, win_ref, convw_ref, poolw_ref, pscale_ref, wout_ref,
                 xg_ref, wq_ref, kt_ref, v_ref, wo_ref, o_ref, ubuf, xbbuf, tmp_a, tmp_b, *, ts):
    si = pl.program_id(1)

    @pl.when(si == 0)
    def _():
        ubuf[0:HALO, :] = jnp.zeros((HALO, SC_WIDTH), F32)
        xbbuf[0:HALO, :] = jnp.zeros((HALO, SC_WIDTH), F32)

    x = x_ref[0]
    h = _rms(x, g_ref[...]).astype(BF16)
    z = _dot(h, win_ref[...])
    g_b = z[:, 0:SC_WIDTH]
    u = z[:, SC_WIDTH:2 * SC_WIDTH] * z[:, 2 * SC_WIDTH:3 * SC_WIDTH]
    xb = z[:, 3 * SC_WIDTH:4 * SC_WIDTH]
    ubuf[HALO:HALO + ts, :] = u
    xbbuf[HALO:HALO + ts, :] = xb

    cw = convw_ref[...]
    conv = cw[2:3, :] * u
    for k in range(CONV_K - 1):
        back = CONV_K - 1 - k
        conv = conv + cw[k:k + 1, :] * ubuf[HALO - back:HALO - back + ts, :]
    ya = g_b * conv

    def window_level(src, c0, dst, r0, shift):
        w2 = (src[r0:HALO + ts, c0:c0 + POOL_GROUP]
              + src[r0 - shift:HALO + ts - shift, c0:c0 + POOL_GROUP])
        if dst is not None:
            dst[r0:HALO + ts, :] = w2
        return w2

    t_idx = si * ts + lax.broadcasted_iota(jnp.int32, (ts, POOL_GROUP), 0)
    ys = [ya.astype(BF16)]
    for gi, w in enumerate(POOL_WINDOWS):
        lo = gi * POOL_GROUP
        xg = xb[:, lo:lo + POOL_GROUP]
        src, c0, shift, level = xbbuf, lo, 1, 1
        while 2 * shift < w:
            dst = tmp_a if src is not tmp_a else tmp_b
            window_level(src, c0, dst, SUBLANES * level, shift)
            src, c0, shift, level = dst, 0, 2 * shift, level + 1
        acc = window_level(src, c0, None, HALO, shift)
        cnt = jnp.minimum(t_idx + 1, w).astype(F32)
        d = (acc / cnt - xg).astype(BF16)
        yg = _dot(d, poolw_ref[gi]) * pscale_ref[:, lo:lo + POOL_GROUP]
        ys.append(yg.astype(BF16))

    ubuf[0:HALO, :] = ubuf[ts:ts + HALO, :]
    xbbuf[0:HALO, :] = xbbuf[ts:ts + HALO, :]

    mix = jnp.concatenate(ys, axis=-1)
    x1 = x + _dot(mix, wout_ref[...])
    o_ref[0] = _xattn_body(x1, xg_ref, wq_ref, kt_ref, v_ref, wo_ref)


def _even_mixer(x, g, w_in, conv_w, pool_w, pool_scale, w_out, xg, wq_all, kv, wo_all, layer, *, ts):
    B, S, D = x.shape
    ts = min(ts, S)
    assert S % ts == 0 and D == 2 * SC_WIDTH and w_in.shape == (D, 4 * SC_WIDTH)
    kern = functools.partial(_even_kernel, ts=ts)
    return pl.pallas_call(
        kern,
        grid=(B, S // ts),
        in_specs=[
            pl.BlockSpec((1, ts, D), lambda b, s: (b, s, 0)),
            _const_spec((1, D)),
            _const_spec(w_in.shape),
            _const_spec(conv_w.shape),
            _const_spec(pool_w.shape),
            _const_spec((1, SC_WIDTH)),
            _const_spec(w_out.shape),
        ] + _xattn_specs(D, wq_all, kv, wo_all, layer),
        out_specs=pl.BlockSpec((1, ts, D), lambda b, s: (b, s, 0)),
        out_shape=jax.ShapeDtypeStruct(x.shape, F32),
        scratch_shapes=[pltpu.VMEM((HALO + ts, SC_WIDTH), F32),
                        pltpu.VMEM((HALO + ts, SC_WIDTH), F32),
                        pltpu.VMEM((HALO + ts, POOL_GROUP), F32),
                        pltpu.VMEM((HALO + ts, POOL_GROUP), F32)],
        compiler_params=_params("arbitrary", "arbitrary"),
        name="even_mixer",
    )(x, g.reshape(1, D), w_in, conv_w, pool_w, pool_scale.reshape(1, SC_WIDTH), w_out,
      xg.reshape(1, D), wq_all, *kv, wo_all)


def _memkv_kernel(mem_ref, g_ref, wkv_ref, kt_ref, v_ref):
    h = _rms(mem_ref[0], g_ref[...]).astype(BF16)
    kv = _dot(h, wkv_ref[...])
    kt_ref[0] = kv[:, :XA_WIDTH].T.astype(BF16)
    v_ref[0] = kv[:, XA_WIDTH:].astype(BF16)


def _mem_kv(mem, g, wkv_all, layer):
    B, M, D = mem.shape
    return pl.pallas_call(
        _memkv_kernel,
        grid=(B,),
        in_specs=[pl.BlockSpec((1, M, D), lambda b: (b, 0, 0)),
                  _const_spec((1, D)), _layer_spec(wkv_all, layer)],
        out_specs=(pl.BlockSpec((1, XA_WIDTH, M), lambda b: (b, 0, 0)),
                   pl.BlockSpec((1, M, XA_WIDTH), lambda b: (b, 0, 0))),
        out_shape=(jax.ShapeDtypeStruct((B, XA_WIDTH, M), BF16),
                   jax.ShapeDtypeStruct((B, M, XA_WIDTH), BF16)),
        compiler_params=_params("arbitrary"),
        name="mem_kv",
    )(mem, g.reshape(1, D), wkv_all)


def _xattn_body(x, g_ref, wq_ref, kt_ref, v_ref, wo_ref):
    h = _rms(x, g_ref[...]).astype(BF16)
    q = (_dot(h, wq_ref[...]) * (XA_HEAD_DIM ** -0.5)).astype(BF16)
    ones_v = jnp.ones((v_ref.shape[1], XA_HEAD_DIM), BF16)
    outs = []
    for hd in range(XA_HEADS):
        lo = hd * XA_HEAD_DIM
        s = _dot(q[:, lo:lo + XA_HEAD_DIM], kt_ref[0, lo:lo + XA_HEAD_DIM, :])
        p = jnp.exp(s - jnp.max(s, axis=-1, keepdims=True)).astype(BF16)
        v_aug = jnp.concatenate([v_ref[0, :, lo:lo + XA_HEAD_DIM], ones_v], axis=1)
        o = _dot(p, v_aug)
        outs.append((o[:, :XA_HEAD_DIM] / o[:, XA_HEAD_DIM:]).astype(BF16))
    a = jnp.concatenate(outs, axis=-1)
    return x + _dot(a, wo_ref[...])


def _xattn_specs(D, wq_all, kv, wo_all, layer):
    kt, v = kv
    return [_const_spec((1, D)), _layer_spec(wq_all, layer),
            pl.BlockSpec((1,) + kt.shape[1:], lambda b, s: (b, 0, 0)),
            pl.BlockSpec((1,) + v.shape[1:], lambda b, s: (b, 0, 0)), _layer_spec(wo_all, layer)]


def _swiglu_kernel(x_ref, g_ref, wgu_ref, wd_ref, fg_ref, o_ref, *, hidden, th, final_norm):
    x = x_ref[...]
    h = _rms(x, g_ref[...]).astype(BF16)
    acc = x
    for j in range(hidden // th):
        gate = _dot(h, wgu_ref[:, j * th:(j + 1) * th])
        up = _dot(h, wgu_ref[:, hidden + j * th:hidden + (j + 1) * th])
        act = (gate * jax.nn.sigmoid(gate) * up).astype(BF16)
        acc = acc + _dot(act, wd_ref[j * th:(j + 1) * th, :])
    if final_norm:
        acc = _rms(acc, fg_ref[...])
    o_ref[...] = acc


def _swiglu(x, g, w_gu_all, w_d_all, final_g, layer, *, tm, th, final_norm):
    B, S, D = x.shape
    T = B * S
    tm = min(tm, T)
    hidden = w_d_all.shape[1]
    assert T % tm == 0 and hidden % th == 0 and w_gu_all.shape[2] == 2 * hidden
    kern = functools.partial(_swiglu_kernel, hidden=hidden, th=th, final_norm=final_norm)
    out = pl.pallas_call(
        kern,
        grid=(T // tm,),
        in_specs=[
            pl.BlockSpec((tm, D), lambda i: (i, 0)),
            _const_spec((1, D)),
            _layer_spec(w_gu_all, layer),
            _layer_spec(w_d_all, layer),
            _const_spec((1, D)),
        ],
        out_specs=pl.BlockSpec((tm, D), lambda i: (i, 0)),
        out_shape=jax.ShapeDtypeStruct((T, D), F32),
        compiler_params=_params("arbitrary"),
        name="swiglu",
    )(x.reshape(T, D), g.reshape(1, D), w_gu_all, w_d_all, final_g.reshape(1, D))
    return out.reshape(B, S, D)


def _odd_proj_kernel(x_ref, pos_ref, g_ref, wmain_ref, wgt_ref, gbias_ref,
                     qg_ref, kvg_ref, wuqt_ref, wukt_ref, wuvt_ref, invf_ref,
                     qm_ref, km_ref, vm_ref, om_ref, gates_ref, qp_ref, kp_ref, vt_ref):
    x = x_ref[0]
    h = _rms(x, g_ref[...]).astype(BF16)
    c0 = 4 * ML_WIDTH

    def ml_cols(i):
        return _dot(h, wmain_ref[:, i * ML_WIDTH:(i + 1) * ML_WIDTH])

    zl = _dot(h, wmain_ref[:, c0:])
    c_q = zl[:, 0:Q_LORA]
    c_kv = zl[:, Q_LORA:Q_LORA + KV_LORA]
    kr = zl[:, Q_LORA + KV_LORA:Q_LORA + KV_LORA + LANES]
    gates_ref[0] = _dot_nt(wgt_ref[...], h) + gbias_ref[...]

    half = QK_ROPE // 2
    ang_t = invf_ref[...] * pos_ref[0].astype(F32)
    cos_c, sin_c = jnp.cos(ang_t), jnp.sin(ang_t)
    ts = ang_t.shape[1]
    pad_lo = jnp.zeros((QK_NOPE, ts), F32)
    pad_hi = jnp.zeros((LANES - QK_NOPE - QK_ROPE, ts), F32)
    cos_tt = jnp.concatenate([pad_lo + 1.0, cos_c, cos_c, pad_hi], axis=0)
    sin_tt = jnp.concatenate([pad_lo, -sin_c, sin_c, pad_hi], axis=0)
    scale = (QK_NOPE + QK_ROPE) ** -0.5 * LOG2_E
    cos_q, sin_q = cos_tt * scale, sin_tt * scale

    def rotate(t, cos_tab, sin_tab):
        r0, r1, r2 = QK_NOPE, QK_NOPE + half, QK_NOPE + QK_ROPE
        t_sw = jnp.concatenate([t[:r0], t[r1:r2], t[r0:r1], t[r2:]], axis=0)
        return t * cos_tab + t_sw * sin_tab

    qm_ref[0] = ml_cols(0).astype(BF16)
    km_ref[0] = (ml_cols(1) * (ML_HEAD_DIM ** -0.5)).astype(BF16)

    cqn = _rms(c_q, qg_ref[...]).astype(BF16)
    qa_t = _dot_nt(wuqt_ref[...], cqn)
    ckn = _rms(c_kv, kvg_ref[...]).astype(BF16)
    kn_t = _dot_nt(wukt_ref[...], ckn)
    vt_ref[0] = _dot_nt(wuvt_ref[...], ckn).astype(BF16)
    k_rope_t = rotate(kr.T, cos_tt, sin_tt)[QK_NOPE:QK_NOPE + QK_ROPE]
    qk_dim = QK_NOPE + QK_ROPE

    def rope_heads(first, last):
        for hd in range(first, last):
            lo = hd * LANES
            q_blk = jnp.concatenate([qa_t[hd * qk_dim:(hd + 1) * qk_dim], pad_hi], axis=0)
            qp_ref[0, lo:lo + LANES, :] = rotate(q_blk, cos_q, sin_q).astype(BF16)
            k_blk = jnp.concatenate([kn_t[hd * QK_NOPE:(hd + 1) * QK_NOPE], k_rope_t, pad_hi], axis=0)
            kp_ref[0, :, lo:lo + LANES] = k_blk.T.astype(BF16)

    vm_ref[0] = ml_cols(2).astype(BF16)
    rope_heads(0, MLA_HEADS // 2)
    om_ref[0] = ml_cols(3).astype(BF16)
    rope_heads(MLA_HEADS // 2, MLA_HEADS)


def _odd_proj(x, pos3, g, w_main, w_gt, gbias, q_g, kv_g, w_uqt, w_ukt, w_uvt, invf, *, ts):
    B, S, D = x.shape
    ts = min(ts, S)
    assert S % ts == 0
    tok = lambda n: pl.BlockSpec((1, ts, n), lambda b, s: (b, s, 0))
    out_shapes = (
        jax.ShapeDtypeStruct((B, S, ML_WIDTH), BF16),
        jax.ShapeDtypeStruct((B, S, ML_WIDTH), BF16),
        jax.ShapeDtypeStruct((B, S, ML_WIDTH), BF16),
        jax.ShapeDtypeStruct((B, S, ML_WIDTH), BF16),
        jax.ShapeDtypeStruct((B, 2 * ML_HEADS, S), F32),
        jax.ShapeDtypeStruct((B, MLA_HEADS * LANES, S), BF16),
        jax.ShapeDtypeStruct((B, S, MLA_HEADS * LANES), BF16),
        jax.ShapeDtypeStruct((B, MLA_WIDTH, S), BF16),
    )
    return pl.pallas_call(
        _odd_proj_kernel,
        grid=(B, S // ts),
        in_specs=[
            tok(D),
            pl.BlockSpec((1, 1, ts), lambda b, s: (b, 0, s)),
            _const_spec((1, D)),
            _const_spec(w_main.shape),
            _const_spec(w_gt.shape),
            _const_spec(gbias.shape),
            _const_spec(q_g.shape),
            _const_spec(kv_g.shape),
            _const_spec(w_uqt.shape),
            _const_spec(w_ukt.shape),
            _const_spec(w_uvt.shape),
            _const_spec(invf.shape),
        ],
        out_specs=(tok(ML_WIDTH), tok(ML_WIDTH), tok(ML_WIDTH), tok(ML_WIDTH),
                   pl.BlockSpec((1, 2 * ML_HEADS, ts), lambda b, s: (b, 0, s)),
                   pl.BlockSpec((1, MLA_HEADS * LANES, ts), lambda b, s: (b, 0, s)),
                   tok(MLA_HEADS * LANES),
                   pl.BlockSpec((1, MLA_WIDTH, ts), lambda b, s: (b, 0, s))),
        out_shape=out_shapes,
        compiler_params=_params("arbitrary", "arbitrary"),
        name="odd_proj",
    )(x, pos3, g.reshape(1, D), w_main, w_gt, gbias, q_g, kv_g, w_uqt, w_ukt, w_uvt, invf)


def _split3(a):
    hi = a.astype(BF16)
    r1 = a - hi.astype(F32)
    mid = r1.astype(BF16)
    lo = (r1 - mid.astype(F32)).astype(BF16)
    return hi, mid, lo


def _log2_sigmoid(x):
    return (jnp.minimum(x, 0.0) - jnp.log(1.0 + jnp.exp(-jnp.abs(x)))) * LOG2_E


def _mlstm_kernel(q_ref, k_ref, v_ref, o_ref, gt_ref, ng_ref, h_ref, c_scr, m_scr, *, L, nc):
    ci = pl.program_id(1)

    @pl.when(ci == 0)
    def _():
        c_scr[...] = jnp.zeros(c_scr.shape, F32)
        m_scr[...] = jnp.zeros(m_scr.shape, F32)

    row = lax.broadcasted_iota(jnp.int32, (L, L), 0)
    col = lax.broadcasted_iota(jnp.int32, (L, L), 1)
    causal = col <= row
    ones_v = jnp.ones((L, LANES), BF16)
    reps = L // LANES

    def widen(a):
        return jnp.concatenate([a] * reps, axis=1)

    for ch, hd in [(ch, hd) for ch in range(nc) for hd in range(ML_HEADS)]:
        tok = slice(ch * L, (ch + 1) * L)
        if hd == 0:
            gt = gt_ref[0, :, tok]
            lf_t = _log2_sigmoid(gt)
            b_t = sum(_dot(t, (row <= col).astype(BF16)) for t in _split3(lf_t))
            b_c = jnp.concatenate([b_t, jnp.zeros((LANES - 2 * ML_HEADS, L), F32)], axis=0).T
        lo = hd * ML_HEAD_DIM
        fg = ML_HEADS + hd
        q = q_ref[0, tok, lo:lo + ML_HEAD_DIM]
        k = k_ref[0, tok, lo:lo + ML_HEAD_DIM]
        v_aug = jnp.concatenate([v_ref[0, tok, lo:lo + ML_HEAD_DIM], ones_v], axis=1)
        g_row = gt[hd:hd + 1, :] * LOG2_E - b_t[fg:fg + 1, :]
        gtot = jnp.sum(lf_t[fg:fg + 1, :], axis=1, keepdims=True)
        b_col = jnp.broadcast_to(b_c[:, fg:fg + 1], (L, LANES))

        m_prev = m_scr[hd]
        c_prev = c_scr[hd]

        g_mask = jnp.where(causal, g_row, NEG_BIG)
        gmax = jnp.max(g_mask, axis=1, keepdims=True)
        h_col = jnp.maximum(m_prev, jnp.broadcast_to(gmax, (L, LANES)))
        w_inter = jnp.exp2(m_prev - h_col)
        sw = jnp.exp2(g_mask - widen(h_col)) * _dot_nt(q, k)
        out = (jnp.concatenate([w_inter, w_inter], axis=1) * _dot(q, c_prev.astype(BF16))
               + _dot(sw.astype(BF16), v_aug))
        den = jnp.maximum(jnp.abs(out[:, ML_HEAD_DIM:]), jnp.exp2(-(b_col + h_col)))
        hc = out[:, :ML_HEAD_DIM] / den

        og = jax.nn.sigmoid(o_ref[0, tok, lo:lo + ML_HEAD_DIM].astype(F32))
        hg = og * hc
        ms = jnp.mean(hg * hg, axis=1, keepdims=True)
        h_ref[0, tok, lo:lo + ML_HEAD_DIM] = (
            hg * lax.rsqrt(ms + EPS) * ng_ref[:, lo:lo + ML_HEAD_DIM]).astype(BF16)

        m_prev1 = m_prev[:, 0:1]
        a_row = gtot + g_row
        m_loc = jnp.max(a_row, axis=1, keepdims=True)
        m_new = jnp.maximum(gtot + m_prev1, m_loc)
        s_old = jnp.exp2(gtot + m_prev1 - m_new)
        w_loc = jnp.exp2(a_row - m_new)
        kw_t = (k.T.astype(F32) * w_loc).astype(BF16)
        c_scr[hd] = s_old * c_prev + _dot(kw_t, v_aug)
        m_scr[hd] = jnp.broadcast_to(m_new, (1, LANES))


def _mlstm(q, k, v, o, gates_t, norm_g, *, L, nc):
    B, S, W = q.shape
    L = min(L, S)
    nc = min(nc, S // L)
    assert S % (nc * L) == 0 and L % LANES == 0 and W == ML_WIDTH
    kern = functools.partial(_mlstm_kernel, L=L, nc=nc)
    tok = pl.BlockSpec((1, nc * L, W), lambda b, c: (b, c, 0))
    return pl.pallas_call(
        kern,
        grid=(B, S // (nc * L)),
        in_specs=[tok, tok, tok, tok,
                  pl.BlockSpec((1, 2 * ML_HEADS, nc * L), lambda b, c: (b, 0, c)),
                  _const_spec((1, W))],
        out_specs=tok,
        out_shape=jax.ShapeDtypeStruct((B, S, W), BF16),
        scratch_shapes=[pltpu.VMEM((ML_HEADS, ML_HEAD_DIM, 2 * ML_HEAD_DIM), F32),
                        pltpu.VMEM((ML_HEADS, 1, LANES), F32)],
        compiler_params=_params("arbitrary", "arbitrary"),
        name="mlstm",
    )(q, k, v, o, gates_t, norm_g.reshape(1, W))


def _mla_kernel(qt_ref, k_ref, vt_ref, o_ref, s00, s01, s10, s11, *, tq, nt):
    g = pl.program_id(2)
    heads = 2
    pieces = 2
    rows = tq // pieces
    hk = tq // 2
    s_scr = ((s00, s01), (s10, s11))
    ones_rows = jnp.ones((BF16_ROWS, tq), BF16)

    def run_tile(t, qi, sa, sb):
        lanes = slice(t * tq, (t + 1) * tq)

        def score_piece(j, slot, hh, pc):
            start = pl.multiple_of(j * tq + pc * rows, rows)
            kb = k_ref[0, pl.ds(start, rows), hh * LANES:(hh + 1) * LANES]
            qt = qt_ref[0, hh * LANES:(hh + 1) * LANES, lanes]
            s_scr[slot][hh][pc * rows:(pc + 1) * rows, :] = _dot(kb, qt)

        def stage_exp(j, slot, hh, m_new, acc, half):
            start = pl.multiple_of(j * tq + half * hk, hk)
            vtb = jnp.concatenate([vt_ref[0, hh * V_DIM:(hh + 1) * V_DIM, pl.ds(start, hk)],
                                   ones_rows[:, :hk]], axis=0)
            p = jnp.exp2(s_scr[slot][hh][half * hk:(half + 1) * hk, :] - m_new).astype(BF16)
            return acc + _dot(vtb, p)

        def block(j, slot, carry, nxt):
            new = []
            for hh in range(heads):
                m, acc = carry[hh]
                score_piece(nxt[0], nxt[1], hh, 0)
                m_new = jnp.maximum(m, jnp.max(s_scr[slot][hh][...], axis=0, keepdims=True))
                score_piece(nxt[0], nxt[1], hh, 1)
                acc = stage_exp(j, slot, hh, m_new, jnp.exp2(m - m_new) * acc, 0)
                acc = stage_exp(j, slot, hh, m_new, acc, 1)
                new.append((m_new, acc))
            return tuple(new)

        def diag_block(slot, carry):
            start = pl.multiple_of(qi * tq, tq)
            kk = lax.broadcasted_iota(jnp.int32, (hk, tq), 0)
            qq = lax.broadcasted_iota(jnp.int32, (hk, tq), 1)
            kb = lax.broadcasted_iota(jnp.int32, (hk, hk), 0)
            qb = lax.broadcasted_iota(jnp.int32, (hk, hk), 1)
            hs = range(heads)
            top = [jnp.where(kk <= qq, s_scr[slot][hh][0:hk, :], NEG_BIG) for hh in hs]
            bot = [jnp.where(kb <= qb, s_scr[slot][hh][hk:, hk:], NEG_BIG) for hh in hs]
            neg = jnp.full((hk, hk), NEG_BIG, F32)
            m_new = [jnp.maximum(carry[hh][0], jnp.max(
                jnp.maximum(top[hh], jnp.concatenate([neg, bot[hh]], axis=1)), axis=0, keepdims=True))
                for hh in hs]
            vt_aug = [jnp.concatenate([vt_ref[0, hh * V_DIM:(hh + 1) * V_DIM, pl.ds(start, tq)],
                                       ones_rows], axis=0) for hh in hs]
            p_top = [jnp.exp2(top[hh] - m_new[hh]).astype(BF16) for hh in hs]
            p_bot = [jnp.exp2(bot[hh] - jnp.broadcast_to(m_new[hh], (hk, tq))[:, hk:]).astype(BF16)
                     for hh in hs]
            outs = []
            for hh in hs:
                m, acc = carry[hh]
                acc = jnp.exp2(m - m_new[hh]) * acc + _dot(vt_aug[hh][:, :hk], p_top[hh])
                hi = acc[:, hk:] + _dot(vt_aug[hh][:, hk:], p_bot[hh])
                acc = jnp.concatenate([acc[:, :hk], hi], axis=1)
                outs.append(acc[:V_DIM] / acc[V_DIM:V_DIM + 1])
            return jnp.concatenate(outs, axis=0)

        for hh in range(heads):
            for pc in range(pieces):
                score_piece(0, sa, hh, pc)

        def body(i, c):
            c = block(2 * i, sa, c, (2 * i + 1, sb))
            return block(2 * i + 1, sb, c, (2 * i + 2, sa))

        init = tuple((jnp.full((1, tq), NEG_BIG, F32), jnp.zeros((V_DIM + BF16_ROWS, tq), F32))
                     for _ in range(heads))
        carry = lax.fori_loop(0, qi // 2, body, init)
        if t % 2 == 0:
            ot = diag_block(sa, carry)
        else:
            ot = diag_block(sb, block(qi - 1, sa, carry, (qi, sb)))
        o_ref[0, :, lanes] = ot.astype(BF16)

    slot_pairs = ((0, 1), (1, 0), (1, 0), (0, 1))
    for t in range(nt):
        run_tile(t, nt * g + t, *slot_pairs[t % 4])


def _mla_attention(qpt, kp, vt, *, tq, nt):
    B, S, _ = kp.shape
    tq = min(tq, S // 2)
    nt = min(nt, S // tq)
    assert nt % 2 == 0 and S % (nt * tq) == 0
    pairs = MLA_HEADS // 2
    kern = functools.partial(_mla_kernel, tq=tq, nt=nt)
    return pl.pallas_call(
        kern,
        grid=(B, pairs, S // (nt * tq)),
        in_specs=[
            pl.BlockSpec((1, 2 * LANES, nt * tq), lambda b, p, i: (b, p, i)),
            pl.BlockSpec((1, S, 2 * LANES), lambda b, p, i: (b, 0, p)),
            pl.BlockSpec((1, 2 * V_DIM, S), lambda b, p, i: (b, p, 0)),
        ],
        out_specs=pl.BlockSpec((1, LANES, nt * tq), lambda b, p, i: (b, p, i)),
        out_shape=jax.ShapeDtypeStruct((B, MLA_WIDTH, S), BF16),
        scratch_shapes=[pltpu.VMEM((tq, tq), F32) for _ in range(4)],
        compiler_params=_params("arbitrary", "arbitrary", "arbitrary"),
        name="mla_attention",
    )(qpt, kp, vt)


def _odd_out_kernel(x_ref, hm_ref, ha_ref, wout_ref, xg_ref, wq_ref, kt_ref, v_ref, wo_ref, o_ref):
    mix = jnp.concatenate([hm_ref[0], ha_ref[0].T], axis=-1)
    x1 = x_ref[0] + _dot(mix, wout_ref[...])
    o_ref[0] = _xattn_body(x1, xg_ref, wq_ref, kt_ref, v_ref, wo_ref)


def _odd_out(x, hm, ha, w_out, xg, wq_all, kv, wo_all, layer, *, ts):
    B, S, D = x.shape
    ts = min(ts, S)
    assert S % ts == 0
    tok = lambda n: pl.BlockSpec((1, ts, n), lambda b, s: (b, s, 0))
    return pl.pallas_call(
        _odd_out_kernel,
        grid=(B, S // ts),
        in_specs=[tok(D), tok(ML_WIDTH), pl.BlockSpec((1, MLA_WIDTH, ts), lambda b, s: (b, 0, s)),
                  _const_spec(w_out.shape)]
        + _xattn_specs(D, wq_all, kv, wo_all, layer),
        out_specs=tok(D),
        out_shape=jax.ShapeDtypeStruct(x.shape, F32),
        compiler_params=_params("arbitrary", "arbitrary"),
        name="odd_out",
    )(x, hm, ha, w_out, xg.reshape(1, D), wq_all, *kv, wo_all)


def _pack_odd_weights(w_in, w_uq, w_ukv):
    D = w_in.shape[0]
    c = 4 * ML_WIDTH
    w_g = w_in[:, c:c + 2 * ML_HEADS]
    c += 2 * ML_HEADS
    w_cq = w_in[:, c:c + Q_LORA]
    c += Q_LORA
    w_ckv = w_in[:, c:c + KV_LORA]
    c += KV_LORA
    w_kr = w_in[:, c:c + QK_ROPE]
    zl = jnp.zeros((D, QK_NOPE), w_in.dtype)
    zr = jnp.zeros((D, LANES - QK_NOPE - QK_ROPE), w_in.dtype)
    w_main = jnp.concatenate([w_in[:, :4 * ML_WIDTH], w_cq, w_ckv,
                              zl, w_kr, zr], axis=1).astype(BF16)
    w_gt = w_g.T.astype(BF16)

    w_uqt = w_uq.T
    ukv = w_ukv.reshape(KV_LORA, MLA_HEADS, QK_NOPE + V_DIM)
    w_ukt = ukv[..., :QK_NOPE].reshape(KV_LORA, MLA_HEADS * QK_NOPE).T
    w_uvt = ukv[..., QK_NOPE:].reshape(KV_LORA, MLA_WIDTH).T
    return w_main, w_gt, w_uqt.astype(BF16), w_ukt.astype(BF16), w_uvt.astype(BF16)


def _rope_inv_freq():
    half = QK_ROPE // 2
    inv = ROPE_THETA ** (-jnp.arange(half, dtype=F32) / half)
    return inv.reshape(half, 1)


TS_EVEN = 1024
TM_FFN = 1024
TH_FFN = 256
TS_ODD = 1024
L_MLSTM = 256
NC_MLSTM = 4
TQ_MLA = 512
NT_MLA = 8
TS_OUT = 1024


def kernel(x, mem, positions, norm_mix_g, norm_xattn_g, mem_norm_g, xattn_wq, xattn_wkv, xattn_wo,
           norm_ffn_g, ffn_w_gate_up, ffn_w_down, ev_w_in, ev_conv_w, ev_pool_w, ev_pool_scale,
           ev_w_out, od_w_in, od_gate_bias, od_ml_norm_g, od_q_norm_g, od_kv_norm_g, od_w_uq,
           od_w_ukv, od_w_out, final_norm_g):
    depth = norm_mix_g.shape[0]
    B, S, D = x.shape
    pos3 = positions.reshape(B, 1, S)
    invf = _rope_inv_freq()
    wq_all, wkv_all, wo_all = (w.astype(BF16) for w in (xattn_wq, xattn_wkv, xattn_wo))
    w_gu_all, w_d_all = ffn_w_gate_up.astype(BF16), ffn_w_down.astype(BF16)
    for layer in range(depth):
        kv = _mem_kv(mem, mem_norm_g[layer], wkv_all, layer)
        xa = (norm_xattn_g[layer], wq_all, kv, wo_all, layer)
        if layer % 2 == 0:
            e = layer // 2
            x = _even_mixer(x, norm_mix_g[layer], ev_w_in[e].astype(BF16), ev_conv_w[e],
                            ev_pool_w[e].astype(BF16), ev_pool_scale[e], ev_w_out[e].astype(BF16),
                            *xa, ts=TS_EVEN)
        else:
            o = layer // 2
            w_main, w_gt, w_uqt, w_ukt, w_uvt = _pack_odd_weights(
                od_w_in[o], od_w_uq[o], od_w_ukv[o])
            qm, km, vm, om, gates_t, qp, kp, vt = _odd_proj(
                x, pos3, norm_mix_g[layer], w_main, w_gt, od_gate_bias[o].reshape(2 * ML_HEADS, 1),
                od_q_norm_g[o].reshape(1, Q_LORA), od_kv_norm_g[o].reshape(1, KV_LORA),
                w_uqt, w_ukt, w_uvt, invf, ts=TS_ODD)
            hm = _mlstm(qm, km, vm, om, gates_t, od_ml_norm_g[o], L=L_MLSTM, nc=NC_MLSTM)
            ha = _mla_attention(qp, kp, vt, tq=TQ_MLA, nt=NT_MLA)
            x = _odd_out(x, hm, ha, od_w_out[o].astype(BF16), *xa, ts=TS_OUT)
        x = _swiglu(x, norm_ffn_g[layer], w_gu_all, w_d_all, final_norm_g, layer,
                    tm=TM_FFN, th=TH_FFN, final_norm=(layer == depth - 1))
    return x
```

```python
import functools

import jax
import jax.numpy as jnp
from jax import lax
from jax.experimental import pallas as pl
from jax.experimental.pallas import tpu as pltpu

F32 = jnp.float32
BF16 = jnp.bfloat16

EPS = 1e-6
ROPE_THETA = 10000.0
LANES = 128
SUBLANES = 8
BF16_ROWS = 16
VMEM_LIMIT_BYTES = 56 * 1024 * 1024
NEG_BIG = -1e30
LOG2_E = 1.4426950408889634

SC_WIDTH = 512
CONV_K = 3
POOL_WINDOWS = (2, 4, 8, 16)
POOL_GROUP = 128
HALO = 2 * max(POOL_WINDOWS)
ML_HEADS = 4
ML_HEAD_DIM = 128
ML_WIDTH = ML_HEADS * ML_HEAD_DIM
MLA_HEADS = 8
QK_NOPE = 64
QK_ROPE = 32
V_DIM = 64
Q_LORA = 384
KV_LORA = 256
MLA_WIDTH = MLA_HEADS * V_DIM
XA_HEADS = 4
XA_HEAD_DIM = 128
XA_WIDTH = XA_HEADS * XA_HEAD_DIM


def _params(*sem):
    return pltpu.CompilerParams(dimension_semantics=sem, vmem_limit_bytes=VMEM_LIMIT_BYTES)


def _const_spec(shape):
    nd = len(shape)
    return pl.BlockSpec(shape, lambda *_: (0,) * nd, pipeline_mode=pl.Buffered(1))


def _layer_spec(stacked, layer):
    return pl.BlockSpec((None,) + stacked.shape[1:], lambda *_: (layer, 0, 0),
                        pipeline_mode=pl.Buffered(1))


def _rms(x, g):
    ms = jnp.mean(x * x, axis=-1, keepdims=True)
    return x * lax.rsqrt(ms + EPS) * g


def _dot(a, b):
    return jnp.dot(a, b, preferred_element_type=F32)


def _dot_nt(a, b):
    return lax.dot_general(a, b, (((1,), (1,)), ((), ())), preferred_element_type=F32)


def _even_kernel(x_ref, g_ref, win_ref, convw_ref, poolw_ref, pscale_ref, wout_ref,
                 xg_ref, wq_ref, kt_ref, v_ref, wo_ref, o_ref, ubuf, xbbuf, tmp_a, tmp_b, *, ts):
    si = pl.program_id(1)

    @pl.when(si == 0)
    def _():
        ubuf[0:HALO, :] = jnp.zeros((HALO, SC_WIDTH), F32)
        xbbuf[0:HALO, :] = jnp.zeros((HALO, SC_WIDTH), F32)

    x = x_ref[0]
    h = _rms(x, g_ref[...]).astype(BF16)
    z = _dot(h, win_ref[...])
    g_b = z[:, 0:SC_WIDTH]
    u = z[:, SC_WIDTH:2 * SC_WIDTH] * z[:, 2 * SC_WIDTH:3 * SC_WIDTH]
    xb = z[:, 3 * SC_WIDTH:4 * SC_WIDTH]
    ubuf[HALO:HALO + ts, :] = u
    xbbuf[HALO:HALO + ts, :] = xb

    cw = convw_ref[...]
    conv = cw[2:3, :] * u
    for k in range(CONV_K - 1):
        back = CONV_K - 1 - k
        conv = conv + cw[k:k + 1, :] * ubuf[HALO - back:HALO - back + ts, :]
    ya = g_b * conv

    def window_level(src, c0, dst, r0, shift):
        w2 = (src[r0:HALO + ts, c0:c0 + POOL_GROUP]
              + src[r0 - shift:HALO + ts - shift, c0:c0 + POOL_GROUP])
        if dst is not None:
            dst[r0:HALO + ts, :] = w2
        return w2

    t_idx = si * ts + lax.broadcasted_iota(jnp.int32, (ts, POOL_GROUP), 0)
    ys = [ya.astype(BF16)]
    for gi, w in enumerate(POOL_WINDOWS):
        lo = gi * POOL_GROUP
        xg = xb[:, lo:lo + POOL_GROUP]
        src, c0, shift, level = xbbuf, lo, 1, 1
        while 2 * shift < w:
            dst = tmp_a if src is not tmp_a else tmp_b
            window_level(src, c0, dst, SUBLANES * level, shift)
            src, c0, shift, level = dst, 0, 2 * shift, level + 1
        acc = window_level(src, c0, None, HALO, shift)
        cnt = jnp.minimum(t_idx + 1, w).astype(F32)
        d = (acc / cnt - xg).astype(BF16)
        yg = _dot(d, poolw_ref[gi]) * pscale_ref[:, lo:lo + POOL_GROUP]
        ys.append(yg.astype(BF16))

    ubuf[0:HALO, :] = ubuf[ts:ts + HALO, :]
    xbbuf[0:HALO, :] = xbbuf[ts:ts + HALO, :]

    mix = jnp.concatenate(ys, axis=-1)
    x1 = x + _dot(mix, wout_ref[...])
    o_ref[0] = _xattn_body(x1, xg_ref, wq_ref, kt_ref, v_ref, wo_ref)


def _even_mixer(x, g, w_in, conv_w, pool_w, pool_scale, w_out, xg, wq_all, kv, wo_all, layer, *, ts):
    B, S, D = x.shape
    ts = min(ts, S)
    assert S % ts == 0 and D == 2 * SC_WIDTH and w_in.shape == (D, 4 * SC_WIDTH)
    kern = functools.partial(_even_kernel, ts=ts)
    return pl.pallas_call(
        kern,
        grid=(B, S // ts),
        in_specs=[
            pl.BlockSpec((1, ts, D), lambda b, s: (b, s, 0)),
            _const_spec((1, D)),
            _const_spec(w_in.shape),
            _const_spec(conv_w.shape),
            _const_spec(pool_w.shape),
            _const_spec((1, SC_WIDTH)),
            _const_spec(w_out.shape),
        ] + _xattn_specs(D, wq_all, kv, wo_all, layer),
        out_specs=pl.BlockSpec((1, ts, D), lambda b, s: (b, s, 0)),
        out_shape=jax.ShapeDtypeStruct(x.shape, F32),
        scratch_shapes=[pltpu.VMEM((HALO + ts, SC_WIDTH), F32),
                        pltpu.VMEM((HALO + ts, SC_WIDTH), F32),
                        pltpu.VMEM((HALO + ts, POOL_GROUP), F32),
                        pltpu.VMEM((HALO + ts, POOL_GROUP), F32)],
        compiler_params=_params("arbitrary", "arbitrary"),
        name="even_mixer",
    )(x, g.reshape(1, D), w_in, conv_w, pool_w, pool_scale.reshape(1, SC_WIDTH), w_out,
      xg.reshape(1, D), wq_all, *kv, wo_all)


def _memkv_kernel(mem_ref, g_ref, wkv_ref, kt_ref, v_ref):
    h = _rms(mem_ref[0], g_ref[...]).astype(BF16)
    kv = _dot(h, wkv_ref[...])
    kt_ref[0] = kv[:, :XA_WIDTH].T.astype(BF16)
    v_ref[0] = kv[:, XA_WIDTH:].astype(BF16)


def _mem_kv(mem, g, wkv_all, layer):
    B, M, D = mem.shape
    return pl.pallas_call(
        _memkv_kernel,
        grid=(B,),
        in_specs=[pl.BlockSpec((1, M, D), lambda b: (b, 0, 0)),
                  _const_spec((1, D)), _layer_spec(wkv_all, layer)],
        out_specs=(pl.BlockSpec((1, XA_WIDTH, M), lambda b: (b, 0, 0)),
                   pl.BlockSpec((1, M, XA_WIDTH), lambda b: (b, 0, 0))),
        out_shape=(jax.ShapeDtypeStruct((B, XA_WIDTH, M), BF16),
                   jax.ShapeDtypeStruct((B, M, XA_WIDTH), BF16)),
        compiler_params=_params("arbitrary"),
        name="mem_kv",
    )(mem, g.reshape(1, D), wkv_all)


def _xattn_body(x, g_ref, wq_ref, kt_ref, v_ref, wo_ref):
    h = _rms(x, g_ref[...]).astype(BF16)
    q = (_dot(h, wq_ref[...]) * (XA_HEAD_DIM ** -0.5)).astype(BF16)
    ones_v = jnp.ones((v_ref.shape[1], XA_HEAD_DIM), BF16)
    outs = []
    for hd in range(XA_HEADS):
        lo = hd * XA_HEAD_DIM
        s = _dot(q[:, lo:lo + XA_HEAD_DIM], kt_ref[0, lo:lo + XA_HEAD_DIM, :])
        p = jnp.exp(s - jnp.max(s, axis=-1, keepdims=True)).astype(BF16)
        v_aug = jnp.concatenate([v_ref[0, :, lo:lo + XA_HEAD_DIM], ones_v], axis=1)
        o = _dot(p, v_aug)
        outs.append((o[:, :XA_HEAD_DIM] / o[:, XA_HEAD_DIM:]).astype(BF16))
    a = jnp.concatenate(outs, axis=-1)
    return x + _dot(a, wo_ref[...])


def _xattn_specs(D, wq_all, kv, wo_all, layer):
    kt, v = kv
    return [_const_spec((1, D)), _layer_spec(wq_all, layer),
            pl.BlockSpec((1,) + kt.shape[1:], lambda b, s: (b, 0, 0)),
            pl.BlockSpec((1,) + v.shape[1:], lambda b, s: (b, 0, 0)), _layer_spec(wo_all, layer)]


def _swiglu_kernel(x_ref, g_ref, wgu_ref, wd_ref, fg_ref, o_ref, *, hidden, th, final_norm):
    x = x_ref[...]
    h = _rms(x, g_ref[...]).astype(BF16)
    acc = x
    for j in range(hidden // th):
        gate = _dot(h, wgu_ref[:, j * th:(j + 1) * th])
        up = _dot(h, wgu_ref[:, hidden + j * th:hidden + (j + 1) * th])
        act = (gate * jax.nn.sigmoid(gate) * up).astype(BF16)
        acc = acc + _dot(act, wd_ref[j * th:(j + 1) * th, :])
    if final_norm:
        acc = _rms(acc, fg_ref[...])
    o_ref[...] = acc


def _swiglu(x, g, w_gu_all, w_d_all, final_g, layer, *, tm, th, final_norm):
    B, S, D = x.shape
    T = B * S
    tm = min(tm, T)
    hidden = w_d_all.shape[1]
    assert T % tm == 0 and hidden % th == 0 and w_gu_all.shape[2] == 2 * hidden
    kern = functools.partial(_swiglu_kernel, hidden=hidden, th=th, final_norm=final_norm)
    out = pl.pallas_call(
        kern,
        grid=(T // tm,),
        in_specs=[
            pl.BlockSpec((tm, D), lambda i: (i, 0)),
            _const_spec((1, D)),
            _layer_spec(w_gu_all, layer),
            _layer_spec(w_d_all, layer),
            _const_spec((1, D)),
        ],
        out_specs=pl.BlockSpec((tm, D), lambda i: (i, 0)),
        out_shape=jax.ShapeDtypeStruct((T, D), F32),
        compiler_params=_params("arbitrary"),
        name="swiglu",
    )(x.reshape(T, D), g.reshape(1, D), w_gu_all, w_d_all, final_g.reshape(1, D))
    return out.reshape(B, S, D)


def _odd_proj_kernel(x_ref, pos_ref, g_ref, wmain_ref, wgt_ref, gbias_ref,
                     qg_ref, kvg_ref, wuqt_ref, wukt_ref, wuvt_ref, invf_ref,
                     qm_ref, km_ref, vm_ref, om_ref, gates_ref, qp_ref, kp_ref, vt_ref):
    x = x_ref[0]
    h = _rms(x, g_ref[...]).astype(BF16)
    c0 = 4 * ML_WIDTH

    def ml_cols(i):
        return _dot(h, wmain_ref[:, i * ML_WIDTH:(i + 1) * ML_WIDTH])

    zl = _dot(h, wmain_ref[:, c0:])
    c_q = zl[:, 0:Q_LORA]
    c_kv = zl[:, Q_LORA:Q_LORA + KV_LORA]
    kr = zl[:, Q_LORA + KV_LORA:Q_LORA + KV_LORA + LANES]
    gates_ref[0] = _dot_nt(wgt_ref[...], h) + gbias_ref[...]

    half = QK_ROPE // 2
    ang_t = invf_ref[...] * pos_ref[0].astype(F32)
    cos_c, sin_c = jnp.cos(ang_t), jnp.sin(ang_t)
    ts = ang_t.shape[1]
    pad_lo = jnp.zeros((QK_NOPE, ts), F32)
    pad_hi = jnp.zeros((LANES - QK_NOPE - QK_ROPE, ts), F32)
    cos_tt = jnp.concatenate([pad_lo + 1.0, cos_c, cos_c, pad_hi], axis=0)
    sin_tt = jnp.concatenate([pad_lo, -sin_c, sin_c, pad_hi], axis=0)
    scale = (QK_NOPE + QK_ROPE) ** -0.5 * LOG2_E
    cos_q, sin_q = cos_tt * scale, sin_tt * scale

    def rotate(t, cos_tab, sin_tab):
        r0, r1, r2 = QK_NOPE, QK_NOPE + half, QK_NOPE + QK_ROPE
        t_sw = jnp.concatenate([t[:r0], t[r1:r2], t[r0:r1], t[r2:]], axis=0)
        return t * cos_tab + t_sw * sin_tab

    qm_ref[0] = ml_cols(0).astype(BF16)
    km_ref[0] = (ml_cols(1) * (ML_HEAD_DIM ** -0.5)).astype(BF16)

    cqn = _rms(c_q, qg_ref[...]).astype(BF16)
    qa_t = _dot_nt(wuqt_ref[...], cqn)
    ckn = _rms(c_kv, kvg_ref[...]).astype(BF16)
    kn_t = _dot_nt(wukt_ref[...], ckn)
    vt_ref[0] = _dot_nt(wuvt_ref[...], ckn).astype(BF16)
    k_rope_t = rotate(kr.T, cos_tt, sin_tt)[QK_NOPE:QK_NOPE + QK_ROPE]
    qk_dim = QK_NOPE + QK_ROPE

    def rope_heads(first, last):
        for hd in range(first, last):
            lo = hd * LANES
            q_blk = jnp.concatenate([qa_t[hd * qk_dim:(hd + 1) * qk_dim], pad_hi], axis=0)
            qp_ref[0, lo:lo + LANES, :] = rotate(q_blk, cos_q, sin_q).astype(BF16)
            k_blk = jnp.concatenate([kn_t[hd * QK_NOPE:(hd + 1) * QK_NOPE], k_rope_t, pad_hi], axis=0)
            kp_ref[0, :, lo:lo + LANES] = k_blk.T.astype(BF16)

    vm_ref[0] = ml_cols(2).astype(BF16)
    rope_heads(0, MLA_HEADS // 2)
    om_ref[0] = ml_cols(3).astype(BF16)
    rope_heads(MLA_HEADS // 2, MLA_HEADS)


def _odd_proj(x, pos3, g, w_main, w_gt, gbias, q_g, kv_g, w_uqt, w_ukt, w_uvt, invf, *, ts):
    B, S, D = x.shape
    ts = min(ts, S)
    assert S % ts == 0
    tok = lambda n: pl.BlockSpec((1, ts, n), lambda b, s: (b, s, 0))
    out_shapes = (
        jax.ShapeDtypeStruct((B, S, ML_WIDTH), BF16),
        jax.ShapeDtypeStruct((B, S, ML_WIDTH), BF16),
        jax.ShapeDtypeStruct((B, S, ML_WIDTH), BF16),
        jax.ShapeDtypeStruct((B, S, ML_WIDTH), BF16),
        jax.ShapeDtypeStruct((B, 2 * ML_HEADS, S), F32),
        jax.ShapeDtypeStruct((B, MLA_HEADS * LANES, S), BF16),
        jax.ShapeDtypeStruct((B, S, MLA_HEADS * LANES), BF16),
        jax.ShapeDtypeStruct((B, MLA_WIDTH, S), BF16),
    )
    return pl.pallas_call(
        _odd_proj_kernel,
        grid=(B, S // ts),
        in_specs=[
            tok(D),
            pl.BlockSpec((1, 1, ts), lambda b, s: (b, 0, s)),
            _const_spec((1, D)),
            _const_spec(w_main.shape),
            _const_spec(w_gt.shape),
            _const_spec(gbias.shape),
            _const_spec(q_g.shape),
            _const_spec(kv_g.shape),
            _const_spec(w_uqt.shape),
            _const_spec(w_ukt.shape),
            _const_spec(w_uvt.shape),
            _const_spec(invf.shape),
        ],
        out_specs=(tok(ML_WIDTH), tok(ML_WIDTH), tok(ML_WIDTH), tok(ML_WIDTH),
                   pl.BlockSpec((1, 2 * ML_HEADS, ts), lambda b, s: (b, 0, s)),
                   pl.BlockSpec((1, MLA_HEADS * LANES, ts), lambda b, s: (b, 0, s)),
                   tok(MLA_HEADS * LANES),
                   pl.BlockSpec((1, MLA_WIDTH, ts), lambda b, s: (b, 0, s))),
        out_shape=out_shapes,
        compiler_params=_params("arbitrary", "arbitrary"),
        name="odd_proj",
    )(x, pos3, g.reshape(1, D), w_main, w_gt, gbias, q_g, kv_g, w_uqt, w_ukt, w_uvt, invf)


def _split3(a):
    hi = a.astype(BF16)
    r1 = a - hi.astype(F32)
    mid = r1.astype(BF16)
    lo = (r1 - mid.astype(F32)).astype(BF16)
    return hi, mid, lo


def _log2_sigmoid(x):
    return (jnp.minimum(x, 0.0) - jnp.log(1.0 + jnp.exp(-jnp.abs(x)))) * LOG2_E


def _mlstm_kernel(q_ref, k_ref, v_ref, o_ref, gt_ref, ng_ref, h_ref, c_scr, m_scr, *, L, nc):
    ci = pl.program_id(1)

    @pl.when(ci == 0)
    def _():
        c_scr[...] = jnp.zeros(c_scr.shape, F32)
        m_scr[...] = jnp.zeros(m_scr.shape, F32)

    row = lax.broadcasted_iota(jnp.int32, (L, L), 0)
    col = lax.broadcasted_iota(jnp.int32, (L, L), 1)
    causal = col <= row
    ones_v = jnp.ones((L, LANES), BF16)
    reps = L // LANES

    def widen(a):
        return jnp.concatenate([a] * reps, axis=1)

    for ch, hd in [(ch, hd) for ch in range(nc) for hd in range(ML_HEADS)]:
        tok = slice(ch * L, (ch + 1) * L)
        if hd == 0:
            gt = gt_ref[0, :, tok]
            lf_t = _log2_sigmoid(gt)
            b_t = sum(_dot(t, (row <= col).astype(BF16)) for t in _split3(lf_t))
            b_c = jnp.concatenate([b_t, jnp.zeros((LANES - 2 * ML_HEADS, L), F32)], axis=0).T
        lo = hd * ML_HEAD_DIM
        fg = ML_HEADS + hd
        q = q_ref[0, tok, lo:lo + ML_HEAD_DIM]
        k = k_ref[0, tok, lo:lo + ML_HEAD_DIM]
        v_aug = jnp.concatenate([v_ref[0, tok, lo:lo + ML_HEAD_DIM], ones_v], axis=1)
        g_row = gt[hd:hd + 1, :] * LOG2_E - b_t[fg:fg + 1, :]
        gtot = jnp.sum(lf_t[fg:fg + 1, :], axis=1, keepdims=True)
        b_col = jnp.broadcast_to(b_c[:, fg:fg + 1], (L, LANES))

        m_prev = m_scr[hd]
        c_prev = c_scr[hd]

        g_mask = jnp.where(causal, g_row, NEG_BIG)
        gmax = jnp.max(g_mask, axis=1, keepdims=True)
        h_col = jnp.maximum(m_prev, jnp.broadcast_to(gmax, (L, LANES)))
        w_inter = jnp.exp2(m_prev - h_col)
        sw = jnp.exp2(g_mask - widen(h_col)) * _dot_nt(q, k)
        out = (jnp.concatenate([w_inter, w_inter], axis=1) * _dot(q, c_prev.astype(BF16))
               + _dot(sw.astype(BF16), v_aug))
        den = jnp.maximum(jnp.abs(out[:, ML_HEAD_DIM:]), jnp.exp2(-(b_col + h_col)))
        hc = out[:, :ML_HEAD_DIM] / den

        og = jax.nn.sigmoid(o_ref[0, tok, lo:lo + ML_HEAD_DIM].astype(F32))
        hg = og * hc
        ms = jnp.mean(hg * hg, axis=1, keepdims=True)
        h_ref[0, tok, lo:lo + ML_HEAD_DIM] = (
            hg * lax.rsqrt(ms + EPS) * ng_ref[:, lo:lo + ML_HEAD_DIM]).astype(BF16)

        m_prev1 = m_prev[:, 0:1]
        a_row = gtot + g_row
        m_loc = jnp.max(a_row, axis=1, keepdims=True)
        m_new = jnp.maximum(gtot + m_prev1, m_loc)
        s_old = jnp.exp2(gtot + m_prev1 - m_new)
        w_loc = jnp.exp2(a_row - m_new)
        kw_t = (k.T.astype(F32) * w_loc).astype(BF16)
        c_scr[hd] = s_old * c_prev + _dot(kw_t, v_aug)
        m_scr[hd] = jnp.broadcast_to(m_new, (1, LANES))


def _mlstm(q, k, v, o, gates_t, norm_g, *, L, nc):
    B, S, W = q.shape
    L = min(L, S)
    nc = min(nc, S // L)
    assert S % (nc * L) == 0 and L % LANES == 0 and W == ML_WIDTH
    kern = functools.partial(_mlstm_kernel, L=L, nc=nc)
    tok = pl.BlockSpec((1, nc * L, W), lambda b, c: (b, c, 0))
    return pl.pallas_call(
        kern,
        grid=(B, S // (nc * L)),
        in_specs=[tok, tok, tok, tok,
                  pl.BlockSpec((1, 2 * ML_HEADS, nc * L), lambda b, c: (b, 0, c)),
                  _const_spec((1, W))],
        out_specs=tok,
        out_shape=jax.ShapeDtypeStruct((B, S, W), BF16),
        scratch_shapes=[pltpu.VMEM((ML_HEADS, ML_HEAD_DIM, 2 * ML_HEAD_DIM), F32),
                        pltpu.VMEM((ML_HEADS, 1, LANES), F32)],
        compiler_params=_params("arbitrary", "arbitrary"),
        name="mlstm",
    )(q, k, v, o, gates_t, norm_g.reshape(1, W))


def _mla_kernel(qt_ref, k_ref, vt_ref, o_ref, s00, s01, s10, s11, *, tq, nt):
    g = pl.program_id(2)
    heads = 2
    pieces = 1
    rows = tq // pieces
    hk = tq // 2
    s_scr = ((s00, s01), (s10, s11))
    ones_rows = jnp.ones((BF16_ROWS, tq), BF16)

    def run_tile(t, qi, sa, sb):
        lanes = slice(t * tq, (t + 1) * tq)

        def score_piece(j, slot, hh, pc):
            start = pl.multiple_of(j * tq + pc * rows, rows)
            kb = k_ref[0, pl.ds(start, rows), hh * LANES:(hh + 1) * LANES]
            qt = qt_ref[0, hh * LANES:(hh + 1) * LANES, lanes]
            s_scr[slot][hh][pc * rows:(pc + 1) * rows, :] = _dot(kb, qt)

        def stage_exp(j, slot, hh, m_new, acc, half):
            start = pl.multiple_of(j * tq + half * hk, hk)
            vtb = jnp.concatenate([vt_ref[0, hh * V_DIM:(hh + 1) * V_DIM, pl.ds(start, hk)],
                                   ones_rows[:, :hk]], axis=0)
            p = jnp.exp2(s_scr[slot][hh][half * hk:(half + 1) * hk, :] - m_new).astype(BF16)
            return acc + _dot(vtb, p)

        def block(j, slot, carry, nxt):
            new = []
            for hh in range(heads):
                m, acc = carry[hh]
                score_piece(nxt[0], nxt[1], hh, 0)
                m_new = jnp.maximum(m, jnp.max(s_scr[slot][hh][...], axis=0, keepdims=True))
                for pc in range(1, pieces):
                    score_piece(nxt[0], nxt[1], hh, pc)
                acc = stage_exp(j, slot, hh, m_new, jnp.exp2(m - m_new) * acc, 0)
                acc = stage_exp(j, slot, hh, m_new, acc, 1)
                new.append((m_new, acc))
            return tuple(new)

        def diag_block(slot, carry):
            start = pl.multiple_of(qi * tq, tq)
            kk = lax.broadcasted_iota(jnp.int32, (hk, tq), 0)
            qq = lax.broadcasted_iota(jnp.int32, (hk, tq), 1)
            kb = lax.broadcasted_iota(jnp.int32, (hk, hk), 0)
            qb = lax.broadcasted_iota(jnp.int32, (hk, hk), 1)
            hs = range(heads)
            top = [jnp.where(kk <= qq, s_scr[slot][hh][0:hk, :], NEG_BIG) for hh in hs]
            bot = [jnp.where(kb <= qb, s_scr[slot][hh][hk:, hk:], NEG_BIG) for hh in hs]
            neg = jnp.full((hk, hk), NEG_BIG, F32)
            m_new = [jnp.maximum(carry[hh][0], jnp.max(
                jnp.maximum(top[hh], jnp.concatenate([neg, bot[hh]], axis=1)), axis=0, keepdims=True))
                for hh in hs]
            vt_aug = [jnp.concatenate([vt_ref[0, hh * V_DIM:(hh + 1) * V_DIM, pl.ds(start, tq)],
                                       ones_rows], axis=0) for hh in hs]
            p_top = [jnp.exp2(top[hh] - m_new[hh]).astype(BF16) for hh in hs]
            p_bot = [jnp.exp2(bot[hh] - jnp.broadcast_to(m_new[hh], (hk, tq))[:, hk:]).astype(BF16)
                     for hh in hs]
            outs = []
            for hh in hs:
                m, acc = carry[hh]
                acc = jnp.exp2(m - m_new[hh]) * acc + _dot(vt_aug[hh][:, :hk], p_top[hh])
                hi = acc[:, hk:] + _dot(vt_aug[hh][:, hk:], p_bot[hh])
                acc = jnp.concatenate([acc[:, :hk], hi], axis=1)
                outs.append(acc[:V_DIM] / acc[V_DIM:V_DIM + 1])
            return jnp.concatenate(outs, axis=0)

        for hh in range(heads):
            for pc in range(pieces):
                score_piece(0, sa, hh, pc)

        def body(i, c):
            c = block(2 * i, sa, c, (2 * i + 1, sb))
            return block(2 * i + 1, sb, c, (2 * i + 2, sa))

        init = tuple((jnp.full((1, tq), NEG_BIG, F32), jnp.zeros((V_DIM + BF16_ROWS, tq), F32))
                     for _ in range(heads))
        carry = lax.fori_loop(0, qi // 2, body, init)
        if t % 2 == 0:
            ot = diag_block(sa, carry)
        else:
            ot = diag_block(sb, block(qi - 1, sa, carry, (qi, sb)))
        o_ref[0, :, lanes] = ot.astype(BF16)

    slot_pairs = ((0, 1), (1, 0), (1, 0), (0, 1))
    for t in range(nt):
        run_tile(t, nt * g + t, *slot_pairs[t % 4])


def _mla_attention(qpt, kp, vt, *, tq, nt):
    B, S, _ = kp.shape
    tq = min(tq, S // 2)
    nt = min(nt, S // tq)
    assert nt % 2 == 0 and S % (nt * tq) == 0
    pairs = MLA_HEADS // 2
    kern = functools.partial(_mla_kernel, tq=tq, nt=nt)
    return pl.pallas_call(
        kern,
        grid=(B, pairs, S // (nt * tq)),
        in_specs=[
            pl.BlockSpec((1, 2 * LANES, nt * tq), lambda b, p, i: (b, p, i)),
            pl.BlockSpec((1, S, 2 * LANES), lambda b, p, i: (b, 0, p)),
            pl.BlockSpec((1, 2 * V_DIM, S), lambda b, p, i: (b, p, 0)),
        ],
        out_specs=pl.BlockSpec((1, LANES, nt * tq), lambda b, p, i: (b, p, i)),
        out_shape=jax.ShapeDtypeStruct((B, MLA_WIDTH, S), BF16),
        scratch_shapes=[pltpu.VMEM((tq, tq), F32) for _ in range(4)],
        compiler_params=_params("arbitrary", "arbitrary", "arbitrary"),
        name="mla_attention",
    )(qpt, kp, vt)


def _odd_out_kernel(x_ref, hm_ref, ha_ref, wout_ref, xg_ref, wq_ref, kt_ref, v_ref, wo_ref, o_ref):
    mix = jnp.concatenate([hm_ref[0], ha_ref[0].T], axis=-1)
    x1 = x_ref[0] + _dot(mix, wout_ref[...])
    o_ref[0] = _xattn_body(x1, xg_ref, wq_ref, kt_ref, v_ref, wo_ref)


def _odd_out(x, hm, ha, w_out, xg, wq_all, kv, wo_all, layer, *, ts):
    B, S, D = x.shape
    ts = min(ts, S)
    assert S % ts == 0
    tok = lambda n: pl.BlockSpec((1, ts, n), lambda b, s: (b, s, 0))
    return pl.pallas_call(
        _odd_out_kernel,
        grid=(B, S // ts),
        in_specs=[tok(D), tok(ML_WIDTH), pl.BlockSpec((1, MLA_WIDTH, ts), lambda b, s: (b, 0, s)),
                  _const_spec(w_out.shape)]
        + _xattn_specs(D, wq_all, kv, wo_all, layer),
        out_specs=tok(D),
        out_shape=jax.ShapeDtypeStruct(x.shape, F32),
        compiler_params=_params("arbitrary", "arbitrary"),
        name="odd_out",
    )(x, hm, ha, w_out, xg.reshape(1, D), wq_all, *kv, wo_all)


def _pack_odd_weights(w_in, w_uq, w_ukv):
    D = w_in.shape[0]
    c = 4 * ML_WIDTH
    w_g = w_in[:, c:c + 2 * ML_HEADS]
    c += 2 * ML_HEADS
    w_cq = w_in[:, c:c + Q_LORA]
    c += Q_LORA
    w_ckv = w_in[:, c:c + KV_LORA]
    c += KV_LORA
    w_kr = w_in[:, c:c + QK_ROPE]
    zl = jnp.zeros((D, QK_NOPE), w_in.dtype)
    zr = jnp.zeros((D, LANES - QK_NOPE - QK_ROPE), w_in.dtype)
    w_main = jnp.concatenate([w_in[:, :4 * ML_WIDTH], w_cq, w_ckv,
                              zl, w_kr, zr], axis=1).astype(BF16)
    w_gt = w_g.T.astype(BF16)

    w_uqt = w_uq.T
    ukv = w_ukv.reshape(KV_LORA, MLA_HEADS, QK_NOPE + V_DIM)
    w_ukt = ukv[..., :QK_NOPE].reshape(KV_LORA, MLA_HEADS * QK_NOPE).T
    w_uvt = ukv[..., QK_NOPE:].reshape(KV_LORA, MLA_WIDTH).T
    return w_main, w_gt, w_uqt.astype(BF16), w_ukt.astype(BF16), w_uvt.astype(BF16)


def _rope_inv_freq():
    half = QK_ROPE // 2
    inv = ROPE_THETA ** (-jnp.arange(half, dtype=F32) / half)
    return inv.reshape(half, 1)


TS_EVEN = 1024
TM_FFN = 1024
TH_FFN = 256
TS_ODD = 1024
L_MLSTM = 256
NC_MLSTM = 4
TQ_MLA = 512
NT_MLA = 8
TS_OUT = 1024


def kernel(x, mem, positions, norm_mix_g, norm_xattn_g, mem_norm_g, xattn_wq, xattn_wkv, xattn_wo,
           norm_ffn_g, ffn_w_gate_up, ffn_w_down, ev_w_in, ev_conv_w, ev_pool_w, ev_pool_scale,
           ev_w_out, od_w_in, od_gate_bias, od_ml_norm_g, od_q_norm_g, od_kv_norm_g, od_w_uq,
           od_w_ukv, od_w_out, final_norm_g):
    depth = norm_mix_g.shape[0]
    B, S, D = x.shape
    pos3 = positions.reshape(B, 1, S)
    invf = _rope_inv_freq()
    wq_all, wkv_all, wo_all = (w.astype(BF16) for w in (xattn_wq, xattn_wkv, xattn_wo))
    w_gu_all, w_d_all = ffn_w_gate_up.astype(BF16), ffn_w_down.astype(BF16)
    for layer in range(depth):
        kv = _mem_kv(mem, mem_norm_g[layer], wkv_all, layer)
        xa = (norm_xattn_g[layer], wq_all, kv, wo_all, layer)
        if layer % 2 == 0:
            e = layer // 2
            x = _even_mixer(x, norm_mix_g[layer], ev_w_in[e].astype(BF16), ev_conv_w[e],
                            ev_pool_w[e].astype(BF16), ev_pool_scale[e], ev_w_out[e].astype(BF16),
                            *xa, ts=TS_EVEN)
        else:
            o = layer // 2
            w_main, w_gt, w_uqt, w_ukt, w_uvt = _pack_odd_weights(
                od_w_in[o], od_w_uq[o], od_w_ukv[o])
            qm, km, vm, om, gates_t, qp, kp, vt = _odd_proj(
                x, pos3, norm_mix_g[layer], w_main, w_gt, od_gate_bias[o].reshape(2 * ML_HEADS, 1),
                od_q_norm_g[o].reshape(1, Q_LORA), od_kv_norm_g[o].reshape(1, KV_LORA),
                w_uqt, w_ukt, w_uvt, invf, ts=TS_ODD)
            hm = _mlstm(qm, km, vm, om, gates_t, od_ml_norm_g[o], L=L_MLSTM, nc=NC_MLSTM)
            ha = _mla_attention(qp, kp, vt, tq=TQ_MLA, nt=NT_MLA)
            x = _odd_out(x, hm, ha, od_w_out[o].astype(BF16), *xa, ts=TS_OUT)
        x = _swiglu(x, norm_ffn_g[layer], w_gu_all, w_d_all, final_norm_g, layer,
                    tm=TM_FFN, th=TH_FFN, final_norm=(layer == depth - 1))
    return x
```

```python
import functools

import jax
import jax.numpy as jnp
from jax import lax
from jax.experimental import pallas as pl
from jax.experimental.pallas import tpu as pltpu

F32 = jnp.float32
BF16 = jnp.bfloat16

EPS = 1e-6
ROPE_THETA = 10000.0
LANES = 128
SUBLANES = 8
BF16_ROWS = 16
VMEM_LIMIT_BYTES = 56 * 1024 * 1024
NEG_BIG = -1e30
LOG2_E = 1.4426950408889634

SC_WIDTH = 512
CONV_K = 3
POOL_WINDOWS = (2, 4, 8, 16)
POOL_GROUP = 128
HALO = 2 * max(POOL_WINDOWS)
ML_HEADS = 4
ML_HEAD_DIM = 128
ML_WIDTH = ML_HEADS * ML_HEAD_DIM
MLA_HEADS = 8
QK_NOPE = 64
QK_ROPE = 32
V_DIM = 64
Q_LORA = 384
KV_LORA = 256
MLA_WIDTH = MLA_HEADS * V_DIM
XA_HEADS = 4
XA_HEAD_DIM = 128
XA_WIDTH = XA_HEADS * XA_HEAD_DIM


def _params(*sem):
    return pltpu.CompilerParams(dimension_semantics=sem, vmem_limit_bytes=VMEM_LIMIT_BYTES)


def _const_spec(shape):
    nd = len(shape)
    return pl.BlockSpec(shape, lambda *_: (0,) * nd, pipeline_mode=pl.Buffered(1))


def _layer_spec(stacked, layer):
    return pl.BlockSpec((None,) + stacked.shape[1:], lambda *_: (layer, 0, 0),
                        pipeline_mode=pl.Buffered(1))


def _rms(x, g):
    ms = jnp.mean(x * x, axis=-1, keepdims=True)
    return x * lax.rsqrt(ms + EPS) * g


def _dot(a, b):
    return jnp.dot(a, b, preferred_element_type=F32)


def _dot_nt(a, b):
    return lax.dot_general(a, b, (((1,), (1,)), ((), ())), preferred_element_type=F32)


def _even_kernel(x_ref, g_ref, win_ref, convw_ref, poolw_ref, pscale_ref, wout_ref,
                 xg_ref, wq_ref, kt_ref, v_ref, wo_ref, o_ref, ubuf, xbbuf, tmp_a, tmp_b, *, ts):
    si = pl.program_id(1)

    @pl.when(si == 0)
    def _():
        ubuf[0:HALO, :] = jnp.zeros((HALO, SC_WIDTH), F32)
        xbbuf[0:HALO, :] = jnp.zeros((HALO, SC_WIDTH), F32)

    x = x_ref[0]
    h = _rms(x, g_ref[...]).astype(BF16)
    z = _dot(h, win_ref[...])
    g_b = z[:, 0:SC_WIDTH]
    u = z[:, SC_WIDTH:2 * SC_WIDTH] * z[:, 2 * SC_WIDTH:3 * SC_WIDTH]
    xb = z[:, 3 * SC_WIDTH:4 * SC_WIDTH]
    ubuf[HALO:HALO + ts, :] = u
    xbbuf[HALO:HALO + ts, :] = xb

    cw = convw_ref[...]
    conv = cw[2:3, :] * u
    for k in range(CONV_K - 1):
        back = CONV_K - 1 - k
        conv = conv + cw[k:k + 1, :] * ubuf[HALO - back:HALO - back + ts, :]
    ya = g_b * conv

    def window_level(src, c0, dst, r0, shift):
        w2 = (src[r0:HALO + ts, c0:c0 + POOL_GROUP]
              + src[r0 - shift:HALO + ts - shift, c0:c0 + POOL_GROUP])
        if dst is not None:
            dst[r0:HALO + ts, :] = w2
        return w2

    t_idx = si * ts + lax.broadcasted_iota(jnp.int32, (ts, POOL_GROUP), 0)
    ys = [ya.astype(BF16)]
    for gi, w in enumerate(POOL_WINDOWS):
        lo = gi * POOL_GROUP
        xg = xb[:, lo:lo + POOL_GROUP]
        src, c0, shift, level = xbbuf, lo, 1, 1
        while 2 * shift < w:
            dst = tmp_a if src is not tmp_a else tmp_b
            window_level(src, c0, dst, SUBLANES * level, shift)
            src, c0, shift, level = dst, 0, 2 * shift, level + 1
        acc = window_level(src, c0, None, HALO, shift)
        cnt = jnp.minimum(t_idx + 1, w).astype(F32)
        d = (acc / cnt - xg).astype(BF16)
        yg = _dot(d, poolw_ref[gi]) * pscale_ref[:, lo:lo + POOL_GROUP]
        ys.append(yg.astype(BF16))

    ubuf[0:HALO, :] = ubuf[ts:ts + HALO, :]
    xbbuf[0:HALO, :] = xbbuf[ts:ts + HALO, :]

    mix = jnp.concatenate(ys, axis=-1)
    x1 = x + _dot(mix, wout_ref[...])
    o_ref[0] = _xattn_body(x1, xg_ref, wq_ref, kt_ref, v_ref, wo_ref)


def _even_mixer(x, g, w_in, conv_w, pool_w, pool_scale, w_out, xg, wq_all, kv, wo_all, layer, *, ts):
    B, S, D = x.shape
    ts = min(ts, S)
    assert S % ts == 0 and D == 2 * SC_WIDTH and w_in.shape == (D, 4 * SC_WIDTH)
    kern = functools.partial(_even_kernel, ts=ts)
    return pl.pallas_call(
        kern,
        grid=(B, S // ts),
        in_specs=[
            pl.BlockSpec((1, ts, D), lambda b, s: (b, s, 0)),
            _const_spec((1, D)),
            _const_spec(w_in.shape),
            _const_spec(conv_w.shape),
            _const_spec(pool_w.shape),
            _const_spec((1, SC_WIDTH)),
            _const_spec(w_out.shape),
        ] + _xattn_specs(D, wq_all, kv, wo_all, layer),
        out_specs=pl.BlockSpec((1, ts, D), lambda b, s: (b, s, 0)),
        out_shape=jax.ShapeDtypeStruct(x.shape, F32),
        scratch_shapes=[pltpu.VMEM((HALO + ts, SC_WIDTH), F32),
                        pltpu.VMEM((HALO + ts, SC_WIDTH), F32),
                        pltpu.VMEM((HALO + ts, POOL_GROUP), F32),
                        pltpu.VMEM((HALO + ts, POOL_GROUP), F32)],
        compiler_params=_params("arbitrary", "arbitrary"),
        name="even_mixer",
    )(x, g.reshape(1, D), w_in, conv_w, pool_w, pool_scale.reshape(1, SC_WIDTH), w_out,
      xg.reshape(1, D), wq_all, *kv, wo_all)


def _memkv_kernel(mem_ref, g_ref, wkv_ref, kt_ref, v_ref):
    h = _rms(mem_ref[0], g_ref[...]).astype(BF16)
    kv = _dot(h, wkv_ref[...])
    kt_ref[0] = kv[:, :XA_WIDTH].T.astype(BF16)
    v_ref[0] = kv[:, XA_WIDTH:].astype(BF16)


def _mem_kv(mem, g, wkv_all, layer):
    B, M, D = mem.shape
    return pl.pallas_call(
        _memkv_kernel,
        grid=(B,),
        in_specs=[pl.BlockSpec((1, M, D), lambda b: (b, 0, 0)),
                  _const_spec((1, D)), _layer_spec(wkv_all, layer)],
        out_specs=(pl.BlockSpec((1, XA_WIDTH, M), lambda b: (b, 0, 0)),
                   pl.BlockSpec((1, M, XA_WIDTH), lambda b: (b, 0, 0))),
        out_shape=(jax.ShapeDtypeStruct((B, XA_WIDTH, M), BF16),
                   jax.ShapeDtypeStruct((B, M, XA_WIDTH), BF16)),
        compiler_params=_params("arbitrary"),
        name="mem_kv",
    )(mem, g.reshape(1, D), wkv_all)


def _xattn_body(x, g_ref, wq_ref, kt_ref, v_ref, wo_ref):
    h = _rms(x, g_ref[...]).astype(BF16)
    q = (_dot(h, wq_ref[...]) * (XA_HEAD_DIM ** -0.5)).astype(BF16)
    ones_v = jnp.ones((v_ref.shape[1], XA_HEAD_DIM), BF16)
    outs = []
    for hd in range(XA_HEADS):
        lo = hd * XA_HEAD_DIM
        s = _dot(q[:, lo:lo + XA_HEAD_DIM], kt_ref[0, lo:lo + XA_HEAD_DIM, :])
        p = jnp.exp(s - jnp.max(s, axis=-1, keepdims=True)).astype(BF16)
        v_aug = jnp.concatenate([v_ref[0, :, lo:lo + XA_HEAD_DIM], ones_v], axis=1)
        o = _dot(p, v_aug)
        outs.append((o[:, :XA_HEAD_DIM] / o[:, XA_HEAD_DIM:]).astype(BF16))
    a = jnp.concatenate(outs, axis=-1)
    return x + _dot(a, wo_ref[...])


def _xattn_specs(D, wq_all, kv, wo_all, layer):
    kt, v = kv
    return [_const_spec((1, D)), _layer_spec(wq_all, layer),
            pl.BlockSpec((1,) + kt.shape[1:], lambda b, s: (b, 0, 0)),
            pl.BlockSpec((1,) + v.shape[1:], lambda b, s: (b, 0, 0)), _layer_spec(wo_all, layer)]


def _swiglu_kernel(x_ref, g_ref, wgu_ref, wd_ref, fg_ref, o_ref, *, hidden, th, final_norm):
    x = x_ref[...]
    h = _rms(x, g_ref[...]).astype(BF16)
    acc = x
    for j in range(hidden // th):
        gate = _dot(h, wgu_ref[:, j * th:(j + 1) * th])
        up = _dot(h, wgu_ref[:, hidden + j * th:hidden + (j + 1) * th])
        act = (gate * jax.nn.sigmoid(gate) * up).astype(BF16)
        acc = acc + _dot(act, wd_ref[j * th:(j + 1) * th, :])
    if final_norm:
        acc = _rms(acc, fg_ref[...])
    o_ref[...] = acc


def _swiglu(x, g, w_gu_all, w_d_all, final_g, layer, *, tm, th, final_norm):
    B, S, D = x.shape
    T = B * S
    tm = min(tm, T)
    hidden = w_d_all.shape[1]
    assert T % tm == 0 and hidden % th == 0 and w_gu_all.shape[2] == 2 * hidden
    kern = functools.partial(_swiglu_kernel, hidden=hidden, th=th, final_norm=final_norm)
    out = pl.pallas_call(
        kern,
        grid=(T // tm,),
        in_specs=[
            pl.BlockSpec((tm, D), lambda i: (i, 0)),
            _const_spec((1, D)),
            _layer_spec(w_gu_all, layer),
            _layer_spec(w_d_all, layer),
            _const_spec((1, D)),
        ],
        out_specs=pl.BlockSpec((tm, D), lambda i: (i, 0)),
        out_shape=jax.ShapeDtypeStruct((T, D), F32),
        compiler_params=_params("arbitrary"),
        name="swiglu",
    )(x.reshape(T, D), g.reshape(1, D), w_gu_all, w_d_all, final_g.reshape(1, D))
    return out.reshape(B, S, D)


def _odd_proj_kernel(x_ref, pos_ref, g_ref, wmain_ref, wgt_ref, gbias_ref,
                     qg_ref, kvg_ref, wuqt_ref, wukt_ref, wuvt_ref, invf_ref,
                     qm_ref, km_ref, vm_ref, om_ref, gates_ref, qp_ref, kp_ref, vt_ref):
    x = x_ref[0]
    h = _rms(x, g_ref[...]).astype(BF16)
    c0 = 4 * ML_WIDTH

    def ml_cols(i):
        return _dot(h, wmain_ref[:, i * ML_WIDTH:(i + 1) * ML_WIDTH])

    zl = _dot(h, wmain_ref[:, c0:])
    c_q = zl[:, 0:Q_LORA]
    c_kv = zl[:, Q_LORA:Q_LORA + KV_LORA]
    kr = zl[:, Q_LORA + KV_LORA:Q_LORA + KV_LORA + LANES]
    gates_ref[0] = _dot_nt(wgt_ref[...], h) + gbias_ref[...]

    half = QK_ROPE // 2
    ang_t = invf_ref[...] * pos_ref[0].astype(F32)
    cos_c, sin_c = jnp.cos(ang_t), jnp.sin(ang_t)
    ts = ang_t.shape[1]
    pad_lo = jnp.zeros((QK_NOPE, ts), F32)
    pad_hi = jnp.zeros((LANES - QK_NOPE - QK_ROPE, ts), F32)
    cos_tt = jnp.concatenate([pad_lo + 1.0, cos_c, cos_c, pad_hi], axis=0)
    sin_tt = jnp.concatenate([pad_lo, -sin_c, sin_c, pad_hi], axis=0)
    scale = (QK_NOPE + QK_ROPE) ** -0.5 * LOG2_E
    cos_q, sin_q = cos_tt * scale, sin_tt * scale

    def rotate(t, cos_tab, sin_tab):
        r0, r1, r2 = QK_NOPE, QK_NOPE + half, QK_NOPE + QK_ROPE
        t_sw = jnp.concatenate([t[:r0], t[r1:r2], t[r0:r1], t[r2:]], axis=0)
        return t * cos_tab + t_sw * sin_tab

    qm_ref[0] = ml_cols(0).astype(BF16)
    km_ref[0] = (ml_cols(1) * (ML_HEAD_DIM ** -0.5)).astype(BF16)

    cqn = _rms(c_q, qg_ref[...]).astype(BF16)
    qa_t = _dot_nt(wuqt_ref[...], cqn)
    ckn = _rms(c_kv, kvg_ref[...]).astype(BF16)
    kn_t = _dot_nt(wukt_ref[...], ckn)
    vt_ref[0] = _dot_nt(wuvt_ref[...], ckn).astype(BF16)
    k_rope_t = rotate(kr.T, cos_tt, sin_tt)[QK_NOPE:QK_NOPE + QK_ROPE]
    qk_dim = QK_NOPE + QK_ROPE

    def rope_heads(first, last):
        for hd in range(first, last):
            lo = hd * LANES
            q_blk = jnp.concatenate([qa_t[hd * qk_dim:(hd + 1) * qk_dim], pad_hi], axis=0)
            qp_ref[0, lo:lo + LANES, :] = rotate(q_blk, cos_q, sin_q).astype(BF16)
            k_blk = jnp.concatenate([kn_t[hd * QK_NOPE:(hd + 1) * QK_NOPE], k_rope_t, pad_hi], axis=0)
            kp_ref[0, :, lo:lo + LANES] = k_blk.T.astype(BF16)

    vm_ref[0] = ml_cols(2).astype(BF16)
    rope_heads(0, MLA_HEADS // 2)
    om_ref[0] = ml_cols(3).astype(BF16)
    rope_heads(MLA_HEADS // 2, MLA_HEADS)


def _odd_proj(x, pos3, g, w_main, w_gt, gbias, q_g, kv_g, w_uqt, w_ukt, w_uvt, invf, *, ts):
    B, S, D = x.shape
    ts = min(ts, S)
    assert S % ts == 0
    tok = lambda n: pl.BlockSpec((1, ts, n), lambda b, s: (b, s, 0))
    out_shapes = (
        jax.ShapeDtypeStruct((B, S, ML_WIDTH), BF16),
        jax.ShapeDtypeStruct((B, S, ML_WIDTH), BF16),
        jax.ShapeDtypeStruct((B, S, ML_WIDTH), BF16),
        jax.ShapeDtypeStruct((B, S, ML_WIDTH), BF16),
        jax.ShapeDtypeStruct((B, 2 * ML_HEADS, S), F32),
        jax.ShapeDtypeStruct((B, MLA_HEADS * LANES, S), BF16),
        jax.ShapeDtypeStruct((B, S, MLA_HEADS * LANES), BF16),
        jax.ShapeDtypeStruct((B, MLA_WIDTH, S), BF16),
    )
    return pl.pallas_call(
        _odd_proj_kernel,
        grid=(B, S // ts),
        in_specs=[
            tok(D),
            pl.BlockSpec((1, 1, ts), lambda b, s: (b, 0, s)),
            _const_spec((1, D)),
            _const_spec(w_main.shape),
            _const_spec(w_gt.shape),
            _const_spec(gbias.shape),
            _const_spec(q_g.shape),
            _const_spec(kv_g.shape),
            _const_spec(w_uqt.shape),
            _const_spec(w_ukt.shape),
            _const_spec(w_uvt.shape),
            _const_spec(invf.shape),
        ],
        out_specs=(tok(ML_WIDTH), tok(ML_WIDTH), tok(ML_WIDTH), tok(ML_WIDTH),
                   pl.BlockSpec((1, 2 * ML_HEADS, ts), lambda b, s: (b, 0, s)),
                   pl.BlockSpec((1, MLA_HEADS * LANES, ts), lambda b, s: (b, 0, s)),
                   tok(MLA_HEADS * LANES),
                   pl.BlockSpec((1, MLA_WIDTH, ts), lambda b, s: (b, 0, s))),
        out_shape=out_shapes,
        compiler_params=_params("arbitrary", "arbitrary"),
        name="odd_proj",
    )(x, pos3, g.reshape(1, D), w_main, w_gt, gbias, q_g, kv_g, w_uqt, w_ukt, w_uvt, invf)


def _split3(a):
    hi = a.astype(BF16)
    r1 = a - hi.astype(F32)
    mid = r1.astype(BF16)
    lo = (r1 - mid.astype(F32)).astype(BF16)
    return hi, mid, lo


def _log2_sigmoid(x):
    return (jnp.minimum(x, 0.0) - jnp.log(1.0 + jnp.exp(-jnp.abs(x)))) * LOG2_E


def _mlstm_kernel(q_ref, k_ref, v_ref, o_ref, gt_ref, ng_ref, h_ref, c_scr, m_scr, *, L, nc):
    ci = pl.program_id(1)

    @pl.when(ci == 0)
    def _():
        c_scr[...] = jnp.zeros(c_scr.shape, F32)
        m_scr[...] = jnp.zeros(m_scr.shape, F32)

    row = lax.broadcasted_iota(jnp.int32, (L, L), 0)
    col = lax.broadcasted_iota(jnp.int32, (L, L), 1)
    causal = col <= row
    ones_v = jnp.ones((L, LANES), BF16)
    reps = L // LANES

    def widen(a):
        return jnp.concatenate([a] * reps, axis=1)

    for ch, hd in [(ch, hd) for ch in range(nc) for hd in range(ML_HEADS)]:
        tok = slice(ch * L, (ch + 1) * L)
        if hd == 0:
            gt = gt_ref[0, :, tok]
            lf_t = _log2_sigmoid(gt)
            b_t = sum(_dot(t, (row <= col).astype(BF16)) for t in _split3(lf_t))
            b_c = jnp.concatenate([b_t, jnp.zeros((LANES - 2 * ML_HEADS, L), F32)], axis=0).T
        lo = hd * ML_HEAD_DIM
        fg = ML_HEADS + hd
        q = q_ref[0, tok, lo:lo + ML_HEAD_DIM]
        k = k_ref[0, tok, lo:lo + ML_HEAD_DIM]
        v_aug = jnp.concatenate([v_ref[0, tok, lo:lo + ML_HEAD_DIM], ones_v], axis=1)
        g_row = gt[hd:hd + 1, :] * LOG2_E - b_t[fg:fg + 1, :]
        gtot = jnp.sum(lf_t[fg:fg + 1, :], axis=1, keepdims=True)
        b_col = jnp.broadcast_to(b_c[:, fg:fg + 1], (L, LANES))

        m_prev = m_scr[hd]
        c_prev = c_scr[hd]

        g_mask = jnp.where(causal, g_row, NEG_BIG)
        gmax = jnp.max(g_mask, axis=1, keepdims=True)
        h_col = jnp.maximum(m_prev, jnp.broadcast_to(gmax, (L, LANES)))
        w_inter = jnp.exp2(m_prev - h_col)
        sw = jnp.exp2(g_mask - widen(h_col)) * _dot_nt(q, k)
        out = (jnp.concatenate([w_inter, w_inter], axis=1) * _dot(q, c_prev.astype(BF16))
               + _dot(sw.astype(BF16), v_aug))
        den = jnp.maximum(jnp.abs(out[:, ML_HEAD_DIM:]), jnp.exp2(-(b_col + h_col)))
        hc = out[:, :ML_HEAD_DIM] / den

        og = jax.nn.sigmoid(o_ref[0, tok, lo:lo + ML_HEAD_DIM].astype(F32))
        hg = og * hc
        ms = jnp.mean(hg * hg, axis=1, keepdims=True)
        h_ref[0, tok, lo:lo + ML_HEAD_DIM] = (
            hg * lax.rsqrt(ms + EPS) * ng_ref[:, lo:lo + ML_HEAD_DIM]).astype(BF16)

        m_prev1 = m_prev[:, 0:1]
        a_row = gtot + g_row
        m_loc = jnp.max(a_row, axis=1, keepdims=True)
        m_new = jnp.maximum(gtot + m_prev1, m_loc)
        s_old = jnp.exp2(gtot + m_prev1 - m_new)
        w_loc = jnp.exp2(a_row - m_new)
        kw_t = (k.T.astype(F32) * w_loc).astype(BF16)
        c_scr[hd] = s_old * c_prev + _dot(kw_t, v_aug)
        m_scr[hd] = jnp.broadcast_to(m_new, (1, LANES))


def _mlstm(q, k, v, o, gates_t, norm_g, *, L, nc):
    B, S, W = q.shape
    L = min(L, S)
    nc = min(nc, S // L)
    assert S % (nc * L) == 0 and L % LANES == 0 and W == ML_WIDTH
    kern = functools.partial(_mlstm_kernel, L=L, nc=nc)
    tok = pl.BlockSpec((1, nc * L, W), lambda b, c: (b, c, 0))
    return pl.pallas_call(
        kern,
        grid=(B, S // (nc * L)),
        in_specs=[tok, tok, tok, tok,
                  pl.BlockSpec((1, 2 * ML_HEADS, nc * L), lambda b, c: (b, 0, c)),
                  _const_spec((1, W))],
        out_specs=tok,
        out_shape=jax.ShapeDtypeStruct((B, S, W), BF16),
        scratch_shapes=[pltpu.VMEM((ML_HEADS, ML_HEAD_DIM, 2 * ML_HEAD_DIM), F32),
                        pltpu.VMEM((ML_HEADS, 1, LANES), F32)],
        compiler_params=_params("arbitrary", "arbitrary"),
        name="mlstm",
    )(q, k, v, o, gates_t, norm_g.reshape(1, W))


def _mla_kernel(qt_ref, k_ref, vt_ref, o_ref, s00, s01, s10, s11, *, tq, nt):
    g = pl.program_id(2)
    heads = 2
    pieces = 2
    rows = tq // pieces
    hk = tq // 2
    s_scr = ((s00, s01), (s10, s11))
    ones_rows = jnp.ones((BF16_ROWS, tq), BF16)

    def run_tile(t, qi, sa, sb):
        lanes = slice(t * tq, (t + 1) * tq)

        def score_piece(j, slot, hh, pc):
            start = pl.multiple_of(j * tq + pc * rows, rows)
            kb = k_ref[0, pl.ds(start, rows), hh * LANES:(hh + 1) * LANES]
            qt = qt_ref[0, hh * LANES:(hh + 1) * LANES, lanes]
            s_scr[slot][hh][pc * rows:(pc + 1) * rows, :] = _dot(kb, qt)

        def stage_exp(j, slot, hh, m_new, acc, half):
            start = pl.multiple_of(j * tq + half * hk, hk)
            vtb = jnp.concatenate([vt_ref[0, hh * V_DIM:(hh + 1) * V_DIM, pl.ds(start, hk)],
                                   ones_rows[:, :hk]], axis=0)
            p = jnp.exp2(s_scr[slot][hh][half * hk:(half + 1) * hk, :] - m_new).astype(BF16)
            return acc + _dot(vtb, p)

        def block(j, slot, carry, nxt):
            new = []
            for hh in range(heads):
                m, acc = carry[hh]
                score_piece(nxt[0], nxt[1], hh, 0)
                m_new = jnp.maximum(m, jnp.max(s_scr[slot][hh][...], axis=0, keepdims=True))
                score_piece(nxt[0], nxt[1], hh, 1)
                acc = stage_exp(j, slot, hh, m_new, jnp.exp2(m - m_new) * acc, 0)
                acc = stage_exp(j, slot, hh, m_new, acc, 1)
                new.append((m_new, acc))
            return tuple(new)

        def diag_block(slot, carry):
            start = pl.multiple_of(qi * tq, tq)
            kk = lax.broadcasted_iota(jnp.int32, (hk, tq), 0)
            qq = lax.broadcasted_iota(jnp.int32, (hk, tq), 1)
            kb = lax.broadcasted_iota(jnp.int32, (hk, hk), 0)
            qb = lax.broadcasted_iota(jnp.int32, (hk, hk), 1)
            hs = range(heads)
            top = [jnp.where(kk <= qq, s_scr[slot][hh][0:hk, :], NEG_BIG) for hh in hs]
            bot = [jnp.where(kb <= qb, s_scr[slot][hh][hk:, hk:], NEG_BIG) for hh in hs]
            neg = jnp.full((hk, hk), NEG_BIG, F32)
            m_new = [jnp.maximum(carry[hh][0], jnp.max(
                jnp.maximum(top[hh], jnp.concatenate([neg, bot[hh]], axis=1)), axis=0, keepdims=True))
                for hh in hs]
            vt_aug = [jnp.concatenate([vt_ref[0, hh * V_DIM:(hh + 1) * V_DIM, pl.ds(start, tq)],
                                       ones_rows], axis=0) for hh in hs]
            p_top = [jnp.exp2(top[hh] - m_new[hh]).astype(BF16) for hh in hs]
            p_bot = [jnp.exp2(bot[hh] - jnp.broadcast_to(m_new[hh], (hk, tq))[:, hk:]).astype(BF16)
                     for hh in hs]
            outs = []
            for hh in hs:
                m, acc = carry[hh]
                acc = jnp.exp2(m - m_new[hh]) * acc + _dot(vt_aug[hh][:, :hk], p_top[hh])
                hi = acc[:, hk:] + _dot(vt_aug[hh][:, hk:], p_bot[hh])
                acc = jnp.concatenate([acc[:, :hk], hi], axis=1)
                outs.append(acc[:V_DIM] / acc[V_DIM:V_DIM + 1])
            return jnp.concatenate(outs, axis=0)

        for hh in range(heads):
            for pc in range(pieces):
                score_piece(0, sa, hh, pc)

        def body(i, c):
            c = block(2 * i, sa, c, (2 * i + 1, sb))
            return block(2 * i + 1, sb, c, (2 * i + 2, sa))

        init = tuple((jnp.full((1, tq), NEG_BIG, F32), jnp.zeros((V_DIM + BF16_ROWS, tq), F32))
                     for _ in range(heads))
        carry = lax.fori_loop(0, qi // 2, body, init)
        if t % 2 == 0:
            ot = diag_block(sa, carry)
        else:
            ot = diag_block(sb, block(qi - 1, sa, carry, (qi, sb)))
        o_ref[0, :, lanes] = ot.astype(BF16)

    slot_pairs = ((0, 1), (1, 0), (1, 0), (0, 1))
    for t in range(nt):
        run_tile(t, nt * g + t, *slot_pairs[t % 4])


def _mla_attention(qpt, kp, vt, *, tq, nt):
    B, S, _ = kp.shape
    tq = min(tq, S // 2)
    nt = min(nt, S // tq)
    assert nt % 2 == 0 and S % (nt * tq) == 0
    pairs = MLA_HEADS // 2
    kern = functools.partial(_mla_kernel, tq=tq, nt=nt)
    return pl.pallas_call(
        kern,
        grid=(B, pairs, S // (nt * tq)),
        in_specs=[
            pl.BlockSpec((1, 2 * LANES, nt * tq), lambda b, p, i: (b, p, i)),
            pl.BlockSpec((1, S, 2 * LANES), lambda b, p, i: (b, 0, p)),
            pl.BlockSpec((1, 2 * V_DIM, S), lambda b, p, i: (b, p, 0)),
        ],
        out_specs=pl.BlockSpec((1, LANES, nt * tq), lambda b, p, i: (b, p, i)),
        out_shape=jax.ShapeDtypeStruct((B, MLA_WIDTH, S), BF16),
        scratch_shapes=[pltpu.VMEM((tq, tq), F32) for _ in range(4)],
        compiler_params=_params("arbitrary", "arbitrary", "arbitrary"),
        name="mla_attention",
    )(qpt, kp, vt)


def _odd_out_kernel(x_ref, hm_ref, ha_ref, wout_ref, xg_ref, wq_ref, kt_ref, v_ref, wo_ref, o_ref):
    mix = jnp.concatenate([hm_ref[0], ha_ref[0].T], axis=-1)
    x1 = x_ref[0] + _dot(mix, wout_ref[...])
    o_ref[0] = _xattn_body(x1, xg_ref, wq_ref, kt_ref, v_ref, wo_ref)


def _odd_out(x, hm, ha, w_out, xg, wq_all, kv, wo_all, layer, *, ts):
    B, S, D = x.shape
    ts = min(ts, S)
    assert S % ts == 0
    tok = lambda n: pl.BlockSpec((1, ts, n), lambda b, s: (b, s, 0))
    return pl.pallas_call(
        _odd_out_kernel,
        grid=(B, S // ts),
        in_specs=[tok(D), tok(ML_WIDTH), pl.BlockSpec((1, MLA_WIDTH, ts), lambda b, s: (b, 0, s)),
                  _const_spec(w_out.shape)]
        + _xattn_specs(D, wq_all, kv, wo_all, layer),
        out_specs=tok(D),
        out_shape=jax.ShapeDtypeStruct(x.shape, F32),
        compiler_params=_params("arbitrary", "arbitrary"),
        name="odd_out",
    )(x, hm, ha, w_out, xg.reshape(1, D), wq_all, *kv, wo_all)


def _pack_odd_weights(w_in, w_uq, w_ukv):
    D = w_in.shape[0]
    c = 4 * ML_WIDTH
    w_g = w_in[:, c:c + 2 * ML_HEADS]
    c += 2 * ML_HEADS
    w_cq = w_in[:, c:c + Q_LORA]
    c += Q_LORA
    w_ckv = w_in[:, c:c + KV_LORA]
    c += KV_LORA
    w_kr = w_in[:, c:c + QK_ROPE]
    zl = jnp.zeros((D, QK_NOPE), w_in.dtype)
    zr = jnp.zeros((D, LANES - QK_NOPE - QK_ROPE), w_in.dtype)
    w_main = jnp.concatenate([w_in[:, :4 * ML_WIDTH], w_cq, w_ckv,
                              zl, w_kr, zr], axis=1).astype(BF16)
    w_gt = w_g.T.astype(BF16)

    w_uqt = w_uq.T
    ukv = w_ukv.reshape(KV_LORA, MLA_HEADS, QK_NOPE + V_DIM)
    w_ukt = ukv[..., :QK_NOPE].reshape(KV_LORA, MLA_HEADS * QK_NOPE).T
    w_uvt = ukv[..., QK_NOPE:].reshape(KV_LORA, MLA_WIDTH).T
    return w_main, w_gt, w_uqt.astype(BF16), w_ukt.astype(BF16), w_uvt.astype(BF16)


def _rope_inv_freq():
    half = QK_ROPE // 2
    inv = ROPE_THETA ** (-jnp.arange(half, dtype=F32) / half)
    return inv.reshape(half, 1)


TS_EVEN = 1024
TM_FFN = 1024
TH_FFN = 256
TS_ODD = 1024
L_MLSTM = 256
NC_MLSTM = 4
TQ_MLA = 256
NT_MLA = 16
TS_OUT = 1024


def kernel(x, mem, positions, norm_mix_g, norm_xattn_g, mem_norm_g, xattn_wq, xattn_wkv, xattn_wo,
           norm_ffn_g, ffn_w_gate_up, ffn_w_down, ev_w_in, ev_conv_w, ev_pool_w, ev_pool_scale,
           ev_w_out, od_w_in, od_gate_bias, od_ml_norm_g, od_q_norm_g, od_kv_norm_g, od_w_uq,
           od_w_ukv, od_w_out, final_norm_g):
    depth = norm_mix_g.shape[0]
    B, S, D = x.shape
    pos3 = positions.reshape(B, 1, S)
    invf = _rope_inv_freq()
    wq_all, wkv_all, wo_all = (w.astype(BF16) for w in (xattn_wq, xattn_wkv, xattn_wo))
    w_gu_all, w_d_all = ffn_w_gate_up.astype(BF16), ffn_w_down.astype(BF16)
    for layer in range(depth):
        kv = _mem_kv(mem, mem_norm_g[layer], wkv_all, layer)
        xa = (norm_xattn_g[layer], wq_all, kv, wo_all, layer)
        if layer % 2 == 0:
            e = layer // 2
            x = _even_mixer(x, norm_mix_g[layer], ev_w_in[e].astype(BF16), ev_conv_w[e],
                            ev_pool_w[e].astype(BF16), ev_pool_scale[e], ev_w_out[e].astype(BF16),
                            *xa, ts=TS_EVEN)
        else:
            o = layer // 2
            w_main, w_gt, w_uqt, w_ukt, w_uvt = _pack_odd_weights(
                od_w_in[o], od_w_uq[o], od_w_ukv[o])
            qm, km, vm, om, gates_t, qp, kp, vt = _odd_proj(
                x, pos3, norm_mix_g[layer], w_main, w_gt, od_gate_bias[o].reshape(2 * ML_HEADS, 1),
                od_q_norm_g[o].reshape(1, Q_LORA), od_kv_norm_g[o].reshape(1, KV_LORA),
                w_uqt, w_ukt, w_uvt, invf, ts=TS_ODD)
            hm = _mlstm(qm, km, vm, om, gates_t, od_ml_norm_g[o], L=L_MLSTM, nc=NC_MLSTM)
            ha = _mla_attention(qp, kp, vt, tq=TQ_MLA, nt=NT_MLA)
            x = _odd_out(x, hm, ha, od_w_out[o].astype(BF16), *xa, ts=TS_OUT)
        x = _swiglu(x, norm_ffn_g[layer], w_gu_all, w_d_all, final_norm_g, layer,
                    tm=TM_FFN, th=TH_FFN, final_norm=(layer == depth - 1))
    return x
```

```python
import functools

import jax
import jax.numpy as jnp
from jax import lax
from jax.experimental import pallas as pl
from jax.experimental.pallas import tpu as pltpu

F32 = jnp.float32
BF16 = jnp.bfloat16

EPS = 1e-6
ROPE_THETA = 10000.0
LANES = 128
SUBLANES = 8
BF16_ROWS = 16
VMEM_LIMIT_BYTES = 56 * 1024 * 1024
NEG_BIG = -1e30
LOG2_E = 1.4426950408889634

SC_WIDTH = 512
CONV_K = 3
POOL_WINDOWS = (2, 4, 8, 16)
POOL_GROUP = 128
HALO = 2 * max(POOL_WINDOWS)
ML_HEADS = 4
ML_HEAD_DIM = 128
ML_WIDTH = ML_HEADS * ML_HEAD_DIM
MLA_HEADS = 8
QK_NOPE = 64
QK_ROPE = 32
V_DIM = 64
Q_LORA = 384
KV_LORA = 256
MLA_WIDTH = MLA_HEADS * V_DIM
XA_HEADS = 4
XA_HEAD_DIM = 128
XA_WIDTH = XA_HEADS * XA_HEAD_DIM


def _params(*sem):
    return pltpu.CompilerParams(dimension_semantics=sem, vmem_limit_bytes=VMEM_LIMIT_BYTES)


def _const_spec(shape):
    nd = len(shape)
    return pl.BlockSpec(shape, lambda *_: (0,) * nd, pipeline_mode=pl.Buffered(1))


def _layer_spec(stacked, layer):
    return pl.BlockSpec((None,) + stacked.shape[1:], lambda *_: (layer, 0, 0),
                        pipeline_mode=pl.Buffered(1))


def _rms(x, g):
    ms = jnp.mean(x * x, axis=-1, keepdims=True)
    return x * lax.rsqrt(ms + EPS) * g


def _dot(a, b):
    return jnp.dot(a, b, preferred_element_type=F32)


def _dot_nt(a, b):
    return lax.dot_general(a, b, (((1,), (1,)), ((), ())), preferred_element_type=F32)


def _even_kernel(x_ref, g_ref, win_ref, convw_ref, poolw_ref, pscale_ref, wout_ref,
                 xg_ref, wq_ref, kt_ref, v_ref, wo_ref, o_ref, ubuf, xbbuf, tmp_a, tmp_b, *, ts):
    si = pl.program_id(1)

    @pl.when(si == 0)
    def _():
        ubuf[0:HALO, :] = jnp.zeros((HALO, SC_WIDTH), F32)
        xbbuf[0:HALO, :] = jnp.zeros((HALO, SC_WIDTH), F32)

    x = x_ref[0]
    h = _rms(x, g_ref[...]).astype(BF16)
    z = _dot(h, win_ref[...])
    g_b = z[:, 0:SC_WIDTH]
    u = z[:, SC_WIDTH:2 * SC_WIDTH] * z[:, 2 * SC_WIDTH:3 * SC_WIDTH]
    xb = z[:, 3 * SC_WIDTH:4 * SC_WIDTH]
    ubuf[HALO:HALO + ts, :] = u
    xbbuf[HALO:HALO + ts, :] = xb

    cw = convw_ref[...]
    conv = cw[2:3, :] * u
    for k in range(CONV_K - 1):
        back = CONV_K - 1 - k
        conv = conv + cw[k:k + 1, :] * ubuf[HALO - back:HALO - back + ts, :]
    ya = g_b * conv

    def window_level(src, c0, dst, r0, shift):
        w2 = (src[r0:HALO + ts, c0:c0 + POOL_GROUP]
              + src[r0 - shift:HALO + ts - shift, c0:c0 + POOL_GROUP])
        if dst is not None:
            dst[r0:HALO + ts, :] = w2
        return w2

    t_idx = si * ts + lax.broadcasted_iota(jnp.int32, (ts, POOL_GROUP), 0)
    ys = [ya.astype(BF16)]
    for gi, w in enumerate(POOL_WINDOWS):
        lo = gi * POOL_GROUP
        xg = xb[:, lo:lo + POOL_GROUP]
        src, c0, shift, level = xbbuf, lo, 1, 1
        while 2 * shift < w:
            dst = tmp_a if src is not tmp_a else tmp_b
            window_level(src, c0, dst, SUBLANES * level, shift)
            src, c0, shift, level = dst, 0, 2 * shift, level + 1
        acc = window_level(src, c0, None, HALO, shift)
        cnt = jnp.minimum(t_idx + 1, w).astype(F32)
        d = (acc / cnt - xg).astype(BF16)
        yg = _dot(d, poolw_ref[gi]) * pscale_ref[:, lo:lo + POOL_GROUP]
        ys.append(yg.astype(BF16))

    ubuf[0:HALO, :] = ubuf[ts:ts + HALO, :]
    xbbuf[0:HALO, :] = xbbuf[ts:ts + HALO, :]

    mix = jnp.concatenate(ys, axis=-1)
    x1 = x + _dot(mix, wout_ref[...])
    o_ref[0] = _xattn_body(x1, xg_ref, wq_ref, kt_ref, v_ref, wo_ref)


def _even_mixer(x, g, w_in, conv_w, pool_w, pool_scale, w_out, xg, wq_all, kv, wo_all, layer, *, ts):
    B, S, D = x.shape
    ts = min(ts, S)
    assert S % ts == 0 and D == 2 * SC_WIDTH and w_in.shape == (D, 4 * SC_WIDTH)
    kern = functools.partial(_even_kernel, ts=ts)
    return pl.pallas_call(
        kern,
        grid=(B, S // ts),
        in_specs=[
            pl.BlockSpec((1, ts, D), lambda b, s: (b, s, 0)),
            _const_spec((1, D)),
            _const_spec(w_in.shape),
            _const_spec(conv_w.shape),
            _const_spec(pool_w.shape),
            _const_spec((1, SC_WIDTH)),
            _const_spec(w_out.shape),
        ] + _xattn_specs(D, wq_all, kv, wo_all, layer),
        out_specs=pl.BlockSpec((1, ts, D), lambda b, s: (b, s, 0)),
        out_shape=jax.ShapeDtypeStruct(x.shape, F32),
        scratch_shapes=[pltpu.VMEM((HALO + ts, SC_WIDTH), F32),
                        pltpu.VMEM((HALO + ts, SC_WIDTH), F32),
                        pltpu.VMEM((HALO + ts, POOL_GROUP), F32),
                        pltpu.VMEM((HALO + ts, POOL_GROUP), F32)],
        compiler_params=_params("arbitrary", "arbitrary"),
        name="even_mixer",
    )(x, g.reshape(1, D), w_in, conv_w, pool_w, pool_scale.reshape(1, SC_WIDTH), w_out,
      xg.reshape(1, D), wq_all, *kv, wo_all)


def _memkv_kernel(mem_ref, g_ref, wkv_ref, kt_ref, v_ref):
    h = _rms(mem_ref[0], g_ref[...]).astype(BF16)
    kv = _dot(h, wkv_ref[...])
    kt_ref[0] = kv[:, :XA_WIDTH].T.astype(BF16)
    v_ref[0] = kv[:, XA_WIDTH:].astype(BF16)


def _mem_kv(mem, g, wkv_all, layer):
    B, M, D = mem.shape
    return pl.pallas_call(
        _memkv_kernel,
        grid=(B,),
        in_specs=[pl.BlockSpec((1, M, D), lambda b: (b, 0, 0)),
                  _const_spec((1, D)), _layer_spec(wkv_all, layer)],
        out_specs=(pl.BlockSpec((1, XA_WIDTH, M), lambda b: (b, 0, 0)),
                   pl.BlockSpec((1, M, XA_WIDTH), lambda b: (b, 0, 0))),
        out_shape=(jax.ShapeDtypeStruct((B, XA_WIDTH, M), BF16),
                   jax.ShapeDtypeStruct((B, M, XA_WIDTH), BF16)),
        compiler_params=_params("arbitrary"),
        name="mem_kv",
    )(mem, g.reshape(1, D), wkv_all)


def _xattn_body(x, g_ref, wq_ref, kt_ref, v_ref, wo_ref):
    h = _rms(x, g_ref[...]).astype(BF16)
    q = (_dot(h, wq_ref[...]) * (XA_HEAD_DIM ** -0.5)).astype(BF16)
    ones_v = jnp.ones((v_ref.shape[1], XA_HEAD_DIM), BF16)
    outs = []
    for hd in range(XA_HEADS):
        lo = hd * XA_HEAD_DIM
        s = _dot(q[:, lo:lo + XA_HEAD_DIM], kt_ref[0, lo:lo + XA_HEAD_DIM, :])
        p = jnp.exp(s - jnp.max(s, axis=-1, keepdims=True)).astype(BF16)
        v_aug = jnp.concatenate([v_ref[0, :, lo:lo + XA_HEAD_DIM], ones_v], axis=1)
        o = _dot(p, v_aug)
        outs.append((o[:, :XA_HEAD_DIM] / o[:, XA_HEAD_DIM:]).astype(BF16))
    a = jnp.concatenate(outs, axis=-1)
    return x + _dot(a, wo_ref[...])


def _xattn_specs(D, wq_all, kv, wo_all, layer):
    kt, v = kv
    return [_const_spec((1, D)), _layer_spec(wq_all, layer),
            pl.BlockSpec((1,) + kt.shape[1:], lambda b, s: (b, 0, 0)),
            pl.BlockSpec((1,) + v.shape[1:], lambda b, s: (b, 0, 0)), _layer_spec(wo_all, layer)]


def _swiglu_kernel(x_ref, g_ref, wgu_ref, wd_ref, fg_ref, o_ref, *, hidden, th, final_norm):
    x = x_ref[...]
    h = _rms(x, g_ref[...]).astype(BF16)
    acc = x
    for j in range(hidden // th):
        gate = _dot(h, wgu_ref[:, j * th:(j + 1) * th])
        up = _dot(h, wgu_ref[:, hidden + j * th:hidden + (j + 1) * th])
        act = (gate * jax.nn.sigmoid(gate) * up).astype(BF16)
        acc = acc + _dot(act, wd_ref[j * th:(j + 1) * th, :])
    if final_norm:
        acc = _rms(acc, fg_ref[...])
    o_ref[...] = acc


def _swiglu(x, g, w_gu_all, w_d_all, final_g, layer, *, tm, th, final_norm):
    B, S, D = x.shape
    T = B * S
    tm = min(tm, T)
    hidden = w_d_all.shape[1]
    assert T % tm == 0 and hidden % th == 0 and w_gu_all.shape[2] == 2 * hidden
    kern = functools.partial(_swiglu_kernel, hidden=hidden, th=th, final_norm=final_norm)
    out = pl.pallas_call(
        kern,
        grid=(T // tm,),
        in_specs=[
            pl.BlockSpec((tm, D), lambda i: (i, 0)),
            _const_spec((1, D)),
            _layer_spec(w_gu_all, layer),
            _layer_spec(w_d_all, layer),
            _const_spec((1, D)),
        ],
        out_specs=pl.BlockSpec((tm, D), lambda i: (i, 0)),
        out_shape=jax.ShapeDtypeStruct((T, D), F32),
        compiler_params=_params("arbitrary"),
        name="swiglu",
    )(x.reshape(T, D), g.reshape(1, D), w_gu_all, w_d_all, final_g.reshape(1, D))
    return out.reshape(B, S, D)


def _odd_proj_kernel(x_ref, pos_ref, g_ref, wmain_ref, wgt_ref, gbias_ref,
                     qg_ref, kvg_ref, wuqt_ref, wukt_ref, wuvt_ref, invf_ref,
                     qm_ref, km_ref, vm_ref, om_ref, gates_ref, qp_ref, kp_ref, vt_ref):
    x = x_ref[0]
    h = _rms(x, g_ref[...]).astype(BF16)
    c0 = 4 * ML_WIDTH

    def ml_cols(i):
        return _dot(h, wmain_ref[:, i * ML_WIDTH:(i + 1) * ML_WIDTH])

    zl = _dot(h, wmain_ref[:, c0:])
    c_q = zl[:, 0:Q_LORA]
    c_kv = zl[:, Q_LORA:Q_LORA + KV_LORA]
    kr = zl[:, Q_LORA + KV_LORA:Q_LORA + KV_LORA + LANES]
    gates_ref[0] = _dot_nt(wgt_ref[...], h) + gbias_ref[...]

    half = QK_ROPE // 2
    ang_t = invf_ref[...] * pos_ref[0].astype(F32)
    cos_c, sin_c = jnp.cos(ang_t), jnp.sin(ang_t)
    ts = ang_t.shape[1]
    pad_lo = jnp.zeros((QK_NOPE, ts), F32)
    pad_hi = jnp.zeros((LANES - QK_NOPE - QK_ROPE, ts), F32)
    cos_tt = jnp.concatenate([pad_lo + 1.0, cos_c, cos_c, pad_hi], axis=0)
    sin_tt = jnp.concatenate([pad_lo, -sin_c, sin_c, pad_hi], axis=0)
    scale = (QK_NOPE + QK_ROPE) ** -0.5 * LOG2_E
    cos_q, sin_q = cos_tt * scale, sin_tt * scale

    def rotate(t, cos_tab, sin_tab):
        r0, r1, r2 = QK_NOPE, QK_NOPE + half, QK_NOPE + QK_ROPE
        t_sw = jnp.concatenate([t[:r0], t[r1:r2], t[r0:r1], t[r2:]], axis=0)
        return t * cos_tab + t_sw * sin_tab

    qm_ref[0] = ml_cols(0).astype(BF16)
    km_ref[0] = (ml_cols(1) * (ML_HEAD_DIM ** -0.5)).astype(BF16)

    cqn = _rms(c_q, qg_ref[...]).astype(BF16)
    qa_t = _dot_nt(wuqt_ref[...], cqn)
    ckn = _rms(c_kv, kvg_ref[...]).astype(BF16)
    kn_t = _dot_nt(wukt_ref[...], ckn)
    vt_ref[0] = _dot_nt(wuvt_ref[...], ckn).astype(BF16)
    k_rope_t = rotate(kr.T, cos_tt, sin_tt)[QK_NOPE:QK_NOPE + QK_ROPE]
    qk_dim = QK_NOPE + QK_ROPE

    def rope_heads(first, last):
        for hd in range(first, last):
            lo = hd * LANES
            q_blk = jnp.concatenate([qa_t[hd * qk_dim:(hd + 1) * qk_dim], pad_hi], axis=0)
            qp_ref[0, lo:lo + LANES, :] = rotate(q_blk, cos_q, sin_q).astype(BF16)
            k_blk = jnp.concatenate([kn_t[hd * QK_NOPE:(hd + 1) * QK_NOPE], k_rope_t, pad_hi], axis=0)
            kp_ref[0, :, lo:lo + LANES] = k_blk.T.astype(BF16)

    vm_ref[0] = ml_cols(2).astype(BF16)
    rope_heads(0, MLA_HEADS // 2)
    om_ref[0] = ml_cols(3).astype(BF16)
    rope_heads(MLA_HEADS // 2, MLA_HEADS)


def _odd_proj(x, pos3, g, w_main, w_gt, gbias, q_g, kv_g, w_uqt, w_ukt, w_uvt, invf, *, ts):
    B, S, D = x.shape
    ts = min(ts, S)
    assert S % ts == 0
    tok = lambda n: pl.BlockSpec((1, ts, n), lambda b, s: (b, s, 0))
    out_shapes = (
        jax.ShapeDtypeStruct((B, S, ML_WIDTH), BF16),
        jax.ShapeDtypeStruct((B, S, ML_WIDTH), BF16),
        jax.ShapeDtypeStruct((B, S, ML_WIDTH), BF16),
        jax.ShapeDtypeStruct((B, S, ML_WIDTH), BF16),
        jax.ShapeDtypeStruct((B, 2 * ML_HEADS, S), F32),
        jax.ShapeDtypeStruct((B, MLA_HEADS * LANES, S), BF16),
        jax.ShapeDtypeStruct((B, S, MLA_HEADS * LANES), BF16),
        jax.ShapeDtypeStruct((B, MLA_WIDTH, S), BF16),
    )
    return pl.pallas_call(
        _odd_proj_kernel,
        grid=(B, S // ts),
        in_specs=[
            tok(D),
            pl.BlockSpec((1, 1, ts), lambda b, s: (b, 0, s)),
            _const_spec((1, D)),
            _const_spec(w_main.shape),
            _const_spec(w_gt.shape),
            _const_spec(gbias.shape),
            _const_spec(q_g.shape),
            _const_spec(kv_g.shape),
            _const_spec(w_uqt.shape),
            _const_spec(w_ukt.shape),
            _const_spec(w_uvt.shape),
            _const_spec(invf.shape),
        ],
        out_specs=(tok(ML_WIDTH), tok(ML_WIDTH), tok(ML_WIDTH), tok(ML_WIDTH),
                   pl.BlockSpec((1, 2 * ML_HEADS, ts), lambda b, s: (b, 0, s)),
                   pl.BlockSpec((1, MLA_HEADS * LANES, ts), lambda b, s: (b, 0, s)),
                   tok(MLA_HEADS * LANES),
                   pl.BlockSpec((1, MLA_WIDTH, ts), lambda b, s: (b, 0, s))),
        out_shape=out_shapes,
        compiler_params=_params("arbitrary", "arbitrary"),
        name="odd_proj",
    )(x, pos3, g.reshape(1, D), w_main, w_gt, gbias, q_g, kv_g, w_uqt, w_ukt, w_uvt, invf)


def _split3(a):
    hi = a.astype(BF16)
    r1 = a - hi.astype(F32)
    mid = r1.astype(BF16)
    lo = (r1 - mid.astype(F32)).astype(BF16)
    return hi, mid, lo


def _log2_sigmoid(x):
    return (jnp.minimum(x, 0.0) - jnp.log(1.0 + jnp.exp(-jnp.abs(x)))) * LOG2_E


def _mlstm_kernel(q_ref, k_ref, v_ref, o_ref, gt_ref, ng_ref, h_ref, c_scr, m_scr, *, L, nc):
    ci = pl.program_id(1)

    @pl.when(ci == 0)
    def _():
        c_scr[...] = jnp.zeros(c_scr.shape, F32)
        m_scr[...] = jnp.zeros(m_scr.shape, F32)

    row = lax.broadcasted_iota(jnp.int32, (L, L), 0)
    col = lax.broadcasted_iota(jnp.int32, (L, L), 1)
    causal = col <= row
    ones_v = jnp.ones((L, LANES), BF16)
    reps = L // LANES

    def widen(a):
        return jnp.concatenate([a] * reps, axis=1)

    for ch, hd in [(ch, hd) for ch in range(nc) for hd in range(ML_HEADS)]:
        tok = slice(ch * L, (ch + 1) * L)
        if hd == 0:
            gt = gt_ref[0, :, tok]
            lf_t = _log2_sigmoid(gt)
            b_t = sum(_dot(t, (row <= col).astype(BF16)) for t in _split3(lf_t))
            b_c = jnp.concatenate([b_t, jnp.zeros((LANES - 2 * ML_HEADS, L), F32)], axis=0).T
        lo = hd * ML_HEAD_DIM
        fg = ML_HEADS + hd
        q = q_ref[0, tok, lo:lo + ML_HEAD_DIM]
        k = k_ref[0, tok, lo:lo + ML_HEAD_DIM]
        v_aug = jnp.concatenate([v_ref[0, tok, lo:lo + ML_HEAD_DIM], ones_v], axis=1)
        g_row = gt[hd:hd + 1, :] * LOG2_E - b_t[fg:fg + 1, :]
        gtot = jnp.sum(lf_t[fg:fg + 1, :], axis=1, keepdims=True)
        b_col = jnp.broadcast_to(b_c[:, fg:fg + 1], (L, LANES))

        m_prev = m_scr[hd]
        c_prev = c_scr[hd]

        g_mask = jnp.where(causal, g_row, NEG_BIG)
        gmax = jnp.max(g_mask, axis=1, keepdims=True)
        h_col = jnp.maximum(m_prev, jnp.broadcast_to(gmax, (L, LANES)))
        w_inter = jnp.exp2(m_prev - h_col)
        sw = jnp.exp2(g_mask - widen(h_col)) * _dot_nt(q, k)
        out = (jnp.concatenate([w_inter, w_inter], axis=1) * _dot(q, c_prev.astype(BF16))
               + _dot(sw.astype(BF16), v_aug))
        den = jnp.maximum(jnp.abs(out[:, ML_HEAD_DIM:]), jnp.exp2(-(b_col + h_col)))
        hc = out[:, :ML_HEAD_DIM] / den

        og = jax.nn.sigmoid(o_ref[0, tok, lo:lo + ML_HEAD_DIM].astype(F32))
        hg = og * hc
        ms = jnp.mean(hg * hg, axis=1, keepdims=True)
        h_ref[0, tok, lo:lo + ML_HEAD_DIM] = (
            hg * lax.rsqrt(ms + EPS) * ng_ref[:, lo:lo + ML_HEAD_DIM]).astype(BF16)

        m_prev1 = m_prev[:, 0:1]
        a_row = gtot + g_row
        m_loc = jnp.max(a_row, axis=1, keepdims=True)
        m_new = jnp.maximum(gtot + m_prev1, m_loc)
        s_old = jnp.exp2(gtot + m_prev1 - m_new)
        w_loc = jnp.exp2(a_row - m_new)
        kw_t = (k.T.astype(F32) * w_loc).astype(BF16)
        c_scr[hd] = s_old * c_prev + _dot(kw_t, v_aug)
        m_scr[hd] = jnp.broadcast_to(m_new, (1, LANES))


def _mlstm(q, k, v, o, gates_t, norm_g, *, L, nc):
    B, S, W = q.shape
    L = min(L, S)
    nc = min(nc, S // L)
    assert S % (nc * L) == 0 and L % LANES == 0 and W == ML_WIDTH
    kern = functools.partial(_mlstm_kernel, L=L, nc=nc)
    tok = pl.BlockSpec((1, nc * L, W), lambda b, c: (b, c, 0))
    return pl.pallas_call(
        kern,
        grid=(B, S // (nc * L)),
        in_specs=[tok, tok, tok, tok,
                  pl.BlockSpec((1, 2 * ML_HEADS, nc * L), lambda b, c: (b, 0, c)),
                  _const_spec((1, W))],
        out_specs=tok,
        out_shape=jax.ShapeDtypeStruct((B, S, W), BF16),
        scratch_shapes=[pltpu.VMEM((ML_HEADS, ML_HEAD_DIM, 2 * ML_HEAD_DIM), F32),
                        pltpu.VMEM((ML_HEADS, 1, LANES), F32)],
        compiler_params=_params("arbitrary", "arbitrary"),
        name="mlstm",
    )(q, k, v, o, gates_t, norm_g.reshape(1, W))


def _mla_kernel(qt_ref, k_ref, vt_ref, o_ref, s00, s01, s10, s11, *, tq, nt):
    g = pl.program_id(2)
    heads = 2
    pieces = 2
    rows = tq // pieces
    hk = tq // 2
    s_scr = ((s00, s01), (s10, s11))
    ones_rows = jnp.ones((BF16_ROWS, tq), BF16)

    def run_tile(t, qi, sa, sb):
        lanes = slice(t * tq, (t + 1) * tq)

        def score_piece(j, slot, hh, pc):
            start = pl.multiple_of(j * tq + pc * rows, rows)
            kb = k_ref[0, pl.ds(start, rows), hh * LANES:(hh + 1) * LANES]
            qt = qt_ref[0, hh * LANES:(hh + 1) * LANES, lanes]
            s_scr[slot][hh][pc * rows:(pc + 1) * rows, :] = _dot(kb, qt)

        def stage_exp(j, slot, hh, m_new, acc, half):
            start = pl.multiple_of(j * tq + half * hk, hk)
            vtb = jnp.concatenate([vt_ref[0, hh * V_DIM:(hh + 1) * V_DIM, pl.ds(start, hk)],
                                   ones_rows[:, :hk]], axis=0)
            p = jnp.exp2(s_scr[slot][hh][half * hk:(half + 1) * hk, :] - m_new).astype(BF16)
            return acc + _dot(vtb, p)

        def block(j, slot, carry, nxt):
            new = []
            for hh in range(heads):
                m, acc = carry[hh]
                score_piece(nxt[0], nxt[1], hh, 0)
                m_new = jnp.maximum(m, jnp.max(s_scr[slot][hh][...], axis=0, keepdims=True))
                score_piece(nxt[0], nxt[1], hh, 1)
                acc = stage_exp(j, slot, hh, m_new, jnp.exp2(m - m_new) * acc, 0)
                acc = stage_exp(j, slot, hh, m_new, acc, 1)
                new.append((m_new, acc))
            return tuple(new)

        def diag_block(slot, carry):
            start = pl.multiple_of(qi * tq, tq)
            kk = lax.broadcasted_iota(jnp.int32, (hk, tq), 0)
            qq = lax.broadcasted_iota(jnp.int32, (hk, tq), 1)
            kb = lax.broadcasted_iota(jnp.int32, (hk, hk), 0)
            qb = lax.broadcasted_iota(jnp.int32, (hk, hk), 1)
            hs = range(heads)
            top = [jnp.where(kk <= qq, s_scr[slot][hh][0:hk, :], NEG_BIG) for hh in hs]
            bot = [jnp.where(kb <= qb, s_scr[slot][hh][hk:, hk:], NEG_BIG) for hh in hs]
            neg = jnp.full((hk, hk), NEG_BIG, F32)
            m_new = [jnp.maximum(carry[hh][0], jnp.max(
                jnp.maximum(top[hh], jnp.concatenate([neg, bot[hh]], axis=1)), axis=0, keepdims=True))
                for hh in hs]
            vt_aug = [jnp.concatenate([vt_ref[0, hh * V_DIM:(hh + 1) * V_DIM, pl.ds(start, tq)],
                                       ones_rows], axis=0) for hh in hs]
            p_top = [jnp.exp2(top[hh] - m_new[hh]).astype(BF16) for hh in hs]
            p_bot = [jnp.exp2(bot[hh] - jnp.broadcast_to(m_new[hh], (hk, tq))[:, hk:]).astype(BF16)
                     for hh in hs]
            outs = []
            for hh in hs:
                m, acc = carry[hh]
                acc = jnp.exp2(m - m_new[hh]) * acc + _dot(vt_aug[hh][:, :hk], p_top[hh])
                hi = acc[:, hk:] + _dot(vt_aug[hh][:, hk:], p_bot[hh])
                acc = jnp.concatenate([acc[:, :hk], hi], axis=1)
                outs.append(acc[:V_DIM] / acc[V_DIM:V_DIM + 1])
            return jnp.concatenate(outs, axis=0)

        for hh in range(heads):
            for pc in range(pieces):
                score_piece(0, sa, hh, pc)

        def body(i, c):
            c = block(2 * i, sa, c, (2 * i + 1, sb))
            return block(2 * i + 1, sb, c, (2 * i + 2, sa))

        init = tuple((jnp.full((1, tq), NEG_BIG, F32), jnp.zeros((V_DIM + BF16_ROWS, tq), F32))
                     for _ in range(heads))
        carry = lax.fori_loop(0, qi // 2, body, init)
        if t % 2 == 0:
            ot = diag_block(sa, carry)
        else:
            ot = diag_block(sb, block(qi - 1, sa, carry, (qi, sb)))
        o_ref[0, :, lanes] = ot.astype(BF16)

    slot_pairs = ((0, 1), (1, 0), (1, 0), (0, 1))
    for t in range(nt):
        run_tile(t, nt * g + t, *slot_pairs[t % 4])


def _mla_attention(qpt, kp, vt, *, tq, nt):
    B, S, _ = kp.shape
    tq = min(tq, S // 2)
    nt = min(nt, S // tq)
    assert nt % 2 == 0 and S % (nt * tq) == 0
    pairs = MLA_HEADS // 2
    kern = functools.partial(_mla_kernel, tq=tq, nt=nt)
    return pl.pallas_call(
        kern,
        grid=(B, pairs, S // (nt * tq)),
        in_specs=[
            pl.BlockSpec((1, 2 * LANES, nt * tq), lambda b, p, i: (b, p, i)),
            pl.BlockSpec((1, S, 2 * LANES), lambda b, p, i: (b, 0, p)),
            pl.BlockSpec((1, 2 * V_DIM, S), lambda b, p, i: (b, p, 0)),
        ],
        out_specs=pl.BlockSpec((1, LANES, nt * tq), lambda b, p, i: (b, p, i)),
        out_shape=jax.ShapeDtypeStruct((B, MLA_WIDTH, S), BF16),
        scratch_shapes=[pltpu.VMEM((tq, tq), F32) for _ in range(4)],
        compiler_params=_params("arbitrary", "arbitrary", "arbitrary"),
        name="mla_attention",
    )(qpt, kp, vt)


def _odd_out_kernel(x_ref, hm_ref, ha_ref, wout_ref, xg_ref, wq_ref, kt_ref, v_ref, wo_ref, o_ref):
    mix = jnp.concatenate([hm_ref[0], ha_ref[0].T], axis=-1)
    x1 = x_ref[0] + _dot(mix, wout_ref[...])
    o_ref[0] = _xattn_body(x1, xg_ref, wq_ref, kt_ref, v_ref, wo_ref)


def _odd_out(x, hm, ha, w_out, xg, wq_all, kv, wo_all, layer, *, ts):
    B, S, D = x.shape
    ts = min(ts, S)
    assert S % ts == 0
    tok = lambda n: pl.BlockSpec((1, ts, n), lambda b, s: (b, s, 0))
    return pl.pallas_call(
        _odd_out_kernel,
        grid=(B, S // ts),
        in_specs=[tok(D), tok(ML_WIDTH), pl.BlockSpec((1, MLA_WIDTH, ts), lambda b, s: (b, 0, s)),
                  _const_spec(w_out.shape)]
        + _xattn_specs(D, wq_all, kv, wo_all, layer),
        out_specs=tok(D),
        out_shape=jax.ShapeDtypeStruct(x.shape, F32),
        compiler_params=_params("arbitrary", "arbitrary"),
        name="odd_out",
    )(x, hm, ha, w_out, xg.reshape(1, D), wq_all, *kv, wo_all)


def _pack_odd_weights(w_in, w_uq, w_ukv):
    D = w_in.shape[0]
    c = 4 * ML_WIDTH
    w_g = w_in[:, c:c + 2 * ML_HEADS]
    c += 2 * ML_HEADS
    w_cq = w_in[:, c:c + Q_LORA]
    c += Q_LORA
    w_ckv = w_in[:, c:c + KV_LORA]
    c += KV_LORA
    w_kr = w_in[:, c:c + QK_ROPE]
    zl = jnp.zeros((D, QK_NOPE), w_in.dtype)
    zr = jnp.zeros((D, LANES - QK_NOPE - QK_ROPE), w_in.dtype)
    w_main = jnp.concatenate([w_in[:, :4 * ML_WIDTH], w_cq, w_ckv,
                              zl, w_kr, zr], axis=1).astype(BF16)
    w_gt = w_g.T.astype(BF16)

    w_uqt = w_uq.T
    ukv = w_ukv.reshape(KV_LORA, MLA_HEADS, QK_NOPE + V_DIM)
    w_ukt = ukv[..., :QK_NOPE].reshape(KV_LORA, MLA_HEADS * QK_NOPE).T
    w_uvt = ukv[..., QK_NOPE:].reshape(KV_LORA, MLA_WIDTH).T
    return w_main, w_gt, w_uqt.astype(BF16), w_ukt.astype(BF16), w_uvt.astype(BF16)


def _rope_inv_freq():
    half = QK_ROPE // 2
    inv = ROPE_THETA ** (-jnp.arange(half, dtype=F32) / half)
    return inv.reshape(half, 1)


TS_EVEN = 1024
TM_FFN = 1024
TH_FFN = 256
TS_ODD = 1024
L_MLSTM = 256
NC_MLSTM = 4
TQ_MLA = 1024
NT_MLA = 4
TS_OUT = 1024


def kernel(x, mem, positions, norm_mix_g, norm_xattn_g, mem_norm_g, xattn_wq, xattn_wkv, xattn_wo,
           norm_ffn_g, ffn_w_gate_up, ffn_w_down, ev_w_in, ev_conv_w, ev_pool_w, ev_pool_scale,
           ev_w_out, od_w_in, od_gate_bias, od_ml_norm_g, od_q_norm_g, od_kv_norm_g, od_w_uq,
           od_w_ukv, od_w_out, final_norm_g):
    depth = norm_mix_g.shape[0]
    B, S, D = x.shape
    pos3 = positions.reshape(B, 1, S)
    invf = _rope_inv_freq()
    wq_all, wkv_all, wo_all = (w.astype(BF16) for w in (xattn_wq, xattn_wkv, xattn_wo))
    w_gu_all, w_d_all = ffn_w_gate_up.astype(BF16), ffn_w_down.astype(BF16)
    for layer in range(depth):
        kv = _mem_kv(mem, mem_norm_g[layer], wkv_all, layer)
        xa = (norm_xattn_g[layer], wq_all, kv, wo_all, layer)
        if layer % 2 == 0:
            e = layer // 2
            x = _even_mixer(x, norm_mix_g[layer], ev_w_in[e].astype(BF16), ev_conv_w[e],
                            ev_pool_w[e].astype(BF16), ev_pool_scale[e], ev_w_out[e].astype(BF16),
                            *xa, ts=TS_EVEN)
        else:
            o = layer // 2
            w_main, w_gt, w_uqt, w_ukt, w_uvt = _pack_odd_weights(
                od_w_in[o], od_w_uq[o], od_w_ukv[o])
            qm, km, vm, om, gates_t, qp, kp, vt = _odd_proj(
                x, pos3, norm_mix_g[layer], w_main, w_gt, od_gate_bias[o].reshape(2 * ML_HEADS, 1),
                od_q_norm_g[o].reshape(1, Q_LORA), od_kv_norm_g[o].reshape(1, KV_LORA),
                w_uqt, w_ukt, w_uvt, invf, ts=TS_ODD)
            hm = _mlstm(qm, km, vm, om, gates_t, od_ml_norm_g[o], L=L_MLSTM, nc=NC_MLSTM)
            ha = _mla_attention(qp, kp, vt, tq=TQ_MLA, nt=NT_MLA)
            x = _odd_out(x, hm, ha, od_w_out[o].astype(BF16), *xa, ts=TS_OUT)
        x = _swiglu(x, norm_ffn_g[layer], w_gu_all, w_d_all, final_norm_g, layer,
                    tm=TM_FFN, th=TH_FFN, final_norm=(layer == depth - 1))
    return x
```

```python
import functools

import jax
import jax.numpy as jnp
from jax import lax
from jax.experimental import pallas as pl
from jax.experimental.pallas import tpu as pltpu

F32 = jnp.float32
BF16 = jnp.bfloat16

EPS = 1e-6
ROPE_THETA = 10000.0
LANES = 128
SUBLANES = 8
BF16_ROWS = 16
VMEM_LIMIT_BYTES = 56 * 1024 * 1024
NEG_BIG = -1e30
LOG2_E = 1.4426950408889634

SC_WIDTH = 512
CONV_K = 3
POOL_WINDOWS = (2, 4, 8, 16)
POOL_GROUP = 128
HALO = 2 * max(POOL_WINDOWS)
ML_HEADS = 4
ML_HEAD_DIM = 128
ML_WIDTH = ML_HEADS * ML_HEAD_DIM
MLA_HEADS = 8
QK_NOPE = 64
QK_ROPE = 32
V_DIM = 64
Q_LORA = 384
KV_LORA = 256
MLA_WIDTH = MLA_HEADS * V_DIM
XA_HEADS = 4
XA_HEAD_DIM = 128
XA_WIDTH = XA_HEADS * XA_HEAD_DIM


def _params(*sem):
    return pltpu.CompilerParams(dimension_semantics=sem, vmem_limit_bytes=VMEM_LIMIT_BYTES)


def _const_spec(shape):
    nd = len(shape)
    return pl.BlockSpec(shape, lambda *_: (0,) * nd, pipeline_mode=pl.Buffered(1))


def _layer_spec(stacked, layer):
    return pl.BlockSpec((None,) + stacked.shape[1:], lambda *_: (layer, 0, 0),
                        pipeline_mode=pl.Buffered(1))


def _rms(x, g):
    ms = jnp.mean(x * x, axis=-1, keepdims=True)
    return x * lax.rsqrt(ms + EPS) * g


def _dot(a, b):
    return jnp.dot(a, b, preferred_element_type=F32)


def _dot_nt(a, b):
    return lax.dot_general(a, b, (((1,), (1,)), ((), ())), preferred_element_type=F32)


def _even_kernel(x_ref, g_ref, win_ref, convw_ref, poolw_ref, pscale_ref, wout_ref,
                 xg_ref, wq_ref, kt_ref, v_ref, wo_ref, o_ref, ubuf, xbbuf, tmp_a, tmp_b, *, ts):
    si = pl.program_id(1)

    @pl.when(si == 0)
    def _():
        ubuf[0:HALO, :] = jnp.zeros((HALO, SC_WIDTH), F32)
        xbbuf[0:HALO, :] = jnp.zeros((HALO, SC_WIDTH), F32)

    x = x_ref[0]
    h = _rms(x, g_ref[...]).astype(BF16)
    z = _dot(h, win_ref[...])
    g_b = z[:, 0:SC_WIDTH]
    u = z[:, SC_WIDTH:2 * SC_WIDTH] * z[:, 2 * SC_WIDTH:3 * SC_WIDTH]
    xb = z[:, 3 * SC_WIDTH:4 * SC_WIDTH]
    ubuf[HALO:HALO + ts, :] = u
    xbbuf[HALO:HALO + ts, :] = xb

    cw = convw_ref[...]
    conv = cw[2:3, :] * u
    for k in range(CONV_K - 1):
        back = CONV_K - 1 - k
        conv = conv + cw[k:k + 1, :] * ubuf[HALO - back:HALO - back + ts, :]
    ya = g_b * conv

    def window_level(src, c0, dst, r0, shift):
        w2 = (src[r0:HALO + ts, c0:c0 + POOL_GROUP]
              + src[r0 - shift:HALO + ts - shift, c0:c0 + POOL_GROUP])
        if dst is not None:
            dst[r0:HALO + ts, :] = w2
        return w2

    t_idx = si * ts + lax.broadcasted_iota(jnp.int32, (ts, POOL_GROUP), 0)
    ys = [ya.astype(BF16)]
    for gi, w in enumerate(POOL_WINDOWS):
        lo = gi * POOL_GROUP
        xg = xb[:, lo:lo + POOL_GROUP]
        src, c0, shift, level = xbbuf, lo, 1, 1
        while 2 * shift < w:
            dst = tmp_a if src is not tmp_a else tmp_b
            window_level(src, c0, dst, SUBLANES * level, shift)
            src, c0, shift, level = dst, 0, 2 * shift, level + 1
        acc = window_level(src, c0, None, HALO, shift)
        cnt = jnp.minimum(t_idx + 1, w).astype(F32)
        d = (acc / cnt - xg).astype(BF16)
        yg = _dot(d, poolw_ref[gi]) * pscale_ref[:, lo:lo + POOL_GROUP]
        ys.append(yg.astype(BF16))

    ubuf[0:HALO, :] = ubuf[ts:ts + HALO, :]
    xbbuf[0:HALO, :] = xbbuf[ts:ts + HALO, :]

    mix = jnp.concatenate(ys, axis=-1)
    x1 = x + _dot(mix, wout_ref[...])
    o_ref[0] = _xattn_body(x1, xg_ref, wq_ref, kt_ref, v_ref, wo_ref)


def _even_mixer(x, g, w_in, conv_w, pool_w, pool_scale, w_out, xg, wq_all, kv, wo_all, layer, *, ts):
    B, S, D = x.shape
    ts = min(ts, S)
    assert S % ts == 0 and D == 2 * SC_WIDTH and w_in.shape == (D, 4 * SC_WIDTH)
    kern = functools.partial(_even_kernel, ts=ts)
    return pl.pallas_call(
        kern,
        grid=(B, S // ts),
        in_specs=[
            pl.BlockSpec((1, ts, D), lambda b, s: (b, s, 0)),
            _const_spec((1, D)),
            _const_spec(w_in.shape),
            _const_spec(conv_w.shape),
            _const_spec(pool_w.shape),
            _const_spec((1, SC_WIDTH)),
            _const_spec(w_out.shape),
        ] + _xattn_specs(D, wq_all, kv, wo_all, layer),
        out_specs=pl.BlockSpec((1, ts, D), lambda b, s: (b, s, 0)),
        out_shape=jax.ShapeDtypeStruct(x.shape, F32),
        scratch_shapes=[pltpu.VMEM((HALO + ts, SC_WIDTH), F32),
                        pltpu.VMEM((HALO + ts, SC_WIDTH), F32),
                        pltpu.VMEM((HALO + ts, POOL_GROUP), F32),
                        pltpu.VMEM((HALO + ts, POOL_GROUP), F32)],
        compiler_params=_params("arbitrary", "arbitrary"),
        name="even_mixer",
    )(x, g.reshape(1, D), w_in, conv_w, pool_w, pool_scale.reshape(1, SC_WIDTH), w_out,
      xg.reshape(1, D), wq_all, *kv, wo_all)


def _memkv_kernel(mem_ref, g_ref, wkv_ref, kt_ref, v_ref):
    h = _rms(mem_ref[0], g_ref[...]).astype(BF16)
    kv = _dot(h, wkv_ref[...])
    kt_ref[...] = kv[:, :XA_WIDTH].T.astype(BF16)
    v_ref[...] = kv[:, XA_WIDTH:].astype(BF16)


def _mem_kv(mem, g_all, wkv_all):
    B, M, D = mem.shape
    depth = wkv_all.shape[0]
    return pl.pallas_call(
        _memkv_kernel,
        grid=(depth, B),
        in_specs=[pl.BlockSpec((1, M, D), lambda l, b: (b, 0, 0)),
                  pl.BlockSpec((None, 1, D), lambda l, b: (l, 0, 0)),
                  pl.BlockSpec((None,) + wkv_all.shape[1:], lambda l, b: (l, 0, 0))],
        out_specs=(pl.BlockSpec((None, None, XA_WIDTH, M), lambda l, b: (l, b, 0, 0)),
                   pl.BlockSpec((None, None, M, XA_WIDTH), lambda l, b: (l, b, 0, 0))),
        out_shape=(jax.ShapeDtypeStruct((depth, B, XA_WIDTH, M), BF16),
                   jax.ShapeDtypeStruct((depth, B, M, XA_WIDTH), BF16)),
        compiler_params=_params("arbitrary", "arbitrary"),
        name="mem_kv",
    )(mem, g_all.reshape(depth, 1, D), wkv_all)


def _xattn_body(x, g_ref, wq_ref, kt_ref, v_ref, wo_ref):
    h = _rms(x, g_ref[...]).astype(BF16)
    q = (_dot(h, wq_ref[...]) * (XA_HEAD_DIM ** -0.5)).astype(BF16)
    ones_v = jnp.ones((v_ref.shape[1], XA_HEAD_DIM), BF16)
    outs = []
    for hd in range(XA_HEADS):
        lo = hd * XA_HEAD_DIM
        s = _dot(q[:, lo:lo + XA_HEAD_DIM], kt_ref[0, lo:lo + XA_HEAD_DIM, :])
        p = jnp.exp(s - jnp.max(s, axis=-1, keepdims=True)).astype(BF16)
        v_aug = jnp.concatenate([v_ref[0, :, lo:lo + XA_HEAD_DIM], ones_v], axis=1)
        o = _dot(p, v_aug)
        outs.append((o[:, :XA_HEAD_DIM] / o[:, XA_HEAD_DIM:]).astype(BF16))
    a = jnp.concatenate(outs, axis=-1)
    return x + _dot(a, wo_ref[...])


def _xattn_specs(D, wq_all, kv, wo_all, layer):
    kt, v = kv
    return [_const_spec((1, D)), _layer_spec(wq_all, layer),
            pl.BlockSpec((None, 1) + kt.shape[2:], lambda b, s: (layer, b, 0, 0)),
            pl.BlockSpec((None, 1) + v.shape[2:], lambda b, s: (layer, b, 0, 0)),
            _layer_spec(wo_all, layer)]


def _swiglu_kernel(x_ref, g_ref, wgu_ref, wd_ref, fg_ref, o_ref, *, hidden, th, final_norm):
    x = x_ref[...]
    h = _rms(x, g_ref[...]).astype(BF16)
    acc = x
    for j in range(hidden // th):
        gate = _dot(h, wgu_ref[:, j * th:(j + 1) * th])
        up = _dot(h, wgu_ref[:, hidden + j * th:hidden + (j + 1) * th])
        act = (gate * jax.nn.sigmoid(gate) * up).astype(BF16)
        acc = acc + _dot(act, wd_ref[j * th:(j + 1) * th, :])
    if final_norm:
        acc = _rms(acc, fg_ref[...])
    o_ref[...] = acc


def _swiglu(x, g, w_gu_all, w_d_all, final_g, layer, *, tm, th, final_norm):
    B, S, D = x.shape
    T = B * S
    tm = min(tm, T)
    hidden = w_d_all.shape[1]
    assert T % tm == 0 and hidden % th == 0 and w_gu_all.shape[2] == 2 * hidden
    kern = functools.partial(_swiglu_kernel, hidden=hidden, th=th, final_norm=final_norm)
    out = pl.pallas_call(
        kern,
        grid=(T // tm,),
        in_specs=[
            pl.BlockSpec((tm, D), lambda i: (i, 0)),
            _const_spec((1, D)),
            _layer_spec(w_gu_all, layer),
            _layer_spec(w_d_all, layer),
            _const_spec((1, D)),
        ],
        out_specs=pl.BlockSpec((tm, D), lambda i: (i, 0)),
        out_shape=jax.ShapeDtypeStruct((T, D), F32),
        compiler_params=_params("arbitrary"),
        name="swiglu",
    )(x.reshape(T, D), g.reshape(1, D), w_gu_all, w_d_all, final_g.reshape(1, D))
    return out.reshape(B, S, D)


def _odd_proj_kernel(x_ref, pos_ref, g_ref, wmain_ref, wgt_ref, gbias_ref,
                     qg_ref, kvg_ref, wuqt_ref, wukt_ref, wuvt_ref, invf_ref,
                     qm_ref, km_ref, vm_ref, om_ref, gates_ref, qp_ref, kp_ref, vt_ref):
    x = x_ref[0]
    h = _rms(x, g_ref[...]).astype(BF16)
    c0 = 4 * ML_WIDTH

    def ml_cols(i):
        return _dot(h, wmain_ref[:, i * ML_WIDTH:(i + 1) * ML_WIDTH])

    zl = _dot(h, wmain_ref[:, c0:])
    c_q = zl[:, 0:Q_LORA]
    c_kv = zl[:, Q_LORA:Q_LORA + KV_LORA]
    kr = zl[:, Q_LORA + KV_LORA:Q_LORA + KV_LORA + LANES]
    gates_ref[0] = _dot_nt(wgt_ref[...], h) + gbias_ref[...]

    half = QK_ROPE // 2
    ang_t = invf_ref[...] * pos_ref[0].astype(F32)
    cos_c, sin_c = jnp.cos(ang_t), jnp.sin(ang_t)
    ts = ang_t.shape[1]
    pad_lo = jnp.zeros((QK_NOPE, ts), F32)
    pad_hi = jnp.zeros((LANES - QK_NOPE - QK_ROPE, ts), F32)
    cos_tt = jnp.concatenate([pad_lo + 1.0, cos_c, cos_c, pad_hi], axis=0)
    sin_tt = jnp.concatenate([pad_lo, -sin_c, sin_c, pad_hi], axis=0)
    scale = (QK_NOPE + QK_ROPE) ** -0.5 * LOG2_E
    cos_q, sin_q = cos_tt * scale, sin_tt * scale

    def rotate(t, cos_tab, sin_tab):
        r0, r1, r2 = QK_NOPE, QK_NOPE + half, QK_NOPE + QK_ROPE
        t_sw = jnp.concatenate([t[:r0], t[r1:r2], t[r0:r1], t[r2:]], axis=0)
        return t * cos_tab + t_sw * sin_tab

    qm_ref[0] = ml_cols(0).astype(BF16)
    km_ref[0] = (ml_cols(1) * (ML_HEAD_DIM ** -0.5)).astype(BF16)

    cqn = _rms(c_q, qg_ref[...]).astype(BF16)
    qa_t = _dot_nt(wuqt_ref[...], cqn)
    ckn = _rms(c_kv, kvg_ref[...]).astype(BF16)
    kn_t = _dot_nt(wukt_ref[...], ckn)
    vt_ref[0] = _dot_nt(wuvt_ref[...], ckn).astype(BF16)
    k_rope_t = rotate(kr.T, cos_tt, sin_tt)[QK_NOPE:QK_NOPE + QK_ROPE]
    qk_dim = QK_NOPE + QK_ROPE

    def rope_heads(first, last):
        for hd in range(first, last):
            lo = hd * LANES
            q_blk = jnp.concatenate([qa_t[hd * qk_dim:(hd + 1) * qk_dim], pad_hi], axis=0)
            qp_ref[0, lo:lo + LANES, :] = rotate(q_blk, cos_q, sin_q).astype(BF16)
            k_blk = jnp.concatenate([kn_t[hd * QK_NOPE:(hd + 1) * QK_NOPE], k_rope_t, pad_hi], axis=0)
            kp_ref[0, :, lo:lo + LANES] = k_blk.T.astype(BF16)

    vm_ref[0] = ml_cols(2).astype(BF16)
    rope_heads(0, MLA_HEADS // 2)
    om_ref[0] = ml_cols(3).astype(BF16)
    rope_heads(MLA_HEADS // 2, MLA_HEADS)


def _odd_proj(x, pos3, g, w_main, w_gt, gbias, q_g, kv_g, w_uqt, w_ukt, w_uvt, invf, *, ts):
    B, S, D = x.shape
    ts = min(ts, S)
    assert S % ts == 0
    tok = lambda n: pl.BlockSpec((1, ts, n), lambda b, s: (b, s, 0))
    out_shapes = (
        jax.ShapeDtypeStruct((B, S, ML_WIDTH), BF16),
        jax.ShapeDtypeStruct((B, S, ML_WIDTH), BF16),
        jax.ShapeDtypeStruct((B, S, ML_WIDTH), BF16),
        jax.ShapeDtypeStruct((B, S, ML_WIDTH), BF16),
        jax.ShapeDtypeStruct((B, 2 * ML_HEADS, S), F32),
        jax.ShapeDtypeStruct((B, MLA_HEADS * LANES, S), BF16),
        jax.ShapeDtypeStruct((B, S, MLA_HEADS * LANES), BF16),
        jax.ShapeDtypeStruct((B, MLA_WIDTH, S), BF16),
    )
    return pl.pallas_call(
        _odd_proj_kernel,
        grid=(B, S // ts),
        in_specs=[
            tok(D),
            pl.BlockSpec((1, 1, ts), lambda b, s: (b, 0, s)),
            _const_spec((1, D)),
            _const_spec(w_main.shape),
            _const_spec(w_gt.shape),
            _const_spec(gbias.shape),
            _const_spec(q_g.shape),
            _const_spec(kv_g.shape),
            _const_spec(w_uqt.shape),
            _const_spec(w_ukt.shape),
            _const_spec(w_uvt.shape),
            _const_spec(invf.shape),
        ],
        out_specs=(tok(ML_WIDTH), tok(ML_WIDTH), tok(ML_WIDTH), tok(ML_WIDTH),
                   pl.BlockSpec((1, 2 * ML_HEADS, ts), lambda b, s: (b, 0, s)),
                   pl.BlockSpec((1, MLA_HEADS * LANES, ts), lambda b, s: (b, 0, s)),
                   tok(MLA_HEADS * LANES),
                   pl.BlockSpec((1, MLA_WIDTH, ts), lambda b, s: (b, 0, s))),
        out_shape=out_shapes,
        compiler_params=_params("arbitrary", "arbitrary"),
        name="odd_proj",
    )(x, pos3, g.reshape(1, D), w_main, w_gt, gbias, q_g, kv_g, w_uqt, w_ukt, w_uvt, invf)


def _split3(a):
    hi = a.astype(BF16)
    r1 = a - hi.astype(F32)
    mid = r1.astype(BF16)
    lo = (r1 - mid.astype(F32)).astype(BF16)
    return hi, mid, lo


def _log2_sigmoid(x):
    return (jnp.minimum(x, 0.0) - jnp.log(1.0 + jnp.exp(-jnp.abs(x)))) * LOG2_E


def _mlstm_kernel(q_ref, k_ref, v_ref, o_ref, gt_ref, ng_ref, h_ref, c_scr, m_scr, *, L, nc):
    ci = pl.program_id(1)

    @pl.when(ci == 0)
    def _():
        c_scr[...] = jnp.zeros(c_scr.shape, F32)
        m_scr[...] = jnp.zeros(m_scr.shape, F32)

    row = lax.broadcasted_iota(jnp.int32, (L, L), 0)
    col = lax.broadcasted_iota(jnp.int32, (L, L), 1)
    causal = col <= row
    ones_v = jnp.ones((L, LANES), BF16)
    reps = L // LANES

    def widen(a):
        return jnp.concatenate([a] * reps, axis=1)

    for ch, hd in [(ch, hd) for ch in range(nc) for hd in range(ML_HEADS)]:
        tok = slice(ch * L, (ch + 1) * L)
        if hd == 0:
            gt = gt_ref[0, :, tok]
            lf_t = _log2_sigmoid(gt)
            b_t = sum(_dot(t, (row <= col).astype(BF16)) for t in _split3(lf_t))
            b_c = jnp.concatenate([b_t, jnp.zeros((LANES - 2 * ML_HEADS, L), F32)], axis=0).T
        lo = hd * ML_HEAD_DIM
        fg = ML_HEADS + hd
        q = q_ref[0, tok, lo:lo + ML_HEAD_DIM]
        k = k_ref[0, tok, lo:lo + ML_HEAD_DIM]
        v_aug = jnp.concatenate([v_ref[0, tok, lo:lo + ML_HEAD_DIM], ones_v], axis=1)
        g_row = gt[hd:hd + 1, :] * LOG2_E - b_t[fg:fg + 1, :]
        gtot = jnp.sum(lf_t[fg:fg + 1, :], axis=1, keepdims=True)
        b_col = jnp.broadcast_to(b_c[:, fg:fg + 1], (L, LANES))

        m_prev = m_scr[hd]
        c_prev = c_scr[hd]

        g_mask = jnp.where(causal, g_row, NEG_BIG)
        gmax = jnp.max(g_mask, axis=1, keepdims=True)
        h_col = jnp.maximum(m_prev, jnp.broadcast_to(gmax, (L, LANES)))
        w_inter = jnp.exp2(m_prev - h_col)
        sw = jnp.exp2(g_mask - widen(h_col)) * _dot_nt(q, k)
        out = (jnp.concatenate([w_inter, w_inter], axis=1) * _dot(q, c_prev.astype(BF16))
               + _dot(sw.astype(BF16), v_aug))
        den = jnp.maximum(jnp.abs(out[:, ML_HEAD_DIM:]), jnp.exp2(-(b_col + h_col)))
        hc = out[:, :ML_HEAD_DIM] / den

        og = jax.nn.sigmoid(o_ref[0, tok, lo:lo + ML_HEAD_DIM].astype(F32))
        hg = og * hc
        ms = jnp.mean(hg * hg, axis=1, keepdims=True)
        h_ref[0, tok, lo:lo + ML_HEAD_DIM] = (
            hg * lax.rsqrt(ms + EPS) * ng_ref[:, lo:lo + ML_HEAD_DIM]).astype(BF16)

        m_prev1 = m_prev[:, 0:1]
        a_row = gtot + g_row
        m_loc = jnp.max(a_row, axis=1, keepdims=True)
        m_new = jnp.maximum(gtot + m_prev1, m_loc)
        s_old = jnp.exp2(gtot + m_prev1 - m_new)
        w_loc = jnp.exp2(a_row - m_new)
        kw_t = (k.T.astype(F32) * w_loc).astype(BF16)
        c_scr[hd] = s_old * c_prev + _dot(kw_t, v_aug)
        m_scr[hd] = jnp.broadcast_to(m_new, (1, LANES))


def _mlstm(q, k, v, o, gates_t, norm_g, *, L, nc):
    B, S, W = q.shape
    L = min(L, S)
    nc = min(nc, S // L)
    assert S % (nc * L) == 0 and L % LANES == 0 and W == ML_WIDTH
    kern = functools.partial(_mlstm_kernel, L=L, nc=nc)
    tok = pl.BlockSpec((1, nc * L, W), lambda b, c: (b, c, 0))
    return pl.pallas_call(
        kern,
        grid=(B, S // (nc * L)),
        in_specs=[tok, tok, tok, tok,
                  pl.BlockSpec((1, 2 * ML_HEADS, nc * L), lambda b, c: (b, 0, c)),
                  _const_spec((1, W))],
        out_specs=tok,
        out_shape=jax.ShapeDtypeStruct((B, S, W), BF16),
        scratch_shapes=[pltpu.VMEM((ML_HEADS, ML_HEAD_DIM, 2 * ML_HEAD_DIM), F32),
                        pltpu.VMEM((ML_HEADS, 1, LANES), F32)],
        compiler_params=_params("arbitrary", "arbitrary"),
        name="mlstm",
    )(q, k, v, o, gates_t, norm_g.reshape(1, W))


def _mla_kernel(qt_ref, k_ref, vt_ref, o_ref, s00, s01, s10, s11, *, tq, nt):
    g = pl.program_id(2)
    heads = 2
    pieces = 2
    rows = tq // pieces
    hk = tq // 2
    s_scr = ((s00, s01), (s10, s11))
    ones_rows = jnp.ones((BF16_ROWS, tq), BF16)

    def run_tile(t, qi, sa, sb):
        lanes = slice(t * tq, (t + 1) * tq)

        def score_piece(j, slot, hh, pc):
            start = pl.multiple_of(j * tq + pc * rows, rows)
            kb = k_ref[0, pl.ds(start, rows), hh * LANES:(hh + 1) * LANES]
            qt = qt_ref[0, hh * LANES:(hh + 1) * LANES, lanes]
            s_scr[slot][hh][pc * rows:(pc + 1) * rows, :] = _dot(kb, qt)

        def stage_exp(j, slot, hh, m_new, acc, half):
            start = pl.multiple_of(j * tq + half * hk, hk)
            vtb = jnp.concatenate([vt_ref[0, hh * V_DIM:(hh + 1) * V_DIM, pl.ds(start, hk)],
                                   ones_rows[:, :hk]], axis=0)
            p = jnp.exp2(s_scr[slot][hh][half * hk:(half + 1) * hk, :] - m_new).astype(BF16)
            return acc + _dot(vtb, p)

        def block(j, slot, carry, nxt):
            new = []
            for hh in range(heads):
                m, acc = carry[hh]
                score_piece(nxt[0], nxt[1], hh, 0)
                m_new = jnp.maximum(m, jnp.max(s_scr[slot][hh][...], axis=0, keepdims=True))
                score_piece(nxt[0], nxt[1], hh, 1)
                acc = stage_exp(j, slot, hh, m_new, jnp.exp2(m - m_new) * acc, 0)
                acc = stage_exp(j, slot, hh, m_new, acc, 1)
                new.append((m_new, acc))
            return tuple(new)

        def diag_block(slot, carry):
            start = pl.multiple_of(qi * tq, tq)
            kk = lax.broadcasted_iota(jnp.int32, (hk, tq), 0)
            qq = lax.broadcasted_iota(jnp.int32, (hk, tq), 1)
            kb = lax.broadcasted_iota(jnp.int32, (hk, hk), 0)
            qb = lax.broadcasted_iota(jnp.int32, (hk, hk), 1)
            hs = range(heads)
            top = [jnp.where(kk <= qq, s_scr[slot][hh][0:hk, :], NEG_BIG) for hh in hs]
            bot = [jnp.where(kb <= qb, s_scr[slot][hh][hk:, hk:], NEG_BIG) for hh in hs]
            neg = jnp.full((hk, hk), NEG_BIG, F32)
            m_new = [jnp.maximum(carry[hh][0], jnp.max(
                jnp.maximum(top[hh], jnp.concatenate([neg, bot[hh]], axis=1)), axis=0, keepdims=True))
                for hh in hs]
            vt_aug = [jnp.concatenate([vt_ref[0, hh * V_DIM:(hh + 1) * V_DIM, pl.ds(start, tq)],
                                       ones_rows], axis=0) for hh in hs]
            p_top = [jnp.exp2(top[hh] - m_new[hh]).astype(BF16) for hh in hs]
            p_bot = [jnp.exp2(bot[hh] - jnp.broadcast_to(m_new[hh], (hk, tq))[:, hk:]).astype(BF16)
                     for hh in hs]
            outs = []
            for hh in hs:
                m, acc = carry[hh]
                acc = jnp.exp2(m - m_new[hh]) * acc + _dot(vt_aug[hh][:, :hk], p_top[hh])
                hi = acc[:, hk:] + _dot(vt_aug[hh][:, hk:], p_bot[hh])
                acc = jnp.concatenate([acc[:, :hk], hi], axis=1)
                outs.append(acc[:V_DIM] / acc[V_DIM:V_DIM + 1])
            return jnp.concatenate(outs, axis=0)

        for hh in range(heads):
            for pc in range(pieces):
                score_piece(0, sa, hh, pc)

        def body(i, c):
            c = block(2 * i, sa, c, (2 * i + 1, sb))
            return block(2 * i + 1, sb, c, (2 * i + 2, sa))

        init = tuple((jnp.full((1, tq), NEG_BIG, F32), jnp.zeros((V_DIM + BF16_ROWS, tq), F32))
                     for _ in range(heads))
        carry = lax.fori_loop(0, qi // 2, body, init)
        if t % 2 == 0:
            ot = diag_block(sa, carry)
        else:
            ot = diag_block(sb, block(qi - 1, sa, carry, (qi, sb)))
        o_ref[0, :, lanes] = ot.astype(BF16)

    slot_pairs = ((0, 1), (1, 0), (1, 0), (0, 1))
    for t in range(nt):
        run_tile(t, nt * g + t, *slot_pairs[t % 4])


def _mla_attention(qpt, kp, vt, *, tq, nt):
    B, S, _ = kp.shape
    tq = min(tq, S // 2)
    nt = min(nt, S // tq)
    assert nt % 2 == 0 and S % (nt * tq) == 0
    pairs = MLA_HEADS // 2
    kern = functools.partial(_mla_kernel, tq=tq, nt=nt)
    return pl.pallas_call(
        kern,
        grid=(B, pairs, S // (nt * tq)),
        in_specs=[
            pl.BlockSpec((1, 2 * LANES, nt * tq), lambda b, p, i: (b, p, i)),
            pl.BlockSpec((1, S, 2 * LANES), lambda b, p, i: (b, 0, p)),
            pl.BlockSpec((1, 2 * V_DIM, S), lambda b, p, i: (b, p, 0)),
        ],
        out_specs=pl.BlockSpec((1, LANES, nt * tq), lambda b, p, i: (b, p, i)),
        out_shape=jax.ShapeDtypeStruct((B, MLA_WIDTH, S), BF16),
        scratch_shapes=[pltpu.VMEM((tq, tq), F32) for _ in range(4)],
        compiler_params=_params("arbitrary", "arbitrary", "arbitrary"),
        name="mla_attention",
    )(qpt, kp, vt)


def _odd_out_kernel(x_ref, hm_ref, ha_ref, wout_ref, xg_ref, wq_ref, kt_ref, v_ref, wo_ref, o_ref):
    mix = jnp.concatenate([hm_ref[0], ha_ref[0].T], axis=-1)
    x1 = x_ref[0] + _dot(mix, wout_ref[...])
    o_ref[0] = _xattn_body(x1, xg_ref, wq_ref, kt_ref, v_ref, wo_ref)


def _odd_out(x, hm, ha, w_out, xg, wq_all, kv, wo_all, layer, *, ts):
    B, S, D = x.shape
    ts = min(ts, S)
    assert S % ts == 0
    tok = lambda n: pl.BlockSpec((1, ts, n), lambda b, s: (b, s, 0))
    return pl.pallas_call(
        _odd_out_kernel,
        grid=(B, S // ts),
        in_specs=[tok(D), tok(ML_WIDTH), pl.BlockSpec((1, MLA_WIDTH, ts), lambda b, s: (b, 0, s)),
                  _const_spec(w_out.shape)]
        + _xattn_specs(D, wq_all, kv, wo_all, layer),
        out_specs=tok(D),
        out_shape=jax.ShapeDtypeStruct(x.shape, F32),
        compiler_params=_params("arbitrary", "arbitrary"),
        name="odd_out",
    )(x, hm, ha, w_out, xg.reshape(1, D), wq_all, *kv, wo_all)


def _pack_odd_weights(w_in, w_uq, w_ukv):
    D = w_in.shape[0]
    c = 4 * ML_WIDTH
    w_g = w_in[:, c:c + 2 * ML_HEADS]
    c += 2 * ML_HEADS
    w_cq = w_in[:, c:c + Q_LORA]
    c += Q_LORA
    w_ckv = w_in[:, c:c + KV_LORA]
    c += KV_LORA
    w_kr = w_in[:, c:c + QK_ROPE]
    zl = jnp.zeros((D, QK_NOPE), w_in.dtype)
    zr = jnp.zeros((D, LANES - QK_NOPE - QK_ROPE), w_in.dtype)
    w_main = jnp.concatenate([w_in[:, :4 * ML_WIDTH], w_cq, w_ckv,
                              zl, w_kr, zr], axis=1).astype(BF16)
    w_gt = w_g.T.astype(BF16)

    w_uqt = w_uq.T
    ukv = w_ukv.reshape(KV_LORA, MLA_HEADS, QK_NOPE + V_DIM)
    w_ukt = ukv[..., :QK_NOPE].reshape(KV_LORA, MLA_HEADS * QK_NOPE).T
    w_uvt = ukv[..., QK_NOPE:].reshape(KV_LORA, MLA_WIDTH).T
    return w_main, w_gt, w_uqt.astype(BF16), w_ukt.astype(BF16), w_uvt.astype(BF16)


def _rope_inv_freq():
    half = QK_ROPE // 2
    inv = ROPE_THETA ** (-jnp.arange(half, dtype=F32) / half)
    return inv.reshape(half, 1)


TS_EVEN = 1024
TM_FFN = 1024
TH_FFN = 256
TS_ODD = 1024
L_MLSTM = 256
NC_MLSTM = 4
TQ_MLA = 512
NT_MLA = 8
TS_OUT = 1024


def kernel(x, mem, positions, norm_mix_g, norm_xattn_g, mem_norm_g, xattn_wq, xattn_wkv, xattn_wo,
           norm_ffn_g, ffn_w_gate_up, ffn_w_down, ev_w_in, ev_conv_w, ev_pool_w, ev_pool_scale,
           ev_w_out, od_w_in, od_gate_bias, od_ml_norm_g, od_q_norm_g, od_kv_norm_g, od_w_uq,
           od_w_ukv, od_w_out, final_norm_g):
    depth = norm_mix_g.shape[0]
    B, S, D = x.shape
    pos3 = positions.reshape(B, 1, S)
    invf = _rope_inv_freq()
    wq_all, wkv_all, wo_all = (w.astype(BF16) for w in (xattn_wq, xattn_wkv, xattn_wo))
    w_gu_all, w_d_all = ffn_w_gate_up.astype(BF16), ffn_w_down.astype(BF16)
    kv = _mem_kv(mem, mem_norm_g, wkv_all)
    for layer in range(depth):
        xa = (norm_xattn_g[layer], wq_all, kv, wo_all, layer)
        if layer % 2 == 0:
            e = layer // 2
            x = _even_mixer(x, norm_mix_g[layer], ev_w_in[e].astype(BF16), ev_conv_w[e],
                            ev_pool_w[e].astype(BF16), ev_pool_scale[e], ev_w_out[e].astype(BF16),
                            *xa, ts=TS_EVEN)
        else:
            o = layer // 2
            w_main, w_gt, w_uqt, w_ukt, w_uvt = _pack_odd_weights(
                od_w_in[o], od_w_uq[o], od_w_ukv[o])
            qm, km, vm, om, gates_t, qp, kp, vt = _odd_proj(
                x, pos3, norm_mix_g[layer], w_main, w_gt, od_gate_bias[o].reshape(2 * ML_HEADS, 1),
                od_q_norm_g[o].reshape(1, Q_LORA), od_kv_norm_g[o].reshape(1, KV_LORA),
                w_uqt, w_ukt, w_uvt, invf, ts=TS_ODD)
            hm = _mlstm(qm, km, vm, om, gates_t, od_ml_norm_g[o], L=L_MLSTM, nc=NC_MLSTM)
            ha = _mla_attention(qp, kp, vt, tq=TQ_MLA, nt=NT_MLA)
            x = _odd_out(x, hm, ha, od_w_out[o].astype(BF16), *xa, ts=TS_OUT)
        x = _swiglu(x, norm_ffn_g[layer], w_gu_all, w_d_all, final_norm_g, layer,
                    tm=TM_FFN, th=TH_FFN, final_norm=(layer == depth - 1))
    return x
```
